```python
import math
import jax
import jax.numpy as jnp
from jax import lax
import numpy as np

D_MODEL = 4096
BATCH = 2
SEQ = 8192
DEPTH = 4

PLE_DIM = 256
N_MIXERS = 3
N_FOX = (DEPTH + 2) // 3
N_GDN = (DEPTH + 1) // 3
N_SSM = DEPTH // 3
NORM_EPS = 1e-6

FOX_HEADS = 32
FOX_HEAD_DIM = D_MODEL // FOX_HEADS
FOX_WIDTH = FOX_HEADS * FOX_HEAD_DIM
Q_BLOCK = 128

GDN_HEADS = 32
GDN_HEAD_DIM = D_MODEL // GDN_HEADS
GDN_WIDTH = GDN_HEADS * GDN_HEAD_DIM
GDN_CONV = 4
GDN_CHUNK = 64

SSM_WIDTH = D_MODEL
SSM_GROUP = 16
SSM_GROUPS = SSM_WIDTH // SSM_GROUP
SSM_STATE = 64
SSM_CHUNK_MAX = 1024
DT_MIN = 1e-3
DT_MAX = 1e-1

kernel_name = 'hybrid_fox_gdn_s5_ple_trunk'


def rmsnorm(x, g):
    xf = x.astype(jnp.float32)
    y = xf * lax.rsqrt(jnp.mean(xf * xf, axis=-1, keepdims=True) + NORM_EPS)
    return (y * g.astype(jnp.float32)).astype(x.dtype)


def l2norm(x):
    return x * lax.rsqrt(jnp.sum(x * x, axis=-1, keepdims=True) + NORM_EPS)


def fox_mixer(h, w_in, b_f, w_out):
    bsz, s, _ = h.shape
    H, dh, W = FOX_HEADS, FOX_HEAD_DIM, FOX_WIDTH
    proj = h @ w_in
    q, k, v, z, f_logit = jnp.split(proj, [W, 2 * W, 3 * W, 4 * W], axis=-1)
    to_heads = lambda t: t.reshape(bsz, s, H, dh).transpose(0, 2, 1, 3)
    q, k, v = to_heads(q), to_heads(k), to_heads(v)
    log_f = jax.nn.log_sigmoid(f_logit.astype(jnp.float32) + b_f.astype(jnp.float32))
    cum = jnp.cumsum(log_f, axis=1).transpose(0, 2, 1)
    nb = s // Q_BLOCK
    q_blocks = q.reshape(bsz, H, nb, Q_BLOCK, dh).transpose(2, 0, 1, 3, 4)
    c_blocks = cum.reshape(bsz, H, nb, Q_BLOCK).transpose(2, 0, 1, 3)
    key_pos = jnp.arange(s)
    scale = dh ** -0.5

    def attend_block(args):
        q_blk, c_blk, blk = args
        query_pos = blk * Q_BLOCK + jnp.arange(Q_BLOCK)
        logits = jnp.einsum('bhqd,bhkd->bhqk', q_blk, k, preferred_element_type=jnp.float32) * scale
        logits = logits + c_blk[..., :, None] - cum[..., None, :]
        logits = jnp.where(key_pos[None, :] <= query_pos[:, None], logits, -jnp.inf)
        probs = jax.nn.softmax(logits, axis=-1)
        return jnp.einsum('bhqk,bhkd->bhqd', probs.astype(v.dtype), v)

    o = lax.map(attend_block, (q_blocks, c_blocks, jnp.arange(nb)))
    o = o.transpose(1, 0, 3, 2, 4).reshape(bsz, s, W)
    return (o * jax.nn.silu(z)) @ w_out


def causal_conv_silu(x, w):
    k_width, s = w.shape[0], x.shape[1]
    xp = jnp.pad(x, ((0, 0), (k_width - 1, 0), (0, 0)))
    y = xp[:, 0:s] * w[0]
    for j in range(1, k_width):
        y = y + xp[:, j:j + s] * w[j]
    return jax.nn.silu(y)


def chunk_gated_delta_rule(q, k, v, beta, g):
    bsz, nh, s, dk = q.shape
    dv = v.shape[-1]
    c = GDN_CHUNK
    nc = s // c
    q, k, v = (t.reshape(bsz, nh, nc, c, t.shape[-1]) for t in (q, k, v))
    beta = beta.reshape(bsz, nh, nc, c)
    g = jnp.cumsum(g.reshape(bsz, nh, nc, c), axis=-1)
    incl = jnp.tril(jnp.ones((c, c), dtype=bool))
    strict = jnp.tril(jnp.ones((c, c), dtype=bool), k=-1)
    decay = jnp.exp(jnp.where(incl, g[..., :, None] - g[..., None, :], -jnp.inf))
    k_beta = k * beta[..., None]
    lower = jnp.where(strict, jnp.einsum('bhnid,bhnjd->bhnij', k_beta, k) * decay, 0.0)
    rhs = jnp.concatenate([v * beta[..., None], k_beta * jnp.exp(g)[..., None]], axis=-1)
    sol = lax.linalg.triangular_solve(lower + jnp.eye(c, dtype=lower.dtype), rhs,
                                      left_side=True, lower=True, unit_diagonal=True)
    u, w = sol[..., :dv], sol[..., dv:]
    intra = jnp.einsum('bhnid,bhnjd->bhnij', q, k) * decay
    g_last = g[..., -1]
    q_dec = q * jnp.exp(g)[..., None]
    k_dec = k * jnp.exp(g_last[..., None] - g)[..., None]
    xs = tuple(jnp.moveaxis(t, 2, 0) for t in (q_dec, k_dec, intra, u, w, jnp.exp(g_last)))

    def step(state, inp):
        qd, kd, a, u_c, w_c, gl = inp
        v_new = u_c - jnp.einsum('bhck,bhkv->bhcv', w_c, state)
        out = jnp.einsum('bhck,bhkv->bhcv', qd, state) + jnp.einsum('bhcj,bhjv->bhcv', a, v_new)
        state = state * gl[..., None, None] + jnp.einsum('bhck,bhcv->bhkv', kd, v_new)
        return state, out

    state0 = jnp.zeros((bsz, nh, dk, dv), jnp.float32)
    _, o = lax.scan(step, state0, xs)
    return jnp.moveaxis(o, 0, 2).reshape(bsz, nh, s, dv)


def gdn_mixer(h, w_in, conv_w, a_log, dt_bias, norm_w, w_out):
    bsz, s, _ = h.shape
    H, dh, W = GDN_HEADS, GDN_HEAD_DIM, GDN_WIDTH
    proj = h @ w_in
    qkv, z, b_logit, a_logit = jnp.split(proj, [3 * W, 4 * W, 4 * W + H], axis=-1)
    qkv = causal_conv_silu(qkv, conv_w).astype(jnp.float32)
    to_heads = lambda t: t.reshape(bsz, s, H, dh).transpose(0, 2, 1, 3)
    q, k, v = (to_heads(t) for t in jnp.split(qkv, 3, axis=-1))
    q = l2norm(q) * dh ** -0.5
    k = l2norm(k)
    beta = jax.nn.sigmoid(b_logit.astype(jnp.float32)).transpose(0, 2, 1)
    g = -(jnp.exp(a_log.astype(jnp.float32))
          * jax.nn.softplus(a_logit.astype(jnp.float32) + dt_bias.astype(jnp.float32))).transpose(0, 2, 1)
    o = chunk_gated_delta_rule(q, k, v, beta, g).transpose(0, 2, 1, 3)
    o = rmsnorm(o, norm_w) * jax.nn.silu(z.astype(jnp.float32).reshape(bsz, s, H, dh))
    return o.reshape(bsz, s, W).astype(h.dtype) @ w_out


def complex_linear_combine(e1, e2):
    a1r, a1i, b1r, b1i = e1
    a2r, a2i, b2r, b2i = e2
    return (a2r * a1r - a2i * a1i,
            a2r * a1i + a2i * a1r,
            a2r * b1r - a2i * b1i + b2r,
            a2r * b1i + a2i * b1r + b2i)


def ssm_mixer(h, w_in, lam_re, lam_im, b_re, b_im, c_re, c_im, log_step, d_skip, w_glu, b_glu, w_out):
    f32 = jnp.float32
    bsz, s, _ = h.shape
    G, N, P, E = SSM_GROUPS, SSM_GROUP, SSM_STATE, SSM_WIDTH
    u, z = jnp.split(h @ w_in, 2, axis=-1)
    uf = u.astype(f32)
    lam_re, lam_im, b_re, b_im, c_re, c_im = (t.astype(f32) for t in (lam_re, lam_im, b_re, b_im, c_re, c_im))
    step = jnp.exp(log_step.astype(f32))[:, None]
    mag = jnp.exp(lam_re * step)
    lb_re, lb_im = mag * jnp.cos(lam_im * step), mag * jnp.sin(lam_im * step)
    den = lam_re * lam_re + lam_im * lam_im
    num_re = lb_re - 1.0
    zoh_re = (num_re * lam_re + lb_im * lam_im) / den
    zoh_im = (lb_im * lam_re - num_re * lam_im) / den
    bb_re = zoh_re[..., None] * b_re - zoh_im[..., None] * b_im
    bb_im = zoh_re[..., None] * b_im + zoh_im[..., None] * b_re
    ck = math.gcd(s, SSM_CHUNK_MAX)
    nck = s // ck
    u_chunks = uf.reshape(bsz, nck, ck, G, N).transpose(1, 2, 0, 3, 4)
    a_re = jnp.broadcast_to(lb_re, (ck, 1, G, P))
    a_im = jnp.broadcast_to(lb_im, (ck, 1, G, P))

    def chunk_step(carry, u_c):
        h_re, h_im = carry
        bu_re = jnp.einsum('tbgn,gpn->tbgp', u_c, bb_re)
        bu_im = jnp.einsum('tbgn,gpn->tbgp', u_c, bb_im)
        pw_re, pw_im, x_re, x_im = lax.associative_scan(
            complex_linear_combine, (a_re, a_im, bu_re, bu_im), axis=0)
        x_re, x_im = (x_re + pw_re * h_re - pw_im * h_im,
                      x_im + pw_re * h_im + pw_im * h_re)
        y = jnp.einsum('tbgp,gnp->tbgn', x_re, c_re) - jnp.einsum('tbgp,gnp->tbgn', x_im, c_im)
        return (x_re[-1], x_im[-1]), y

    carry0 = (jnp.zeros((bsz, G, P), f32), jnp.zeros((bsz, G, P), f32))
    _, y = lax.scan(chunk_step, carry0, u_chunks)
    y = y.transpose(2, 0, 1, 3, 4).reshape(bsz, s, E) + d_skip.astype(f32) * uf
    y = jax.nn.gelu(y)
    y = y * jax.nn.sigmoid(y @ w_glu.astype(f32) + b_glu.astype(f32))
    y = y * jax.nn.silu(z.astype(f32))
    return y.astype(h.dtype) @ w_out


def setup_inputs(seed: int = 0) -> dict:
    key = jax.random.key(seed)
    ks = jax.random.split(key, 32)
    f32 = jnp.float32
    D = D_MODEL
    nrm = lambda k, shape, scale: jax.random.normal(k, shape, f32) * scale
    unif = lambda k, shape, lo, hi: jax.random.uniform(k, shape, f32, lo, hi)
    x = nrm(ks[0], (BATCH, SEQ, D), 1.0)
    p = nrm(ks[1], (DEPTH, BATCH, SEQ, PLE_DIM), 1.0)
    norm_mix = 1.0 + nrm(ks[2], (DEPTH, D), 0.01)
    fox_w_in = nrm(ks[3], (N_FOX, D, 4 * FOX_WIDTH + FOX_HEADS), D ** -0.5)
    fox_b_f = nrm(ks[4], (N_FOX, FOX_HEADS), 0.1)
    fox_w_out = nrm(ks[5], (N_FOX, FOX_WIDTH, D), FOX_WIDTH ** -0.5)
    gdn_w_in = nrm(ks[6], (N_GDN, D, 4 * GDN_WIDTH + 2 * GDN_HEADS), D ** -0.5)
    gdn_conv = nrm(ks[7], (N_GDN, GDN_CONV, 3 * GDN_WIDTH), GDN_CONV ** -0.5)
    gdn_a_log = jnp.log(unif(ks[8], (N_GDN, GDN_HEADS), 1.0, 16.0))
    dt = jnp.exp(unif(ks[9], (N_GDN, GDN_HEADS), math.log(DT_MIN), math.log(DT_MAX)))
    gdn_dt_bias = dt + jnp.log(-jnp.expm1(-dt))
    gdn_norm = 1.0 + nrm(ks[10], (N_GDN, GDN_HEAD_DIM), 0.01)
    gdn_w_out = nrm(ks[11], (N_GDN, GDN_WIDTH, D), GDN_WIDTH ** -0.5)
    G, N, P, E = SSM_GROUPS, SSM_GROUP, SSM_STATE, SSM_WIDTH
    ssm_w_in = nrm(ks[12], (N_SSM, D, 2 * E), D ** -0.5)
    ssm_lam_re = -0.5 + nrm(ks[13], (N_SSM, G, P), 0.01)
    ssm_lam_im = jnp.pi * jnp.arange(P, dtype=f32) + nrm(ks[14], (N_SSM, G, P), 0.01)
    ssm_b_re = nrm(ks[15], (N_SSM, G, P, N), (2 * N) ** -0.5)
    ssm_b_im = nrm(ks[16], (N_SSM, G, P, N), (2 * N) ** -0.5)
    ssm_c_re = nrm(ks[17], (N_SSM, G, N, P), P ** -0.5)
    ssm_c_im = nrm(ks[18], (N_SSM, G, N, P), P ** -0.5)
    ssm_log_step = unif(ks[19], (N_SSM, G), math.log(DT_MIN), math.log(DT_MAX))
    ssm_d = nrm(ks[20], (N_SSM, E), 1.0)
    ssm_w_glu = nrm(ks[21], (N_SSM, E, E), E ** -0.5)
    ssm_b_glu = nrm(ks[22], (N_SSM, E), 0.01)
    ssm_w_out = nrm(ks[23], (N_SSM, E, D), E ** -0.5)
    norm_ple = 1.0 + nrm(ks[24], (DEPTH, D), 0.01)
    ple_w_proj = nrm(ks[25], (DEPTH, PLE_DIM, D), PLE_DIM ** -0.5)
    ple_w_gate = nrm(ks[26], (DEPTH, D, D), D ** -0.5)
    final_norm = 1.0 + nrm(ks[27], (D,), 0.01)
    return {'x': x, 'p': p, 'norm_mix': norm_mix,
            'fox_w_in': fox_w_in, 'fox_b_f': fox_b_f, 'fox_w_out': fox_w_out,
            'gdn_w_in': gdn_w_in, 'gdn_conv': gdn_conv, 'gdn_a_log': gdn_a_log,
            'gdn_dt_bias': gdn_dt_bias, 'gdn_norm': gdn_norm, 'gdn_w_out': gdn_w_out,
            'ssm_w_in': ssm_w_in, 'ssm_lam_re': ssm_lam_re, 'ssm_lam_im': ssm_lam_im,
            'ssm_b_re': ssm_b_re, 'ssm_b_im': ssm_b_im, 'ssm_c_re': ssm_c_re, 'ssm_c_im': ssm_c_im,
            'ssm_log_step': ssm_log_step, 'ssm_d': ssm_d, 'ssm_w_glu': ssm_w_glu,
            'ssm_b_glu': ssm_b_glu, 'ssm_w_out': ssm_w_out,
            'norm_ple': norm_ple, 'ple_w_proj': ple_w_proj, 'ple_w_gate': ple_w_gate,
            'final_norm': final_norm}


def reference(x, p, norm_mix, fox_w_in, fox_b_f, fox_w_out, gdn_w_in, gdn_conv, gdn_a_log,
              gdn_dt_bias, gdn_norm, gdn_w_out, ssm_w_in, ssm_lam_re, ssm_lam_im, ssm_b_re,
              ssm_b_im, ssm_c_re, ssm_c_im, ssm_log_step, ssm_d, ssm_w_glu, ssm_b_glu, ssm_w_out,
              norm_ple, ple_w_proj, ple_w_gate, final_norm):
    h = x
    for i in range(DEPTH):
        kind, j = i % N_MIXERS, i // N_MIXERS
        hn = rmsnorm(h, norm_mix[i])
        if kind == 0:
            y = fox_mixer(hn, fox_w_in[j], fox_b_f[j], fox_w_out[j])
        elif kind == 1:
            y = gdn_mixer(hn, gdn_w_in[j], gdn_conv[j], gdn_a_log[j], gdn_dt_bias[j],
                          gdn_norm[j], gdn_w_out[j])
        else:
            y = ssm_mixer(hn, ssm_w_in[j], ssm_lam_re[j], ssm_lam_im[j], ssm_b_re[j], ssm_b_im[j],
                          ssm_c_re[j], ssm_c_im[j], ssm_log_step[j], ssm_d[j], ssm_w_glu[j],
                          ssm_b_glu[j], ssm_w_out[j])
        h = h + y
        gate = jax.nn.sigmoid(rmsnorm(h, norm_ple[i]) @ ple_w_gate[i])
        h = h + gate * (p[i] @ ple_w_proj[i])
    return rmsnorm(h, final_norm)
```

```python
import functools
import math

import jax
import jax.numpy as jnp
from jax import lax
from jax.experimental import pallas as pl
from jax.experimental.pallas import tpu as pltpu

F32 = jnp.float32
BF16 = jnp.bfloat16

LANES = 128
SUBLANES = 8
VMEM_LIMIT_BYTES = 56 * 1024 * 1024

NORM_EPS = 1e-6
LOG2E = 1.4426950408889634
HEAD_DIM = 128
GDN_CHUNK = 64
GDN_CONV = 4
SSM_GROUP = 16
SSM_STATE = 64
SSM_PACK = LANES // SSM_GROUP
SSM_L = 16
NEG_BIG = -1e30

HIGHEST = lax.Precision.HIGHEST


def _cparams(*sem):
    return pltpu.CompilerParams(dimension_semantics=sem, vmem_limit_bytes=VMEM_LIMIT_BYTES)


def _sigmoid(x):
    return 1.0 / (1.0 + jnp.exp(-x))


def _silu(x):
    return x * _sigmoid(x)


def _softplus(x):
    return jnp.maximum(x, 0.0) + jnp.log(1.0 + jnp.exp(-jnp.abs(x)))


def _rmsnorm_kernel(x_ref, g_ref, o_ref):
    x = x_ref[...]
    ms = jnp.mean(x * x, axis=-1, keepdims=True)
    o_ref[...] = (x * lax.rsqrt(ms + NORM_EPS) * g_ref[...]).astype(o_ref.dtype)


def _rmsnorm(x, g, out_dtype):
    m, d = x.shape
    tm = min(256, m)
    return pl.pallas_call(
        _rmsnorm_kernel,
        grid=(m // tm,),
        in_specs=[pl.BlockSpec((tm, d), lambda i: (i, 0)),
                  pl.BlockSpec((1, d), lambda i: (0, 0))],
        out_specs=pl.BlockSpec((tm, d), lambda i: (i, 0)),
        out_shape=jax.ShapeDtypeStruct((m, d), out_dtype),
        compiler_params=_cparams("parallel"),
        name="rmsnorm",
    )(x, g.reshape(1, d).astype(F32))


def _mm_kernel(*refs, mode, nk):
    a_ref, w_ref = refs[0], refs[1]
    o_ref = refs[-2] if nk > 1 else refs[-1]
    extra = refs[2:-2] if nk > 1 else refs[2:-1]

    def epilogue(acc):
        if mode == "plain":
            out = acc
        elif mode == "colscale":
            out = acc * extra[0][...]
        elif mode == "residual":
            out = extra[0][...] + acc
        elif mode == "ple":
            res_ref, p_ref, wp_ref = extra
            emb = jnp.dot(p_ref[...], wp_ref[...], preferred_element_type=F32)
            out = res_ref[...] + _sigmoid(acc) * emb
        elif mode == "glu":
            y_ref, z_ref, b_ref = extra
            out = y_ref[...] * _sigmoid(acc + b_ref[...]) * _silu(z_ref[...])
        else:
            raise ValueError(mode)
        o_ref[...] = out.astype(o_ref.dtype)

    part = jnp.dot(a_ref[...], w_ref[...], preferred_element_type=F32)
    if nk == 1:
        epilogue(part)
        return
    acc_ref = refs[-1]
    k = pl.program_id(2)

    @pl.when(k == 0)
    def _():
        acc_ref[...] = part

    @pl.when(k > 0)
    def _():
        acc_ref[...] += part

    @pl.when(k == nk - 1)
    def _():
        epilogue(acc_ref[...])


def _matmul(a, w, *, out_dtype, mode="plain", extras=(), tm=512, tn=1024, tk=None):
    m, kdim = a.shape
    n = w.shape[1]
    tm, tn = min(tm, m), min(tn, n)
    tk = kdim if tk is None else min(tk, kdim)
    nk = kdim // tk
    assert m % tm == 0 and n % tn == 0 and kdim % tk == 0
    in_specs = [pl.BlockSpec((tm, tk), lambda i, j, k: (i, k)),
                pl.BlockSpec((tk, tn), lambda i, j, k: (k, j))]
    tile = pl.BlockSpec((tm, tn), lambda i, j, k: (i, j))
    row = pl.BlockSpec((1, tn), lambda i, j, k: (0, j))
    if mode == "colscale":
        in_specs += [row]
    elif mode == "residual":
        in_specs += [tile]
    elif mode == "ple":
        pd = extras[1].shape[1]
        in_specs += [tile, pl.BlockSpec((tm, pd), lambda i, j, k: (i, 0)),
                     pl.BlockSpec((pd, tn), lambda i, j, k: (0, j))]
    elif mode == "glu":
        in_specs += [tile, tile, row]
    scratch = [pltpu.VMEM((tm, tn), F32)] if nk > 1 else []
    return pl.pallas_call(
        functools.partial(_mm_kernel, mode=mode, nk=nk),
        grid=(m // tm, n // tn, nk),
        in_specs=in_specs,
        out_specs=tile,
        out_shape=jax.ShapeDtypeStruct((m, n), out_dtype),
        scratch_shapes=scratch,
        compiler_params=_cparams("parallel", "parallel", "arbitrary"),
        name="matmul_" + mode,
    )(a, w, *extras)


def _fox_cum_kernel(f_ref, b_ref, o_ref, carry_ref, *, ts):
    @pl.when(pl.program_id(1) == 0)
    def _():
        carry_ref[...] = jnp.zeros_like(carry_ref)

    x = f_ref[...] + b_ref[...]
    log_f = jnp.minimum(x, 0.0) - jnp.log(1.0 + jnp.exp(-jnp.abs(x)))
    r = lax.broadcasted_iota(jnp.int32, (ts, ts), 0)
    c = lax.broadcasted_iota(jnp.int32, (ts, ts), 1)
    lower = (c <= r).astype(F32)
    cum = jnp.dot(lower, log_f, precision=HIGHEST, preferred_element_type=F32) + carry_ref[...]
    carry_ref[...] = cum[ts - 1:ts, :]
    o_ref[0] = cum.T


def _fox_cum(f_logit, b_f, bsz, s):
    ts = min(512, s)
    return pl.pallas_call(
        functools.partial(_fox_cum_kernel, ts=ts),
        grid=(bsz, s // ts),
        in_specs=[pl.BlockSpec((ts, LANES), lambda b, t: (b * (s // ts) + t, 0)),
                  pl.BlockSpec((1, LANES), lambda b, t: (0, 0))],
        out_specs=pl.BlockSpec((1, LANES, ts), lambda b, t: (b, 0, t)),
        out_shape=jax.ShapeDtypeStruct((bsz, LANES, s), F32),
        scratch_shapes=[pltpu.VMEM((1, LANES), F32)],
        compiler_params=_cparams("parallel", "arbitrary"),
        name="fox_cum",
    )(f_logit, b_f)


def _fox_attn_kernel(q_ref, k_ref, v_ref, c_ref, z_ref, o_ref, *, tq):
    i = pl.program_id(2)
    q = q_ref[...]
    c0 = c_ref[0, 0, :, pl.ds(pl.multiple_of(i * tq, tq), LANES)][:, 0:1]

    def block(j, carry, masked):
        m, l, acc = carry
        start = pl.multiple_of(j * tq, tq)
        kb = k_ref[pl.ds(start, tq), :]
        vb = v_ref[pl.ds(start, tq), :]
        s = lax.dot_general(q, kb, (((1,), (1,)), ((), ())), preferred_element_type=F32)
        bias = (c0 - c_ref[0, 0, :, pl.ds(start, tq)]) * LOG2E
        s = s + bias
        if masked:
            r = lax.broadcasted_iota(jnp.int32, (tq, tq), 0)
            c = lax.broadcasted_iota(jnp.int32, (tq, tq), 1)
            s = jnp.where(c <= r, s, NEG_BIG)
        m_new = jnp.maximum(m, jnp.max(s, axis=-1, keepdims=True))
        alpha = jnp.exp2(m - m_new)
        p = jnp.exp2(s - m_new)
        l = alpha * l + jnp.sum(p, axis=-1, keepdims=True)
        acc = alpha * acc + jnp.dot(p.astype(BF16), vb, preferred_element_type=F32)
        return m_new, l, acc

    init = (jnp.full((tq, 1), NEG_BIG, F32), jnp.zeros((tq, 1), F32),
            jnp.zeros((tq, HEAD_DIM), F32))
    carry = lax.fori_loop(0, i, lambda j, cr: block(j, cr, False), init)
    _, l, acc = block(i, carry, True)
    o_ref[...] = ((acc / l) * _silu(z_ref[...])).astype(o_ref.dtype)


def _fox_attention(qkv, cum4, z, bsz, s, heads):
    tq = min(512, s)
    nq = s // tq
    return pl.pallas_call(
        functools.partial(_fox_attn_kernel, tq=tq),
        grid=(bsz, heads, nq),
        in_specs=[pl.BlockSpec((tq, HEAD_DIM), lambda b, h, i: (b * nq + i, h)),
                  pl.BlockSpec((s, HEAD_DIM), lambda b, h, i: (b, heads + h)),
                  pl.BlockSpec((s, HEAD_DIM), lambda b, h, i: (b, 2 * heads + h)),
                  pl.BlockSpec((1, 1, 1, s), lambda b, h, i: (b, h, 0, 0)),
                  pl.BlockSpec((tq, HEAD_DIM), lambda b, h, i: (b * nq + i, h))],
        out_specs=pl.BlockSpec((tq, HEAD_DIM), lambda b, h, i: (b * nq + i, h)),
        out_shape=jax.ShapeDtypeStruct((bsz * s, heads * HEAD_DIM), BF16),
        compiler_params=_cparams("parallel", "parallel", "arbitrary"),
        name="fox_attn",
    )(qkv, qkv, qkv, cum4, z)


def _fox_layer(hn, w_in, b_f, w_out, h, bsz, s):
    d = hn.shape[1]
    width = w_out.shape[0]
    heads = width // HEAD_DIM
    w_qkv = w_in[:, :3 * width].astype(BF16)
    w_z = w_in[:, 3 * width:4 * width].astype(BF16)
    w_f = jnp.pad(w_in[:, 4 * width:], ((0, 0), (0, LANES - heads))).astype(BF16)
    q_scale = jnp.concatenate([jnp.full((width,), HEAD_DIM ** -0.5 * LOG2E, F32),
                               jnp.ones((2 * width,), F32)]).reshape(1, 3 * width)
    qkv = _matmul(hn, w_qkv, out_dtype=BF16, mode="colscale", extras=(q_scale,))
    z = _matmul(hn, w_z, out_dtype=F32)
    f_logit = _matmul(hn, w_f, out_dtype=F32)
    b_row = jnp.pad(b_f.astype(F32), (0, LANES - heads)).reshape(1, LANES)
    cum = _fox_cum(f_logit, b_row, bsz, s)
    gated = _fox_attention(qkv, cum.reshape(bsz, LANES, 1, s), z, bsz, s, heads)
    return _matmul(gated, w_out.astype(BF16), out_dtype=F32, mode="residual", extras=(h,))


def _dot3(a, b):
    a_hi = a.astype(BF16)
    b_hi = b.astype(BF16)
    a_lo = (a - a_hi.astype(F32)).astype(BF16)
    b_lo = (b - b_hi.astype(F32)).astype(BF16)
    out = jnp.dot(a_hi, b_hi, preferred_element_type=F32)
    out = out + jnp.dot(a_hi, b_lo, preferred_element_type=F32)
    return out + jnp.dot(a_lo, b_hi, preferred_element_type=F32)


def _gdn_kernel(q_ref, k_ref, v_ref, z_ref, t_ref, cw_ref, gp_ref, nw_ref, o_ref,
                xq_ref, xk_ref, xv_ref, qs_ref, ks_ref, vs_ref, bg_ref, st_ref, *, tb, heads):
    h = pl.program_id(1)
    c = GDN_CHUNK
    pad = SUBLANES

    @pl.when(pl.program_id(2) == 0)
    def _():
        for r in (xq_ref, xk_ref, xv_ref):
            r[0:pad, :] = jnp.zeros((pad, HEAD_DIM), F32)
        st_ref[...] = jnp.zeros_like(st_ref)

    def conv(x_ref, xs_ref, col):
        xs_ref[pad:pad + tb, :] = x_ref[...]
        y = xs_ref[pad - (GDN_CONV - 1):pad - (GDN_CONV - 1) + tb, :] * cw_ref[col, 0:1, :]
        for j in range(1, GDN_CONV):
            off = pad - (GDN_CONV - 1) + j
            y = y + xs_ref[off:off + tb, :] * cw_ref[col, j:j + 1, :]
        xs_ref[0:pad, :] = xs_ref[tb:tb + pad, :]
        return _silu(y)

    def l2n(x):
        return x * lax.rsqrt(jnp.sum(x * x, axis=-1, keepdims=True) + NORM_EPS)

    qs_ref[...] = l2n(conv(q_ref, xq_ref, 0)) * (HEAD_DIM ** -0.5)
    ks_ref[...] = l2n(conv(k_ref, xk_ref, 1))
    vs_ref[...] = conv(v_ref, xv_ref, 2)

    t = t_ref[...]
    lane = lax.broadcasted_iota(jnp.int32, (tb, LANES), 1)
    beta_all = _sigmoid(t)
    g_all = -jnp.exp(gp_ref[0:1, :]) * _softplus(t + gp_ref[1:2, :])
    bg_ref[:, 0:1] = jnp.sum(jnp.where(lane == h, beta_all, 0.0), axis=-1, keepdims=True)
    bg_ref[:, 1:2] = jnp.sum(jnp.where(lane == heads + h, g_all, 0.0), axis=-1, keepdims=True)

    ri = lax.broadcasted_iota(jnp.int32, (c, c), 0)
    ci = lax.broadcasted_iota(jnp.int32, (c, c), 1)
    incl = ci <= ri
    strict = ci < ri
    eye = (ci == ri).astype(F32)
    tri = incl.astype(F32)
    rl = lax.broadcasted_iota(jnp.int32, (c, LANES), 0)
    cl = lax.broadcasted_iota(jnp.int32, (c, LANES), 1)
    seg_mask = (rl > cl) | (cl == c)

    def chunk(n, carry):
        r0 = pl.multiple_of(n * c, c)
        q = qs_ref[pl.ds(r0, c), :]
        k = ks_ref[pl.ds(r0, c), :]
        v = vs_ref[pl.ds(r0, c), :]
        beta = bg_ref[pl.ds(r0, c), 0:1]
        g = bg_ref[pl.ds(r0, c), 1:2]
        seg = jnp.dot(tri, jnp.where(seg_mask, g, 0.0), precision=HIGHEST,
                      preferred_element_type=F32)
        gc = seg[:, c:c + 1]
        g_last = seg[c - 1:c, c:c + 1]
        decay = jnp.exp(jnp.where(incl, seg[:, 0:c], NEG_BIG))
        k_beta = k * beta
        kk = lax.dot_general(k_beta, k, (((1,), (1,)), ((), ())), preferred_element_type=F32)
        a = jnp.where(strict, kk * decay, 0.0)
        inv = eye - a
        pw = a
        for _ in range(int(math.log2(c)) - 1):
            pw = _dot3(pw, pw)
            inv = inv + _dot3(inv, pw)
        egc = jnp.exp(gc)
        u = _dot3(inv, v * beta)
        w = _dot3(inv, k_beta * egc)
        qk = lax.dot_general(q, k, (((1,), (1,)), ((), ())), preferred_element_type=F32)
        intra = qk * decay
        state = st_ref[...]
        v_new = u - jnp.dot(w, state, preferred_element_type=F32)
        out = (jnp.dot(q * egc, state, preferred_element_type=F32)
               + jnp.dot(intra, v_new, preferred_element_type=F32))
        k_dec = k * jnp.exp(g_last - gc)
        st_ref[...] = state * jnp.exp(g_last) + lax.dot_general(
            k_dec, v_new, (((0,), (0,)), ((), ())), preferred_element_type=F32)
        ms = jnp.mean(out * out, axis=-1, keepdims=True)
        o = out * lax.rsqrt(ms + NORM_EPS) * nw_ref[...]
        o_ref[pl.ds(r0, c), :] = (o * _silu(z_ref[pl.ds(r0, c), :])).astype(o_ref.dtype)
        return carry

    lax.fori_loop(0, tb // c, chunk, 0)


def _gdn_mixer(qkv, z, tail, conv_w, gate_p, norm_w, bsz, s, heads):
    tb = min(512, s)
    nt = s // tb
    col = lambda off: pl.BlockSpec((tb, HEAD_DIM), lambda b, h, t: (b * nt + t, off + h))
    scr = lambda rows: pltpu.VMEM((rows, HEAD_DIM), F32)
    return pl.pallas_call(
        functools.partial(_gdn_kernel, tb=tb, heads=heads),
        grid=(bsz, heads, nt),
        in_specs=[col(0), col(heads), col(2 * heads),
                  pl.BlockSpec((tb, HEAD_DIM), lambda b, h, t: (b * nt + t, h)),
                  pl.BlockSpec((tb, LANES), lambda b, h, t: (b * nt + t, 0)),
                  pl.BlockSpec((3, GDN_CONV, HEAD_DIM), lambda b, h, t: (0, 0, h)),
                  pl.BlockSpec((2, LANES), lambda b, h, t: (0, 0)),
                  pl.BlockSpec((1, HEAD_DIM), lambda b, h, t: (0, 0))],
        out_specs=pl.BlockSpec((tb, HEAD_DIM), lambda b, h, t: (b * nt + t, h)),
        out_shape=jax.ShapeDtypeStruct((bsz * s, heads * HEAD_DIM), BF16),
        scratch_shapes=[scr(tb + 2 * SUBLANES)] * 3 + [scr(tb)] * 3
                       + [pltpu.VMEM((tb, LANES), F32), pltpu.VMEM((HEAD_DIM, HEAD_DIM), F32)],
        compiler_params=_cparams("parallel", "parallel", "arbitrary"),
        name="gdn_mixer",
    )(qkv, qkv, qkv, z, tail, conv_w, gate_p, norm_w)


def _gdn_layer(hn, w_in, conv_w, a_log, dt_bias, norm_w, w_out, h, bsz, s):
    width = w_out.shape[0]
    heads = width // HEAD_DIM
    w_qkv = w_in[:, :3 * width].astype(BF16)
    w_z = w_in[:, 3 * width:4 * width].astype(BF16)
    w_t = jnp.pad(w_in[:, 4 * width:], ((0, 0), (0, LANES - 2 * heads))).astype(BF16)
    qkv = _matmul(hn, w_qkv, out_dtype=F32)
    z = _matmul(hn, w_z, out_dtype=F32)
    tail = _matmul(hn, w_t, out_dtype=F32)
    cw = conv_w.astype(F32).reshape(GDN_CONV, 3, width).transpose(1, 0, 2)
    lane_pad = lambda x: jnp.pad(x.astype(F32), (heads, LANES - 2 * heads))
    gate_p = jnp.stack([lane_pad(a_log), lane_pad(dt_bias)])
    gated = _gdn_mixer(qkv, z, tail, cw, gate_p, norm_w.astype(F32).reshape(1, HEAD_DIM),
                       bsz, s, heads)
    return _matmul(gated, w_out.astype(BF16), out_dtype=F32, mode="residual", extras=(h,))


def _ssm_prep_kernel(lam_ref, bf_ref, cf_ref, bd_ref, wz_ref, wyt_ref, laml_ref):
    ns = SSM_PACK * SSM_STATE
    lam_re, lam_im, step_log = lam_ref[0, 0:1, :], lam_ref[0, 1:2, :], lam_ref[0, 2:3, :]
    step = jnp.exp(step_log)
    mag = jnp.exp(lam_re * step)
    lb_re, lb_im = mag * jnp.cos(lam_im * step), mag * jnp.sin(lam_im * step)
    den = lam_re * lam_re + lam_im * lam_im
    num_re = lb_re - 1.0
    zoh_re = (num_re * lam_re + lb_im * lam_im) / den
    zoh_im = (lb_im * lam_re - num_re * lam_im) / den
    b_re, b_im = bf_ref[0, 0], bf_ref[0, 1]
    bb_re = zoh_re * b_re - zoh_im * b_im
    bb_im = zoh_re * b_im + zoh_im * b_re
    c_re, c_im = cf_ref[0, 0], cf_ref[0, 1]
    cc = jnp.concatenate([c_re, -c_im], axis=1)

    pw_re, pw_im = jnp.ones_like(lb_re), jnp.zeros_like(lb_re)
    for d in range(SSM_L + 1):
        if d < SSM_L:
            a_re = pw_re * bb_re - pw_im * bb_im
            a_im = pw_re * bb_im + pw_im * bb_re
            a_d = jnp.concatenate([a_re, a_im], axis=1)
            bd_ref[0, d] = lax.dot_general(a_d, cc, (((1,), (1,)), ((), ())), precision=HIGHEST,
                                           preferred_element_type=F32).astype(bd_ref.dtype)
            tin = SSM_L - 1 - d
            wz_ref[0, tin * LANES:(tin + 1) * LANES, :] = a_d.astype(wz_ref.dtype)
        if d >= 1:
            y_re = pw_re * c_re - pw_im * c_im
            y_im = pw_re * c_im + pw_im * c_re
            wyt_ref[0, (d - 1) * LANES:d * LANES, :] = jnp.concatenate(
                [y_re, -y_im], axis=1).astype(wyt_ref.dtype)
        if d == SSM_L:
            laml_ref[0] = jnp.concatenate([pw_re, pw_im], axis=1)
        pw_re, pw_im = pw_re * lb_re - pw_im * lb_im, pw_re * lb_im + pw_im * lb_re


def _ssm_prep(lam, bfull, cfull):
    nsg = lam.shape[0]
    ns = SSM_PACK * SSM_STATE
    rows = SSM_L * LANES
    return pl.pallas_call(
        _ssm_prep_kernel,
        grid=(nsg,),
        in_specs=[pl.BlockSpec((1, 3, ns), lambda g: (g, 0, 0)),
                  pl.BlockSpec((1, 2, LANES, ns), lambda g: (g, 0, 0, 0)),
                  pl.BlockSpec((1, 2, LANES, ns), lambda g: (g, 0, 0, 0))],
        out_specs=[pl.BlockSpec((1, SSM_L, LANES, LANES), lambda g: (g, 0, 0, 0)),
                   pl.BlockSpec((1, rows, 2 * ns), lambda g: (g, 0, 0)),
                   pl.BlockSpec((1, rows, 2 * ns), lambda g: (g, 0, 0)),
                   pl.BlockSpec((1, 1, 2 * ns), lambda g: (g, 0, 0))],
        out_shape=[jax.ShapeDtypeStruct((nsg, SSM_L, LANES, LANES), BF16),
                   jax.ShapeDtypeStruct((nsg, rows, 2 * ns), BF16),
                   jax.ShapeDtypeStruct((nsg, rows, 2 * ns), BF16),
                   jax.ShapeDtypeStruct((nsg, 1, 2 * ns), F32)],
        compiler_params=_cparams("parallel"),
        name="ssm_prep",
    )(lam, bfull, cfull)


def _ssm_scan_kernel(u_ref, bd_ref, wz_ref, wyt_ref, laml_ref, d_ref, o_ref,
                     x_ref, z_ref, xp_ref, st_ref, *, tc):
    ns = SSM_PACK * SSM_STATE

    @pl.when(pl.program_id(2) == 0)
    def _():
        st_ref[...] = jnp.zeros_like(st_ref)

    for tau in range(SSM_L):
        x_ref[:, tau * LANES:(tau + 1) * LANES] = u_ref[pl.ds(tau, tc, stride=SSM_L), :].astype(BF16)

    z_ref[...] = jnp.dot(x_ref[...], wz_ref[0], preferred_element_type=F32)

    a_re, a_im = laml_ref[0, :, 0:ns], laml_ref[0, :, ns:2 * ns]

    def row(r, carry):
        s_re, s_im = carry
        xp_ref[pl.ds(r, 1), :] = jnp.concatenate([s_re, s_im], axis=1)
        zr = z_ref[pl.ds(r, 1), :]
        n_re = a_re * s_re - a_im * s_im + zr[:, 0:ns]
        n_im = a_re * s_im + a_im * s_re + zr[:, ns:2 * ns]
        return n_re, n_im

    s_re, s_im = lax.fori_loop(0, tc, row, (st_ref[:, 0:ns], st_ref[:, ns:2 * ns]))
    st_ref[...] = jnp.concatenate([s_re, s_im], axis=1)

    xprev = xp_ref[...].astype(BF16)
    for tau in range(SSM_L):
        y = lax.dot_general(xprev, wyt_ref[0, tau * LANES:(tau + 1) * LANES, :],
                            (((1,), (1,)), ((), ())), preferred_element_type=F32)
        for tin in range(tau + 1):
            y = y + jnp.dot(x_ref[:, tin * LANES:(tin + 1) * LANES], bd_ref[0, tau - tin],
                            preferred_element_type=F32)
        u_tau = u_ref[pl.ds(tau, tc, stride=SSM_L), :]
        y = y + d_ref[...] * u_tau
        y = 0.5 * y * (1.0 + jnp.tanh(math.sqrt(2.0 / math.pi) * (y + 0.044715 * (y * y * y))))
        o_ref[pl.ds(tau, tc, stride=SSM_L), :] = y


def _ssm_scan(u, bd, wz, wyt, laml, d_skip, bsz, s):
    nsg = u.shape[1] // LANES
    ns = SSM_PACK * SSM_STATE
    tb = min(4096, s)
    nt = s // tb
    tc = tb // SSM_L
    rows = SSM_L * LANES
    return pl.pallas_call(
        functools.partial(_ssm_scan_kernel, tc=tc),
        grid=(nsg, bsz, nt),
        in_specs=[pl.BlockSpec((tb, LANES), lambda g, b, t: (b * nt + t, g)),
                  pl.BlockSpec((1, SSM_L, LANES, LANES), lambda g, b, t: (g, 0, 0, 0)),
                  pl.BlockSpec((1, rows, 2 * ns), lambda g, b, t: (g, 0, 0)),
                  pl.BlockSpec((1, rows, 2 * ns), lambda g, b, t: (g, 0, 0)),
                  pl.BlockSpec((1, 1, 2 * ns), lambda g, b, t: (g, 0, 0)),
                  pl.BlockSpec((1, LANES), lambda g, b, t: (0, g))],
        out_specs=pl.BlockSpec((tb, LANES), lambda g, b, t: (b * nt + t, g)),
        out_shape=jax.ShapeDtypeStruct(u.shape, F32),
        scratch_shapes=[pltpu.VMEM((tc, rows), BF16), pltpu.VMEM((tc, 2 * ns), F32),
                        pltpu.VMEM((tc, 2 * ns), F32), pltpu.VMEM((1, 2 * ns), F32)],
        compiler_params=_cparams("parallel", "parallel", "arbitrary"),
        name="ssm_scan",
    )(u, bd, wz, wyt, laml, d_skip)


def _ssm_layer(hn, w_in, lam_re, lam_im, b_re, b_im, c_re, c_im, log_step, d_skip,
               w_glu, b_glu, w_out, h, bsz, s):
    e = w_out.shape[0]
    groups, nstate = lam_re.shape
    assert nstate == SSM_STATE and e == groups * SSM_GROUP and groups % SSM_PACK == 0
    nsg = groups // SSM_PACK
    ns = SSM_PACK * SSM_STATE
    lam = jnp.stack([lam_re.astype(F32).reshape(nsg, ns), lam_im.astype(F32).reshape(nsg, ns),
                     jnp.repeat(log_step.astype(F32), SSM_STATE).reshape(nsg, ns)], axis=1)
    eye = jnp.eye(SSM_PACK, dtype=F32)

    def expand_b(b):
        return jnp.einsum("sgpm,gh->sgmhp", b.astype(F32).reshape(nsg, SSM_PACK, SSM_STATE, SSM_GROUP),
                          eye).reshape(nsg, LANES, ns)

    def expand_c(c):
        return jnp.einsum("sgnp,gh->sgnhp", c.astype(F32).reshape(nsg, SSM_PACK, SSM_GROUP, SSM_STATE),
                          eye).reshape(nsg, LANES, ns)

    bfull = jnp.stack([expand_b(b_re), expand_b(b_im)], axis=1)
    cfull = jnp.stack([expand_c(c_re), expand_c(c_im)], axis=1)
    bd, wz, wyt, laml = _ssm_prep(lam, bfull, cfull)

    u = _matmul(hn, w_in[:, :e].astype(BF16), out_dtype=F32)
    z = _matmul(hn, w_in[:, e:].astype(BF16), out_dtype=F32)
    y = _ssm_scan(u, bd, wz, wyt, laml, d_skip.astype(F32).reshape(1, e), bsz, s)
    y2 = _matmul(y.astype(BF16), w_glu.astype(BF16), out_dtype=BF16, mode="glu",
                 extras=(y, z, b_glu.astype(F32).reshape(1, e)))
    return _matmul(y2, w_out.astype(BF16), out_dtype=F32, mode="residual", extras=(h,))


def kernel(x, p, norm_mix, fox_w_in, fox_b_f, fox_w_out, gdn_w_in, gdn_conv, gdn_a_log, gdn_dt_bias, gdn_norm, gdn_w_out, ssm_w_in, ssm_lam_re, ssm_lam_im, ssm_b_re, ssm_b_im, ssm_c_re, ssm_c_im, ssm_log_step, ssm_d, ssm_w_glu, ssm_b_glu, ssm_w_out, norm_ple, ple_w_proj, ple_w_gate, final_norm):
    bsz, s, d = x.shape
    depth = p.shape[0]
    h = x.reshape(bsz * s, d).astype(F32)
    for i in range(depth):
        kind, j = i % 3, i // 3
        hn = _rmsnorm(h, norm_mix[i], BF16)
        if kind == 0:
            h = _fox_layer(hn, fox_w_in[j], fox_b_f[j], fox_w_out[j], h, bsz, s)
        elif kind == 1:
            h = _gdn_layer(hn, gdn_w_in[j], gdn_conv[j], gdn_a_log[j], gdn_dt_bias[j],
                           gdn_norm[j], gdn_w_out[j], h, bsz, s)
        else:
            h = _ssm_layer(hn, ssm_w_in[j], ssm_lam_re[j], ssm_lam_im[j], ssm_b_re[j], ssm_b_im[j],
                           ssm_c_re[j], ssm_c_im[j], ssm_log_step[j], ssm_d[j], ssm_w_glu[j],
                           ssm_b_glu[j], ssm_w_out[j], h, bsz, s)
        hg = _rmsnorm(h, norm_ple[i], BF16)
        h = _matmul(hg, ple_w_gate[i].astype(BF16), out_dtype=F32, mode="ple",
                    extras=(h, p[i].reshape(bsz * s, -1).astype(BF16), ple_w_proj[i].astype(BF16)))
    return _rmsnorm(h, final_norm, F32).reshape(bsz, s, d)
```

```python
import functools
import math

import jax
import jax.numpy as jnp
from jax import lax
from jax.experimental import pallas as pl
from jax.experimental.pallas import tpu as pltpu

F32 = jnp.float32
BF16 = jnp.bfloat16

LANES = 128
SUBLANES = 8
VMEM_LIMIT_BYTES = 56 * 1024 * 1024

NORM_EPS = 1e-6
LOG2E = 1.4426950408889634
HEAD_DIM = 128
FOX_BLOCK = 1024
GDN_CHUNK = 64
GDN_CONV = 4
GDN_HEADS_PER_STEP = 4
GDN_TIME_BLOCK = 256
SSM_GROUP = 16
SSM_STATE = 64
SSM_PACK = LANES // SSM_GROUP
SSM_L = 16
NEG_BIG = -1e30

HIGHEST = lax.Precision.HIGHEST


def _cparams(*sem):
    return pltpu.CompilerParams(dimension_semantics=sem, vmem_limit_bytes=VMEM_LIMIT_BYTES)


def _sigmoid(x):
    return 1.0 / (1.0 + jnp.exp(-x))


def _silu(x):
    return x * _sigmoid(x)


def _softplus(x):
    return jnp.maximum(x, 0.0) + jnp.log(1.0 + jnp.exp(-jnp.abs(x)))


def _rmsnorm_kernel(x_ref, g_ref, o_ref):
    x = x_ref[...]
    ms = jnp.mean(x * x, axis=-1, keepdims=True)
    o_ref[...] = (x * lax.rsqrt(ms + NORM_EPS) * g_ref[...]).astype(o_ref.dtype)


def _rmsnorm(x, g, out_dtype):
    m, d = x.shape
    tm = min(256, m)
    return pl.pallas_call(
        _rmsnorm_kernel,
        grid=(m // tm,),
        in_specs=[pl.BlockSpec((tm, d), lambda i: (i, 0)),
                  pl.BlockSpec((1, d), lambda i: (0, 0))],
        out_specs=pl.BlockSpec((tm, d), lambda i: (i, 0)),
        out_shape=jax.ShapeDtypeStruct((m, d), out_dtype),
        compiler_params=_cparams("parallel"),
        name="rmsnorm",
    )(x, g.reshape(1, d).astype(F32))


def _mm_kernel(*refs, mode, nk):
    a_ref, w_ref = refs[0], refs[1]
    o_ref = refs[-2] if nk > 1 else refs[-1]
    extra = refs[2:-2] if nk > 1 else refs[2:-1]

    def epilogue(acc):
        if mode == "plain":
            out = acc
        elif mode == "colscale":
            out = acc * extra[0][...]
        elif mode == "residual":
            out = extra[0][...] + acc
        elif mode == "ple":
            res_ref, p_ref, wp_ref = extra
            emb = jnp.dot(p_ref[...], wp_ref[...], preferred_element_type=F32)
            out = res_ref[...] + _sigmoid(acc) * emb
        elif mode == "glu":
            y_ref, z_ref, b_ref = extra
            out = y_ref[...] * _sigmoid(acc + b_ref[...]) * _silu(z_ref[...])
        else:
            raise ValueError(mode)
        o_ref[...] = out.astype(o_ref.dtype)

    part = jnp.dot(a_ref[...], w_ref[...], preferred_element_type=F32)
    if nk == 1:
        epilogue(part)
        return
    acc_ref = refs[-1]
    k = pl.program_id(2)

    @pl.when(k == 0)
    def _():
        acc_ref[...] = part

    @pl.when(k > 0)
    def _():
        acc_ref[...] += part

    @pl.when(k == nk - 1)
    def _():
        epilogue(acc_ref[...])


def _matmul(a, w, *, out_dtype, mode="plain", extras=(), tm=512, tn=1024, tk=None):
    m, kdim = a.shape
    n = w.shape[1]
    tm, tn = min(tm, m), min(tn, n)
    tk = kdim if tk is None else min(tk, kdim)
    nk = kdim // tk
    assert m % tm == 0 and n % tn == 0 and kdim % tk == 0
    in_specs = [pl.BlockSpec((tm, tk), lambda i, j, k: (i, k)),
                pl.BlockSpec((tk, tn), lambda i, j, k: (k, j))]
    tile = pl.BlockSpec((tm, tn), lambda i, j, k: (i, j))
    row = pl.BlockSpec((1, tn), lambda i, j, k: (0, j))
    if mode == "colscale":
        in_specs += [row]
    elif mode == "residual":
        in_specs += [tile]
    elif mode == "ple":
        pd = extras[1].shape[1]
        in_specs += [tile, pl.BlockSpec((tm, pd), lambda i, j, k: (i, 0)),
                     pl.BlockSpec((pd, tn), lambda i, j, k: (0, j))]
    elif mode == "glu":
        in_specs += [tile, tile, row]
    scratch = [pltpu.VMEM((tm, tn), F32)] if nk > 1 else []
    return pl.pallas_call(
        functools.partial(_mm_kernel, mode=mode, nk=nk),
        grid=(m // tm, n // tn, nk),
        in_specs=in_specs,
        out_specs=tile,
        out_shape=jax.ShapeDtypeStruct((m, n), out_dtype),
        scratch_shapes=scratch,
        compiler_params=_cparams("parallel", "parallel", "arbitrary"),
        name="matmul_" + mode,
    )(a, w, *extras)


def _fox_cum_kernel(f_ref, b_ref, o_ref, carry_ref, *, ts):
    @pl.when(pl.program_id(1) == 0)
    def _():
        carry_ref[...] = jnp.zeros_like(carry_ref)

    x = f_ref[...] + b_ref[...]
    log_f = jnp.minimum(x, 0.0) - jnp.log(1.0 + jnp.exp(-jnp.abs(x)))
    r = lax.broadcasted_iota(jnp.int32, (ts, ts), 0)
    c = lax.broadcasted_iota(jnp.int32, (ts, ts), 1)
    lower = (c <= r).astype(F32)
    cum = jnp.dot(lower, log_f, precision=HIGHEST, preferred_element_type=F32) + carry_ref[...]
    carry_ref[...] = cum[ts - 1:ts, :]
    o_ref[0] = cum.T


def _fox_cum(f_logit, b_f, bsz, s):
    ts = min(512, s)
    return pl.pallas_call(
        functools.partial(_fox_cum_kernel, ts=ts),
        grid=(bsz, s // ts),
        in_specs=[pl.BlockSpec((ts, LANES), lambda b, t: (b * (s // ts) + t, 0)),
                  pl.BlockSpec((1, LANES), lambda b, t: (0, 0))],
        out_specs=pl.BlockSpec((1, LANES, ts), lambda b, t: (b, 0, t)),
        out_shape=jax.ShapeDtypeStruct((bsz, LANES, s), F32),
        scratch_shapes=[pltpu.VMEM((1, LANES), F32)],
        compiler_params=_cparams("parallel", "arbitrary"),
        name="fox_cum",
    )(f_logit, b_f)


def _fox_attn_kernel(q_ref, k_ref, v_ref, c_ref, z_ref, o_ref, sa_ref, sb_ref, *, tq):
    tk = tq // 2
    i = pl.program_id(2)
    q = q_ref[...]
    c0 = c_ref[0, 0, :, pl.ds(pl.multiple_of(i * tq, tq), LANES)][:, 0:1]
    ones = jnp.ones((tk, HEAD_DIM), BF16)

    def scores(qrows, j):
        start = j * tk if isinstance(j, int) else pl.multiple_of(j * tk, tk)
        s = lax.dot_general(qrows, k_ref[pl.ds(start, tk), :], (((1,), (1,)), ((), ())),
                            preferred_element_type=F32)
        return s + (c0 - c_ref[0, 0, :, pl.ds(start, tk)]) * LOG2E

    def update(s, j, m, acc):
        start = j * tk if isinstance(j, int) else pl.multiple_of(j * tk, tk)
        vb = jnp.concatenate([v_ref[pl.ds(start, tk), :], ones], axis=1)
        m_new = jnp.maximum(m, jnp.max(s, axis=-1, keepdims=True))
        p = jnp.exp2(s - m_new)
        acc = jnp.exp2(m - m_new) * acc + jnp.dot(p.astype(BF16), vb, preferred_element_type=F32)
        return m_new, acc

    sa_ref[...] = scores(q, 0)

    def pair(t, carry):
        m, acc = carry
        sb_ref[...] = scores(q, 2 * t + 1)
        m, acc = update(sa_ref[...], 2 * t, m, acc)
        sa_ref[...] = scores(q, 2 * t + 2)
        return update(sb_ref[...], 2 * t + 1, m, acc)

    init = (jnp.full((tq, 1), NEG_BIG, F32), jnp.zeros((tq, 2 * HEAD_DIM), F32))
    m, acc = lax.fori_loop(0, i, pair, init)

    r = lax.broadcasted_iota(jnp.int32, (tq, tk), 0)
    c = lax.broadcasted_iota(jnp.int32, (tq, tk), 1)
    s_bot = scores(q[tk:tq], 2 * i + 1)
    m, acc = update(jnp.where(c <= r, sa_ref[...], NEG_BIG), 2 * i, m, acc)
    r_bot = lax.broadcasted_iota(jnp.int32, (tk, tk), 0)
    c_bot = lax.broadcasted_iota(jnp.int32, (tk, tk), 1)
    _, acc_bot = update(jnp.where(c_bot <= r_bot, s_bot, NEG_BIG), 2 * i + 1,
                        m[tk:tq], acc[tk:tq])

    def finish(a, rows):
        o = a[:, 0:HEAD_DIM] / a[:, HEAD_DIM:HEAD_DIM + 1]
        o_ref[rows, :] = (o * _silu(z_ref[rows, :])).astype(o_ref.dtype)

    finish(acc[0:tk], slice(0, tk))
    finish(acc_bot, slice(tk, tq))


def _fox_attention(qkv, cum4, z, bsz, s, heads):
    tq = min(FOX_BLOCK, s)
    nq = s // tq
    return pl.pallas_call(
        functools.partial(_fox_attn_kernel, tq=tq),
        grid=(bsz, heads, nq),
        in_specs=[pl.BlockSpec((tq, HEAD_DIM), lambda b, h, i: (b * nq + i, h)),
                  pl.BlockSpec((s, HEAD_DIM), lambda b, h, i: (b, heads + h)),
                  pl.BlockSpec((s, HEAD_DIM), lambda b, h, i: (b, 2 * heads + h)),
                  pl.BlockSpec((1, 1, 1, s), lambda b, h, i: (b, h, 0, 0)),
                  pl.BlockSpec((tq, HEAD_DIM), lambda b, h, i: (b * nq + i, h))],
        out_specs=pl.BlockSpec((tq, HEAD_DIM), lambda b, h, i: (b * nq + i, h)),
        out_shape=jax.ShapeDtypeStruct((bsz * s, heads * HEAD_DIM), BF16),
        scratch_shapes=[pltpu.VMEM((tq, tq // 2), F32)] * 2,
        compiler_params=_cparams("parallel", "parallel", "arbitrary"),
        name="fox_attn",
    )(qkv, qkv, qkv, cum4, z)


def _fox_layer(hn, w_in, b_f, w_out, h, bsz, s):
    d = hn.shape[1]
    width = w_out.shape[0]
    heads = width // HEAD_DIM
    w_qkv = w_in[:, :3 * width].astype(BF16)
    w_z = w_in[:, 3 * width:4 * width].astype(BF16)
    w_f = jnp.pad(w_in[:, 4 * width:], ((0, 0), (0, LANES - heads))).astype(BF16)
    q_scale = jnp.concatenate([jnp.full((width,), HEAD_DIM ** -0.5 * LOG2E, F32),
                               jnp.ones((2 * width,), F32)]).reshape(1, 3 * width)
    qkv = _matmul(hn, w_qkv, out_dtype=BF16, mode="colscale", extras=(q_scale,))
    z = _matmul(hn, w_z, out_dtype=F32)
    f_logit = _matmul(hn, w_f, out_dtype=F32)
    b_row = jnp.pad(b_f.astype(F32), (0, LANES - heads)).reshape(1, LANES)
    cum = _fox_cum(f_logit, b_row, bsz, s)
    gated = _fox_attention(qkv, cum.reshape(bsz, LANES, 1, s), z, bsz, s, heads)
    return _matmul(gated, w_out.astype(BF16), out_dtype=F32, mode="residual", extras=(h,))


def _bdot(a, b):
    return jnp.einsum("bmk,bkn->bmn", a, b, preferred_element_type=F32)


def _split_bf16(x):
    hi = x.astype(BF16)
    return hi, (x - hi.astype(F32)).astype(BF16)


def _bdot3(a, b):
    a_hi, a_lo = _split_bf16(a)
    b_hi, b_lo = _split_bf16(b)
    return _bdot(a_hi, b_hi) + _bdot(a_hi, b_lo) + _bdot(a_lo, b_hi)


def _bdot_exact_lhs(a_bf16, b):
    b_hi = b.astype(BF16)
    b_mid, b_lo = _split_bf16(b - b_hi.astype(F32))
    return _bdot(a_bf16, b_hi) + _bdot(a_bf16, b_mid) + _bdot(a_bf16, b_lo)


def _gdn_kernel(q_ref, k_ref, v_ref, z_ref, t_ref, cw_ref, gp_ref, nw_ref, o_ref,
                xq_ref, xk_ref, xv_ref, st_ref, u_ref, wq_ref, ik_ref, gl_ref, *, tb, heads, hps):
    hb = pl.program_id(1)
    c, dh = GDN_CHUNK, HEAD_DIM
    nb = tb // c
    nbat = hps * nb
    pad = SUBLANES

    @pl.when(pl.program_id(2) == 0)
    def _():
        for r in (xq_ref, xk_ref, xv_ref):
            r[0:pad, :] = jnp.zeros((pad, hps * dh), F32)
        st_ref[...] = jnp.zeros_like(st_ref)

    def conv(x_ref, xs_ref, col):
        xs_ref[pad:pad + tb, :] = x_ref[...]
        y = xs_ref[pad - (GDN_CONV - 1):pad - (GDN_CONV - 1) + tb, :] * cw_ref[col, 0:1, :]
        for j in range(1, GDN_CONV):
            off = pad - (GDN_CONV - 1) + j
            y = y + xs_ref[off:off + tb, :] * cw_ref[col, j:j + 1, :]
        xs_ref[0:pad, :] = xs_ref[tb:tb + pad, :]
        return _silu(y)

    def by_chunk(x):
        w = x.shape[1] // hps
        return jnp.concatenate([x[:, g * w:(g + 1) * w].reshape(nb, c, w) for g in range(hps)], axis=0)

    def l2n(x):
        return x * lax.rsqrt(jnp.sum(x * x, axis=-1, keepdims=True) + NORM_EPS)

    q = l2n(by_chunk(conv(q_ref, xq_ref, 0))) * (dh ** -0.5)
    k = l2n(by_chunk(conv(k_ref, xk_ref, 1)))
    v = by_chunk(conv(v_ref, xv_ref, 2))

    t = t_ref[...]
    lane = lax.broadcasted_iota(jnp.int32, (tb, LANES), 1)
    beta_all = _sigmoid(t)
    g_all = -jnp.exp(gp_ref[0:1, :]) * _softplus(t + gp_ref[1:2, :])

    def pick(x, first):
        cols = [jnp.sum(jnp.where(lane == first + hb * hps + g, x, 0.0), axis=-1, keepdims=True)
                for g in range(hps)]
        return jnp.concatenate([col.reshape(nb, c, 1) for col in cols], axis=0)

    beta = pick(beta_all, 0)
    g = pick(g_all, heads)

    ri = lax.broadcasted_iota(jnp.int32, (c, c), 0)
    ci = lax.broadcasted_iota(jnp.int32, (c, c), 1)
    incl = ci <= ri
    strict = ci < ri
    eye = (ci == ri).astype(F32)
    tri = jnp.broadcast_to(incl.astype(BF16), (nbat, c, c))
    rl = lax.broadcasted_iota(jnp.int32, (c, LANES), 0)
    cl = lax.broadcasted_iota(jnp.int32, (c, LANES), 1)
    seg_mask = (rl > cl) | (cl == c)

    seg = _bdot_exact_lhs(tri, jnp.where(seg_mask, g, 0.0))
    gc = seg[:, :, c:c + 1]
    g_last = seg[:, c - 1:c, c:c + 1]
    decay = jnp.exp(jnp.where(incl, seg[:, :, 0:c], NEG_BIG))
    egc = jnp.exp(gc)
    k_beta = k * beta
    kq = jnp.einsum("bmd,bnd->bmn", jnp.concatenate([k_beta, q], axis=1).astype(BF16),
                    k.astype(BF16), preferred_element_type=F32)
    a = jnp.where(strict, kq[:, 0:c] * decay, 0.0)
    intra = kq[:, c:2 * c] * decay
    inv = eye - a
    pw = _bdot3(a, a)
    levels = int(math.log2(c)) - 1
    for lvl in range(levels):
        if lvl < levels - 1:
            both = _bdot3(jnp.concatenate([inv, pw], axis=1), pw)
            inv = inv + both[:, 0:c]
            pw = both[:, c:2 * c]
        else:
            inv = inv + _bdot3(inv, pw)
    uw = _bdot3(inv, jnp.concatenate([v * beta, k_beta * egc], axis=2))
    u_ref[...] = uw[:, :, 0:dh].reshape(nbat * c, dh)
    wq_ref[...] = jnp.concatenate([uw[:, :, dh:2 * dh], q * egc], axis=1).astype(BF16).reshape(
        nbat * 2 * c, dh)
    k_dec = k * jnp.exp(g_last - gc)
    for b in range(nbat):
        ik_ref[b * (c + dh):(b + 1) * (c + dh), :] = jnp.concatenate(
            [intra[b], k_dec[b].T], axis=0).astype(BF16)
    gl_ref[...] = jnp.broadcast_to(jnp.exp(g_last), (nbat, SUBLANES, dh)).reshape(nbat * SUBLANES, dh)

    def step(n, states):
        rows = lambda g, size: pl.ds(pl.multiple_of((g * nb + n) * size, size), size)
        ws = [jnp.dot(wq_ref[rows(g, 2 * c), :], states[g].astype(BF16), preferred_element_type=F32)
              for g in range(hps)]
        v_new = [u_ref[rows(g, c), :] - ws[g][0:c] for g in range(hps)]
        iv = [jnp.dot(ik_ref[pl.ds(pl.multiple_of((g * nb + n) * (c + dh), c), c + dh), :],
                      v_new[g].astype(BF16), preferred_element_type=F32) for g in range(hps)]
        new_states = []
        r0 = pl.multiple_of(n * c, c)
        for g in range(hps):
            out = ws[g][c:2 * c] + iv[g][0:c]
            gl = gl_ref[pl.ds(pl.multiple_of((g * nb + n) * SUBLANES, SUBLANES), 1), :]
            new_states.append(states[g] * gl + iv[g][c:c + dh])
            ms = jnp.mean(out * out, axis=-1, keepdims=True)
            o = out * lax.rsqrt(ms + NORM_EPS) * nw_ref[...]
            o_ref[pl.ds(r0, c), g * dh:(g + 1) * dh] = (
                o * _silu(z_ref[pl.ds(r0, c), g * dh:(g + 1) * dh])).astype(o_ref.dtype)
        return tuple(new_states)

    states = lax.fori_loop(0, nb, step, tuple(st_ref[g] for g in range(hps)))
    for g in range(hps):
        st_ref[g] = states[g]


def _gdn_mixer(qkv, z, tail, conv_w, gate_p, norm_w, bsz, s, heads):
    hps = math.gcd(GDN_HEADS_PER_STEP, heads)
    tb = min(GDN_TIME_BLOCK, s)
    nt = s // tb
    hblocks = heads // hps
    width = hps * HEAD_DIM
    nbat = hps * (tb // GDN_CHUNK)
    col = lambda off: pl.BlockSpec((tb, width), lambda b, h, t: (b * nt + t, off + h))
    return pl.pallas_call(
        functools.partial(_gdn_kernel, tb=tb, heads=heads, hps=hps),
        grid=(bsz, hblocks, nt),
        in_specs=[col(0), col(hblocks), col(2 * hblocks),
                  pl.BlockSpec((tb, width), lambda b, h, t: (b * nt + t, h)),
                  pl.BlockSpec((tb, LANES), lambda b, h, t: (b * nt + t, 0)),
                  pl.BlockSpec((3, GDN_CONV, width), lambda b, h, t: (0, 0, h)),
                  pl.BlockSpec((2, LANES), lambda b, h, t: (0, 0)),
                  pl.BlockSpec((1, HEAD_DIM), lambda b, h, t: (0, 0))],
        out_specs=pl.BlockSpec((tb, width), lambda b, h, t: (b * nt + t, h)),
        out_shape=jax.ShapeDtypeStruct((bsz * s, heads * HEAD_DIM), BF16),
        scratch_shapes=[pltpu.VMEM((tb + SUBLANES, width), F32)] * 3
                       + [pltpu.VMEM((hps, HEAD_DIM, HEAD_DIM), F32),
                          pltpu.VMEM((nbat * GDN_CHUNK, HEAD_DIM), F32),
                          pltpu.VMEM((nbat * 2 * GDN_CHUNK, HEAD_DIM), BF16),
                          pltpu.VMEM((nbat * (GDN_CHUNK + HEAD_DIM), GDN_CHUNK), BF16),
                          pltpu.VMEM((nbat * SUBLANES, HEAD_DIM), F32)],
        compiler_params=_cparams("parallel", "parallel", "arbitrary"),
        name="gdn_mixer",
    )(qkv, qkv, qkv, z, tail, conv_w, gate_p, norm_w)


def _gdn_layer(hn, w_in, conv_w, a_log, dt_bias, norm_w, w_out, h, bsz, s):
    width = w_out.shape[0]
    heads = width // HEAD_DIM
    w_qkv = w_in[:, :3 * width].astype(BF16)
    w_z = w_in[:, 3 * width:4 * width].astype(BF16)
    w_t = jnp.pad(w_in[:, 4 * width:], ((0, 0), (0, LANES - 2 * heads))).astype(BF16)
    qkv = _matmul(hn, w_qkv, out_dtype=F32)
    z = _matmul(hn, w_z, out_dtype=F32)
    tail = _matmul(hn, w_t, out_dtype=F32)
    cw = conv_w.astype(F32).reshape(GDN_CONV, 3, width).transpose(1, 0, 2)
    lane_pad = lambda x: jnp.pad(x.astype(F32), (heads, LANES - 2 * heads))
    gate_p = jnp.stack([lane_pad(a_log), lane_pad(dt_bias)])
    gated = _gdn_mixer(qkv, z, tail, cw, gate_p, norm_w.astype(F32).reshape(1, HEAD_DIM),
                       bsz, s, heads)
    return _matmul(gated, w_out.astype(BF16), out_dtype=F32, mode="residual", extras=(h,))


def _ssm_prep_kernel(lam_ref, bf_ref, cf_ref, bd_ref, wz_ref, wyt_ref, laml_ref):
    ns = SSM_PACK * SSM_STATE
    lam_re, lam_im, step_log = lam_ref[0, 0:1, :], lam_ref[0, 1:2, :], lam_ref[0, 2:3, :]
    step = jnp.exp(step_log)
    mag = jnp.exp(lam_re * step)
    lb_re, lb_im = mag * jnp.cos(lam_im * step), mag * jnp.sin(lam_im * step)
    den = lam_re * lam_re + lam_im * lam_im
    num_re = lb_re - 1.0
    zoh_re = (num_re * lam_re + lb_im * lam_im) / den
    zoh_im = (lb_im * lam_re - num_re * lam_im) / den
    b_re, b_im = bf_ref[0, 0], bf_ref[0, 1]
    bb_re = zoh_re * b_re - zoh_im * b_im
    bb_im = zoh_re * b_im + zoh_im * b_re
    c_re, c_im = cf_ref[0, 0], cf_ref[0, 1]
    cc = jnp.concatenate([c_re, -c_im], axis=1)

    pw_re, pw_im = jnp.ones_like(lb_re), jnp.zeros_like(lb_re)
    for d in range(SSM_L + 1):
        if d < SSM_L:
            a_re = pw_re * bb_re - pw_im * bb_im
            a_im = pw_re * bb_im + pw_im * bb_re
            a_d = jnp.concatenate([a_re, a_im], axis=1)
            bd_ref[0, d] = lax.dot_general(a_d, cc, (((1,), (1,)), ((), ())), precision=HIGHEST,
                                           preferred_element_type=F32).astype(bd_ref.dtype)
            tin = SSM_L - 1 - d
            wz_ref[0, tin * LANES:(tin + 1) * LANES, :] = a_d.astype(wz_ref.dtype)
        if d >= 1:
            y_re = pw_re * c_re - pw_im * c_im
            y_im = pw_re * c_im + pw_im * c_re
            wyt_ref[0, (d - 1) * LANES:d * LANES, :] = jnp.concatenate(
                [y_re, -y_im], axis=1).astype(wyt_ref.dtype)
        if d == SSM_L:
            laml_ref[0] = jnp.concatenate([pw_re, pw_im], axis=1)
        pw_re, pw_im = pw_re * lb_re - pw_im * lb_im, pw_re * lb_im + pw_im * lb_re


def _ssm_prep(lam, bfull, cfull):
    nsg = lam.shape[0]
    ns = SSM_PACK * SSM_STATE
    rows = SSM_L * LANES
    return pl.pallas_call(
        _ssm_prep_kernel,
        grid=(nsg,),
        in_specs=[pl.BlockSpec((1, 3, ns), lambda g: (g, 0, 0)),
                  pl.BlockSpec((1, 2, LANES, ns), lambda g: (g, 0, 0, 0)),
                  pl.BlockSpec((1, 2, LANES, ns), lambda g: (g, 0, 0, 0))],
        out_specs=[pl.BlockSpec((1, SSM_L, LANES, LANES), lambda g: (g, 0, 0, 0)),
                   pl.BlockSpec((1, rows, 2 * ns), lambda g: (g, 0, 0)),
                   pl.BlockSpec((1, rows, 2 * ns), lambda g: (g, 0, 0)),
                   pl.BlockSpec((1, 1, 2 * ns), lambda g: (g, 0, 0))],
        out_shape=[jax.ShapeDtypeStruct((nsg, SSM_L, LANES, LANES), BF16),
                   jax.ShapeDtypeStruct((nsg, rows, 2 * ns), BF16),
                   jax.ShapeDtypeStruct((nsg, rows, 2 * ns), BF16),
                   jax.ShapeDtypeStruct((nsg, 1, 2 * ns), F32)],
        compiler_params=_cparams("parallel"),
        name="ssm_prep",
    )(lam, bfull, cfull)


def _ssm_scan_kernel(u_ref, bd_ref, wz_ref, wyt_ref, laml_ref, d_ref, o_ref,
                     x_ref, z_ref, xp_ref, st_ref, *, tc):
    ns = SSM_PACK * SSM_STATE

    @pl.when(pl.program_id(2) == 0)
    def _():
        st_ref[...] = jnp.zeros_like(st_ref)

    for tau in range(SSM_L):
        x_ref[:, tau * LANES:(tau + 1) * LANES] = u_ref[pl.ds(tau, tc, stride=SSM_L), :].astype(BF16)

    z_ref[...] = jnp.dot(x_ref[...], wz_ref[0], preferred_element_type=F32)

    a_re, a_im = laml_ref[0, :, 0:ns], laml_ref[0, :, ns:2 * ns]

    def row(r, carry):
        s_re, s_im = carry
        xp_ref[pl.ds(r, 1), :] = jnp.concatenate([s_re, s_im], axis=1)
        zr = z_ref[pl.ds(r, 1), :]
        n_re = a_re * s_re - a_im * s_im + zr[:, 0:ns]
        n_im = a_re * s_im + a_im * s_re + zr[:, ns:2 * ns]
        return n_re, n_im

    s_re, s_im = lax.fori_loop(0, tc, row, (st_ref[:, 0:ns], st_ref[:, ns:2 * ns]))
    st_ref[...] = jnp.concatenate([s_re, s_im], axis=1)

    xprev = xp_ref[...].astype(BF16)
    for tau in range(SSM_L):
        y = lax.dot_general(xprev, wyt_ref[0, tau * LANES:(tau + 1) * LANES, :],
                            (((1,), (1,)), ((), ())), preferred_element_type=F32)
        for tin in range(tau + 1):
            y = y + jnp.dot(x_ref[:, tin * LANES:(tin + 1) * LANES], bd_ref[0, tau - tin],
                            preferred_element_type=F32)
        u_tau = u_ref[pl.ds(tau, tc, stride=SSM_L), :]
        y = y + d_ref[...] * u_tau
        y = 0.5 * y * (1.0 + jnp.tanh(math.sqrt(2.0 / math.pi) * (y + 0.044715 * (y * y * y))))
        o_ref[pl.ds(tau, tc, stride=SSM_L), :] = y


def _ssm_scan(u, bd, wz, wyt, laml, d_skip, bsz, s):
    nsg = u.shape[1] // LANES
    ns = SSM_PACK * SSM_STATE
    tb = min(4096, s)
    nt = s // tb
    tc = tb // SSM_L
    rows = SSM_L * LANES
    return pl.pallas_call(
        functools.partial(_ssm_scan_kernel, tc=tc),
        grid=(nsg, bsz, nt),
        in_specs=[pl.BlockSpec((tb, LANES), lambda g, b, t: (b * nt + t, g)),
                  pl.BlockSpec((1, SSM_L, LANES, LANES), lambda g, b, t: (g, 0, 0, 0)),
                  pl.BlockSpec((1, rows, 2 * ns), lambda g, b, t: (g, 0, 0)),
                  pl.BlockSpec((1, rows, 2 * ns), lambda g, b, t: (g, 0, 0)),
                  pl.BlockSpec((1, 1, 2 * ns), lambda g, b, t: (g, 0, 0)),
                  pl.BlockSpec((1, LANES), lambda g, b, t: (0, g))],
        out_specs=pl.BlockSpec((tb, LANES), lambda g, b, t: (b * nt + t, g)),
        out_shape=jax.ShapeDtypeStruct(u.shape, F32),
        scratch_shapes=[pltpu.VMEM((tc, rows), BF16), pltpu.VMEM((tc, 2 * ns), F32),
                        pltpu.VMEM((tc, 2 * ns), F32), pltpu.VMEM((1, 2 * ns), F32)],
        compiler_params=_cparams("parallel", "parallel", "arbitrary"),
        name="ssm_scan",
    )(u, bd, wz, wyt, laml, d_skip)


def _ssm_layer(hn, w_in, lam_re, lam_im, b_re, b_im, c_re, c_im, log_step, d_skip,
               w_glu, b_glu, w_out, h, bsz, s):
    e = w_out.shape[0]
    groups, nstate = lam_re.shape
    assert nstate == SSM_STATE and e == groups * SSM_GROUP and groups % SSM_PACK == 0
    nsg = groups // SSM_PACK
    ns = SSM_PACK * SSM_STATE
    lam = jnp.stack([lam_re.astype(F32).reshape(nsg, ns), lam_im.astype(F32).reshape(nsg, ns),
                     jnp.repeat(log_step.astype(F32), SSM_STATE).reshape(nsg, ns)], axis=1)
    eye = jnp.eye(SSM_PACK, dtype=F32)

    def expand_b(b):
        return jnp.einsum("sgpm,gh->sgmhp", b.astype(F32).reshape(nsg, SSM_PACK, SSM_STATE, SSM_GROUP),
                          eye).reshape(nsg, LANES, ns)

    def expand_c(c):
        return jnp.einsum("sgnp,gh->sgnhp", c.astype(F32).reshape(nsg, SSM_PACK, SSM_GROUP, SSM_STATE),
                          eye).reshape(nsg, LANES, ns)

    bfull = jnp.stack([expand_b(b_re), expand_b(b_im)], axis=1)
    cfull = jnp.stack([expand_c(c_re), expand_c(c_im)], axis=1)
    bd, wz, wyt, laml = _ssm_prep(lam, bfull, cfull)

    u = _matmul(hn, w_in[:, :e].astype(BF16), out_dtype=F32)
    z = _matmul(hn, w_in[:, e:].astype(BF16), out_dtype=F32)
    y = _ssm_scan(u, bd, wz, wyt, laml, d_skip.astype(F32).reshape(1, e), bsz, s)
    y2 = _matmul(y.astype(BF16), w_glu.astype(BF16), out_dtype=BF16, mode="glu",
                 extras=(y, z, b_glu.astype(F32).reshape(1, e)))
    return _matmul(y2, w_out.astype(BF16), out_dtype=F32, mode="residual", extras=(h,))


def kernel(x, p, norm_mix, fox_w_in, fox_b_f, fox_w_out, gdn_w_in, gdn_conv, gdn_a_log, gdn_dt_bias, gdn_norm, gdn_w_out, ssm_w_in, ssm_lam_re, ssm_lam_im, ssm_b_re, ssm_b_im, ssm_c_re, ssm_c_im, ssm_log_step, ssm_d, ssm_w_glu, ssm_b_glu, ssm_w_out, norm_ple, ple_w_proj, ple_w_gate, final_norm):
    bsz, s, d = x.shape
    depth = p.shape[0]
    h = x.reshape(bsz * s, d).astype(F32)
    for i in range(depth):
        kind, j = i % 3, i // 3
        hn = _rmsnorm(h, norm_mix[i], BF16)
        if kind == 0:
            h = _fox_layer(hn, fox_w_in[j], fox_b_f[j], fox_w_out[j], h, bsz, s)
        elif kind == 1:
            h = _gdn_layer(hn, gdn_w_in[j], gdn_conv[j], gdn_a_log[j], gdn_dt_bias[j],
                           gdn_norm[j], gdn_w_out[j], h, bsz, s)
        else:
            h = _ssm_layer(hn, ssm_w_in[j], ssm_lam_re[j], ssm_lam_im[j], ssm_b_re[j], ssm_b_im[j],
                           ssm_c_re[j], ssm_c_im[j], ssm_log_step[j], ssm_d[j], ssm_w_glu[j],
                           ssm_b_glu[j], ssm_w_out[j], h, bsz, s)
        hg = _rmsnorm(h, norm_ple[i], BF16)
        h = _matmul(hg, ple_w_gate[i].astype(BF16), out_dtype=F32, mode="ple",
                    extras=(h, p[i].reshape(bsz * s, -1).astype(BF16), ple_w_proj[i].astype(BF16)))
    return _rmsnorm(h, final_norm, F32).reshape(bsz, s, d)
```

```python
import functools
import math

import jax
import jax.numpy as jnp
from jax import lax
from jax.experimental import pallas as pl
from jax.experimental.pallas import tpu as pltpu

F32 = jnp.float32
BF16 = jnp.bfloat16

LANES = 128
SUBLANES = 8
VMEM_LIMIT_BYTES = 56 * 1024 * 1024

A_PREP_ROWS = 32
NORM_EPS = 1e-6
LOG2E = 1.4426950408889634
HEAD_DIM = 128
FOX_BLOCK = 1024
GDN_CHUNK = 64
GDN_CONV = 4
GDN_HEADS_PER_STEP = 4
GDN_TIME_BLOCK = 256
SSM_GROUP = 16
SSM_STATE = 64
SSM_PACK = LANES // SSM_GROUP
SSM_L = 16
NEG_BIG = -1e30

HIGHEST = lax.Precision.HIGHEST


def _cparams(*sem):
    return pltpu.CompilerParams(dimension_semantics=sem, vmem_limit_bytes=VMEM_LIMIT_BYTES)


def _sigmoid(x):
    return 1.0 / (1.0 + jnp.exp(-x))


def _silu(x):
    return x * _sigmoid(x)


def _softplus(x):
    return jnp.maximum(x, 0.0) + jnp.log(1.0 + jnp.exp(-jnp.abs(x)))


def _rmsnorm_kernel(x_ref, g_ref, o_ref):
    x = x_ref[...]
    ms = jnp.mean(x * x, axis=-1, keepdims=True)
    o_ref[...] = (x * lax.rsqrt(ms + NORM_EPS) * g_ref[...]).astype(o_ref.dtype)


def _rmsnorm(x, g, out_dtype):
    m, d = x.shape
    tm = min(256, m)
    return pl.pallas_call(
        _rmsnorm_kernel,
        grid=(m // tm,),
        in_specs=[pl.BlockSpec((tm, d), lambda i: (i, 0)),
                  pl.BlockSpec((1, d), lambda i: (0, 0))],
        out_specs=pl.BlockSpec((tm, d), lambda i: (i, 0)),
        out_shape=jax.ShapeDtypeStruct((m, d), out_dtype),
        compiler_params=_cparams("parallel"),
        name="rmsnorm",
    )(x, g.reshape(1, d).astype(F32))


def _mm_kernel(*refs, mode, a_prep, emit_a):
    a_ref = refs[0]
    n_lead = 2 if a_prep == "norm" else 1
    w_ref = refs[n_lead]
    if a_prep is None:
        extra, o_ref = refs[n_lead + 1:-1], refs[-1]
        a = a_ref[...]
    else:
        extra, o_ref, a_scr = refs[n_lead + 1:-2], refs[-2], refs[-1]

        @pl.when(pl.program_id(1) == 0)
        def _():
            def prep(r, carry):
                rows = pl.ds(pl.multiple_of(r * A_PREP_ROWS, A_PREP_ROWS), A_PREP_ROWS)
                x = a_ref[rows, :]
                if a_prep == "norm":
                    ms = jnp.mean(x * x, axis=-1, keepdims=True)
                    x = x * lax.rsqrt(ms + NORM_EPS) * refs[1][...]
                a_scr[rows, :] = x.astype(BF16)
                return carry

            lax.fori_loop(0, a_ref.shape[0] // A_PREP_ROWS, prep, 0)

        a = a_scr[...]

    acc = jnp.dot(a, w_ref[...], preferred_element_type=F32)
    if mode == "plain":
        out = acc
    elif mode == "colscale":
        out = acc * extra[0][...]
    elif mode == "residual":
        out = extra[0][...] + acc
    elif mode == "ple":
        res_ref, p_ref, wp_ref = extra
        emb = jnp.dot(p_ref[...], wp_ref[...], preferred_element_type=F32)
        out = res_ref[...] + _sigmoid(acc) * emb
    elif mode == "glu":
        y_ref, z_ref, b_ref = extra
        out = y_ref[...] * _sigmoid(acc + b_ref[...]) * _silu(z_ref[...])
    else:
        raise ValueError(mode)
    o_ref[...] = out.astype(o_ref.dtype)


def _matmul(a, w, *, out_dtype, mode="plain", extras=(), norm_g=None, cast_a=False, emit_a=False,
            z_col_off=0, tm=512, tn=1024):
    m, kdim = a.shape
    n = w.shape[1]
    tm, tn = min(tm, m), min(tn, n)
    assert m % tm == 0 and n % tn == 0 and z_col_off % tn == 0
    a_prep = "norm" if norm_g is not None else ("cast" if cast_a else None)
    operands = [a]
    in_specs = [pl.BlockSpec((tm, kdim), lambda i, j: (i, 0))]
    if a_prep == "norm":
        operands.append(norm_g.astype(F32).reshape(1, kdim))
        in_specs.append(pl.BlockSpec((1, kdim), lambda i, j: (0, 0)))
    operands.append(w)
    in_specs.append(pl.BlockSpec((kdim, tn), lambda i, j: (0, j)))
    tile = pl.BlockSpec((tm, tn), lambda i, j: (i, j))
    row = pl.BlockSpec((1, tn), lambda i, j: (0, j))
    if mode == "colscale":
        in_specs += [row]
    elif mode == "residual":
        in_specs += [tile]
    elif mode == "ple":
        pd = extras[1].shape[1]
        in_specs += [tile, pl.BlockSpec((tm, pd), lambda i, j: (i, 0)),
                     pl.BlockSpec((pd, tn), lambda i, j: (0, j))]
    elif mode == "glu":
        in_specs += [tile, pl.BlockSpec((tm, tn), lambda i, j: (i, j + z_col_off // tn)), row]
    assert a_prep is not None or not emit_a
    out_specs, out_shape, scratch = tile, jax.ShapeDtypeStruct((m, n), out_dtype), []
    if emit_a:
        out_specs = [tile, pl.BlockSpec((tm, kdim), lambda i, j: (i, 0))]
        out_shape = [out_shape, jax.ShapeDtypeStruct((m, kdim), BF16)]
    elif a_prep is not None:
        scratch = [pltpu.VMEM((tm, kdim), BF16)]
    return pl.pallas_call(
        functools.partial(_mm_kernel, mode=mode, a_prep=a_prep, emit_a=emit_a),
        grid=(m // tm, n // tn),
        in_specs=in_specs,
        out_specs=out_specs,
        out_shape=out_shape,
        scratch_shapes=scratch,
        compiler_params=_cparams("parallel", "arbitrary"),
        name="matmul_" + mode,
    )(*operands, *extras)


def _fox_cum_kernel(f_ref, b_ref, o_ref, carry_ref, *, ts):
    @pl.when(pl.program_id(1) == 0)
    def _():
        carry_ref[...] = jnp.zeros_like(carry_ref)

    x = f_ref[...] + b_ref[...]
    log_f = jnp.minimum(x, 0.0) - jnp.log(1.0 + jnp.exp(-jnp.abs(x)))
    r = lax.broadcasted_iota(jnp.int32, (ts, ts), 0)
    c = lax.broadcasted_iota(jnp.int32, (ts, ts), 1)
    lower = (c <= r).astype(F32)
    cum = jnp.dot(lower, log_f, precision=HIGHEST, preferred_element_type=F32) + carry_ref[...]
    carry_ref[...] = cum[ts - 1:ts, :]
    o_ref[0] = cum.T


def _fox_cum(f_logit, b_f, bsz, s):
    ts = min(512, s)
    return pl.pallas_call(
        functools.partial(_fox_cum_kernel, ts=ts),
        grid=(bsz, s // ts),
        in_specs=[pl.BlockSpec((ts, LANES), lambda b, t: (b * (s // ts) + t, 0)),
                  pl.BlockSpec((1, LANES), lambda b, t: (0, 0))],
        out_specs=pl.BlockSpec((1, LANES, ts), lambda b, t: (b, 0, t)),
        out_shape=jax.ShapeDtypeStruct((bsz, LANES, s), F32),
        scratch_shapes=[pltpu.VMEM((1, LANES), F32)],
        compiler_params=_cparams("parallel", "arbitrary"),
        name="fox_cum",
    )(f_logit, b_f)


def _fox_attn_kernel(q_ref, k_ref, v_ref, c_ref, z_ref, o_ref, sa_ref, sb_ref, *, tq):
    tk = tq // 2
    i = pl.program_id(2)
    q = q_ref[...]
    c0 = c_ref[0, 0, :, pl.ds(pl.multiple_of(i * tq, tq), LANES)][:, 0:1]

    def scores(qrows, j):
        start = j * tk if isinstance(j, int) else pl.multiple_of(j * tk, tk)
        s = lax.dot_general(qrows, k_ref[pl.ds(start, tk), :], (((1,), (1,)), ((), ())),
                            preferred_element_type=F32)
        return s + (c0 - c_ref[0, 0, :, pl.ds(start, tk)]) * LOG2E

    def update(s, j, m, acc, width=1):
        rows = width * tk
        start = j * tk if isinstance(j, int) else pl.multiple_of(j * tk, tk)
        vb = jnp.concatenate([v_ref[pl.ds(start, rows), :], jnp.ones((rows, HEAD_DIM), BF16)], axis=1)
        m_new = jnp.maximum(m, jnp.max(s, axis=-1, keepdims=True))
        p = jnp.exp2(s - m_new)
        acc = jnp.exp2(m - m_new) * acc + jnp.dot(p.astype(BF16), vb, preferred_element_type=F32)
        return m_new, acc

    sa_ref[...] = scores(q, 0)

    def pair(t, carry):
        m, acc = carry
        sb_ref[...] = scores(q, 2 * t + 1)
        m, acc = update(sa_ref[...], 2 * t, m, acc)
        sa_ref[...] = scores(q, 2 * t + 2)
        return update(sb_ref[...], 2 * t + 1, m, acc)

    init = (jnp.full((tq, 1), NEG_BIG, F32), jnp.zeros((tq, 2 * HEAD_DIM), F32))
    carry = lax.fori_loop(0, i // 2, lambda t2, cr: pair(2 * t2 + 1, pair(2 * t2, cr)), init)
    m, acc = lax.cond(i % 2 == 1, lambda cr: pair(i - 1, cr), lambda cr: cr, carry)

    r = lax.broadcasted_iota(jnp.int32, (tk, tk), 0)
    c = lax.broadcasted_iota(jnp.int32, (tk, tk), 1)
    s_bot = jnp.concatenate([sa_ref[tk:tq, :], jnp.where(c <= r, scores(q[tk:tq], 2 * i + 1), NEG_BIG)],
                            axis=1)
    _, acc_top = update(jnp.where(c <= r, sa_ref[0:tk, :], NEG_BIG), 2 * i, m[0:tk], acc[0:tk])
    _, acc_bot = update(s_bot, 2 * i, m[tk:tq], acc[tk:tq], width=2)

    def finish(a, rows):
        o = a[:, 0:HEAD_DIM] / a[:, HEAD_DIM:HEAD_DIM + 1]
        o_ref[rows, :] = (o * _silu(z_ref[rows, :])).astype(o_ref.dtype)

    finish(acc_top, slice(0, tk))
    finish(acc_bot, slice(tk, tq))


def _fox_attention(qkv, cum4, z, bsz, s, heads):
    tq = min(FOX_BLOCK, s)
    nq = s // tq
    return pl.pallas_call(
        functools.partial(_fox_attn_kernel, tq=tq),
        grid=(bsz, heads, nq),
        in_specs=[pl.BlockSpec((tq, HEAD_DIM), lambda b, h, i: (b * nq + i, h)),
                  pl.BlockSpec((s, HEAD_DIM), lambda b, h, i: (b, heads + h)),
                  pl.BlockSpec((s, HEAD_DIM), lambda b, h, i: (b, 2 * heads + h)),
                  pl.BlockSpec((1, 1, 1, s), lambda b, h, i: (b, h, 0, 0)),
                  pl.BlockSpec((tq, HEAD_DIM), lambda b, h, i: (b * nq + i, h))],
        out_specs=pl.BlockSpec((tq, HEAD_DIM), lambda b, h, i: (b * nq + i, h)),
        out_shape=jax.ShapeDtypeStruct((bsz * s, heads * HEAD_DIM), BF16),
        scratch_shapes=[pltpu.VMEM((tq, tq // 2), F32)] * 2,
        compiler_params=_cparams("parallel", "parallel", "arbitrary"),
        name="fox_attn",
    )(qkv, qkv, qkv, cum4, z)


def _fox_layer(h, g_norm, w_in, b_f, w_out, bsz, s):
    width = w_out.shape[0]
    heads = width // HEAD_DIM
    w_qkv = w_in[:, :3 * width].astype(BF16)
    w_z = w_in[:, 3 * width:4 * width].astype(BF16)
    w_f = jnp.pad(w_in[:, 4 * width:], ((0, 0), (0, LANES - heads))).astype(BF16)
    q_scale = jnp.concatenate([jnp.full((width,), HEAD_DIM ** -0.5 * LOG2E, F32),
                               jnp.ones((2 * width,), F32)]).reshape(1, 3 * width)
    qkv, hn = _matmul(h, w_qkv, out_dtype=BF16, mode="colscale", extras=(q_scale,), norm_g=g_norm,
                      emit_a=True)
    z = _matmul(hn, w_z, out_dtype=F32)
    f_logit = _matmul(hn, w_f, out_dtype=F32)
    b_row = jnp.pad(b_f.astype(F32), (0, LANES - heads)).reshape(1, LANES)
    cum = _fox_cum(f_logit, b_row, bsz, s)
    gated = _fox_attention(qkv, cum.reshape(bsz, LANES, 1, s), z, bsz, s, heads)
    return _matmul(gated, w_out.astype(BF16), out_dtype=F32, mode="residual", extras=(h,))


def _bdot(a, b):
    return jnp.einsum("bmk,bkn->bmn", a, b, preferred_element_type=F32)


def _split_bf16(x):
    hi = x.astype(BF16)
    return hi, (x - hi.astype(F32)).astype(BF16)


def _bdot3(a, b):
    a_hi, a_lo = _split_bf16(a)
    b_hi, b_lo = _split_bf16(b)
    return _bdot(a_hi, b_hi) + _bdot(a_hi, b_lo) + _bdot(a_lo, b_hi)


def _bdot_exact_lhs(a_bf16, b):
    b_hi = b.astype(BF16)
    b_mid, b_lo = _split_bf16(b - b_hi.astype(F32))
    return _bdot(a_bf16, b_hi) + _bdot(a_bf16, b_mid) + _bdot(a_bf16, b_lo)


def _gdn_kernel(q_ref, k_ref, v_ref, z_ref, t_ref, cw_ref, gp_ref, nw_ref, o_ref,
                xq_ref, xk_ref, xv_ref, st_ref, u_ref, wq_ref, ik_ref, gl_ref, *, tb, heads, hps):
    hb = pl.program_id(1)
    c, dh = GDN_CHUNK, HEAD_DIM
    nb = tb // c
    nbat = hps * nb
    pad = SUBLANES

    @pl.when(pl.program_id(2) == 0)
    def _():
        for r in (xq_ref, xk_ref, xv_ref):
            r[0:pad, :] = jnp.zeros((pad, hps * dh), F32)
        st_ref[...] = jnp.zeros_like(st_ref)

    def conv(x_ref, xs_ref, col):
        xs_ref[pad:pad + tb, :] = x_ref[...]
        y = xs_ref[pad - (GDN_CONV - 1):pad - (GDN_CONV - 1) + tb, :] * cw_ref[col, 0:1, :]
        for j in range(1, GDN_CONV):
            off = pad - (GDN_CONV - 1) + j
            y = y + xs_ref[off:off + tb, :] * cw_ref[col, j:j + 1, :]
        xs_ref[0:pad, :] = xs_ref[tb:tb + pad, :]
        return _silu(y)

    def by_chunk(x):
        w = x.shape[1] // hps
        return jnp.concatenate([x[:, g * w:(g + 1) * w].reshape(nb, c, w) for g in range(hps)], axis=0)

    def l2n(x):
        return x * lax.rsqrt(jnp.sum(x * x, axis=-1, keepdims=True) + NORM_EPS)

    q = l2n(by_chunk(conv(q_ref, xq_ref, 0))) * (dh ** -0.5)
    k = l2n(by_chunk(conv(k_ref, xk_ref, 1)))
    v = by_chunk(conv(v_ref, xv_ref, 2))

    t = t_ref[...]
    lane = lax.broadcasted_iota(jnp.int32, (tb, LANES), 1)
    beta_all = _sigmoid(t)
    g_all = -jnp.exp(gp_ref[0:1, :]) * _softplus(t + gp_ref[1:2, :])

    def pick(x, first):
        cols = [jnp.sum(jnp.where(lane == first + hb * hps + g, x, 0.0), axis=-1, keepdims=True)
                for g in range(hps)]
        return jnp.concatenate([col.reshape(nb, c, 1) for col in cols], axis=0)

    beta = pick(beta_all, 0)
    g = pick(g_all, heads)

    ri = lax.broadcasted_iota(jnp.int32, (c, c), 0)
    ci = lax.broadcasted_iota(jnp.int32, (c, c), 1)
    incl = ci <= ri
    strict = ci < ri
    eye = (ci == ri).astype(F32)
    tri = jnp.broadcast_to(incl.astype(BF16), (nbat, c, c))
    rl = lax.broadcasted_iota(jnp.int32, (c, LANES), 0)
    cl = lax.broadcasted_iota(jnp.int32, (c, LANES), 1)
    seg_mask = (rl > cl) | (cl == c)

    seg = _bdot_exact_lhs(tri, jnp.where(seg_mask, g, 0.0))
    gc = seg[:, :, c:c + 1]
    g_last = seg[:, c - 1:c, c:c + 1]
    decay = jnp.exp(jnp.where(incl, seg[:, :, 0:c], NEG_BIG))
    egc = jnp.exp(gc)
    k_beta = k * beta
    kq = jnp.einsum("bmd,bnd->bmn", jnp.concatenate([k_beta, q], axis=1).astype(BF16),
                    k.astype(BF16), preferred_element_type=F32)
    a = jnp.where(strict, kq[:, 0:c] * decay, 0.0)
    intra = kq[:, c:2 * c] * decay
    inv = eye - a
    pw = _bdot3(a, a)
    levels = int(math.log2(c)) - 1
    for lvl in range(levels):
        if lvl < levels - 1:
            both = _bdot3(jnp.concatenate([inv, pw], axis=1), pw)
            inv = inv + both[:, 0:c]
            pw = both[:, c:2 * c]
        else:
            inv = inv + _bdot3(inv, pw)
    uw = _bdot3(inv, jnp.concatenate([v * beta, k_beta * egc], axis=2))
    u_ref[...] = uw[:, :, 0:dh].reshape(nbat * c, dh)
    wq_ref[...] = jnp.concatenate([uw[:, :, dh:2 * dh], q * egc], axis=1).astype(BF16).reshape(
        nbat * 2 * c, dh)
    k_dec = k * jnp.exp(g_last - gc)
    for b in range(nbat):
        ik_ref[b * (c + dh):(b + 1) * (c + dh), :] = jnp.concatenate(
            [intra[b], k_dec[b].T], axis=0).astype(BF16)
    gl_ref[...] = jnp.broadcast_to(jnp.exp(g_last), (nbat, SUBLANES, dh)).reshape(nbat * SUBLANES, dh)

    def step(n, states):
        rows = lambda g, size: pl.ds(pl.multiple_of((g * nb + n) * size, size), size)
        ws = [jnp.dot(wq_ref[rows(g, 2 * c), :], states[g].astype(BF16), preferred_element_type=F32)
              for g in range(hps)]
        v_new = [u_ref[rows(g, c), :] - ws[g][0:c] for g in range(hps)]
        iv = [jnp.dot(ik_ref[pl.ds(pl.multiple_of((g * nb + n) * (c + dh), c), c + dh), :],
                      v_new[g].astype(BF16), preferred_element_type=F32) for g in range(hps)]
        new_states = []
        r0 = pl.multiple_of(n * c, c)
        for g in range(hps):
            out = ws[g][c:2 * c] + iv[g][0:c]
            gl = gl_ref[pl.ds(pl.multiple_of((g * nb + n) * SUBLANES, SUBLANES), 1), :]
            new_states.append(states[g] * gl + iv[g][c:c + dh])
            ms = jnp.mean(out * out, axis=-1, keepdims=True)
            o = out * lax.rsqrt(ms + NORM_EPS) * nw_ref[...]
            o_ref[pl.ds(r0, c), g * dh:(g + 1) * dh] = (
                o * _silu(z_ref[pl.ds(r0, c), g * dh:(g + 1) * dh])).astype(o_ref.dtype)
        return tuple(new_states)

    states = lax.fori_loop(0, nb, step, tuple(st_ref[g] for g in range(hps)))
    for g in range(hps):
        st_ref[g] = states[g]


def _gdn_mixer(qkvz, tail, conv_w, gate_p, norm_w, bsz, s, heads):
    hps = math.gcd(GDN_HEADS_PER_STEP, heads)
    tb = min(GDN_TIME_BLOCK, s)
    nt = s // tb
    hblocks = heads // hps
    width = hps * HEAD_DIM
    nbat = hps * (tb // GDN_CHUNK)
    col = lambda off: pl.BlockSpec((tb, width), lambda b, h, t: (b * nt + t, off + h))
    return pl.pallas_call(
        functools.partial(_gdn_kernel, tb=tb, heads=heads, hps=hps),
        grid=(bsz, hblocks, nt),
        in_specs=[col(0), col(hblocks), col(2 * hblocks), col(3 * hblocks),
                  pl.BlockSpec((tb, LANES), lambda b, h, t: (b * nt + t, 0)),
                  pl.BlockSpec((3, GDN_CONV, width), lambda b, h, t: (0, 0, h)),
                  pl.BlockSpec((2, LANES), lambda b, h, t: (0, 0)),
                  pl.BlockSpec((1, HEAD_DIM), lambda b, h, t: (0, 0))],
        out_specs=pl.BlockSpec((tb, width), lambda b, h, t: (b * nt + t, h)),
        out_shape=jax.ShapeDtypeStruct((bsz * s, heads * HEAD_DIM), BF16),
        scratch_shapes=[pltpu.VMEM((tb + SUBLANES, width), F32)] * 3
                       + [pltpu.VMEM((hps, HEAD_DIM, HEAD_DIM), F32),
                          pltpu.VMEM((nbat * GDN_CHUNK, HEAD_DIM), F32),
                          pltpu.VMEM((nbat * 2 * GDN_CHUNK, HEAD_DIM), BF16),
                          pltpu.VMEM((nbat * (GDN_CHUNK + HEAD_DIM), GDN_CHUNK), BF16),
                          pltpu.VMEM((nbat * SUBLANES, HEAD_DIM), F32)],
        compiler_params=_cparams("parallel", "parallel", "arbitrary"),
        name="gdn_mixer",
    )(qkvz, qkvz, qkvz, qkvz, tail, conv_w, gate_p, norm_w)


def _gdn_layer(h, g_norm, w_in, conv_w, a_log, dt_bias, norm_w, w_out, bsz, s):
    width = w_out.shape[0]
    heads = width // HEAD_DIM
    w_qkvz = w_in[:, :4 * width].astype(BF16)
    w_t = jnp.pad(w_in[:, 4 * width:], ((0, 0), (0, LANES - 2 * heads))).astype(BF16)
    qkvz, hn = _matmul(h, w_qkvz, out_dtype=F32, norm_g=g_norm, emit_a=True)
    tail = _matmul(hn, w_t, out_dtype=F32)
    cw = conv_w.astype(F32).reshape(GDN_CONV, 3, width).transpose(1, 0, 2)
    lane_pad = lambda x: jnp.pad(x.astype(F32), (heads, LANES - 2 * heads))
    gate_p = jnp.stack([lane_pad(a_log), lane_pad(dt_bias)])
    gated = _gdn_mixer(qkvz, tail, cw, gate_p, norm_w.astype(F32).reshape(1, HEAD_DIM),
                       bsz, s, heads)
    return _matmul(gated, w_out.astype(BF16), out_dtype=F32, mode="residual", extras=(h,))


def _ssm_prep_kernel(lam_ref, bf_ref, cf_ref, bd_ref, wz_ref, wyt_ref, laml_ref):
    ns = SSM_PACK * SSM_STATE
    lam_re, lam_im, step_log = lam_ref[0, 0:1, :], lam_ref[0, 1:2, :], lam_ref[0, 2:3, :]
    step = jnp.exp(step_log)
    mag = jnp.exp(lam_re * step)
    lb_re, lb_im = mag * jnp.cos(lam_im * step), mag * jnp.sin(lam_im * step)
    den = lam_re * lam_re + lam_im * lam_im
    num_re = lb_re - 1.0
    zoh_re = (num_re * lam_re + lb_im * lam_im) / den
    zoh_im = (lb_im * lam_re - num_re * lam_im) / den
    b_re, b_im = bf_ref[0, 0], bf_ref[0, 1]
    bb_re = zoh_re * b_re - zoh_im * b_im
    bb_im = zoh_re * b_im + zoh_im * b_re
    c_re, c_im = cf_ref[0, 0], cf_ref[0, 1]
    cc = jnp.concatenate([c_re, -c_im], axis=1)

    bd_ref[0, 0, LANES:2 * LANES, 0:LANES] = jnp.zeros((LANES, LANES), bd_ref.dtype)
    pw_re, pw_im = jnp.ones_like(lb_re), jnp.zeros_like(lb_re)
    for d in range(SSM_L + 1):
        if d < SSM_L:
            a_re = pw_re * bb_re - pw_im * bb_im
            a_im = pw_re * bb_im + pw_im * bb_re
            a_d = jnp.concatenate([a_re, a_im], axis=1)
            blk = lax.dot_general(a_d, cc, (((1,), (1,)), ((), ())), precision=HIGHEST,
                                  preferred_element_type=F32).astype(bd_ref.dtype)
            d2 = d // 2
            if d % 2 == 0:
                bd_ref[0, d2, 0:LANES, 0:LANES] = blk
                bd_ref[0, d2, LANES:2 * LANES, LANES:2 * LANES] = blk
            else:
                bd_ref[0, d2, 0:LANES, LANES:2 * LANES] = blk
                if d2 + 1 < SSM_L // 2:
                    bd_ref[0, d2 + 1, LANES:2 * LANES, 0:LANES] = blk
            tin = SSM_L - 1 - d
            wz_ref[0, tin * LANES:(tin + 1) * LANES, :] = a_d.astype(wz_ref.dtype)
        if d >= 1:
            y_re = pw_re * c_re - pw_im * c_im
            y_im = pw_re * c_im + pw_im * c_re
            wyt_ref[0, (d - 1) * LANES:d * LANES, :] = jnp.concatenate(
                [y_re, -y_im], axis=1).astype(wyt_ref.dtype)
        if d == SSM_L:
            laml_ref[0] = jnp.concatenate([pw_re, pw_im], axis=1)
        pw_re, pw_im = pw_re * lb_re - pw_im * lb_im, pw_re * lb_im + pw_im * lb_re


def _ssm_prep(lam, bfull, cfull):
    nsg = lam.shape[0]
    ns = SSM_PACK * SSM_STATE
    rows = SSM_L * LANES
    return pl.pallas_call(
        _ssm_prep_kernel,
        grid=(nsg,),
        in_specs=[pl.BlockSpec((1, 3, ns), lambda g: (g, 0, 0)),
                  pl.BlockSpec((1, 2, LANES, ns), lambda g: (g, 0, 0, 0)),
                  pl.BlockSpec((1, 2, LANES, ns), lambda g: (g, 0, 0, 0))],
        out_specs=[pl.BlockSpec((1, SSM_L // 2, 2 * LANES, 2 * LANES), lambda g: (g, 0, 0, 0)),
                   pl.BlockSpec((1, rows, 2 * ns), lambda g: (g, 0, 0)),
                   pl.BlockSpec((1, rows, 2 * ns), lambda g: (g, 0, 0)),
                   pl.BlockSpec((1, 1, 2 * ns), lambda g: (g, 0, 0))],
        out_shape=[jax.ShapeDtypeStruct((nsg, SSM_L // 2, 2 * LANES, 2 * LANES), BF16),
                   jax.ShapeDtypeStruct((nsg, rows, 2 * ns), BF16),
                   jax.ShapeDtypeStruct((nsg, rows, 2 * ns), BF16),
                   jax.ShapeDtypeStruct((nsg, 1, 2 * ns), F32)],
        compiler_params=_cparams("parallel"),
        name="ssm_prep",
    )(lam, bfull, cfull)


def _ssm_scan_kernel(u_ref, bd_ref, wz_ref, wyt_ref, laml_ref, d_ref, o_ref,
                     x_ref, z_ref, xp_ref, st_ref, *, tc):
    ns = SSM_PACK * SSM_STATE

    @pl.when(pl.program_id(2) == 0)
    def _():
        st_ref[...] = jnp.zeros_like(st_ref)

    for tau in range(SSM_L):
        x_ref[:, tau * LANES:(tau + 1) * LANES] = u_ref[pl.ds(tau, tc, stride=SSM_L), :].astype(BF16)

    z_ref[...] = jnp.dot(x_ref[...], wz_ref[0], preferred_element_type=F32)

    a_re, a_im = laml_ref[0, :, 0:ns], laml_ref[0, :, ns:2 * ns]

    def row(r, carry):
        s_re, s_im = carry
        xp_ref[pl.ds(r, 1), :] = jnp.concatenate([s_re, s_im], axis=1)
        zr = z_ref[pl.ds(r, 1), :]
        n_re = a_re * s_re - a_im * s_im + zr[:, 0:ns]
        n_im = a_re * s_im + a_im * s_re + zr[:, ns:2 * ns]
        return n_re, n_im

    s_re, s_im = lax.fori_loop(0, tc, row, (st_ref[:, 0:ns], st_ref[:, ns:2 * ns]), unroll=4)
    st_ref[...] = jnp.concatenate([s_re, s_im], axis=1)

    xprev = xp_ref[...].astype(BF16)
    pair = 2 * LANES
    for t2 in range(SSM_L // 2):
        y2 = lax.dot_general(xprev, wyt_ref[0, t2 * pair:(t2 + 1) * pair, :],
                             (((1,), (1,)), ((), ())), preferred_element_type=F32)
        for tin in range(t2 + 1):
            y2 = y2 + jnp.dot(x_ref[:, tin * pair:(tin + 1) * pair], bd_ref[0, t2 - tin],
                              preferred_element_type=F32)
        for half in range(2):
            tau = 2 * t2 + half
            y = y2[:, half * LANES:(half + 1) * LANES]
            y = y + d_ref[...] * u_ref[pl.ds(tau, tc, stride=SSM_L), :]
            y = 0.5 * y * (1.0 + jnp.tanh(math.sqrt(2.0 / math.pi) * (y + 0.044715 * (y * y * y))))
            o_ref[pl.ds(tau, tc, stride=SSM_L), :] = y


def _ssm_scan(uz, bd, wz, wyt, laml, d_skip, bsz, s):
    e = d_skip.shape[1]
    nsg = e // LANES
    ns = SSM_PACK * SSM_STATE
    tb = min(4096, s)
    nt = s // tb
    tc = tb // SSM_L
    rows = SSM_L * LANES
    return pl.pallas_call(
        functools.partial(_ssm_scan_kernel, tc=tc),
        grid=(nsg, bsz, nt),
        in_specs=[pl.BlockSpec((tb, LANES), lambda g, b, t: (b * nt + t, g)),
                  pl.BlockSpec((1, SSM_L // 2, 2 * LANES, 2 * LANES), lambda g, b, t: (g, 0, 0, 0)),
                  pl.BlockSpec((1, rows, 2 * ns), lambda g, b, t: (g, 0, 0)),
                  pl.BlockSpec((1, rows, 2 * ns), lambda g, b, t: (g, 0, 0)),
                  pl.BlockSpec((1, 1, 2 * ns), lambda g, b, t: (g, 0, 0)),
                  pl.BlockSpec((1, LANES), lambda g, b, t: (0, g))],
        out_specs=pl.BlockSpec((tb, LANES), lambda g, b, t: (b * nt + t, g)),
        out_shape=jax.ShapeDtypeStruct((uz.shape[0], e), F32),
        scratch_shapes=[pltpu.VMEM((tc, rows), BF16), pltpu.VMEM((tc, 2 * ns), F32),
                        pltpu.VMEM((tc, 2 * ns), F32), pltpu.VMEM((1, 2 * ns), F32)],
        compiler_params=_cparams("parallel", "parallel", "arbitrary"),
        name="ssm_scan",
    )(uz, bd, wz, wyt, laml, d_skip)


def _ssm_layer(h, g_norm, w_in, lam_re, lam_im, b_re, b_im, c_re, c_im, log_step, d_skip,
               w_glu, b_glu, w_out, bsz, s):
    e = w_out.shape[0]
    groups, nstate = lam_re.shape
    assert nstate == SSM_STATE and e == groups * SSM_GROUP and groups % SSM_PACK == 0
    nsg = groups // SSM_PACK
    ns = SSM_PACK * SSM_STATE
    lam = jnp.stack([lam_re.astype(F32).reshape(nsg, ns), lam_im.astype(F32).reshape(nsg, ns),
                     jnp.repeat(log_step.astype(F32), SSM_STATE).reshape(nsg, ns)], axis=1)
    eye = jnp.eye(SSM_PACK, dtype=F32)

    def expand_b(b):
        return jnp.einsum("sgpm,gh->sgmhp", b.astype(F32).reshape(nsg, SSM_PACK, SSM_STATE, SSM_GROUP),
                          eye).reshape(nsg, LANES, ns)

    def expand_c(c):
        return jnp.einsum("sgnp,gh->sgnhp", c.astype(F32).reshape(nsg, SSM_PACK, SSM_GROUP, SSM_STATE),
                          eye).reshape(nsg, LANES, ns)

    bfull = jnp.stack([expand_b(b_re), expand_b(b_im)], axis=1)
    cfull = jnp.stack([expand_c(c_re), expand_c(c_im)], axis=1)
    bd, wz, wyt, laml = _ssm_prep(lam, bfull, cfull)

    uz = _matmul(h, w_in.astype(BF16), out_dtype=F32, norm_g=g_norm)
    y = _ssm_scan(uz, bd, wz, wyt, laml, d_skip.astype(F32).reshape(1, e), bsz, s)
    y2 = _matmul(y, w_glu.astype(BF16), out_dtype=BF16, mode="glu", cast_a=True, z_col_off=e,
                 extras=(y, uz, b_glu.astype(F32).reshape(1, e)))
    return _matmul(y2, w_out.astype(BF16), out_dtype=F32, mode="residual", extras=(h,))


def kernel(x, p, norm_mix, fox_w_in, fox_b_f, fox_w_out, gdn_w_in, gdn_conv, gdn_a_log, gdn_dt_bias, gdn_norm, gdn_w_out, ssm_w_in, ssm_lam_re, ssm_lam_im, ssm_b_re, ssm_b_im, ssm_c_re, ssm_c_im, ssm_log_step, ssm_d, ssm_w_glu, ssm_b_glu, ssm_w_out, norm_ple, ple_w_proj, ple_w_gate, final_norm):
    bsz, s, d = x.shape
    depth = p.shape[0]
    h = x.reshape(bsz * s, d).astype(F32)
    for i in range(depth):
        kind, j = i % 3, i // 3
        if kind == 0:
            h = _fox_layer(h, norm_mix[i], fox_w_in[j], fox_b_f[j], fox_w_out[j], bsz, s)
        elif kind == 1:
            h = _gdn_layer(h, norm_mix[i], gdn_w_in[j], gdn_conv[j], gdn_a_log[j], gdn_dt_bias[j],
                           gdn_norm[j], gdn_w_out[j], bsz, s)
        else:
            h = _ssm_layer(h, norm_mix[i], ssm_w_in[j], ssm_lam_re[j], ssm_lam_im[j], ssm_b_re[j],
                           ssm_b_im[j], ssm_c_re[j], ssm_c_im[j], ssm_log_step[j], ssm_d[j],
                           ssm_w_glu[j], ssm_b_glu[j], ssm_w_out[j], bsz, s)
        h = _matmul(h, ple_w_gate[i].astype(BF16), out_dtype=F32, mode="ple", norm_g=norm_ple[i],
                    extras=(h, p[i].reshape(bsz * s, -1).astype(BF16), ple_w_proj[i].astype(BF16)))
    return _rmsnorm(h, final_norm, F32).reshape(bsz, s, d)
```

```python
import functools
import math

import jax
import jax.numpy as jnp
from jax import lax
from jax.experimental import pallas as pl
from jax.experimental.pallas import tpu as pltpu

F32 = jnp.float32
BF16 = jnp.bfloat16

LANES = 128
SUBLANES = 8
VMEM_LIMIT_BYTES = 56 * 1024 * 1024

A_PREP_ROWS = 32
MM_TM = 512
MM_TN = 1024
NORM_EPS = 1e-6
LOG2E = 1.4426950408889634
HEAD_DIM = 128
FOX_BLOCK = 1024
GDN_CHUNK = 64
GDN_CONV = 4
GDN_HEADS_PER_STEP = 4
GDN_TIME_BLOCK = 256
SSM_GROUP = 16
SSM_STATE = 64
SSM_PACK = LANES // SSM_GROUP
SSM_L = 16
NEG_BIG = -1e30

HIGHEST = lax.Precision.HIGHEST


def _cparams(*sem):
    return pltpu.CompilerParams(dimension_semantics=sem, vmem_limit_bytes=VMEM_LIMIT_BYTES)


def _sigmoid(x):
    return 1.0 / (1.0 + jnp.exp(-x))


def _silu(x):
    return x * _sigmoid(x)


def _softplus(x):
    return jnp.maximum(x, 0.0) + jnp.log(1.0 + jnp.exp(-jnp.abs(x)))


def _rmsnorm_kernel(x_ref, g_ref, o_ref):
    x = x_ref[...]
    ms = jnp.mean(x * x, axis=-1, keepdims=True)
    o_ref[...] = (x * lax.rsqrt(ms + NORM_EPS) * g_ref[...]).astype(o_ref.dtype)


def _rmsnorm(x, g, out_dtype):
    m, d = x.shape
    tm = min(256, m)
    return pl.pallas_call(
        _rmsnorm_kernel,
        grid=(m // tm,),
        in_specs=[pl.BlockSpec((tm, d), lambda i: (i, 0)),
                  pl.BlockSpec((1, d), lambda i: (0, 0))],
        out_specs=pl.BlockSpec((tm, d), lambda i: (i, 0)),
        out_shape=jax.ShapeDtypeStruct((m, d), out_dtype),
        compiler_params=_cparams("parallel"),
        name="rmsnorm",
    )(x, g.reshape(1, d).astype(F32))


def _cast_kernel(x_ref, o_ref):
    o_ref[...] = x_ref[...].astype(o_ref.dtype)


def _cast_bf16(x, n=None):
    r, c = x.shape
    n = c if n is None else n
    tr, tc = min(512, r), min(2048, n)
    assert r % tr == 0 and n % tc == 0 and (tc % LANES == 0 or tc == c)
    return pl.pallas_call(
        _cast_kernel,
        grid=(r // tr, n // tc),
        in_specs=[pl.BlockSpec((tr, tc), lambda i, j: (i, j))],
        out_specs=pl.BlockSpec((tr, tc), lambda i, j: (i, j)),
        out_shape=jax.ShapeDtypeStruct((r, n), BF16),
        compiler_params=_cparams("parallel", "parallel"),
        name="cast_bf16",
    )(x)


_N_EXTRA = {"plain": 0, "colscale": 1, "residual": 1, "ple": 3, "glu": 3}


def _mm_kernel(*refs, mode, cast_a, has_ssq, emit_norm, inv_k):
    it = iter(refs)
    a_ref, w_ref = next(it), next(it)
    ssq_ref = next(it) if has_ssq else None
    extra = [next(it) for _ in range(_N_EXTRA[mode])]
    gain_ref = next(it) if emit_norm else None
    o_ref = next(it)
    hg_ref, ssq_out_ref = (next(it), next(it)) if emit_norm else (None, None)
    if cast_a:
        a_scr = next(it)

        @pl.when(pl.program_id(1) == 0)
        def _():
            def prep(r, carry):
                rows = pl.ds(pl.multiple_of(r * A_PREP_ROWS, A_PREP_ROWS), A_PREP_ROWS)
                a_scr[rows, :] = a_ref[rows, :].astype(BF16)
                return carry

            lax.fori_loop(0, a_ref.shape[0] // A_PREP_ROWS, prep, 0)

        a = a_scr[...]
    else:
        a = a_ref[...]

    acc = jnp.dot(a, w_ref[...], preferred_element_type=F32)
    if has_ssq:
        parts = ssq_ref[...]
        tot = parts[:, 0:LANES]
        for t in range(1, parts.shape[1] // LANES):
            tot = tot + parts[:, t * LANES:(t + 1) * LANES]
        acc = acc * lax.rsqrt(tot[:, 0:1] * inv_k + NORM_EPS)
    if mode == "plain":
        out = acc
    elif mode == "colscale":
        out = acc * extra[0][...]
    elif mode == "residual":
        out = extra[0][...] + acc
    elif mode == "ple":
        res_ref, p_ref, wp_ref = extra
        emb = jnp.dot(p_ref[...], wp_ref[...], preferred_element_type=F32)
        out = res_ref[...] + _sigmoid(acc) * emb
    elif mode == "glu":
        y_ref, z_ref, b_ref = extra
        out = y_ref[...] * _sigmoid(acc + b_ref[...]) * _silu(z_ref[...])
    else:
        raise ValueError(mode)
    o_ref[...] = out.astype(o_ref.dtype)
    if emit_norm:
        hg_ref[...] = (out * gain_ref[...]).astype(hg_ref.dtype)
        ssq_out_ref[...] = jnp.broadcast_to(jnp.sum(out * out, axis=-1, keepdims=True),
                                            ssq_out_ref.shape)


def _extra_spec(arr, kind, tm, tn, row_blk=0, col_off=0, rows=None):
    if kind == "tile":
        return pl.BlockSpec((tm, tn), lambda i, j: (i + row_blk, j + col_off // tn))
    if kind == "row":
        return pl.BlockSpec((1, tn), lambda i, j: (0, j))
    if kind == "rows":
        return pl.BlockSpec((tm, arr.shape[1]), lambda i, j: (i + row_blk, 0))
    if kind == "cols":
        return pl.BlockSpec((rows, tn), lambda i, j: (row_blk, j))
    raise ValueError(kind)


def _matmul(a, w, *, out_dtype, n=None, mode="plain", extras=(), cast_a=False, row_ssq=None,
            norm_gain=None, w_row_blk=0, w_col_off=0, tm=MM_TM, tn=MM_TN):
    m, kdim = a.shape
    n = w.shape[1] if n is None else n
    tm, tn = min(tm, m), min(tn, n)
    assert m % tm == 0 and n % tn == 0 and w_col_off % tn == 0 and len(extras) == _N_EXTRA[mode]
    operands = [a, w]
    in_specs = [pl.BlockSpec((tm, kdim), lambda i, j: (i, 0)),
                pl.BlockSpec((kdim, tn), lambda i, j: (w_row_blk, j + w_col_off // tn))]
    if row_ssq is not None:
        operands.append(row_ssq)
        in_specs.append(pl.BlockSpec((tm, row_ssq.shape[1]), lambda i, j: (i, 0)))
    for arr, kind, opts in extras:
        assert opts.get("col_off", 0) % tn == 0
        operands.append(arr)
        in_specs.append(_extra_spec(arr, kind, tm, tn, **opts))
    tile = pl.BlockSpec((tm, tn), lambda i, j: (i, j))
    out_specs, out_shape = tile, jax.ShapeDtypeStruct((m, n), out_dtype)
    if norm_gain is not None:
        operands.append(norm_gain.astype(F32).reshape(1, n))
        in_specs.append(pl.BlockSpec((1, tn), lambda i, j: (0, j)))
        out_specs = [tile, tile, pl.BlockSpec((tm, LANES), lambda i, j: (i, j))]
        out_shape = [out_shape, jax.ShapeDtypeStruct((m, n), BF16),
                     jax.ShapeDtypeStruct((m, n // tn * LANES), F32)]
    scratch = [pltpu.VMEM((tm, kdim), BF16)] if cast_a else []
    return pl.pallas_call(
        functools.partial(_mm_kernel, mode=mode, cast_a=cast_a, has_ssq=row_ssq is not None,
                          emit_norm=norm_gain is not None, inv_k=1.0 / kdim),
        grid=(m // tm, n // tn),
        in_specs=in_specs,
        out_specs=out_specs,
        out_shape=out_shape,
        scratch_shapes=scratch,
        compiler_params=_cparams("parallel", "arbitrary"),
        name="matmul_" + mode,
    )(*operands)


def _fox_cum_kernel(f_ref, b_ref, o_ref, carry_ref, *, ts):
    @pl.when(pl.program_id(1) == 0)
    def _():
        carry_ref[...] = jnp.zeros_like(carry_ref)

    x = f_ref[...] + b_ref[...]
    log_f = jnp.minimum(x, 0.0) - jnp.log(1.0 + jnp.exp(-jnp.abs(x)))
    r = lax.broadcasted_iota(jnp.int32, (ts, ts), 0)
    c = lax.broadcasted_iota(jnp.int32, (ts, ts), 1)
    lower = (c <= r).astype(F32)
    cum = jnp.dot(lower, log_f, precision=HIGHEST, preferred_element_type=F32) + carry_ref[...]
    carry_ref[...] = cum[ts - 1:ts, :]
    o_ref[0] = cum.T


def _fox_cum(f_logit, b_f, bsz, s):
    ts = min(512, s)
    return pl.pallas_call(
        functools.partial(_fox_cum_kernel, ts=ts),
        grid=(bsz, s // ts),
        in_specs=[pl.BlockSpec((ts, LANES), lambda b, t: (b * (s // ts) + t, 0)),
                  pl.BlockSpec((1, LANES), lambda b, t: (0, 0))],
        out_specs=pl.BlockSpec((1, LANES, ts), lambda b, t: (b, 0, t)),
        out_shape=jax.ShapeDtypeStruct((bsz, LANES, s), F32),
        scratch_shapes=[pltpu.VMEM((1, LANES), F32)],
        compiler_params=_cparams("parallel", "arbitrary"),
        name="fox_cum",
    )(f_logit, b_f)


def _fox_attn_kernel(q_ref, k_ref, v_ref, c_ref, z_ref, o_ref, sa_ref, sb_ref, *, tq):
    tk = tq // 2
    i = pl.program_id(2)
    q = q_ref[...]
    c0 = c_ref[0, 0, :, pl.ds(pl.multiple_of(i * tq, tq), LANES)][:, 0:1]

    def scores(qrows, j):
        start = j * tk if isinstance(j, int) else pl.multiple_of(j * tk, tk)
        s = lax.dot_general(qrows, k_ref[pl.ds(start, tk), :], (((1,), (1,)), ((), ())),
                            preferred_element_type=F32)
        return s + (c0 - c_ref[0, 0, :, pl.ds(start, tk)]) * LOG2E

    def update(s, j, m, acc, width=1):
        rows = width * tk
        start = j * tk if isinstance(j, int) else pl.multiple_of(j * tk, tk)
        vb = jnp.concatenate([v_ref[pl.ds(start, rows), :], jnp.ones((rows, HEAD_DIM), BF16)], axis=1)
        m_new = jnp.maximum(m, jnp.max(s, axis=-1, keepdims=True))
        p = jnp.exp2(s - m_new)
        acc = jnp.exp2(m - m_new) * acc + jnp.dot(p.astype(BF16), vb, preferred_element_type=F32)
        return m_new, acc

    sa_ref[...] = scores(q, 0)

    def pair(t, carry):
        m, acc = carry
        sb_ref[...] = scores(q, 2 * t + 1)
        m, acc = update(sa_ref[...], 2 * t, m, acc)
        sa_ref[...] = scores(q, 2 * t + 2)
        return update(sb_ref[...], 2 * t + 1, m, acc)

    init = (jnp.full((tq, 1), NEG_BIG, F32), jnp.zeros((tq, 2 * HEAD_DIM), F32))
    carry = lax.fori_loop(0, i // 2, lambda t2, cr: pair(2 * t2 + 1, pair(2 * t2, cr)), init)
    m, acc = lax.cond(i % 2 == 1, lambda cr: pair(i - 1, cr), lambda cr: cr, carry)

    r = lax.broadcasted_iota(jnp.int32, (tk, tk), 0)
    c = lax.broadcasted_iota(jnp.int32, (tk, tk), 1)
    s_bot = jnp.concatenate([sa_ref[tk:tq, :], jnp.where(c <= r, scores(q[tk:tq], 2 * i + 1), NEG_BIG)],
                            axis=1)
    _, acc_top = update(jnp.where(c <= r, sa_ref[0:tk, :], NEG_BIG), 2 * i, m[0:tk], acc[0:tk])
    _, acc_bot = update(s_bot, 2 * i, m[tk:tq], acc[tk:tq], width=2)

    def finish(a, rows):
        o = a[:, 0:HEAD_DIM] / a[:, HEAD_DIM:HEAD_DIM + 1]
        o_ref[rows, :] = (o * _silu(z_ref[rows, :])).astype(o_ref.dtype)

    finish(acc_top, slice(0, tk))
    finish(acc_bot, slice(tk, tq))


def _fox_attention(qkv, cum4, z, bsz, s, heads):
    tq = min(FOX_BLOCK, s)
    nq = s // tq
    return pl.pallas_call(
        functools.partial(_fox_attn_kernel, tq=tq),
        grid=(bsz, heads, nq),
        in_specs=[pl.BlockSpec((tq, HEAD_DIM), lambda b, h, i: (b * nq + i, h)),
                  pl.BlockSpec((s, HEAD_DIM), lambda b, h, i: (b, heads + h)),
                  pl.BlockSpec((s, HEAD_DIM), lambda b, h, i: (b, 2 * heads + h)),
                  pl.BlockSpec((1, 1, 1, s), lambda b, h, i: (b, h, 0, 0)),
                  pl.BlockSpec((tq, HEAD_DIM), lambda b, h, i: (b * nq + i, h))],
        out_specs=pl.BlockSpec((tq, HEAD_DIM), lambda b, h, i: (b * nq + i, h)),
        out_shape=jax.ShapeDtypeStruct((bsz * s, heads * HEAD_DIM), BF16),
        scratch_shapes=[pltpu.VMEM((tq, tq // 2), F32)] * 2,
        compiler_params=_cparams("parallel", "parallel", "arbitrary"),
        name="fox_attn",
    )(qkv, qkv, qkv, cum4, z)


def _fox_layer(h, hn, ssq, w_all, layer, w_tail, b_f, wo_all, ple_gain, bsz, s):
    width = wo_all.shape[1]
    heads = width // HEAD_DIM
    q_scale = jnp.concatenate([jnp.full((width,), HEAD_DIM ** -0.5 * LOG2E, F32),
                               jnp.ones((2 * width,), F32)]).reshape(1, 3 * width)
    qkv = _matmul(hn, w_all, n=3 * width, w_row_blk=layer, out_dtype=BF16, mode="colscale",
                  extras=[(q_scale, "row", {})], row_ssq=ssq)
    z = _matmul(hn, w_all, n=width, w_row_blk=layer, w_col_off=3 * width, out_dtype=F32, row_ssq=ssq)
    w_f = jnp.pad(w_tail, ((0, 0), (0, LANES - heads))).astype(BF16)
    f_logit = _matmul(hn, w_f, out_dtype=F32, row_ssq=ssq)
    b_row = jnp.pad(b_f.astype(F32), (0, LANES - heads)).reshape(1, LANES)
    cum = _fox_cum(f_logit, b_row, bsz, s)
    gated = _fox_attention(qkv, cum.reshape(bsz, LANES, 1, s), z, bsz, s, heads)
    return _matmul(gated, wo_all, w_row_blk=layer, out_dtype=F32, mode="residual",
                   extras=[(h, "tile", {})], norm_gain=ple_gain)


def _bdot(a, b):
    return jnp.einsum("bmk,bkn->bmn", a, b, preferred_element_type=F32)


def _split_bf16(x):
    hi = x.astype(BF16)
    return hi, (x - hi.astype(F32)).astype(BF16)


def _bdot3(a, b):
    a_hi, a_lo = _split_bf16(a)
    b_hi, b_lo = _split_bf16(b)
    return _bdot(a_hi, b_hi) + _bdot(a_hi, b_lo) + _bdot(a_lo, b_hi)


def _bdot_exact_lhs(a_bf16, b):
    b_hi = b.astype(BF16)
    b_mid, b_lo = _split_bf16(b - b_hi.astype(F32))
    return _bdot(a_bf16, b_hi) + _bdot(a_bf16, b_mid) + _bdot(a_bf16, b_lo)


def _gdn_kernel(q_ref, k_ref, v_ref, z_ref, t_ref, cw_ref, gp_ref, nw_ref, o_ref,
                xq_ref, xk_ref, xv_ref, st_ref, u_ref, wq_ref, ik_ref, gl_ref, *, tb, heads, hps):
    hb = pl.program_id(1)
    c, dh = GDN_CHUNK, HEAD_DIM
    nb = tb // c
    nbat = hps * nb
    pad = SUBLANES

    @pl.when(pl.program_id(2) == 0)
    def _():
        for r in (xq_ref, xk_ref, xv_ref):
            r[0:pad, :] = jnp.zeros((pad, hps * dh), F32)
        st_ref[...] = jnp.zeros_like(st_ref)

    def conv(x_ref, xs_ref, col):
        xs_ref[pad:pad + tb, :] = x_ref[...]
        y = xs_ref[pad - (GDN_CONV - 1):pad - (GDN_CONV - 1) + tb, :] * cw_ref[col, 0:1, :]
        for j in range(1, GDN_CONV):
            off = pad - (GDN_CONV - 1) + j
            y = y + xs_ref[off:off + tb, :] * cw_ref[col, j:j + 1, :]
        xs_ref[0:pad, :] = xs_ref[tb:tb + pad, :]
        return _silu(y)

    def by_chunk(x):
        w = x.shape[1] // hps
        return jnp.concatenate([x[:, g * w:(g + 1) * w].reshape(nb, c, w) for g in range(hps)], axis=0)

    def l2n(x):
        return x * lax.rsqrt(jnp.sum(x * x, axis=-1, keepdims=True) + NORM_EPS)

    q = l2n(by_chunk(conv(q_ref, xq_ref, 0))) * (dh ** -0.5)
    k = l2n(by_chunk(conv(k_ref, xk_ref, 1)))
    v = by_chunk(conv(v_ref, xv_ref, 2))

    t = t_ref[...]
    lane = lax.broadcasted_iota(jnp.int32, (tb, LANES), 1)
    beta_all = _sigmoid(t)
    g_all = -jnp.exp(gp_ref[0:1, :]) * _softplus(t + gp_ref[1:2, :])

    def pick(x, first):
        cols = [jnp.sum(jnp.where(lane == first + hb * hps + g, x, 0.0), axis=-1, keepdims=True)
                for g in range(hps)]
        return jnp.concatenate([col.reshape(nb, c, 1) for col in cols], axis=0)

    beta = pick(beta_all, 0)
    g = pick(g_all, heads)

    ri = lax.broadcasted_iota(jnp.int32, (c, c), 0)
    ci = lax.broadcasted_iota(jnp.int32, (c, c), 1)
    incl = ci <= ri
    strict = ci < ri
    eye = (ci == ri).astype(F32)
    tri = jnp.broadcast_to(incl.astype(BF16), (nbat, c, c))
    rl = lax.broadcasted_iota(jnp.int32, (c, LANES), 0)
    cl = lax.broadcasted_iota(jnp.int32, (c, LANES), 1)
    seg_mask = (rl > cl) | (cl == c)

    seg = _bdot_exact_lhs(tri, jnp.where(seg_mask, g, 0.0))
    gc = seg[:, :, c:c + 1]
    g_last = seg[:, c - 1:c, c:c + 1]
    decay = jnp.exp(jnp.where(incl, seg[:, :, 0:c], NEG_BIG))
    egc = jnp.exp(gc)
    k_beta = k * beta
    kq = jnp.einsum("bmd,bnd->bmn", jnp.concatenate([k_beta, q], axis=1).astype(BF16),
                    k.astype(BF16), preferred_element_type=F32)
    a = jnp.where(strict, kq[:, 0:c] * decay, 0.0)
    intra = kq[:, c:2 * c] * decay
    inv = eye - a
    pw = _bdot3(a, a)
    levels = int(math.log2(c)) - 1
    for lvl in range(levels):
        if lvl < levels - 1:
            both = _bdot3(jnp.concatenate([inv, pw], axis=1), pw)
            inv = inv + both[:, 0:c]
            pw = both[:, c:2 * c]
        else:
            inv = inv + _bdot3(inv, pw)
    uw = _bdot3(inv, jnp.concatenate([v * beta, k_beta * egc], axis=2))
    u_ref[...] = uw[:, :, 0:dh].reshape(nbat * c, dh)
    wq_ref[...] = jnp.concatenate([uw[:, :, dh:2 * dh], q * egc], axis=1).astype(BF16).reshape(
        nbat * 2 * c, dh)
    k_dec = k * jnp.exp(g_last - gc)
    for b in range(nbat):
        ik_ref[b * (c + dh):(b + 1) * (c + dh), :] = jnp.concatenate(
            [intra[b], k_dec[b].T], axis=0).astype(BF16)
    gl_ref[...] = jnp.broadcast_to(jnp.exp(g_last), (nbat, SUBLANES, dh)).reshape(nbat * SUBLANES, dh)

    def step(n, states):
        rows = lambda g, size: pl.ds(pl.multiple_of((g * nb + n) * size, size), size)
        ws = [jnp.dot(wq_ref[rows(g, 2 * c), :], states[g].astype(BF16), preferred_element_type=F32)
              for g in range(hps)]
        v_new = [u_ref[rows(g, c), :] - ws[g][0:c] for g in range(hps)]
        iv = [jnp.dot(ik_ref[pl.ds(pl.multiple_of((g * nb + n) * (c + dh), c), c + dh), :],
                      v_new[g].astype(BF16), preferred_element_type=F32) for g in range(hps)]
        new_states = []
        r0 = pl.multiple_of(n * c, c)
        for g in range(hps):
            out = ws[g][c:2 * c] + iv[g][0:c]
            gl = gl_ref[pl.ds(pl.multiple_of((g * nb + n) * SUBLANES, SUBLANES), 1), :]
            new_states.append(states[g] * gl + iv[g][c:c + dh])
            ms = jnp.mean(out * out, axis=-1, keepdims=True)
            o = out * lax.rsqrt(ms + NORM_EPS) * nw_ref[...]
            o_ref[pl.ds(r0, c), g * dh:(g + 1) * dh] = (
                o * _silu(z_ref[pl.ds(r0, c), g * dh:(g + 1) * dh])).astype(o_ref.dtype)
        return tuple(new_states)

    states = lax.fori_loop(0, nb, step, tuple(st_ref[g] for g in range(hps)))
    for g in range(hps):
        st_ref[g] = states[g]


def _gdn_mixer(qkvz, tail, conv_w, gate_p, norm_w, bsz, s, heads):
    hps = math.gcd(GDN_HEADS_PER_STEP, heads)
    tb = min(GDN_TIME_BLOCK, s)
    nt = s // tb
    hblocks = heads // hps
    width = hps * HEAD_DIM
    nbat = hps * (tb // GDN_CHUNK)
    col = lambda off: pl.BlockSpec((tb, width), lambda b, h, t: (b * nt + t, off + h))
    return pl.pallas_call(
        functools.partial(_gdn_kernel, tb=tb, heads=heads, hps=hps),
        grid=(bsz, hblocks, nt),
        in_specs=[col(0), col(hblocks), col(2 * hblocks), col(3 * hblocks),
                  pl.BlockSpec((tb, LANES), lambda b, h, t: (b * nt + t, 0)),
                  pl.BlockSpec((3, GDN_CONV, width), lambda b, h, t: (0, 0, h)),
                  pl.BlockSpec((2, LANES), lambda b, h, t: (0, 0)),
                  pl.BlockSpec((1, HEAD_DIM), lambda b, h, t: (0, 0))],
        out_specs=pl.BlockSpec((tb, width), lambda b, h, t: (b * nt + t, h)),
        out_shape=jax.ShapeDtypeStruct((bsz * s, heads * HEAD_DIM), BF16),
        scratch_shapes=[pltpu.VMEM((tb + SUBLANES, width), F32)] * 3
                       + [pltpu.VMEM((hps, HEAD_DIM, HEAD_DIM), F32),
                          pltpu.VMEM((nbat * GDN_CHUNK, HEAD_DIM), F32),
                          pltpu.VMEM((nbat * 2 * GDN_CHUNK, HEAD_DIM), BF16),
                          pltpu.VMEM((nbat * (GDN_CHUNK + HEAD_DIM), GDN_CHUNK), BF16),
                          pltpu.VMEM((nbat * SUBLANES, HEAD_DIM), F32)],
        compiler_params=_cparams("parallel", "parallel", "arbitrary"),
        name="gdn_mixer",
    )(qkvz, qkvz, qkvz, qkvz, tail, conv_w, gate_p, norm_w)


def _gdn_layer(h, hn, ssq, w_all, layer, w_tail, conv_w, a_log, dt_bias, norm_w, wo_all, ple_gain,
               bsz, s):
    width = wo_all.shape[1]
    heads = width // HEAD_DIM
    qkvz = _matmul(hn, w_all, n=4 * width, w_row_blk=layer, out_dtype=F32, row_ssq=ssq)
    w_t = jnp.pad(w_tail, ((0, 0), (0, LANES - 2 * heads))).astype(BF16)
    tail = _matmul(hn, w_t, out_dtype=F32, row_ssq=ssq)
    cw = conv_w.astype(F32).reshape(GDN_CONV, 3, width).transpose(1, 0, 2)
    lane_pad = lambda x: jnp.pad(x.astype(F32), (heads, LANES - 2 * heads))
    gate_p = jnp.stack([lane_pad(a_log), lane_pad(dt_bias)])
    gated = _gdn_mixer(qkvz, tail, cw, gate_p, norm_w.astype(F32).reshape(1, HEAD_DIM),
                       bsz, s, heads)
    return _matmul(gated, wo_all, w_row_blk=layer, out_dtype=F32, mode="residual",
                   extras=[(h, "tile", {})], norm_gain=ple_gain)


def _ssm_prep_kernel(lam_ref, bf_ref, cf_ref, bd_ref, wz_ref, wyt_ref, laml_ref):
    ns = SSM_PACK * SSM_STATE
    lam_re, lam_im, step_log = lam_ref[0, 0:1, :], lam_ref[0, 1:2, :], lam_ref[0, 2:3, :]
    step = jnp.exp(step_log)
    mag = jnp.exp(lam_re * step)
    lb_re, lb_im = mag * jnp.cos(lam_im * step), mag * jnp.sin(lam_im * step)
    den = lam_re * lam_re + lam_im * lam_im
    num_re = lb_re - 1.0
    zoh_re = (num_re * lam_re + lb_im * lam_im) / den
    zoh_im = (lb_im * lam_re - num_re * lam_im) / den
    b_re, b_im = bf_ref[0, 0], bf_ref[0, 1]
    bb_re = zoh_re * b_re - zoh_im * b_im
    bb_im = zoh_re * b_im + zoh_im * b_re
    c_re, c_im = cf_ref[0, 0], cf_ref[0, 1]
    cc = jnp.concatenate([c_re, -c_im], axis=1)

    bd_ref[0, 0, LANES:2 * LANES, 0:LANES] = jnp.zeros((LANES, LANES), bd_ref.dtype)
    pw_re, pw_im = jnp.ones_like(lb_re), jnp.zeros_like(lb_re)
    for d in range(SSM_L + 1):
        if d < SSM_L:
            a_re = pw_re * bb_re - pw_im * bb_im
            a_im = pw_re * bb_im + pw_im * bb_re
            a_d = jnp.concatenate([a_re, a_im], axis=1)
            blk = lax.dot_general(a_d, cc, (((1,), (1,)), ((), ())), precision=HIGHEST,
                                  preferred_element_type=F32).astype(bd_ref.dtype)
            d2 = d // 2
            if d % 2 == 0:
                bd_ref[0, d2, 0:LANES, 0:LANES] = blk
                bd_ref[0, d2, LANES:2 * LANES, LANES:2 * LANES] = blk
            else:
                bd_ref[0, d2, 0:LANES, LANES:2 * LANES] = blk
                if d2 + 1 < SSM_L // 2:
                    bd_ref[0, d2 + 1, LANES:2 * LANES, 0:LANES] = blk
            tin = SSM_L - 1 - d
            wz_ref[0, tin * LANES:(tin + 1) * LANES, :] = a_d.astype(wz_ref.dtype)
        if d >= 1:
            y_re = pw_re * c_re - pw_im * c_im
            y_im = pw_re * c_im + pw_im * c_re
            wyt_ref[0, (d - 1) * LANES:d * LANES, :] = jnp.concatenate(
                [y_re, -y_im], axis=1).astype(wyt_ref.dtype)
        if d == SSM_L:
            laml_ref[0] = jnp.concatenate([pw_re, pw_im], axis=1)
        pw_re, pw_im = pw_re * lb_re - pw_im * lb_im, pw_re * lb_im + pw_im * lb_re


def _ssm_prep(lam, bfull, cfull):
    nsg = lam.shape[0]
    ns = SSM_PACK * SSM_STATE
    rows = SSM_L * LANES
    return pl.pallas_call(
        _ssm_prep_kernel,
        grid=(nsg,),
        in_specs=[pl.BlockSpec((1, 3, ns), lambda g: (g, 0, 0)),
                  pl.BlockSpec((1, 2, LANES, ns), lambda g: (g, 0, 0, 0)),
                  pl.BlockSpec((1, 2, LANES, ns), lambda g: (g, 0, 0, 0))],
        out_specs=[pl.BlockSpec((1, SSM_L // 2, 2 * LANES, 2 * LANES), lambda g: (g, 0, 0, 0)),
                   pl.BlockSpec((1, rows, 2 * ns), lambda g: (g, 0, 0)),
                   pl.BlockSpec((1, rows, 2 * ns), lambda g: (g, 0, 0)),
                   pl.BlockSpec((1, 1, 2 * ns), lambda g: (g, 0, 0))],
        out_shape=[jax.ShapeDtypeStruct((nsg, SSM_L // 2, 2 * LANES, 2 * LANES), BF16),
                   jax.ShapeDtypeStruct((nsg, rows, 2 * ns), BF16),
                   jax.ShapeDtypeStruct((nsg, rows, 2 * ns), BF16),
                   jax.ShapeDtypeStruct((nsg, 1, 2 * ns), F32)],
        compiler_params=_cparams("parallel"),
        name="ssm_prep",
    )(lam, bfull, cfull)


def _ssm_scan_kernel(u_ref, bd_ref, wz_ref, wyt_ref, laml_ref, d_ref, o_ref,
                     x_ref, z_ref, xp_ref, st_ref, *, tc):
    ns = SSM_PACK * SSM_STATE

    @pl.when(pl.program_id(2) == 0)
    def _():
        st_ref[...] = jnp.zeros_like(st_ref)

    for tau in range(SSM_L):
        x_ref[:, tau * LANES:(tau + 1) * LANES] = u_ref[pl.ds(tau, tc, stride=SSM_L), :].astype(BF16)

    z_ref[...] = jnp.dot(x_ref[...], wz_ref[0], preferred_element_type=F32)

    a_re, a_im = laml_ref[0, :, 0:ns], laml_ref[0, :, ns:2 * ns]

    def row(r, carry):
        s_re, s_im = carry
        xp_ref[pl.ds(r, 1), :] = jnp.concatenate([s_re, s_im], axis=1)
        zr = z_ref[pl.ds(r, 1), :]
        n_re = a_re * s_re - a_im * s_im + zr[:, 0:ns]
        n_im = a_re * s_im + a_im * s_re + zr[:, ns:2 * ns]
        return n_re, n_im

    s_re, s_im = lax.fori_loop(0, tc, row, (st_ref[:, 0:ns], st_ref[:, ns:2 * ns]), unroll=4)
    st_ref[...] = jnp.concatenate([s_re, s_im], axis=1)

    xprev = xp_ref[...].astype(BF16)
    pair = 2 * LANES
    for t2 in range(SSM_L // 2):
        y2 = lax.dot_general(xprev, wyt_ref[0, t2 * pair:(t2 + 1) * pair, :],
                             (((1,), (1,)), ((), ())), preferred_element_type=F32)
        for tin in range(t2 + 1):
            y2 = y2 + jnp.dot(x_ref[:, tin * pair:(tin + 1) * pair], bd_ref[0, t2 - tin],
                              preferred_element_type=F32)
        for half in range(2):
            tau = 2 * t2 + half
            y = y2[:, half * LANES:(half + 1) * LANES]
            y = y + d_ref[...] * u_ref[pl.ds(tau, tc, stride=SSM_L), :]
            y = 0.5 * y * (1.0 + jnp.tanh(math.sqrt(2.0 / math.pi) * (y + 0.044715 * (y * y * y))))
            o_ref[pl.ds(tau, tc, stride=SSM_L), :] = y


def _ssm_scan(uz, bd, wz, wyt, laml, d_skip, bsz, s):
    e = d_skip.shape[1]
    nsg = e // LANES
    ns = SSM_PACK * SSM_STATE
    tb = min(4096, s)
    nt = s // tb
    tc = tb // SSM_L
    rows = SSM_L * LANES
    return pl.pallas_call(
        functools.partial(_ssm_scan_kernel, tc=tc),
        grid=(nsg, bsz, nt),
        in_specs=[pl.BlockSpec((tb, LANES), lambda g, b, t: (b * nt + t, g)),
                  pl.BlockSpec((1, SSM_L // 2, 2 * LANES, 2 * LANES), lambda g, b, t: (g, 0, 0, 0)),
                  pl.BlockSpec((1, rows, 2 * ns), lambda g, b, t: (g, 0, 0)),
                  pl.BlockSpec((1, rows, 2 * ns), lambda g, b, t: (g, 0, 0)),
                  pl.BlockSpec((1, 1, 2 * ns), lambda g, b, t: (g, 0, 0)),
                  pl.BlockSpec((1, LANES), lambda g, b, t: (0, g))],
        out_specs=pl.BlockSpec((tb, LANES), lambda g, b, t: (b * nt + t, g)),
        out_shape=jax.ShapeDtypeStruct((uz.shape[0], e), F32),
        scratch_shapes=[pltpu.VMEM((tc, rows), BF16), pltpu.VMEM((tc, 2 * ns), F32),
                        pltpu.VMEM((tc, 2 * ns), F32), pltpu.VMEM((1, 2 * ns), F32)],
        compiler_params=_cparams("parallel", "parallel", "arbitrary"),
        name="ssm_scan",
    )(uz, bd, wz, wyt, laml, d_skip)


def _ssm_layer(h, hn, ssq, w_all, layer, lam_re, lam_im, b_re, b_im, c_re, c_im, log_step, d_skip,
               wg_all, b_glu, wo_all, ple_gain, bsz, s):
    e = wo_all.shape[1]
    groups, nstate = lam_re.shape
    assert nstate == SSM_STATE and e == groups * SSM_GROUP and groups % SSM_PACK == 0
    nsg = groups // SSM_PACK
    ns = SSM_PACK * SSM_STATE
    lam = jnp.stack([lam_re.astype(F32).reshape(nsg, ns), lam_im.astype(F32).reshape(nsg, ns),
                     jnp.repeat(log_step.astype(F32), SSM_STATE).reshape(nsg, ns)], axis=1)
    eye = jnp.eye(SSM_PACK, dtype=F32)

    def expand_b(b):
        return jnp.einsum("sgpm,gh->sgmhp", b.astype(F32).reshape(nsg, SSM_PACK, SSM_STATE, SSM_GROUP),
                          eye).reshape(nsg, LANES, ns)

    def expand_c(c):
        return jnp.einsum("sgnp,gh->sgnhp", c.astype(F32).reshape(nsg, SSM_PACK, SSM_GROUP, SSM_STATE),
                          eye).reshape(nsg, LANES, ns)

    bfull = jnp.stack([expand_b(b_re), expand_b(b_im)], axis=1)
    cfull = jnp.stack([expand_c(c_re), expand_c(c_im)], axis=1)
    bd, wz, wyt, laml = _ssm_prep(lam, bfull, cfull)

    uz = _matmul(hn, w_all, w_row_blk=layer, out_dtype=F32, row_ssq=ssq)
    y = _ssm_scan(uz, bd, wz, wyt, laml, d_skip.astype(F32).reshape(1, e), bsz, s)
    y2 = _matmul(y, wg_all, w_row_blk=layer, out_dtype=BF16, mode="glu", cast_a=True,
                 extras=[(y, "tile", {}), (uz, "tile", {"col_off": e}),
                         (b_glu.astype(F32).reshape(1, e), "row", {})])
    return _matmul(y2, wo_all, w_row_blk=layer, out_dtype=F32, mode="residual",
                   extras=[(h, "tile", {})], norm_gain=ple_gain)


def kernel(x, p, norm_mix, fox_w_in, fox_b_f, fox_w_out, gdn_w_in, gdn_conv, gdn_a_log, gdn_dt_bias, gdn_norm, gdn_w_out, ssm_w_in, ssm_lam_re, ssm_lam_im, ssm_b_re, ssm_b_im, ssm_c_re, ssm_c_im, ssm_log_step, ssm_d, ssm_w_glu, ssm_b_glu, ssm_w_out, norm_ple, ple_w_proj, ple_w_gate, final_norm):
    bsz, s, d = x.shape
    depth, m = p.shape[0], bsz * s
    pdim = p.shape[-1]
    flat = lambda w: w.reshape(-1, w.shape[-1])
    fox_width, gdn_width = fox_w_out.shape[1], gdn_w_out.shape[1]
    fox_w = _cast_bf16(flat(fox_w_in), 4 * fox_width)
    fox_wo = _cast_bf16(flat(fox_w_out))
    gdn_w = _cast_bf16(flat(gdn_w_in), 4 * gdn_width)
    gdn_wo = _cast_bf16(flat(gdn_w_out))
    ssm_w = _cast_bf16(flat(ssm_w_in))
    ssm_wg = _cast_bf16(flat(ssm_w_glu))
    ssm_wo = _cast_bf16(flat(ssm_w_out))
    ple_wg = _cast_bf16(flat(ple_w_gate))
    ple_wp = _cast_bf16(flat(ple_w_proj))
    p_bf = _cast_bf16(flat(p))

    h = x.reshape(m, d).astype(F32)
    hn, ssq = _rmsnorm(h, norm_mix[0], BF16), None
    for i in range(depth):
        kind, j = i % 3, i // 3
        if kind == 0:
            h, hg, hssq = _fox_layer(h, hn, ssq, fox_w, j, fox_w_in[j, :, 4 * fox_width:], fox_b_f[j],
                                     fox_wo, norm_ple[i], bsz, s)
        elif kind == 1:
            h, hg, hssq = _gdn_layer(h, hn, ssq, gdn_w, j, gdn_w_in[j, :, 4 * gdn_width:], gdn_conv[j],
                                     gdn_a_log[j], gdn_dt_bias[j], gdn_norm[j], gdn_wo, norm_ple[i],
                                     bsz, s)
        else:
            h, hg, hssq = _ssm_layer(h, hn, ssq, ssm_w, j, ssm_lam_re[j], ssm_lam_im[j], ssm_b_re[j],
                                     ssm_b_im[j], ssm_c_re[j], ssm_c_im[j], ssm_log_step[j], ssm_d[j],
                                     ssm_wg, ssm_b_glu[j], ssm_wo, norm_ple[i], bsz, s)
        extras = [(h, "tile", {}), (p_bf, "rows", {"row_blk": i * (m // min(MM_TM, m))}),
                  (ple_wp, "cols", {"row_blk": i, "rows": pdim})]
        outs = _matmul(hg, ple_wg, w_row_blk=i, out_dtype=F32, mode="ple", row_ssq=hssq, extras=extras,
                       norm_gain=norm_mix[i + 1] if i + 1 < depth else None)
        h, hn, ssq = outs if i + 1 < depth else (outs, None, None)
    return _rmsnorm(h, final_norm, F32).reshape(bsz, s, d)
```

```python
import functools
import math

import jax
import jax.numpy as jnp
from jax import lax
from jax.experimental import pallas as pl
from jax.experimental.pallas import tpu as pltpu

F32 = jnp.float32
BF16 = jnp.bfloat16

LANES = 128
SUBLANES = 8
VMEM_LIMIT_BYTES = 56 * 1024 * 1024

A_PREP_ROWS = 32
MM_TM = 512
MM_TN = 1024
NORM_EPS = 1e-6
LOG2E = 1.4426950408889634
HEAD_DIM = 128
FOX_BLOCK = 1024
GDN_CHUNK = 64
GDN_CONV = 4
GDN_HEADS_PER_STEP = 4
GDN_TIME_BLOCK = 256
SSM_GROUP = 16
SSM_STATE = 64
SSM_PACK = LANES // SSM_GROUP
SSM_L = 16
NEG_BIG = -1e30

HIGHEST = lax.Precision.HIGHEST


def _cparams(*sem):
    return pltpu.CompilerParams(dimension_semantics=sem, vmem_limit_bytes=VMEM_LIMIT_BYTES)


def _sigmoid(x):
    return 0.5 * jnp.tanh(0.5 * x) + 0.5


def _silu(x):
    return x * _sigmoid(x)


def _softplus(x):
    return jnp.maximum(x, 0.0) + jnp.log(1.0 + jnp.exp(-jnp.abs(x)))


def _rmsnorm_kernel(x_ref, g_ref, o_ref):
    x = x_ref[...]
    ms = jnp.mean(x * x, axis=-1, keepdims=True)
    o_ref[...] = (x * lax.rsqrt(ms + NORM_EPS) * g_ref[...]).astype(o_ref.dtype)


def _rmsnorm(x, g, out_dtype):
    m, d = x.shape
    tm = min(256, m)
    return pl.pallas_call(
        _rmsnorm_kernel,
        grid=(m // tm,),
        in_specs=[pl.BlockSpec((tm, d), lambda i: (i, 0)),
                  pl.BlockSpec((1, d), lambda i: (0, 0))],
        out_specs=pl.BlockSpec((tm, d), lambda i: (i, 0)),
        out_shape=jax.ShapeDtypeStruct((m, d), out_dtype),
        compiler_params=_cparams("parallel"),
        name="rmsnorm",
    )(x, g.reshape(1, d).astype(F32))


def _cast_kernel(x_ref, o_ref):
    o_ref[...] = x_ref[...].astype(o_ref.dtype)


def _cast_bf16(x, rows=None):
    nl, r, c = x.shape
    rows = r if rows is None else rows
    tr, tc = min(512, rows), min(2048, c)
    assert rows % tr == 0 and c % tc == 0
    out = pl.pallas_call(
        _cast_kernel,
        grid=(nl, rows // tr, c // tc),
        in_specs=[pl.BlockSpec((1, tr, tc), lambda l, i, j: (l, i, j))],
        out_specs=pl.BlockSpec((1, tr, tc), lambda l, i, j: (l, i, j)),
        out_shape=jax.ShapeDtypeStruct((nl, rows, c), BF16),
        compiler_params=_cparams("parallel", "parallel", "parallel"),
        name="cast_bf16",
    )(x)
    return out.reshape(nl * rows, c)


_N_EXTRA = {"plain": 0, "colscale": 1, "residual": 1, "ple": 3, "glu": 3}


def _mm_kernel(*refs, mode, cast_a, w_t, has_ssq, emit_norm, inv_k):
    it = iter(refs)
    a_ref, w_ref = next(it), next(it)
    ssq_ref = next(it) if has_ssq else None
    extra = [next(it) for _ in range(_N_EXTRA[mode])]
    gain_ref = next(it) if emit_norm else None
    o_ref = next(it)
    hg_ref, ssq_out_ref = (next(it), next(it)) if emit_norm else (None, None)
    if cast_a:
        a_scr = next(it)

        @pl.when(pl.program_id(1) == 0)
        def _():
            def prep(r, carry):
                rows = pl.ds(pl.multiple_of(r * A_PREP_ROWS, A_PREP_ROWS), A_PREP_ROWS)
                a_scr[rows, :] = a_ref[rows, :].astype(BF16)
                return carry

            lax.fori_loop(0, a_ref.shape[0] // A_PREP_ROWS, prep, 0)

        a = a_scr[...]
    else:
        a = a_ref[...]

    acc = lax.dot_general(a, w_ref[...], (((1,), (1 if w_t else 0,)), ((), ())),
                          preferred_element_type=F32)
    if has_ssq:
        parts = ssq_ref[...]
        tot = parts[:, 0:LANES]
        for t in range(1, parts.shape[1] // LANES):
            tot = tot + parts[:, t * LANES:(t + 1) * LANES]
        acc = acc * lax.rsqrt(tot[:, 0:1] * inv_k + NORM_EPS)
    if mode == "plain":
        out = acc
    elif mode == "colscale":
        out = acc * extra[0][...]
    elif mode == "residual":
        out = extra[0][...] + acc
    elif mode == "ple":
        res_ref, p_ref, wp_ref = extra
        emb = jnp.dot(p_ref[...], wp_ref[...], preferred_element_type=F32)
        out = res_ref[...] + _sigmoid(acc) * emb
    elif mode == "glu":
        y_ref, z_ref, b_ref = extra
        out = y_ref[...] * _sigmoid(acc + b_ref[...]) * _silu(z_ref[...])
    else:
        raise ValueError(mode)
    o_ref[...] = out.astype(o_ref.dtype)
    if emit_norm:
        hg_ref[...] = (out * gain_ref[...]).astype(hg_ref.dtype)
        ssq_out_ref[...] = jnp.broadcast_to(jnp.sum(out * out, axis=-1, keepdims=True),
                                            ssq_out_ref.shape)


def _extra_spec(arr, kind, tm, tn, row_blk=0, col_off=0, rows=None):
    if kind == "tile":
        return pl.BlockSpec((tm, tn), lambda i, j: (i + row_blk, j + col_off // tn))
    if kind == "row":
        return pl.BlockSpec((1, tn), lambda i, j: (0, j))
    if kind == "rows":
        return pl.BlockSpec((tm, arr.shape[1]), lambda i, j: (i + row_blk, 0))
    if kind == "cols":
        return pl.BlockSpec((rows, tn), lambda i, j: (row_blk, j))
    raise ValueError(kind)


def _matmul(a, w, *, out_dtype, n=None, mode="plain", extras=(), cast_a=False, row_ssq=None,
            norm_gain=None, w_t=None, w_row_blk=0, w_col_off=0, tm=MM_TM, tn=MM_TN):
    m, kdim = a.shape
    n = (w_t if w_t else w.shape[1]) if n is None else n
    tm = min(tm, m)
    tn = functools.reduce(math.gcd, [tn, n, w_col_off, w_t or 0])
    assert m % tm == 0 and tn % LANES == 0 and len(extras) == _N_EXTRA[mode]
    operands = [a, w]
    if w_t:
        first = (w_row_blk * w_t + w_col_off) // tn
        w_spec = pl.BlockSpec((tn, kdim), lambda i, j: (first + j, 0))
    else:
        w_spec = pl.BlockSpec((kdim, tn), lambda i, j: (w_row_blk, j + w_col_off // tn))
    in_specs = [pl.BlockSpec((tm, kdim), lambda i, j: (i, 0)), w_spec]
    if row_ssq is not None:
        operands.append(row_ssq)
        in_specs.append(pl.BlockSpec((tm, row_ssq.shape[1]), lambda i, j: (i, 0)))
    for arr, kind, opts in extras:
        assert opts.get("col_off", 0) % tn == 0
        operands.append(arr)
        in_specs.append(_extra_spec(arr, kind, tm, tn, **opts))
    tile = pl.BlockSpec((tm, tn), lambda i, j: (i, j))
    out_specs, out_shape = tile, jax.ShapeDtypeStruct((m, n), out_dtype)
    if norm_gain is not None:
        operands.append(norm_gain.astype(F32).reshape(1, n))
        in_specs.append(pl.BlockSpec((1, tn), lambda i, j: (0, j)))
        out_specs = [tile, tile, pl.BlockSpec((tm, LANES), lambda i, j: (i, j))]
        out_shape = [out_shape, jax.ShapeDtypeStruct((m, n), BF16),
                     jax.ShapeDtypeStruct((m, n // tn * LANES), F32)]
    scratch = [pltpu.VMEM((tm, kdim), BF16)] if cast_a else []
    return pl.pallas_call(
        functools.partial(_mm_kernel, mode=mode, cast_a=cast_a, w_t=bool(w_t),
                          has_ssq=row_ssq is not None,
                          emit_norm=norm_gain is not None, inv_k=1.0 / kdim),
        grid=(m // tm, n // tn),
        in_specs=in_specs,
        out_specs=out_specs,
        out_shape=out_shape,
        scratch_shapes=scratch,
        compiler_params=_cparams("parallel", "arbitrary"),
        name="matmul_" + mode,
    )(*operands)


def _fox_cum_kernel(f_ref, b_ref, o_ref, carry_ref, *, ts):
    @pl.when(pl.program_id(1) == 0)
    def _():
        carry_ref[...] = jnp.zeros_like(carry_ref)

    x = f_ref[...] + b_ref[...]
    log_f = jnp.minimum(x, 0.0) - jnp.log(1.0 + jnp.exp(-jnp.abs(x)))
    r = lax.broadcasted_iota(jnp.int32, (ts, ts), 0)
    c = lax.broadcasted_iota(jnp.int32, (ts, ts), 1)
    lower = (c <= r).astype(F32)
    cum = jnp.dot(lower, log_f, precision=HIGHEST, preferred_element_type=F32) + carry_ref[...]
    carry_ref[...] = cum[ts - 1:ts, :]
    o_ref[0] = cum.T


def _fox_cum(f_logit, b_f, bsz, s):
    ts = min(512, s)
    return pl.pallas_call(
        functools.partial(_fox_cum_kernel, ts=ts),
        grid=(bsz, s // ts),
        in_specs=[pl.BlockSpec((ts, LANES), lambda b, t: (b * (s // ts) + t, 0)),
                  pl.BlockSpec((1, LANES), lambda b, t: (0, 0))],
        out_specs=pl.BlockSpec((1, LANES, ts), lambda b, t: (b, 0, t)),
        out_shape=jax.ShapeDtypeStruct((bsz, LANES, s), F32),
        scratch_shapes=[pltpu.VMEM((1, LANES), F32)],
        compiler_params=_cparams("parallel", "arbitrary"),
        name="fox_cum",
    )(f_logit, b_f)


def _fox_attn_kernel(q_ref, k_ref, v_ref, c_ref, z_ref, o_ref, sa_ref, sb_ref, *, tq):
    tk = tq // 2
    i = pl.program_id(2)
    q = q_ref[...]
    c0 = c_ref[0, 0, :, pl.ds(pl.multiple_of(i * tq, tq), LANES)][:, 0:1]

    def scores(qrows, j):
        start = j * tk if isinstance(j, int) else pl.multiple_of(j * tk, tk)
        s = lax.dot_general(qrows, k_ref[pl.ds(start, tk), :], (((1,), (1,)), ((), ())),
                            preferred_element_type=F32)
        return s + (c0 - c_ref[0, 0, :, pl.ds(start, tk)]) * LOG2E

    def update(s, j, m, acc, width=1):
        rows = width * tk
        start = j * tk if isinstance(j, int) else pl.multiple_of(j * tk, tk)
        vb = jnp.concatenate([v_ref[pl.ds(start, rows), :], jnp.ones((rows, HEAD_DIM), BF16)], axis=1)
        m_new = jnp.maximum(m, jnp.max(s, axis=-1, keepdims=True))
        p = jnp.exp2(s - m_new)
        acc = jnp.exp2(m - m_new) * acc + jnp.dot(p.astype(BF16), vb, preferred_element_type=F32)
        return m_new, acc

    sa_ref[...] = scores(q, 0)

    def pair(t, carry):
        m, acc = carry
        sb_ref[...] = scores(q, 2 * t + 1)
        m, acc = update(sa_ref[...], 2 * t, m, acc)
        sa_ref[...] = scores(q, 2 * t + 2)
        return update(sb_ref[...], 2 * t + 1, m, acc)

    init = (jnp.full((tq, 1), NEG_BIG, F32), jnp.zeros((tq, 2 * HEAD_DIM), F32))
    carry = lax.fori_loop(0, i // 2, lambda t2, cr: pair(2 * t2 + 1, pair(2 * t2, cr)), init)
    m, acc = lax.cond(i % 2 == 1, lambda cr: pair(i - 1, cr), lambda cr: cr, carry)

    r = lax.broadcasted_iota(jnp.int32, (tk, tk), 0)
    c = lax.broadcasted_iota(jnp.int32, (tk, tk), 1)
    s_bot = jnp.concatenate([sa_ref[tk:tq, :], jnp.where(c <= r, scores(q[tk:tq], 2 * i + 1), NEG_BIG)],
                            axis=1)
    _, acc_top = update(jnp.where(c <= r, sa_ref[0:tk, :], NEG_BIG), 2 * i, m[0:tk], acc[0:tk])
    _, acc_bot = update(s_bot, 2 * i, m[tk:tq], acc[tk:tq], width=2)

    def finish(a, rows):
        o = a[:, 0:HEAD_DIM] / a[:, HEAD_DIM:HEAD_DIM + 1]
        o_ref[rows, :] = (o * _silu(z_ref[rows, :])).astype(o_ref.dtype)

    finish(acc_top, slice(0, tk))
    finish(acc_bot, slice(tk, tq))


def _fox_attention(qkv, cum4, z, bsz, s, heads):
    tq = min(FOX_BLOCK, s)
    nq = s // tq
    return pl.pallas_call(
        functools.partial(_fox_attn_kernel, tq=tq),
        grid=(bsz, heads, nq),
        in_specs=[pl.BlockSpec((tq, HEAD_DIM), lambda b, h, i: (b * nq + i, h)),
                  pl.BlockSpec((s, HEAD_DIM), lambda b, h, i: (b, heads + h)),
                  pl.BlockSpec((s, HEAD_DIM), lambda b, h, i: (b, 2 * heads + h)),
                  pl.BlockSpec((1, 1, 1, s), lambda b, h, i: (b, h, 0, 0)),
                  pl.BlockSpec((tq, HEAD_DIM), lambda b, h, i: (b * nq + i, h))],
        out_specs=pl.BlockSpec((tq, HEAD_DIM), lambda b, h, i: (b * nq + i, h)),
        out_shape=jax.ShapeDtypeStruct((bsz * s, heads * HEAD_DIM), BF16),
        scratch_shapes=[pltpu.VMEM((tq, tq // 2), F32)] * 2,
        compiler_params=_cparams("parallel", "parallel", "arbitrary"),
        name="fox_attn",
    )(qkv, qkv, qkv, cum4, z)


def _fox_layer(h, hn, ssq, w_all, layer, w_tail, b_f, wo_all, ple_gain, bsz, s):
    width = wo_all.shape[1]
    heads = width // HEAD_DIM
    q_scale = jnp.concatenate([jnp.full((width,), HEAD_DIM ** -0.5 * LOG2E, F32),
                               jnp.ones((2 * width,), F32)]).reshape(1, 3 * width)
    qkv = _matmul(hn, w_all, n=3 * width, w_t=4 * width, w_row_blk=layer, out_dtype=BF16,
                  mode="colscale", extras=[(q_scale, "row", {})], row_ssq=ssq)
    z = _matmul(hn, w_all, n=width, w_t=4 * width, w_row_blk=layer, w_col_off=3 * width,
                out_dtype=F32, row_ssq=ssq)
    f_logit = _matmul(hn, w_tail, w_t=LANES, out_dtype=F32, row_ssq=ssq)
    b_row = jnp.pad(b_f.astype(F32), (0, LANES - heads)).reshape(1, LANES)
    cum = _fox_cum(f_logit, b_row, bsz, s)
    gated = _fox_attention(qkv, cum.reshape(bsz, LANES, 1, s), z, bsz, s, heads)
    return _matmul(gated, wo_all, w_row_blk=layer, out_dtype=F32, mode="residual",
                   extras=[(h, "tile", {})], norm_gain=ple_gain)


def _bdot(a, b):
    return jnp.einsum("bmk,bkn->bmn", a, b, preferred_element_type=F32)


def _split_bf16(x):
    hi = x.astype(BF16)
    return hi, (x - hi.astype(F32)).astype(BF16)


def _bdot3(a, b):
    a_hi, a_lo = _split_bf16(a)
    b_hi, b_lo = _split_bf16(b)
    return _bdot(a_hi, b_hi) + _bdot(a_hi, b_lo) + _bdot(a_lo, b_hi)


def _bdot_exact_lhs(a_bf16, b):
    b_hi = b.astype(BF16)
    b_mid, b_lo = _split_bf16(b - b_hi.astype(F32))
    return _bdot(a_bf16, b_hi) + _bdot(a_bf16, b_mid) + _bdot(a_bf16, b_lo)


def _gdn_kernel(q_ref, k_ref, v_ref, z_ref, t_ref, cw_ref, gp_ref, nw_ref, o_ref,
                xq_ref, xk_ref, xv_ref, st_ref, u_ref, wq_ref, ik_ref, gl_ref, *, tb, heads, hps):
    hb = pl.program_id(1)
    c, dh = GDN_CHUNK, HEAD_DIM
    nb = tb // c
    nbat = hps * nb
    pad = SUBLANES

    @pl.when(pl.program_id(2) == 0)
    def _():
        for r in (xq_ref, xk_ref, xv_ref):
            r[0:pad, :] = jnp.zeros((pad, hps * dh), F32)
        st_ref[...] = jnp.zeros_like(st_ref)

    def conv(x_ref, xs_ref, col):
        xs_ref[pad:pad + tb, :] = x_ref[...]
        y = xs_ref[pad - (GDN_CONV - 1):pad - (GDN_CONV - 1) + tb, :] * cw_ref[col, 0:1, :]
        for j in range(1, GDN_CONV):
            off = pad - (GDN_CONV - 1) + j
            y = y + xs_ref[off:off + tb, :] * cw_ref[col, j:j + 1, :]
        xs_ref[0:pad, :] = xs_ref[tb:tb + pad, :]
        return _silu(y)

    def by_chunk(x):
        w = x.shape[1] // hps
        return jnp.concatenate([x[:, g * w:(g + 1) * w].reshape(nb, c, w) for g in range(hps)], axis=0)

    def l2n(x):
        return x * lax.rsqrt(jnp.sum(x * x, axis=-1, keepdims=True) + NORM_EPS)

    q = l2n(by_chunk(conv(q_ref, xq_ref, 0))) * (dh ** -0.5)
    k = l2n(by_chunk(conv(k_ref, xk_ref, 1)))
    v = by_chunk(conv(v_ref, xv_ref, 2))

    t = t_ref[...]
    lane = lax.broadcasted_iota(jnp.int32, (tb, LANES), 1)
    beta_all = _sigmoid(t)
    g_all = -jnp.exp(gp_ref[0:1, :]) * _softplus(t + gp_ref[1:2, :])

    def pick(x, first):
        cols = [jnp.sum(jnp.where(lane == first + hb * hps + g, x, 0.0), axis=-1, keepdims=True)
                for g in range(hps)]
        return jnp.concatenate([col.reshape(nb, c, 1) for col in cols], axis=0)

    beta = pick(beta_all, 0)
    g = pick(g_all, heads)

    ri = lax.broadcasted_iota(jnp.int32, (c, c), 0)
    ci = lax.broadcasted_iota(jnp.int32, (c, c), 1)
    incl = ci <= ri
    strict = ci < ri
    eye = (ci == ri).astype(F32)
    tri = jnp.broadcast_to(incl.astype(BF16), (nbat, c, c))
    rl = lax.broadcasted_iota(jnp.int32, (c, LANES), 0)
    cl = lax.broadcasted_iota(jnp.int32, (c, LANES), 1)
    seg_mask = (rl > cl) | (cl == c)

    seg = _bdot_exact_lhs(tri, jnp.where(seg_mask, g, 0.0))
    gc = seg[:, :, c:c + 1]
    g_last = seg[:, c - 1:c, c:c + 1]
    decay = jnp.exp(jnp.where(incl, seg[:, :, 0:c], NEG_BIG))
    egc = jnp.exp(gc)
    k_beta = k * beta
    kq = jnp.einsum("bmd,bnd->bmn", jnp.concatenate([k_beta, q], axis=1).astype(BF16),
                    k.astype(BF16), preferred_element_type=F32)
    a = jnp.where(strict, kq[:, 0:c] * decay, 0.0)
    intra = kq[:, c:2 * c] * decay
    inv = eye - a
    pw = _bdot3(a, a)
    levels = int(math.log2(c)) - 1
    for lvl in range(levels):
        if lvl < levels - 1:
            both = _bdot3(jnp.concatenate([inv, pw], axis=1), pw)
            inv = inv + both[:, 0:c]
            pw = both[:, c:2 * c]
        else:
            inv = inv + _bdot3(inv, pw)
    uw = _bdot3(inv, jnp.concatenate([v * beta, k_beta * egc], axis=2))
    u_ref[...] = uw[:, :, 0:dh].reshape(nbat * c, dh)
    wq_ref[...] = jnp.concatenate([uw[:, :, dh:2 * dh], q * egc], axis=1).astype(BF16).reshape(
        nbat * 2 * c, dh)
    k_dec = k * jnp.exp(g_last - gc)
    for b in range(nbat):
        ik_ref[b * (c + dh):(b + 1) * (c + dh), :] = jnp.concatenate(
            [intra[b], k_dec[b].T], axis=0).astype(BF16)
    gl_ref[...] = jnp.broadcast_to(jnp.exp(g_last), (nbat, SUBLANES, dh)).reshape(nbat * SUBLANES, dh)

    def step(n, states):
        rows = lambda g, size: pl.ds(pl.multiple_of((g * nb + n) * size, size), size)
        ws = [jnp.dot(wq_ref[rows(g, 2 * c), :], states[g].astype(BF16), preferred_element_type=F32)
              for g in range(hps)]
        v_new = [u_ref[rows(g, c), :] - ws[g][0:c] for g in range(hps)]
        iv = [jnp.dot(ik_ref[pl.ds(pl.multiple_of((g * nb + n) * (c + dh), c), c + dh), :],
                      v_new[g].astype(BF16), preferred_element_type=F32) for g in range(hps)]
        new_states = []
        r0 = pl.multiple_of(n * c, c)
        for g in range(hps):
            out = ws[g][c:2 * c] + iv[g][0:c]
            gl = gl_ref[pl.ds(pl.multiple_of((g * nb + n) * SUBLANES, SUBLANES), 1), :]
            new_states.append(states[g] * gl + iv[g][c:c + dh])
            ms = jnp.mean(out * out, axis=-1, keepdims=True)
            o = out * lax.rsqrt(ms + NORM_EPS) * nw_ref[...]
            o_ref[pl.ds(r0, c), g * dh:(g + 1) * dh] = (
                o * _silu(z_ref[pl.ds(r0, c), g * dh:(g + 1) * dh])).astype(o_ref.dtype)
        return tuple(new_states)

    states = lax.fori_loop(0, nb, step, tuple(st_ref[g] for g in range(hps)))
    for g in range(hps):
        st_ref[g] = states[g]


def _gdn_mixer(qkvz, tail, conv_w, gate_p, norm_w, bsz, s, heads):
    hps = math.gcd(GDN_HEADS_PER_STEP, heads)
    tb = min(GDN_TIME_BLOCK, s)
    nt = s // tb
    hblocks = heads // hps
    width = hps * HEAD_DIM
    nbat = hps * (tb // GDN_CHUNK)
    col = lambda off: pl.BlockSpec((tb, width), lambda b, h, t: (b * nt + t, off + h))
    return pl.pallas_call(
        functools.partial(_gdn_kernel, tb=tb, heads=heads, hps=hps),
        grid=(bsz, hblocks, nt),
        in_specs=[col(0), col(hblocks), col(2 * hblocks), col(3 * hblocks),
                  pl.BlockSpec((tb, LANES), lambda b, h, t: (b * nt + t, 0)),
                  pl.BlockSpec((3, GDN_CONV, width), lambda b, h, t: (0, 0, h)),
                  pl.BlockSpec((2, LANES), lambda b, h, t: (0, 0)),
                  pl.BlockSpec((1, HEAD_DIM), lambda b, h, t: (0, 0))],
        out_specs=pl.BlockSpec((tb, width), lambda b, h, t: (b * nt + t, h)),
        out_shape=jax.ShapeDtypeStruct((bsz * s, heads * HEAD_DIM), BF16),
        scratch_shapes=[pltpu.VMEM((tb + SUBLANES, width), F32)] * 3
                       + [pltpu.VMEM((hps, HEAD_DIM, HEAD_DIM), F32),
                          pltpu.VMEM((nbat * GDN_CHUNK, HEAD_DIM), F32),
                          pltpu.VMEM((nbat * 2 * GDN_CHUNK, HEAD_DIM), BF16),
                          pltpu.VMEM((nbat * (GDN_CHUNK + HEAD_DIM), GDN_CHUNK), BF16),
                          pltpu.VMEM((nbat * SUBLANES, HEAD_DIM), F32)],
        compiler_params=_cparams("parallel", "parallel", "arbitrary"),
        name="gdn_mixer",
    )(qkvz, qkvz, qkvz, qkvz, tail, conv_w, gate_p, norm_w)


def _gdn_layer(h, hn, ssq, w_all, layer, w_tail, conv_w, a_log, dt_bias, norm_w, wo_all, ple_gain,
               bsz, s):
    width = wo_all.shape[1]
    heads = width // HEAD_DIM
    qkvz = _matmul(hn, w_all, w_t=4 * width, w_row_blk=layer, out_dtype=F32, row_ssq=ssq)
    tail = _matmul(hn, w_tail, w_t=LANES, out_dtype=F32, row_ssq=ssq)
    cw = conv_w.astype(F32).reshape(GDN_CONV, 3, width).transpose(1, 0, 2)
    lane_pad = lambda x: jnp.pad(x.astype(F32), (heads, LANES - 2 * heads))
    gate_p = jnp.stack([lane_pad(a_log), lane_pad(dt_bias)])
    gated = _gdn_mixer(qkvz, tail, cw, gate_p, norm_w.astype(F32).reshape(1, HEAD_DIM),
                       bsz, s, heads)
    return _matmul(gated, wo_all, w_row_blk=layer, out_dtype=F32, mode="residual",
                   extras=[(h, "tile", {})], norm_gain=ple_gain)


def _ssm_prep_kernel(lam_ref, bf_ref, cf_ref, bd_ref, wz_ref, wyt_ref, laml_ref):
    ns = SSM_PACK * SSM_STATE
    lam_re, lam_im, step_log = lam_ref[0, 0:1, :], lam_ref[0, 1:2, :], lam_ref[0, 2:3, :]
    step = jnp.exp(step_log)
    mag = jnp.exp(lam_re * step)
    lb_re, lb_im = mag * jnp.cos(lam_im * step), mag * jnp.sin(lam_im * step)
    den = lam_re * lam_re + lam_im * lam_im
    num_re = lb_re - 1.0
    zoh_re = (num_re * lam_re + lb_im * lam_im) / den
    zoh_im = (lb_im * lam_re - num_re * lam_im) / den
    b_re, b_im = bf_ref[0, 0], bf_ref[0, 1]
    bb_re = zoh_re * b_re - zoh_im * b_im
    bb_im = zoh_re * b_im + zoh_im * b_re
    c_re, c_im = cf_ref[0, 0], cf_ref[0, 1]
    cc = jnp.concatenate([c_re, -c_im], axis=1)

    bd_ref[0, 0, LANES:2 * LANES, 0:LANES] = jnp.zeros((LANES, LANES), bd_ref.dtype)
    pw_re, pw_im = jnp.ones_like(lb_re), jnp.zeros_like(lb_re)
    for d in range(SSM_L + 1):
        if d < SSM_L:
            a_re = pw_re * bb_re - pw_im * bb_im
            a_im = pw_re * bb_im + pw_im * bb_re
            a_d = jnp.concatenate([a_re, a_im], axis=1)
            blk = lax.dot_general(a_d, cc, (((1,), (1,)), ((), ())), precision=HIGHEST,
                                  preferred_element_type=F32).astype(bd_ref.dtype)
            d2 = d // 2
            if d % 2 == 0:
                bd_ref[0, d2, 0:LANES, 0:LANES] = blk
                bd_ref[0, d2, LANES:2 * LANES, LANES:2 * LANES] = blk
            else:
                bd_ref[0, d2, 0:LANES, LANES:2 * LANES] = blk
                if d2 + 1 < SSM_L // 2:
                    bd_ref[0, d2 + 1, LANES:2 * LANES, 0:LANES] = blk
            tin = SSM_L - 1 - d
            wz_ref[0, tin * LANES:(tin + 1) * LANES, :] = a_d.astype(wz_ref.dtype)
        if d >= 1:
            y_re = pw_re * c_re - pw_im * c_im
            y_im = pw_re * c_im + pw_im * c_re
            wyt_ref[0, (d - 1) * LANES:d * LANES, :] = jnp.concatenate(
                [y_re, -y_im], axis=1).astype(wyt_ref.dtype)
        if d == SSM_L:
            laml_ref[0] = jnp.concatenate([pw_re, pw_im], axis=1)
        pw_re, pw_im = pw_re * lb_re - pw_im * lb_im, pw_re * lb_im + pw_im * lb_re


def _ssm_prep(lam, bfull, cfull):
    nsg = lam.shape[0]
    ns = SSM_PACK * SSM_STATE
    rows = SSM_L * LANES
    return pl.pallas_call(
        _ssm_prep_kernel,
        grid=(nsg,),
        in_specs=[pl.BlockSpec((1, 3, ns), lambda g: (g, 0, 0)),
                  pl.BlockSpec((1, 2, LANES, ns), lambda g: (g, 0, 0, 0)),
                  pl.BlockSpec((1, 2, LANES, ns), lambda g: (g, 0, 0, 0))],
        out_specs=[pl.BlockSpec((1, SSM_L // 2, 2 * LANES, 2 * LANES), lambda g: (g, 0, 0, 0)),
                   pl.BlockSpec((1, rows, 2 * ns), lambda g: (g, 0, 0)),
                   pl.BlockSpec((1, rows, 2 * ns), lambda g: (g, 0, 0)),
                   pl.BlockSpec((1, 1, 2 * ns), lambda g: (g, 0, 0))],
        out_shape=[jax.ShapeDtypeStruct((nsg, SSM_L // 2, 2 * LANES, 2 * LANES), BF16),
                   jax.ShapeDtypeStruct((nsg, rows, 2 * ns), BF16),
                   jax.ShapeDtypeStruct((nsg, rows, 2 * ns), BF16),
                   jax.ShapeDtypeStruct((nsg, 1, 2 * ns), F32)],
        compiler_params=_cparams("parallel"),
        name="ssm_prep",
    )(lam, bfull, cfull)


def _ssm_scan_kernel(u_ref, bd_ref, wz_ref, wyt_ref, laml_ref, d_ref, o_ref,
                     x_ref, z_ref, xp_ref, st_ref, *, tc):
    ns = SSM_PACK * SSM_STATE

    @pl.when(pl.program_id(2) == 0)
    def _():
        st_ref[...] = jnp.zeros_like(st_ref)

    for tau in range(SSM_L):
        x_ref[:, tau * LANES:(tau + 1) * LANES] = u_ref[pl.ds(tau, tc, stride=SSM_L), :].astype(BF16)

    z_ref[...] = jnp.dot(x_ref[...], wz_ref[0], preferred_element_type=F32)

    a_re, a_im = laml_ref[0, :, 0:ns], laml_ref[0, :, ns:2 * ns]

    def row(r, carry):
        s_re, s_im = carry
        xp_ref[pl.ds(r, 1), :] = jnp.concatenate([s_re, s_im], axis=1)
        zr = z_ref[pl.ds(r, 1), :]
        n_re = a_re * s_re - a_im * s_im + zr[:, 0:ns]
        n_im = a_re * s_im + a_im * s_re + zr[:, ns:2 * ns]
        return n_re, n_im

    s_re, s_im = lax.fori_loop(0, tc, row, (st_ref[:, 0:ns], st_ref[:, ns:2 * ns]), unroll=4)
    st_ref[...] = jnp.concatenate([s_re, s_im], axis=1)

    xprev = xp_ref[...].astype(BF16)
    pair = 2 * LANES
    for t2 in range(SSM_L // 2):
        y2 = lax.dot_general(xprev, wyt_ref[0, t2 * pair:(t2 + 1) * pair, :],
                             (((1,), (1,)), ((), ())), preferred_element_type=F32)
        for tin in range(t2 + 1):
            y2 = y2 + jnp.dot(x_ref[:, tin * pair:(tin + 1) * pair], bd_ref[0, t2 - tin],
                              preferred_element_type=F32)
        for half in range(2):
            tau = 2 * t2 + half
            y = y2[:, half * LANES:(half + 1) * LANES]
            y = y + d_ref[...] * u_ref[pl.ds(tau, tc, stride=SSM_L), :]
            y = 0.5 * y * (1.0 + jnp.tanh(math.sqrt(2.0 / math.pi) * (y + 0.044715 * (y * y * y))))
            o_ref[pl.ds(tau, tc, stride=SSM_L), :] = y


def _ssm_scan(uz, bd, wz, wyt, laml, d_skip, bsz, s):
    e = d_skip.shape[1]
    nsg = e // LANES
    ns = SSM_PACK * SSM_STATE
    tb = min(4096, s)
    nt = s // tb
    tc = tb // SSM_L
    rows = SSM_L * LANES
    return pl.pallas_call(
        functools.partial(_ssm_scan_kernel, tc=tc),
        grid=(nsg, bsz, nt),
        in_specs=[pl.BlockSpec((tb, LANES), lambda g, b, t: (b * nt + t, g)),
                  pl.BlockSpec((1, SSM_L // 2, 2 * LANES, 2 * LANES), lambda g, b, t: (g, 0, 0, 0)),
                  pl.BlockSpec((1, rows, 2 * ns), lambda g, b, t: (g, 0, 0)),
                  pl.BlockSpec((1, rows, 2 * ns), lambda g, b, t: (g, 0, 0)),
                  pl.BlockSpec((1, 1, 2 * ns), lambda g, b, t: (g, 0, 0)),
                  pl.BlockSpec((1, LANES), lambda g, b, t: (0, g))],
        out_specs=pl.BlockSpec((tb, LANES), lambda g, b, t: (b * nt + t, g)),
        out_shape=jax.ShapeDtypeStruct((uz.shape[0], e), F32),
        scratch_shapes=[pltpu.VMEM((tc, rows), BF16), pltpu.VMEM((tc, 2 * ns), F32),
                        pltpu.VMEM((tc, 2 * ns), F32), pltpu.VMEM((1, 2 * ns), F32)],
        compiler_params=_cparams("parallel", "parallel", "arbitrary"),
        name="ssm_scan",
    )(uz, bd, wz, wyt, laml, d_skip)


def _ssm_layer(h, hn, ssq, w_all, layer, lam_re, lam_im, b_re, b_im, c_re, c_im, log_step, d_skip,
               wg_all, b_glu, wo_all, ple_gain, bsz, s):
    e = wo_all.shape[1]
    groups, nstate = lam_re.shape
    assert nstate == SSM_STATE and e == groups * SSM_GROUP and groups % SSM_PACK == 0
    nsg = groups // SSM_PACK
    ns = SSM_PACK * SSM_STATE
    lam = jnp.stack([lam_re.astype(F32).reshape(nsg, ns), lam_im.astype(F32).reshape(nsg, ns),
                     jnp.repeat(log_step.astype(F32), SSM_STATE).reshape(nsg, ns)], axis=1)
    eye = jnp.eye(SSM_PACK, dtype=F32)

    def expand_b(b):
        return jnp.einsum("sgpm,gh->sgmhp", b.astype(F32).reshape(nsg, SSM_PACK, SSM_STATE, SSM_GROUP),
                          eye).reshape(nsg, LANES, ns)

    def expand_c(c):
        return jnp.einsum("sgnp,gh->sgnhp", c.astype(F32).reshape(nsg, SSM_PACK, SSM_GROUP, SSM_STATE),
                          eye).reshape(nsg, LANES, ns)

    bfull = jnp.stack([expand_b(b_re), expand_b(b_im)], axis=1)
    cfull = jnp.stack([expand_c(c_re), expand_c(c_im)], axis=1)
    bd, wz, wyt, laml = _ssm_prep(lam, bfull, cfull)

    uz = _matmul(hn, w_all, w_row_blk=layer, out_dtype=F32, row_ssq=ssq)
    y = _ssm_scan(uz, bd, wz, wyt, laml, d_skip.astype(F32).reshape(1, e), bsz, s)
    y2 = _matmul(y, wg_all, w_row_blk=layer, out_dtype=BF16, mode="glu", cast_a=True,
                 extras=[(y, "tile", {}), (uz, "tile", {"col_off": e}),
                         (b_glu.astype(F32).reshape(1, e), "row", {})])
    return _matmul(y2, wo_all, w_row_blk=layer, out_dtype=F32, mode="residual",
                   extras=[(h, "tile", {})], norm_gain=ple_gain)


def kernel(x, p, norm_mix, fox_w_in, fox_b_f, fox_w_out, gdn_w_in, gdn_conv, gdn_a_log, gdn_dt_bias, gdn_norm, gdn_w_out, ssm_w_in, ssm_lam_re, ssm_lam_im, ssm_b_re, ssm_b_im, ssm_c_re, ssm_c_im, ssm_log_step, ssm_d, ssm_w_glu, ssm_b_glu, ssm_w_out, norm_ple, ple_w_proj, ple_w_gate, final_norm):
    bsz, s, d = x.shape
    depth, m = p.shape[0], bsz * s
    pdim = p.shape[-1]
    fox_width, gdn_width = fox_w_out.shape[1], gdn_w_out.shape[1]
    fox_wt, gdn_wt = jnp.swapaxes(fox_w_in, 1, 2), jnp.swapaxes(gdn_w_in, 1, 2)
    pad_rows = lambda t: jnp.pad(t, ((0, 0), (0, LANES - t.shape[1]), (0, 0)))
    fox_w = _cast_bf16(fox_wt, 4 * fox_width)
    fox_tail = _cast_bf16(pad_rows(fox_wt[:, 4 * fox_width:]))
    gdn_w = _cast_bf16(gdn_wt, 4 * gdn_width)
    gdn_tail = _cast_bf16(pad_rows(gdn_wt[:, 4 * gdn_width:]))
    fox_wo, gdn_wo = _cast_bf16(fox_w_out), _cast_bf16(gdn_w_out)
    ssm_w, ssm_wg, ssm_wo = _cast_bf16(ssm_w_in), _cast_bf16(ssm_w_glu), _cast_bf16(ssm_w_out)
    ple_wg, ple_wp = _cast_bf16(ple_w_gate), _cast_bf16(ple_w_proj)
    p_bf = _cast_bf16(p.reshape(depth, m, pdim))

    h = x.reshape(m, d).astype(F32)
    hn, ssq = _rmsnorm(h, norm_mix[0], BF16), None
    for i in range(depth):
        kind, j = i % 3, i // 3
        if kind == 0:
            h, hg, hssq = _fox_layer(h, hn, ssq, fox_w, j, fox_tail[j * LANES:(j + 1) * LANES],
                                     fox_b_f[j], fox_wo, norm_ple[i], bsz, s)
        elif kind == 1:
            h, hg, hssq = _gdn_layer(h, hn, ssq, gdn_w, j, gdn_tail[j * LANES:(j + 1) * LANES],
                                     gdn_conv[j], gdn_a_log[j], gdn_dt_bias[j], gdn_norm[j], gdn_wo,
                                     norm_ple[i], bsz, s)
        else:
            h, hg, hssq = _ssm_layer(h, hn, ssq, ssm_w, j, ssm_lam_re[j], ssm_lam_im[j], ssm_b_re[j],
                                     ssm_b_im[j], ssm_c_re[j], ssm_c_im[j], ssm_log_step[j], ssm_d[j],
                                     ssm_wg, ssm_b_glu[j], ssm_wo, norm_ple[i], bsz, s)
        extras = [(h, "tile", {}), (p_bf, "rows", {"row_blk": i * (m // min(MM_TM, m))}),
                  (ple_wp, "cols", {"row_blk": i, "rows": pdim})]
        outs = _matmul(hg, ple_wg, w_row_blk=i, out_dtype=F32, mode="ple", row_ssq=hssq, extras=extras,
                       norm_gain=norm_mix[i + 1] if i + 1 < depth else None)
        h, hn, ssq = outs if i + 1 < depth else (outs, None, None)
    return _rmsnorm(h, final_norm, F32).reshape(bsz, s, d)
```

```python
import functools
import math

import jax
import jax.numpy as jnp
from jax import lax
from jax.experimental import pallas as pl
from jax.experimental.pallas import tpu as pltpu

F32 = jnp.float32
BF16 = jnp.bfloat16

LANES = 128
SUBLANES = 8
VMEM_LIMIT_BYTES = 56 * 1024 * 1024

CAST_BLOCK_ELEMS = 1 << 20
A_PREP_ROWS = 32
MM_TM = 512
MM_TM_WIDE = 1024
MM_TN = 1024
NORM_EPS = 1e-6
LOG2E = 1.4426950408889634
HEAD_DIM = 128
FOX_BLOCK = 1024
GDN_CHUNK = 64
GDN_CONV = 4
GDN_HEADS_PER_STEP = 8
GDN_TIME_BLOCK = 128
SSM_GROUP = 16
SSM_STATE = 64
SSM_PACK = LANES // SSM_GROUP
SSM_L = 16
NEG_BIG = -1e30

HIGHEST = lax.Precision.HIGHEST


def _cparams(*sem):
    return pltpu.CompilerParams(dimension_semantics=sem, vmem_limit_bytes=VMEM_LIMIT_BYTES)


def _sigmoid(x):
    return 0.5 * jnp.tanh(0.5 * x) + 0.5


def _silu(x):
    return x * _sigmoid(x)


def _softplus(x):
    return jnp.maximum(x, 0.0) + jnp.log(1.0 + jnp.exp(-jnp.abs(x)))


def _rmsnorm_kernel(x_ref, g_ref, o_ref):
    x = x_ref[...]
    ms = jnp.mean(x * x, axis=-1, keepdims=True)
    o_ref[...] = (x * lax.rsqrt(ms + NORM_EPS) * g_ref[...]).astype(o_ref.dtype)


def _rmsnorm(x, g, out_dtype):
    m, d = x.shape
    tm = min(256, m)
    return pl.pallas_call(
        _rmsnorm_kernel,
        grid=(m // tm,),
        in_specs=[pl.BlockSpec((tm, d), lambda i: (i, 0)),
                  pl.BlockSpec((1, d), lambda i: (0, 0))],
        out_specs=pl.BlockSpec((tm, d), lambda i: (i, 0)),
        out_shape=jax.ShapeDtypeStruct((m, d), out_dtype),
        compiler_params=_cparams("parallel"),
        name="rmsnorm",
    )(x, g.reshape(1, d).astype(F32))


def _cast_kernel(x_ref, o_ref):
    o_ref[...] = x_ref[...].astype(o_ref.dtype)


def _cast_bf16(x, rows=None):
    nl, r, c = x.shape
    rows = r if rows is None else rows
    tc = min(2048, c)
    tr = min(rows, CAST_BLOCK_ELEMS // tc)
    assert rows % tr == 0 and c % tc == 0
    out = pl.pallas_call(
        _cast_kernel,
        grid=(nl, rows // tr, c // tc),
        in_specs=[pl.BlockSpec((1, tr, tc), lambda l, i, j: (l, i, j))],
        out_specs=pl.BlockSpec((1, tr, tc), lambda l, i, j: (l, i, j)),
        out_shape=jax.ShapeDtypeStruct((nl, rows, c), BF16),
        compiler_params=_cparams("parallel", "parallel", "parallel"),
        name="cast_bf16",
    )(x)
    return out.reshape(nl * rows, c)


_N_EXTRA = {"plain": 0, "colscale": 1, "residual": 1, "ple": 3, "glu": 3}


def _mm_kernel(*refs, mode, cast_a, w_t, has_ssq, emit_norm, inv_k):
    it = iter(refs)
    a_ref, w_ref = next(it), next(it)
    ssq_ref = next(it) if has_ssq else None
    extra = [next(it) for _ in range(_N_EXTRA[mode])]
    gain_ref = next(it) if emit_norm else None
    o_ref = next(it)
    hg_ref, ssq_out_ref = (next(it), next(it)) if emit_norm else (None, None)
    if cast_a:
        a_scr = next(it)

        @pl.when(pl.program_id(1) == 0)
        def _():
            def prep(r, carry):
                rows = pl.ds(pl.multiple_of(r * A_PREP_ROWS, A_PREP_ROWS), A_PREP_ROWS)
                a_scr[rows, :] = a_ref[rows, :].astype(BF16)
                return carry

            lax.fori_loop(0, a_ref.shape[0] // A_PREP_ROWS, prep, 0)

        a = a_scr[...]
    else:
        a = a_ref[...]

    acc = lax.dot_general(a, w_ref[...], (((1,), (1 if w_t else 0,)), ((), ())),
                          preferred_element_type=F32)
    if has_ssq:
        parts = ssq_ref[...]
        tot = parts[:, 0:LANES]
        for t in range(1, parts.shape[1] // LANES):
            tot = tot + parts[:, t * LANES:(t + 1) * LANES]
        acc = acc * lax.rsqrt(tot[:, 0:1] * inv_k + NORM_EPS)
    if mode == "plain":
        out = acc
    elif mode == "colscale":
        out = acc * extra[0][...]
    elif mode == "residual":
        out = extra[0][...] + acc
    elif mode == "ple":
        res_ref, p_ref, wp_ref = extra
        emb = jnp.dot(p_ref[...], wp_ref[...], preferred_element_type=F32)
        out = res_ref[...] + _sigmoid(acc) * emb
    elif mode == "glu":
        y_ref, z_ref, b_ref = extra
        out = y_ref[...] * _sigmoid(acc + b_ref[...]) * _silu(z_ref[...])
    else:
        raise ValueError(mode)
    o_ref[...] = out.astype(o_ref.dtype)
    if emit_norm:
        hg_ref[...] = (out * gain_ref[...]).astype(hg_ref.dtype)
        ssq_out_ref[...] = jnp.broadcast_to(jnp.sum(out * out, axis=-1, keepdims=True),
                                            ssq_out_ref.shape)


def _extra_spec(arr, kind, tm, tn, row_blk=0, col_off=0, rows=None):
    if kind == "tile":
        return pl.BlockSpec((tm, tn), lambda i, j: (i + row_blk, j + col_off // tn))
    if kind == "row":
        return pl.BlockSpec((1, tn), lambda i, j: (0, j))
    if kind == "rows":
        return pl.BlockSpec((tm, arr.shape[1]), lambda i, j: (i + row_blk, 0))
    if kind == "cols":
        return pl.BlockSpec((rows, tn), lambda i, j: (row_blk, j))
    raise ValueError(kind)


def _matmul(a, w, *, out_dtype, n=None, mode="plain", extras=(), cast_a=False, row_ssq=None,
            norm_gain=None, w_t=None, w_row_blk=0, w_col_off=0, tm=MM_TM, tn=MM_TN):
    m, kdim = a.shape
    n = (w_t if w_t else w.shape[1]) if n is None else n
    tm = min(tm, m)
    tn = functools.reduce(math.gcd, [tn, n, w_col_off, w_t or 0])
    assert m % tm == 0 and tn % LANES == 0 and len(extras) == _N_EXTRA[mode]
    operands = [a, w]
    if w_t:
        first = (w_row_blk * w_t + w_col_off) // tn
        w_spec = pl.BlockSpec((tn, kdim), lambda i, j: (first + j, 0))
    else:
        w_spec = pl.BlockSpec((kdim, tn), lambda i, j: (w_row_blk, j + w_col_off // tn))
    in_specs = [pl.BlockSpec((tm, kdim), lambda i, j: (i, 0)), w_spec]
    if row_ssq is not None:
        operands.append(row_ssq)
        in_specs.append(pl.BlockSpec((tm, row_ssq.shape[1]), lambda i, j: (i, 0)))
    for arr, kind, opts in extras:
        assert opts.get("col_off", 0) % tn == 0
        operands.append(arr)
        in_specs.append(_extra_spec(arr, kind, tm, tn, **opts))
    tile = pl.BlockSpec((tm, tn), lambda i, j: (i, j))
    out_specs, out_shape = tile, jax.ShapeDtypeStruct((m, n), out_dtype)
    if norm_gain is not None:
        operands.append(norm_gain.astype(F32).reshape(1, n))
        in_specs.append(pl.BlockSpec((1, tn), lambda i, j: (0, j)))
        out_specs = [tile, tile, pl.BlockSpec((tm, LANES), lambda i, j: (i, j))]
        out_shape = [out_shape, jax.ShapeDtypeStruct((m, n), BF16),
                     jax.ShapeDtypeStruct((m, n // tn * LANES), F32)]
    scratch = [pltpu.VMEM((tm, kdim), BF16)] if cast_a else []
    return pl.pallas_call(
        functools.partial(_mm_kernel, mode=mode, cast_a=cast_a, w_t=bool(w_t),
                          has_ssq=row_ssq is not None,
                          emit_norm=norm_gain is not None, inv_k=1.0 / kdim),
        grid=(m // tm, n // tn),
        in_specs=in_specs,
        out_specs=out_specs,
        out_shape=out_shape,
        scratch_shapes=scratch,
        compiler_params=_cparams("parallel", "arbitrary"),
        name="matmul_" + mode,
    )(*operands)


def _fox_cum_kernel(f_ref, b_ref, o_ref, carry_ref, *, ts):
    @pl.when(pl.program_id(1) == 0)
    def _():
        carry_ref[...] = jnp.zeros_like(carry_ref)

    x = f_ref[...] + b_ref[...]
    log_f = jnp.minimum(x, 0.0) - jnp.log(1.0 + jnp.exp(-jnp.abs(x)))
    r = lax.broadcasted_iota(jnp.int32, (ts, ts), 0)
    c = lax.broadcasted_iota(jnp.int32, (ts, ts), 1)
    lower = (c <= r).astype(F32)
    cum = jnp.dot(lower, log_f, precision=HIGHEST, preferred_element_type=F32) + carry_ref[...]
    carry_ref[...] = cum[ts - 1:ts, :]
    o_ref[0] = cum.T


def _fox_cum(f_logit, b_f, bsz, s):
    ts = min(512, s)
    return pl.pallas_call(
        functools.partial(_fox_cum_kernel, ts=ts),
        grid=(bsz, s // ts),
        in_specs=[pl.BlockSpec((ts, LANES), lambda b, t: (b * (s // ts) + t, 0)),
                  pl.BlockSpec((1, LANES), lambda b, t: (0, 0))],
        out_specs=pl.BlockSpec((1, LANES, ts), lambda b, t: (b, 0, t)),
        out_shape=jax.ShapeDtypeStruct((bsz, LANES, s), F32),
        scratch_shapes=[pltpu.VMEM((1, LANES), F32)],
        compiler_params=_cparams("parallel", "arbitrary"),
        name="fox_cum",
    )(f_logit, b_f)


def _fox_attn_kernel(q_ref, k_ref, v_ref, c_ref, z_ref, o_ref, sa_ref, sb_ref, *, tq):
    tk = tq // 2
    i = pl.program_id(2)
    q = q_ref[...]
    c0 = c_ref[0, 0, :, pl.ds(pl.multiple_of(i * tq, tq), LANES)][:, 0:1]

    def scores(qrows, j):
        start = j * tk if isinstance(j, int) else pl.multiple_of(j * tk, tk)
        s = lax.dot_general(qrows, k_ref[pl.ds(start, tk), :], (((1,), (1,)), ((), ())),
                            preferred_element_type=F32)
        return s + (c0 - c_ref[0, 0, :, pl.ds(start, tk)]) * LOG2E

    def update(s, j, m, acc, width=1):
        rows = width * tk
        start = j * tk if isinstance(j, int) else pl.multiple_of(j * tk, tk)
        vb = jnp.concatenate([v_ref[pl.ds(start, rows), :], jnp.ones((rows, HEAD_DIM), BF16)], axis=1)
        m_new = jnp.maximum(m, jnp.max(s, axis=-1, keepdims=True))
        p = jnp.exp2(s - m_new)
        acc = jnp.exp2(m - m_new) * acc + jnp.dot(p.astype(BF16), vb, preferred_element_type=F32)
        return m_new, acc

    sa_ref[...] = scores(q, 0)

    def pair(t, carry):
        m, acc = carry
        sb_ref[...] = scores(q, 2 * t + 1)
        m, acc = update(sa_ref[...], 2 * t, m, acc)
        sa_ref[...] = scores(q, 2 * t + 2)
        return update(sb_ref[...], 2 * t + 1, m, acc)

    init = (jnp.full((tq, 1), NEG_BIG, F32), jnp.zeros((tq, 2 * HEAD_DIM), F32))
    carry = lax.fori_loop(0, i // 2, lambda t2, cr: pair(2 * t2 + 1, pair(2 * t2, cr)), init)
    m, acc = lax.cond(i % 2 == 1, lambda cr: pair(i - 1, cr), lambda cr: cr, carry)

    r = lax.broadcasted_iota(jnp.int32, (tk, tk), 0)
    c = lax.broadcasted_iota(jnp.int32, (tk, tk), 1)
    s_bot = jnp.concatenate([sa_ref[tk:tq, :], jnp.where(c <= r, scores(q[tk:tq], 2 * i + 1), NEG_BIG)],
                            axis=1)
    _, acc_top = update(jnp.where(c <= r, sa_ref[0:tk, :], NEG_BIG), 2 * i, m[0:tk], acc[0:tk])
    _, acc_bot = update(s_bot, 2 * i, m[tk:tq], acc[tk:tq], width=2)

    def finish(a, rows):
        o = a[:, 0:HEAD_DIM] / a[:, HEAD_DIM:HEAD_DIM + 1]
        o_ref[rows, :] = (o * _silu(z_ref[rows, :])).astype(o_ref.dtype)

    finish(acc_top, slice(0, tk))
    finish(acc_bot, slice(tk, tq))


def _fox_attention(qkv, cum4, z, bsz, s, heads):
    tq = min(FOX_BLOCK, s)
    nq = s // tq
    return pl.pallas_call(
        functools.partial(_fox_attn_kernel, tq=tq),
        grid=(bsz, heads, nq),
        in_specs=[pl.BlockSpec((tq, HEAD_DIM), lambda b, h, i: (b * nq + i, h)),
                  pl.BlockSpec((s, HEAD_DIM), lambda b, h, i: (b, heads + h)),
                  pl.BlockSpec((s, HEAD_DIM), lambda b, h, i: (b, 2 * heads + h)),
                  pl.BlockSpec((1, 1, 1, s), lambda b, h, i: (b, h, 0, 0)),
                  pl.BlockSpec((tq, HEAD_DIM), lambda b, h, i: (b * nq + i, h))],
        out_specs=pl.BlockSpec((tq, HEAD_DIM), lambda b, h, i: (b * nq + i, h)),
        out_shape=jax.ShapeDtypeStruct((bsz * s, heads * HEAD_DIM), BF16),
        scratch_shapes=[pltpu.VMEM((tq, tq // 2), F32)] * 2,
        compiler_params=_cparams("parallel", "parallel", "arbitrary"),
        name="fox_attn",
    )(qkv, qkv, qkv, cum4, z)


def _fox_layer(h, hn, ssq, w_all, layer, w_tail, b_f, wo_all, ple_gain, bsz, s):
    width = wo_all.shape[1]
    heads = width // HEAD_DIM
    q_scale = jnp.concatenate([jnp.full((width,), HEAD_DIM ** -0.5 * LOG2E, F32),
                               jnp.ones((2 * width,), F32)]).reshape(1, 3 * width)
    qkv = _matmul(hn, w_all, n=3 * width, w_t=4 * width, w_row_blk=layer, out_dtype=BF16,
                  mode="colscale", extras=[(q_scale, "row", {})], row_ssq=ssq, tm=MM_TM_WIDE)
    z = _matmul(hn, w_all, n=width, w_t=4 * width, w_row_blk=layer, w_col_off=3 * width,
                out_dtype=F32, row_ssq=ssq)
    f_logit = _matmul(hn, w_tail, w_t=LANES, out_dtype=F32, row_ssq=ssq)
    b_row = jnp.pad(b_f.astype(F32), (0, LANES - heads)).reshape(1, LANES)
    cum = _fox_cum(f_logit, b_row, bsz, s)
    gated = _fox_attention(qkv, cum.reshape(bsz, LANES, 1, s), z, bsz, s, heads)
    return _matmul(gated, wo_all, w_row_blk=layer, out_dtype=F32, mode="residual",
                   extras=[(h, "tile", {})], norm_gain=ple_gain)


def _bdot(a, b):
    return jnp.einsum("bmk,bkn->bmn", a, b, preferred_element_type=F32)


def _split_bf16(x):
    hi = x.astype(BF16)
    return hi, (x - hi.astype(F32)).astype(BF16)


def _bdot3_parts(a_hi, a_lo, b_hi, b_lo):
    return _bdot(a_hi, b_hi) + _bdot(a_hi, b_lo) + _bdot(a_lo, b_hi)


def _bdot3(a, b):
    return _bdot3_parts(*_split_bf16(a), *_split_bf16(b))


def _bdot_exact_lhs(a_bf16, b):
    b_hi = b.astype(BF16)
    b_mid, b_lo = _split_bf16(b - b_hi.astype(F32))
    return _bdot(a_bf16, b_hi) + _bdot(a_bf16, b_mid) + _bdot(a_bf16, b_lo)


def _gdn_kernel(q_ref, k_ref, v_ref, z_ref, t_ref, cw_ref, gp_ref, nw_ref, o_ref,
                xq_ref, xk_ref, xv_ref, st_ref, u_ref, wq_ref, ik_ref, gl_ref, *, tb, heads, hps):
    hb = pl.program_id(1)
    c, dh = GDN_CHUNK, HEAD_DIM
    nb = tb // c
    nbat = hps * nb
    pad = SUBLANES

    @pl.when(pl.program_id(2) == 0)
    def _():
        for r in (xq_ref, xk_ref, xv_ref):
            r[0:pad, :] = jnp.zeros((pad, hps * dh), F32)
        st_ref[...] = jnp.zeros_like(st_ref)

    def conv(x_ref, xs_ref, col):
        xs_ref[pad:pad + tb, :] = x_ref[...]
        y = xs_ref[pad - (GDN_CONV - 1):pad - (GDN_CONV - 1) + tb, :] * cw_ref[col, 0:1, :]
        for j in range(1, GDN_CONV):
            off = pad - (GDN_CONV - 1) + j
            y = y + xs_ref[off:off + tb, :] * cw_ref[col, j:j + 1, :]
        xs_ref[0:pad, :] = xs_ref[tb:tb + pad, :]
        return _silu(y)

    def by_chunk(x):
        w = x.shape[1] // hps
        return jnp.concatenate([x[:, g * w:(g + 1) * w].reshape(nb, c, w) for g in range(hps)], axis=0)

    def l2n(x):
        return x * lax.rsqrt(jnp.sum(x * x, axis=-1, keepdims=True) + NORM_EPS)

    q = l2n(by_chunk(conv(q_ref, xq_ref, 0))) * (dh ** -0.5)
    k = l2n(by_chunk(conv(k_ref, xk_ref, 1)))
    v = by_chunk(conv(v_ref, xv_ref, 2))

    t = t_ref[...]
    lane = lax.broadcasted_iota(jnp.int32, (tb, LANES), 1)
    beta_all = _sigmoid(t)
    g_all = -jnp.exp(gp_ref[0:1, :]) * _softplus(t + gp_ref[1:2, :])

    def pick(x, first):
        cols = [jnp.sum(jnp.where(lane == first + hb * hps + g, x, 0.0), axis=-1, keepdims=True)
                for g in range(hps)]
        return jnp.concatenate([col.reshape(nb, c, 1) for col in cols], axis=0)

    beta = pick(beta_all, 0)
    g = pick(g_all, heads)

    ri = lax.broadcasted_iota(jnp.int32, (c, c), 0)
    ci = lax.broadcasted_iota(jnp.int32, (c, c), 1)
    incl = ci <= ri
    strict = ci < ri
    eye = (ci == ri).astype(F32)
    tri = jnp.broadcast_to(incl.astype(BF16), (nbat, c, c))
    rl = lax.broadcasted_iota(jnp.int32, (c, LANES), 0)
    cl = lax.broadcasted_iota(jnp.int32, (c, LANES), 1)
    seg_mask = (rl > cl) | (cl == c)

    seg = _bdot_exact_lhs(tri, jnp.where(seg_mask, g, 0.0))
    gc = seg[:, :, c:c + 1]
    g_last = seg[:, c - 1:c, c:c + 1]
    decay = jnp.exp(jnp.where(incl, seg[:, :, 0:c], NEG_BIG))
    egc = jnp.exp(gc)
    k_beta = k * beta
    kq = jnp.einsum("bmd,bnd->bmn", jnp.concatenate([k_beta, q], axis=1).astype(BF16),
                    k.astype(BF16), preferred_element_type=F32)
    a = jnp.where(strict, kq[:, 0:c] * decay, 0.0)
    intra = kq[:, c:2 * c] * decay
    inv = eye - a
    a_hi, a_lo = _split_bf16(a)
    pw = _bdot3_parts(a_hi, a_lo, a_hi, a_lo)
    levels = int(math.log2(c)) - 1
    for lvl in range(levels):
        inv_hi, inv_lo = _split_bf16(inv)
        pw_hi, pw_lo = _split_bf16(pw)
        if lvl < levels - 1:
            both = _bdot3_parts(jnp.concatenate([inv_hi, pw_hi], axis=1),
                                jnp.concatenate([inv_lo, pw_lo], axis=1), pw_hi, pw_lo)
            inv = inv + both[:, 0:c]
            pw = both[:, c:2 * c]
        else:
            inv = inv + _bdot3_parts(inv_hi, inv_lo, pw_hi, pw_lo)
    uw = _bdot3(inv, jnp.concatenate([v * beta, k_beta * egc], axis=2))
    u_ref[...] = uw[:, :, 0:dh].reshape(nbat * c, dh)
    wq_ref[...] = jnp.concatenate([uw[:, :, dh:2 * dh], q * egc], axis=1).astype(BF16).reshape(
        nbat * 2 * c, dh)
    k_dec = k * jnp.exp(g_last - gc)
    for b in range(nbat):
        ik_ref[b * (c + dh):(b + 1) * (c + dh), :] = jnp.concatenate(
            [intra[b], k_dec[b].T], axis=0).astype(BF16)
    gl_ref[...] = jnp.broadcast_to(jnp.exp(g_last), (nbat, SUBLANES, dh)).reshape(nbat * SUBLANES, dh)

    def step(n, states):
        rows = lambda g, size: pl.ds(pl.multiple_of((g * nb + n) * size, size), size)
        ws = [jnp.dot(wq_ref[rows(g, 2 * c), :], states[g].astype(BF16), preferred_element_type=F32)
              for g in range(hps)]
        v_new = [u_ref[rows(g, c), :] - ws[g][0:c] for g in range(hps)]
        iv = [jnp.dot(ik_ref[pl.ds(pl.multiple_of((g * nb + n) * (c + dh), c), c + dh), :],
                      v_new[g].astype(BF16), preferred_element_type=F32) for g in range(hps)]
        new_states = []
        r0 = pl.multiple_of(n * c, c)
        for g in range(hps):
            out = ws[g][c:2 * c] + iv[g][0:c]
            gl = gl_ref[pl.ds(pl.multiple_of((g * nb + n) * SUBLANES, SUBLANES), 1), :]
            new_states.append(states[g] * gl + iv[g][c:c + dh])
            ms = jnp.mean(out * out, axis=-1, keepdims=True)
            o = out * lax.rsqrt(ms + NORM_EPS) * nw_ref[...]
            o_ref[pl.ds(r0, c), g * dh:(g + 1) * dh] = (
                o * _silu(z_ref[pl.ds(r0, c), g * dh:(g + 1) * dh])).astype(o_ref.dtype)
        return tuple(new_states)

    states = lax.fori_loop(0, nb, step, tuple(st_ref[g] for g in range(hps)))
    for g in range(hps):
        st_ref[g] = states[g]


def _gdn_mixer(qkvz, tail, conv_w, gate_p, norm_w, bsz, s, heads):
    hps = math.gcd(GDN_HEADS_PER_STEP, heads)
    tb = min(GDN_TIME_BLOCK, s)
    nt = s // tb
    hblocks = heads // hps
    width = hps * HEAD_DIM
    nbat = hps * (tb // GDN_CHUNK)
    col = lambda off: pl.BlockSpec((tb, width), lambda b, h, t: (b * nt + t, off + h))
    return pl.pallas_call(
        functools.partial(_gdn_kernel, tb=tb, heads=heads, hps=hps),
        grid=(bsz, hblocks, nt),
        in_specs=[col(0), col(hblocks), col(2 * hblocks), col(3 * hblocks),
                  pl.BlockSpec((tb, LANES), lambda b, h, t: (b * nt + t, 0)),
                  pl.BlockSpec((3, GDN_CONV, width), lambda b, h, t: (0, 0, h)),
                  pl.BlockSpec((2, LANES), lambda b, h, t: (0, 0)),
                  pl.BlockSpec((1, HEAD_DIM), lambda b, h, t: (0, 0))],
        out_specs=pl.BlockSpec((tb, width), lambda b, h, t: (b * nt + t, h)),
        out_shape=jax.ShapeDtypeStruct((bsz * s, heads * HEAD_DIM), BF16),
        scratch_shapes=[pltpu.VMEM((tb + SUBLANES, width), F32)] * 3
                       + [pltpu.VMEM((hps, HEAD_DIM, HEAD_DIM), F32),
                          pltpu.VMEM((nbat * GDN_CHUNK, HEAD_DIM), F32),
                          pltpu.VMEM((nbat * 2 * GDN_CHUNK, HEAD_DIM), BF16),
                          pltpu.VMEM((nbat * (GDN_CHUNK + HEAD_DIM), GDN_CHUNK), BF16),
                          pltpu.VMEM((nbat * SUBLANES, HEAD_DIM), F32)],
        compiler_params=_cparams("parallel", "parallel", "arbitrary"),
        name="gdn_mixer",
    )(qkvz, qkvz, qkvz, qkvz, tail, conv_w, gate_p, norm_w)


def _gdn_layer(h, hn, ssq, w_all, layer, w_tail, conv_w, a_log, dt_bias, norm_w, wo_all, ple_gain,
               bsz, s):
    width = wo_all.shape[1]
    heads = width // HEAD_DIM
    qkvz = _matmul(hn, w_all, w_t=4 * width, w_row_blk=layer, out_dtype=F32, row_ssq=ssq,
                   tm=MM_TM_WIDE)
    tail = _matmul(hn, w_tail, w_t=LANES, out_dtype=F32, row_ssq=ssq)
    cw = conv_w.astype(F32).reshape(GDN_CONV, 3, width).transpose(1, 0, 2)
    lane_pad = lambda x: jnp.pad(x.astype(F32), (heads, LANES - 2 * heads))
    gate_p = jnp.stack([lane_pad(a_log), lane_pad(dt_bias)])
    gated = _gdn_mixer(qkvz, tail, cw, gate_p, norm_w.astype(F32).reshape(1, HEAD_DIM),
                       bsz, s, heads)
    return _matmul(gated, wo_all, w_row_blk=layer, out_dtype=F32, mode="residual",
                   extras=[(h, "tile", {})], norm_gain=ple_gain)


def _ssm_prep_kernel(lam_ref, bf_ref, cf_ref, bd_ref, wz_ref, wyt_ref, laml_ref):
    ns = SSM_PACK * SSM_STATE
    lam_re, lam_im, step_log = lam_ref[0, 0:1, :], lam_ref[0, 1:2, :], lam_ref[0, 2:3, :]
    step = jnp.exp(step_log)
    mag = jnp.exp(lam_re * step)
    lb_re, lb_im = mag * jnp.cos(lam_im * step), mag * jnp.sin(lam_im * step)
    den = lam_re * lam_re + lam_im * lam_im
    num_re = lb_re - 1.0
    zoh_re = (num_re * lam_re + lb_im * lam_im) / den
    zoh_im = (lb_im * lam_re - num_re * lam_im) / den
    b_re, b_im = bf_ref[0, 0], bf_ref[0, 1]
    bb_re = zoh_re * b_re - zoh_im * b_im
    bb_im = zoh_re * b_im + zoh_im * b_re
    c_re, c_im = cf_ref[0, 0], cf_ref[0, 1]
    cc_hi, cc_lo = _split_bf16(jnp.concatenate([c_re, -c_im], axis=1))
    nt_dot = lambda x, y: lax.dot_general(x, y, (((1,), (1,)), ((), ())), preferred_element_type=F32)

    bd_ref[0, 0, LANES:2 * LANES, 0:LANES] = jnp.zeros((LANES, LANES), bd_ref.dtype)
    pw_re, pw_im = jnp.ones_like(lb_re), jnp.zeros_like(lb_re)
    for d in range(SSM_L + 1):
        if d < SSM_L:
            a_re = pw_re * bb_re - pw_im * bb_im
            a_im = pw_re * bb_im + pw_im * bb_re
            a_d = jnp.concatenate([a_re, a_im], axis=1)
            a_hi, a_lo = _split_bf16(a_d)
            blk = (nt_dot(a_hi, cc_hi) + nt_dot(a_hi, cc_lo) + nt_dot(a_lo, cc_hi)).astype(bd_ref.dtype)
            d2 = d // 2
            if d % 2 == 0:
                bd_ref[0, d2, 0:LANES, 0:LANES] = blk
                bd_ref[0, d2, LANES:2 * LANES, LANES:2 * LANES] = blk
            else:
                bd_ref[0, d2, 0:LANES, LANES:2 * LANES] = blk
                if d2 + 1 < SSM_L // 2:
                    bd_ref[0, d2 + 1, LANES:2 * LANES, 0:LANES] = blk
            tin = SSM_L - 1 - d
            wz_ref[0, tin * LANES:(tin + 1) * LANES, :] = a_d.astype(wz_ref.dtype)
        if d >= 1:
            y_re = pw_re * c_re - pw_im * c_im
            y_im = pw_re * c_im + pw_im * c_re
            wyt_ref[0, (d - 1) * LANES:d * LANES, :] = jnp.concatenate(
                [y_re, -y_im], axis=1).astype(wyt_ref.dtype)
        if d == SSM_L:
            laml_ref[0] = jnp.concatenate([pw_re, pw_im], axis=1)
        pw_re, pw_im = pw_re * lb_re - pw_im * lb_im, pw_re * lb_im + pw_im * lb_re


def _ssm_prep(lam, bfull, cfull):
    nsg = lam.shape[0]
    ns = SSM_PACK * SSM_STATE
    rows = SSM_L * LANES
    return pl.pallas_call(
        _ssm_prep_kernel,
        grid=(nsg,),
        in_specs=[pl.BlockSpec((1, 3, ns), lambda g: (g, 0, 0)),
                  pl.BlockSpec((1, 2, LANES, ns), lambda g: (g, 0, 0, 0)),
                  pl.BlockSpec((1, 2, LANES, ns), lambda g: (g, 0, 0, 0))],
        out_specs=[pl.BlockSpec((1, SSM_L // 2, 2 * LANES, 2 * LANES), lambda g: (g, 0, 0, 0)),
                   pl.BlockSpec((1, rows, 2 * ns), lambda g: (g, 0, 0)),
                   pl.BlockSpec((1, rows, 2 * ns), lambda g: (g, 0, 0)),
                   pl.BlockSpec((1, 1, 2 * ns), lambda g: (g, 0, 0))],
        out_shape=[jax.ShapeDtypeStruct((nsg, SSM_L // 2, 2 * LANES, 2 * LANES), BF16),
                   jax.ShapeDtypeStruct((nsg, rows, 2 * ns), BF16),
                   jax.ShapeDtypeStruct((nsg, rows, 2 * ns), BF16),
                   jax.ShapeDtypeStruct((nsg, 1, 2 * ns), F32)],
        compiler_params=_cparams("parallel"),
        name="ssm_prep",
    )(lam, bfull, cfull)


def _ssm_scan_kernel(u_ref, bd_ref, wz_ref, wyt_ref, laml_ref, d_ref, o_ref,
                     x_ref, z_ref, xp_ref, st_ref, *, tc):
    ns = SSM_PACK * SSM_STATE

    @pl.when(pl.program_id(2) == 0)
    def _():
        st_ref[...] = jnp.zeros_like(st_ref)

    for tau in range(SSM_L):
        x_ref[:, tau * LANES:(tau + 1) * LANES] = u_ref[pl.ds(tau, tc, stride=SSM_L), :].astype(BF16)

    z_ref[...] = jnp.dot(x_ref[...], wz_ref[0], preferred_element_type=F32)

    a_re, a_im = laml_ref[0, :, 0:ns], laml_ref[0, :, ns:2 * ns]

    def row(r, carry):
        s_re, s_im = carry
        xp_ref[pl.ds(r, 1), :] = jnp.concatenate([s_re, s_im], axis=1)
        zr = z_ref[pl.ds(r, 1), :]
        n_re = a_re * s_re - a_im * s_im + zr[:, 0:ns]
        n_im = a_re * s_im + a_im * s_re + zr[:, ns:2 * ns]
        return n_re, n_im

    s_re, s_im = lax.fori_loop(0, tc, row, (st_ref[:, 0:ns], st_ref[:, ns:2 * ns]), unroll=4)
    st_ref[...] = jnp.concatenate([s_re, s_im], axis=1)

    xprev = xp_ref[...].astype(BF16)
    pair = 2 * LANES
    for t2 in range(SSM_L // 2):
        y2 = lax.dot_general(xprev, wyt_ref[0, t2 * pair:(t2 + 1) * pair, :],
                             (((1,), (1,)), ((), ())), preferred_element_type=F32)
        for tin in range(t2 + 1):
            y2 = y2 + jnp.dot(x_ref[:, tin * pair:(tin + 1) * pair], bd_ref[0, t2 - tin],
                              preferred_element_type=F32)
        for half in range(2):
            tau = 2 * t2 + half
            y = y2[:, half * LANES:(half + 1) * LANES]
            y = y + d_ref[...] * u_ref[pl.ds(tau, tc, stride=SSM_L), :]
            y = 0.5 * y * (1.0 + jnp.tanh(math.sqrt(2.0 / math.pi) * (y + 0.044715 * (y * y * y))))
            o_ref[pl.ds(tau, tc, stride=SSM_L), :] = y


def _ssm_scan(uz, bd, wz, wyt, laml, d_skip, bsz, s):
    e = d_skip.shape[1]
    nsg = e // LANES
    ns = SSM_PACK * SSM_STATE
    tb = min(4096, s)
    nt = s // tb
    tc = tb // SSM_L
    rows = SSM_L * LANES
    return pl.pallas_call(
        functools.partial(_ssm_scan_kernel, tc=tc),
        grid=(nsg, bsz, nt),
        in_specs=[pl.BlockSpec((tb, LANES), lambda g, b, t: (b * nt + t, g)),
                  pl.BlockSpec((1, SSM_L // 2, 2 * LANES, 2 * LANES), lambda g, b, t: (g, 0, 0, 0)),
                  pl.BlockSpec((1, rows, 2 * ns), lambda g, b, t: (g, 0, 0)),
                  pl.BlockSpec((1, rows, 2 * ns), lambda g, b, t: (g, 0, 0)),
                  pl.BlockSpec((1, 1, 2 * ns), lambda g, b, t: (g, 0, 0)),
                  pl.BlockSpec((1, LANES), lambda g, b, t: (0, g))],
        out_specs=pl.BlockSpec((tb, LANES), lambda g, b, t: (b * nt + t, g)),
        out_shape=jax.ShapeDtypeStruct((uz.shape[0], e), F32),
        scratch_shapes=[pltpu.VMEM((tc, rows), BF16), pltpu.VMEM((tc, 2 * ns), F32),
                        pltpu.VMEM((tc, 2 * ns), F32), pltpu.VMEM((1, 2 * ns), F32)],
        compiler_params=_cparams("parallel", "parallel", "arbitrary"),
        name="ssm_scan",
    )(uz, bd, wz, wyt, laml, d_skip)


def _ssm_layer(h, hn, ssq, w_all, layer, lam_re, lam_im, b_re, b_im, c_re, c_im, log_step, d_skip,
               wg_all, b_glu, wo_all, ple_gain, bsz, s):
    e = wo_all.shape[1]
    groups, nstate = lam_re.shape
    assert nstate == SSM_STATE and e == groups * SSM_GROUP and groups % SSM_PACK == 0
    nsg = groups // SSM_PACK
    ns = SSM_PACK * SSM_STATE
    lam = jnp.stack([lam_re.astype(F32).reshape(nsg, ns), lam_im.astype(F32).reshape(nsg, ns),
                     jnp.repeat(log_step.astype(F32), SSM_STATE).reshape(nsg, ns)], axis=1)
    eye = jnp.eye(SSM_PACK, dtype=F32)

    def expand_b(b):
        return jnp.einsum("sgpm,gh->sgmhp", b.astype(F32).reshape(nsg, SSM_PACK, SSM_STATE, SSM_GROUP),
                          eye).reshape(nsg, LANES, ns)

    def expand_c(c):
        return jnp.einsum("sgnp,gh->sgnhp", c.astype(F32).reshape(nsg, SSM_PACK, SSM_GROUP, SSM_STATE),
                          eye).reshape(nsg, LANES, ns)

    bfull = jnp.stack([expand_b(b_re), expand_b(b_im)], axis=1)
    cfull = jnp.stack([expand_c(c_re), expand_c(c_im)], axis=1)
    bd, wz, wyt, laml = _ssm_prep(lam, bfull, cfull)

    uz = _matmul(hn, w_all, w_row_blk=layer, out_dtype=F32, row_ssq=ssq, tm=MM_TM_WIDE)
    y = _ssm_scan(uz, bd, wz, wyt, laml, d_skip.astype(F32).reshape(1, e), bsz, s)
    y2 = _matmul(y, wg_all, w_row_blk=layer, out_dtype=BF16, mode="glu", cast_a=True,
                 extras=[(y, "tile", {}), (uz, "tile", {"col_off": e}),
                         (b_glu.astype(F32).reshape(1, e), "row", {})])
    return _matmul(y2, wo_all, w_row_blk=layer, out_dtype=F32, mode="residual",
                   extras=[(h, "tile", {})], norm_gain=ple_gain)


def kernel(x, p, norm_mix, fox_w_in, fox_b_f, fox_w_out, gdn_w_in, gdn_conv, gdn_a_log, gdn_dt_bias, gdn_norm, gdn_w_out, ssm_w_in, ssm_lam_re, ssm_lam_im, ssm_b_re, ssm_b_im, ssm_c_re, ssm_c_im, ssm_log_step, ssm_d, ssm_w_glu, ssm_b_glu, ssm_w_out, norm_ple, ple_w_proj, ple_w_gate, final_norm):
    bsz, s, d = x.shape
    depth, m = p.shape[0], bsz * s
    pdim = p.shape[-1]
    fox_width, gdn_width = fox_w_out.shape[1], gdn_w_out.shape[1]
    fox_wt, gdn_wt = jnp.swapaxes(fox_w_in, 1, 2), jnp.swapaxes(gdn_w_in, 1, 2)
    pad_rows = lambda t: jnp.pad(t, ((0, 0), (0, LANES - t.shape[1]), (0, 0)))
    fox_w = _cast_bf16(fox_wt, 4 * fox_width)
    fox_tail = _cast_bf16(pad_rows(fox_wt[:, 4 * fox_width:]))
    gdn_w = _cast_bf16(gdn_wt, 4 * gdn_width)
    gdn_tail = _cast_bf16(pad_rows(gdn_wt[:, 4 * gdn_width:]))
    fox_wo, gdn_wo = _cast_bf16(fox_w_out), _cast_bf16(gdn_w_out)
    ssm_w, ssm_wg, ssm_wo = _cast_bf16(ssm_w_in), _cast_bf16(ssm_w_glu), _cast_bf16(ssm_w_out)
    ple_wg, ple_wp = _cast_bf16(ple_w_gate), _cast_bf16(ple_w_proj)
    p_bf = _cast_bf16(p.reshape(depth, m, pdim))

    h = x.reshape(m, d).astype(F32)
    hn, ssq = _rmsnorm(h, norm_mix[0], BF16), None
    for i in range(depth):
        kind, j = i % 3, i // 3
        if kind == 0:
            h, hg, hssq = _fox_layer(h, hn, ssq, fox_w, j, fox_tail[j * LANES:(j + 1) * LANES],
                                     fox_b_f[j], fox_wo, norm_ple[i], bsz, s)
        elif kind == 1:
            h, hg, hssq = _gdn_layer(h, hn, ssq, gdn_w, j, gdn_tail[j * LANES:(j + 1) * LANES],
                                     gdn_conv[j], gdn_a_log[j], gdn_dt_bias[j], gdn_norm[j], gdn_wo,
                                     norm_ple[i], bsz, s)
        else:
            h, hg, hssq = _ssm_layer(h, hn, ssq, ssm_w, j, ssm_lam_re[j], ssm_lam_im[j], ssm_b_re[j],
                                     ssm_b_im[j], ssm_c_re[j], ssm_c_im[j], ssm_log_step[j], ssm_d[j],
                                     ssm_wg, ssm_b_glu[j], ssm_wo, norm_ple[i], bsz, s)
        extras = [(h, "tile", {}), (p_bf, "rows", {"row_blk": i * (m // min(MM_TM, m))}),
                  (ple_wp, "cols", {"row_blk": i, "rows": pdim})]
        outs = _matmul(hg, ple_wg, w_row_blk=i, out_dtype=F32, mode="ple", row_ssq=hssq, extras=extras,
                       norm_gain=norm_mix[i + 1] if i + 1 < depth else None)
        h, hn, ssq = outs if i + 1 < depth else (outs, None, None)
    return _rmsnorm(h, final_norm, F32).reshape(bsz, s, d)
```

```python
import functools
import math

import jax
import jax.numpy as jnp
from jax import lax
from jax.experimental import pallas as pl
from jax.experimental.pallas import tpu as pltpu

F32 = jnp.float32
BF16 = jnp.bfloat16

LANES = 128
SUBLANES = 8
VMEM_LIMIT_BYTES = 56 * 1024 * 1024

CAST_BLOCK_ELEMS = 1 << 20
A_PREP_ROWS = 32
MM_TM = 512
MM_TM_WIDE = 1024
MM_TN = 1024
NORM_EPS = 1e-6
LOG2E = 1.4426950408889634
HEAD_DIM = 128
FOX_BLOCK = 1024
GDN_CHUNK = 64
GDN_CONV = 4
GDN_HEADS_PER_STEP = 16
GDN_TIME_BLOCK = 128
SSM_GROUP = 16
SSM_STATE = 64
SSM_PACK = LANES // SSM_GROUP
SSM_L = 16
NEG_BIG = -1e30

HIGHEST = lax.Precision.HIGHEST


def _cparams(*sem):
    return pltpu.CompilerParams(dimension_semantics=sem, vmem_limit_bytes=VMEM_LIMIT_BYTES)


def _sigmoid(x):
    return 0.5 * jnp.tanh(0.5 * x) + 0.5


def _silu(x):
    hx = 0.5 * x
    return hx + hx * jnp.tanh(hx)


def _softplus(x):
    return jnp.maximum(x, 0.0) + jnp.log(1.0 + jnp.exp(-jnp.abs(x)))


def _rmsnorm_kernel(x_ref, g_ref, o_ref):
    x = x_ref[...]
    ms = jnp.mean(x * x, axis=-1, keepdims=True)
    o_ref[...] = (x * lax.rsqrt(ms + NORM_EPS) * g_ref[...]).astype(o_ref.dtype)


def _rmsnorm(x, g, out_dtype):
    m, d = x.shape
    tm = min(256, m)
    return pl.pallas_call(
        _rmsnorm_kernel,
        grid=(m // tm,),
        in_specs=[pl.BlockSpec((tm, d), lambda i: (i, 0)),
                  pl.BlockSpec((1, d), lambda i: (0, 0))],
        out_specs=pl.BlockSpec((tm, d), lambda i: (i, 0)),
        out_shape=jax.ShapeDtypeStruct((m, d), out_dtype),
        compiler_params=_cparams("parallel"),
        name="rmsnorm",
    )(x, g.reshape(1, d).astype(F32))


def _cast_kernel(x_ref, o_ref):
    o_ref[...] = x_ref[...].astype(o_ref.dtype)


def _cast_bf16(x, rows=None):
    nl, r, c = x.shape
    rows = r if rows is None else rows
    tc = min(2048, c)
    tr = min(rows, CAST_BLOCK_ELEMS // tc)
    assert rows % tr == 0 and c % tc == 0
    out = pl.pallas_call(
        _cast_kernel,
        grid=(nl, rows // tr, c // tc),
        in_specs=[pl.BlockSpec((1, tr, tc), lambda l, i, j: (l, i, j))],
        out_specs=pl.BlockSpec((1, tr, tc), lambda l, i, j: (l, i, j)),
        out_shape=jax.ShapeDtypeStruct((nl, rows, c), BF16),
        compiler_params=_cparams("parallel", "parallel", "parallel"),
        name="cast_bf16",
    )(x)
    return out.reshape(nl * rows, c)


_N_EXTRA = {"plain": 0, "colscale": 1, "residual": 1, "ple": 3, "glu": 3}


def _mm_kernel(*refs, mode, cast_a, w_t, has_ssq, emit_norm, inv_k):
    it = iter(refs)
    a_ref, w_ref = next(it), next(it)
    ssq_ref = next(it) if has_ssq else None
    extra = [next(it) for _ in range(_N_EXTRA[mode])]
    gain_ref = next(it) if emit_norm else None
    o_ref = next(it)
    hg_ref, ssq_out_ref = (next(it), next(it)) if emit_norm else (None, None)
    if cast_a:
        a_scr = next(it)

        @pl.when(pl.program_id(1) == 0)
        def _():
            def prep(r, carry):
                rows = pl.ds(pl.multiple_of(r * A_PREP_ROWS, A_PREP_ROWS), A_PREP_ROWS)
                a_scr[rows, :] = a_ref[rows, :].astype(BF16)
                return carry

            lax.fori_loop(0, a_ref.shape[0] // A_PREP_ROWS, prep, 0)

        a = a_scr[...]
    else:
        a = a_ref[...]

    acc = lax.dot_general(a, w_ref[...], (((1,), (1 if w_t else 0,)), ((), ())),
                          preferred_element_type=F32)
    if has_ssq:
        parts = ssq_ref[...]
        tot = parts[:, 0:LANES]
        for t in range(1, parts.shape[1] // LANES):
            tot = tot + parts[:, t * LANES:(t + 1) * LANES]
        acc = acc * lax.rsqrt(tot[:, 0:1] * inv_k + NORM_EPS)
    if mode == "plain":
        out = acc
    elif mode == "colscale":
        out = acc * extra[0][...]
    elif mode == "residual":
        out = extra[0][...] + acc
    elif mode == "ple":
        res_ref, p_ref, wp_ref = extra
        emb = jnp.dot(p_ref[...], wp_ref[...], preferred_element_type=F32)
        out = res_ref[...] + _sigmoid(acc) * emb
    elif mode == "glu":
        y_ref, z_ref, b_ref = extra
        out = y_ref[...] * _sigmoid(acc + b_ref[...]) * _silu(z_ref[...])
    else:
        raise ValueError(mode)
    o_ref[...] = out.astype(o_ref.dtype)
    if emit_norm:
        hg_ref[...] = (out * gain_ref[...]).astype(hg_ref.dtype)
        ssq_out_ref[...] = jnp.broadcast_to(jnp.sum(out * out, axis=-1, keepdims=True),
                                            ssq_out_ref.shape)


def _extra_spec(arr, kind, tm, tn, row_blk=0, col_off=0, rows=None):
    if kind == "tile":
        return pl.BlockSpec((tm, tn), lambda i, j: (i + row_blk, j + col_off // tn))
    if kind == "row":
        return pl.BlockSpec((1, tn), lambda i, j: (0, j))
    if kind == "rows":
        return pl.BlockSpec((tm, arr.shape[1]), lambda i, j: (i + row_blk, 0))
    if kind == "cols":
        return pl.BlockSpec((rows, tn), lambda i, j: (row_blk, j))
    raise ValueError(kind)


def _matmul(a, w, *, out_dtype, n=None, mode="plain", extras=(), cast_a=False, row_ssq=None,
            norm_gain=None, w_t=None, w_row_blk=0, w_col_off=0, tm=MM_TM, tn=MM_TN):
    m, kdim = a.shape
    n = (w_t if w_t else w.shape[1]) if n is None else n
    tm = min(tm, m)
    tn = functools.reduce(math.gcd, [tn, n, w_col_off, w_t or 0])
    assert m % tm == 0 and tn % LANES == 0 and len(extras) == _N_EXTRA[mode]
    operands = [a, w]
    if w_t:
        first = (w_row_blk * w_t + w_col_off) // tn
        w_spec = pl.BlockSpec((tn, kdim), lambda i, j: (first + j, 0))
    else:
        w_spec = pl.BlockSpec((kdim, tn), lambda i, j: (w_row_blk, j + w_col_off // tn))
    in_specs = [pl.BlockSpec((tm, kdim), lambda i, j: (i, 0)), w_spec]
    if row_ssq is not None:
        operands.append(row_ssq)
        in_specs.append(pl.BlockSpec((tm, row_ssq.shape[1]), lambda i, j: (i, 0)))
    for arr, kind, opts in extras:
        assert opts.get("col_off", 0) % tn == 0
        operands.append(arr)
        in_specs.append(_extra_spec(arr, kind, tm, tn, **opts))
    tile = pl.BlockSpec((tm, tn), lambda i, j: (i, j))
    out_specs, out_shape = tile, jax.ShapeDtypeStruct((m, n), out_dtype)
    if norm_gain is not None:
        operands.append(norm_gain.astype(F32).reshape(1, n))
        in_specs.append(pl.BlockSpec((1, tn), lambda i, j: (0, j)))
        out_specs = [tile, tile, pl.BlockSpec((tm, LANES), lambda i, j: (i, j))]
        out_shape = [out_shape, jax.ShapeDtypeStruct((m, n), BF16),
                     jax.ShapeDtypeStruct((m, n // tn * LANES), F32)]
    scratch = [pltpu.VMEM((tm, kdim), BF16)] if cast_a else []
    return pl.pallas_call(
        functools.partial(_mm_kernel, mode=mode, cast_a=cast_a, w_t=bool(w_t),
                          has_ssq=row_ssq is not None,
                          emit_norm=norm_gain is not None, inv_k=1.0 / kdim),
        grid=(m // tm, n // tn),
        in_specs=in_specs,
        out_specs=out_specs,
        out_shape=out_shape,
        scratch_shapes=scratch,
        compiler_params=_cparams("parallel", "arbitrary"),
        name="matmul_" + mode,
    )(*operands)


def _fox_cum_kernel(f_ref, b_ref, o_ref, carry_ref, *, ts):
    @pl.when(pl.program_id(1) == 0)
    def _():
        carry_ref[...] = jnp.zeros_like(carry_ref)

    x = f_ref[...] + b_ref[...]
    log_f = jnp.minimum(x, 0.0) - jnp.log(1.0 + jnp.exp(-jnp.abs(x)))
    r = lax.broadcasted_iota(jnp.int32, (ts, ts), 0)
    c = lax.broadcasted_iota(jnp.int32, (ts, ts), 1)
    lower = (c <= r).astype(F32)
    cum = jnp.dot(lower, log_f, precision=HIGHEST, preferred_element_type=F32) + carry_ref[...]
    carry_ref[...] = cum[ts - 1:ts, :]
    o_ref[0] = cum.T


def _fox_cum(f_logit, b_f, bsz, s):
    ts = min(512, s)
    return pl.pallas_call(
        functools.partial(_fox_cum_kernel, ts=ts),
        grid=(bsz, s // ts),
        in_specs=[pl.BlockSpec((ts, LANES), lambda b, t: (b * (s // ts) + t, 0)),
                  pl.BlockSpec((1, LANES), lambda b, t: (0, 0))],
        out_specs=pl.BlockSpec((1, LANES, ts), lambda b, t: (b, 0, t)),
        out_shape=jax.ShapeDtypeStruct((bsz, LANES, s), F32),
        scratch_shapes=[pltpu.VMEM((1, LANES), F32)],
        compiler_params=_cparams("parallel", "arbitrary"),
        name="fox_cum",
    )(f_logit, b_f)


def _fox_attn_kernel(q_ref, k_ref, v_ref, c_ref, z_ref, o_ref, sa_ref, sb_ref, *, tq):
    tk = tq // 2
    i = pl.program_id(2)
    q = q_ref[...]
    c0 = c_ref[0, 0, :, pl.ds(pl.multiple_of(i * tq, tq), LANES)][:, 0:1]

    def scores(qrows, j):
        start = j * tk if isinstance(j, int) else pl.multiple_of(j * tk, tk)
        s = lax.dot_general(qrows, k_ref[pl.ds(start, tk), :], (((1,), (1,)), ((), ())),
                            preferred_element_type=F32)
        return s + (c0 - c_ref[0, 0, :, pl.ds(start, tk)]) * LOG2E

    def update(s, j, m, acc, width=1):
        rows = width * tk
        start = j * tk if isinstance(j, int) else pl.multiple_of(j * tk, tk)
        vb = jnp.concatenate([v_ref[pl.ds(start, rows), :], jnp.ones((rows, HEAD_DIM), BF16)], axis=1)
        m_new = jnp.maximum(m, jnp.max(s, axis=-1, keepdims=True))
        p = jnp.exp2(s - m_new)
        acc = jnp.exp2(m - m_new) * acc + jnp.dot(p.astype(BF16), vb, preferred_element_type=F32)
        return m_new, acc

    sa_ref[...] = scores(q, 0)

    def pair(t, carry):
        m, acc = carry
        sb_ref[...] = scores(q, 2 * t + 1)
        m, acc = update(sa_ref[...], 2 * t, m, acc)
        sa_ref[...] = scores(q, 2 * t + 2)
        return update(sb_ref[...], 2 * t + 1, m, acc)

    init = (jnp.full((tq, 1), NEG_BIG, F32), jnp.zeros((tq, 2 * HEAD_DIM), F32))
    carry = lax.fori_loop(0, i // 2, lambda t2, cr: pair(2 * t2 + 1, pair(2 * t2, cr)), init)
    m, acc = lax.cond(i % 2 == 1, lambda cr: pair(i - 1, cr), lambda cr: cr, carry)

    r = lax.broadcasted_iota(jnp.int32, (tk, tk), 0)
    c = lax.broadcasted_iota(jnp.int32, (tk, tk), 1)
    s_bot = jnp.concatenate([sa_ref[tk:tq, :], jnp.where(c <= r, scores(q[tk:tq], 2 * i + 1), NEG_BIG)],
                            axis=1)
    _, acc_top = update(jnp.where(c <= r, sa_ref[0:tk, :], NEG_BIG), 2 * i, m[0:tk], acc[0:tk])
    _, acc_bot = update(s_bot, 2 * i, m[tk:tq], acc[tk:tq], width=2)

    def finish(a, rows):
        o = a[:, 0:HEAD_DIM] / a[:, HEAD_DIM:HEAD_DIM + 1]
        o_ref[rows, :] = (o * _silu(z_ref[rows, :])).astype(o_ref.dtype)

    finish(acc_top, slice(0, tk))
    finish(acc_bot, slice(tk, tq))


def _fox_attention(qkv, cum4, z, bsz, s, heads):
    tq = min(FOX_BLOCK, s)
    nq = s // tq
    return pl.pallas_call(
        functools.partial(_fox_attn_kernel, tq=tq),
        grid=(bsz, heads, nq),
        in_specs=[pl.BlockSpec((tq, HEAD_DIM), lambda b, h, i: (b * nq + i, h)),
                  pl.BlockSpec((s, HEAD_DIM), lambda b, h, i: (b, heads + h)),
                  pl.BlockSpec((s, HEAD_DIM), lambda b, h, i: (b, 2 * heads + h)),
                  pl.BlockSpec((1, 1, 1, s), lambda b, h, i: (b, h, 0, 0)),
                  pl.BlockSpec((tq, HEAD_DIM), lambda b, h, i: (b * nq + i, h))],
        out_specs=pl.BlockSpec((tq, HEAD_DIM), lambda b, h, i: (b * nq + i, h)),
        out_shape=jax.ShapeDtypeStruct((bsz * s, heads * HEAD_DIM), BF16),
        scratch_shapes=[pltpu.VMEM((tq, tq // 2), F32)] * 2,
        compiler_params=_cparams("parallel", "parallel", "arbitrary"),
        name="fox_attn",
    )(qkv, qkv, qkv, cum4, z)


def _fox_layer(h, hn, ssq, w_all, layer, w_tail, b_f, wo_all, ple_gain, bsz, s):
    width = wo_all.shape[1]
    heads = width // HEAD_DIM
    q_scale = jnp.concatenate([jnp.full((width,), HEAD_DIM ** -0.5 * LOG2E, F32),
                               jnp.ones((2 * width,), F32)]).reshape(1, 3 * width)
    qkv = _matmul(hn, w_all, n=3 * width, w_t=4 * width, w_row_blk=layer, out_dtype=BF16,
                  mode="colscale", extras=[(q_scale, "row", {})], row_ssq=ssq, tm=MM_TM_WIDE)
    z = _matmul(hn, w_all, n=width, w_t=4 * width, w_row_blk=layer, w_col_off=3 * width,
                out_dtype=F32, row_ssq=ssq)
    f_logit = _matmul(hn, w_tail, w_t=LANES, out_dtype=F32, row_ssq=ssq)
    b_row = jnp.pad(b_f.astype(F32), (0, LANES - heads)).reshape(1, LANES)
    cum = _fox_cum(f_logit, b_row, bsz, s)
    gated = _fox_attention(qkv, cum.reshape(bsz, LANES, 1, s), z, bsz, s, heads)
    return _matmul(gated, wo_all, w_row_blk=layer, out_dtype=F32, mode="residual",
                   extras=[(h, "tile", {})], norm_gain=ple_gain)


def _bdot(a, b):
    return jnp.einsum("bmk,bkn->bmn", a, b, preferred_element_type=F32)


def _split_bf16(x):
    hi = x.astype(BF16)
    return hi, (x - hi.astype(F32)).astype(BF16)


def _bdot3_parts(a_hi, a_lo, b_hi, b_lo):
    return _bdot(a_hi, b_hi) + _bdot(a_hi, b_lo) + _bdot(a_lo, b_hi)


def _bdot3(a, b):
    return _bdot3_parts(*_split_bf16(a), *_split_bf16(b))


def _bdot_exact_lhs(a_bf16, b):
    b_hi = b.astype(BF16)
    b_mid, b_lo = _split_bf16(b - b_hi.astype(F32))
    return _bdot(a_bf16, b_hi) + _bdot(a_bf16, b_mid) + _bdot(a_bf16, b_lo)


def _gdn_kernel(q_ref, k_ref, v_ref, z_ref, t_ref, cw_ref, gp_ref, nw_ref, o_ref,
                xq_ref, xk_ref, xv_ref, st_ref, u_ref, wq_ref, ik_ref, gl_ref, *, tb, heads, hps):
    hb = pl.program_id(1)
    c, dh = GDN_CHUNK, HEAD_DIM
    nb = tb // c
    nbat = hps * nb
    pad = SUBLANES

    @pl.when(pl.program_id(2) == 0)
    def _():
        for r in (xq_ref, xk_ref, xv_ref):
            r[0:pad, :] = jnp.zeros((pad, hps * dh), F32)
        st_ref[...] = jnp.zeros_like(st_ref)

    def conv(x_ref, xs_ref, col):
        x = x_ref[...]
        xs_ref[pad:pad + tb, :] = x
        xs = xs_ref[...]
        y = x * cw_ref[col, GDN_CONV - 1:GDN_CONV, :]
        for j in range(GDN_CONV - 1):
            shifted = pltpu.roll(xs, GDN_CONV - 1 - j, axis=0)[pad:pad + tb, :]
            y = y + shifted * cw_ref[col, j:j + 1, :]
        xs_ref[0:pad, :] = x[tb - pad:tb, :]
        return _silu(y)

    def by_chunk(x):
        w = x.shape[1] // hps
        return jnp.concatenate([x[:, g * w:(g + 1) * w].reshape(nb, c, w) for g in range(hps)], axis=0)

    def l2n(x):
        return x * lax.rsqrt(jnp.sum(x * x, axis=-1, keepdims=True) + NORM_EPS)

    q = l2n(by_chunk(conv(q_ref, xq_ref, 0))) * (dh ** -0.5)
    k = l2n(by_chunk(conv(k_ref, xk_ref, 1)))
    v = by_chunk(conv(v_ref, xv_ref, 2))

    t = t_ref[...]
    lane = lax.broadcasted_iota(jnp.int32, (tb, LANES), 1)
    beta_all = _sigmoid(t)
    g_all = -jnp.exp(gp_ref[0:1, :]) * _softplus(t + gp_ref[1:2, :])

    def pick(x, first):
        cols = [jnp.sum(jnp.where(lane == first + hb * hps + g, x, 0.0), axis=-1, keepdims=True)
                for g in range(hps)]
        return jnp.concatenate([col.reshape(nb, c, 1) for col in cols], axis=0)

    beta = pick(beta_all, 0)
    g = pick(g_all, heads)

    ri = lax.broadcasted_iota(jnp.int32, (c, c), 0)
    ci = lax.broadcasted_iota(jnp.int32, (c, c), 1)
    incl = ci <= ri
    strict = ci < ri
    eye = (ci == ri).astype(F32)
    tri = jnp.broadcast_to(incl.astype(BF16), (nbat, c, c))
    rl = lax.broadcasted_iota(jnp.int32, (c, LANES), 0)
    cl = lax.broadcasted_iota(jnp.int32, (c, LANES), 1)
    seg_mask = (rl > cl) | (cl == c)

    seg = _bdot_exact_lhs(tri, jnp.where(seg_mask, g, 0.0))
    gc = seg[:, :, c:c + 1]
    g_last = seg[:, c - 1:c, c:c + 1]
    decay = jnp.exp(jnp.where(incl, seg[:, :, 0:c], NEG_BIG))
    egc = jnp.exp(gc)
    k_beta = k * beta
    kq = jnp.einsum("bmd,bnd->bmn", jnp.concatenate([k_beta, q], axis=1).astype(BF16),
                    k.astype(BF16), preferred_element_type=F32)
    a = jnp.where(strict, kq[:, 0:c] * decay, 0.0)
    intra = kq[:, c:2 * c] * decay
    npair = nbat // 2
    pair_lanes = lambda x: jnp.concatenate(
        [x.reshape(npair, 2, c, x.shape[2])[:, 0], x.reshape(npair, 2, c, x.shape[2])[:, 1]], axis=2)
    left = lax.broadcasted_iota(jnp.int32, (c, 2 * c), 1) < c

    def block_diag(x2):
        zero = jnp.zeros_like(x2)
        return jnp.concatenate([jnp.where(left, x2, zero), jnp.where(left, zero, x2)], axis=1)

    a2 = pair_lanes(a)
    inv2 = jnp.concatenate([eye, eye], axis=1) - a2
    a_hi, a_lo = _split_bf16(a2)
    pw2 = _bdot3_parts(a_hi, a_lo, block_diag(a_hi), block_diag(a_lo))
    levels = int(math.log2(c)) - 1
    for lvl in range(levels):
        inv_hi, inv_lo = _split_bf16(inv2)
        pw_hi, pw_lo = _split_bf16(pw2)
        pd_hi, pd_lo = block_diag(pw_hi), block_diag(pw_lo)
        if lvl < levels - 1:
            both = _bdot3_parts(jnp.concatenate([inv_hi, pw_hi], axis=1),
                                jnp.concatenate([inv_lo, pw_lo], axis=1), pd_hi, pd_lo)
            inv2 = inv2 + both[:, 0:c]
            pw2 = both[:, c:2 * c]
        else:
            inv2 = inv2 + _bdot3_parts(inv_hi, inv_lo, pd_hi, pd_lo)
    rhs = jnp.concatenate([v * beta, k_beta * egc], axis=2).reshape(npair, 2, c, 2 * dh)
    zero = jnp.zeros((npair, c, 2 * dh), F32)
    rhs_bd = jnp.concatenate([jnp.concatenate([rhs[:, 0], zero], axis=2),
                              jnp.concatenate([zero, rhs[:, 1]], axis=2)], axis=1)
    uw2 = _bdot3(inv2, rhs_bd)
    uw = jnp.stack([uw2[:, :, 0:2 * dh], uw2[:, :, 2 * dh:4 * dh]], axis=1).reshape(nbat, c, 2 * dh)
    u_ref[...] = uw[:, :, 0:dh].reshape(nbat * c, dh)
    wq_ref[...] = jnp.concatenate([uw[:, :, dh:2 * dh], q * egc], axis=1).astype(BF16).reshape(
        nbat * 2 * c, dh)
    k_dec = k * jnp.exp(g_last - gc)
    for b in range(nbat):
        ik_ref[b * (c + dh):(b + 1) * (c + dh), :] = jnp.concatenate(
            [intra[b], k_dec[b].T], axis=0).astype(BF16)
    gl_ref[...] = jnp.broadcast_to(jnp.exp(g_last), (nbat, SUBLANES, dh)).reshape(nbat * SUBLANES, dh)

    def step(n, states):
        rows = lambda g, size: pl.ds(pl.multiple_of((g * nb + n) * size, size), size)
        ws = [jnp.dot(wq_ref[rows(g, 2 * c), :], states[g].astype(BF16), preferred_element_type=F32)
              for g in range(hps)]
        v_new = [u_ref[rows(g, c), :] - ws[g][0:c] for g in range(hps)]
        iv = [jnp.dot(ik_ref[pl.ds(pl.multiple_of((g * nb + n) * (c + dh), c), c + dh), :],
                      v_new[g].astype(BF16), preferred_element_type=F32) for g in range(hps)]
        new_states = []
        r0 = pl.multiple_of(n * c, c)
        for g in range(hps):
            out = ws[g][c:2 * c] + iv[g][0:c]
            gl = gl_ref[pl.ds(pl.multiple_of((g * nb + n) * SUBLANES, SUBLANES), 1), :]
            new_states.append(states[g] * gl + iv[g][c:c + dh])
            ms = jnp.mean(out * out, axis=-1, keepdims=True)
            o = out * lax.rsqrt(ms + NORM_EPS) * nw_ref[...]
            o_ref[pl.ds(r0, c), g * dh:(g + 1) * dh] = (
                o * _silu(z_ref[pl.ds(r0, c), g * dh:(g + 1) * dh])).astype(o_ref.dtype)
        return tuple(new_states)

    states = lax.fori_loop(0, nb, step, tuple(st_ref[g] for g in range(hps)))
    for g in range(hps):
        st_ref[g] = states[g]


def _gdn_mixer(qkvz, tail, conv_w, gate_p, norm_w, bsz, s, heads):
    hps = math.gcd(GDN_HEADS_PER_STEP, heads)
    tb = min(GDN_TIME_BLOCK, s)
    nt = s // tb
    hblocks = heads // hps
    width = hps * HEAD_DIM
    nbat = hps * (tb // GDN_CHUNK)
    col = lambda off: pl.BlockSpec((tb, width), lambda b, h, t: (b * nt + t, off + h))
    return pl.pallas_call(
        functools.partial(_gdn_kernel, tb=tb, heads=heads, hps=hps),
        grid=(bsz, hblocks, nt),
        in_specs=[col(0), col(hblocks), col(2 * hblocks), col(3 * hblocks),
                  pl.BlockSpec((tb, LANES), lambda b, h, t: (b * nt + t, 0)),
                  pl.BlockSpec((3, GDN_CONV, width), lambda b, h, t: (0, 0, h)),
                  pl.BlockSpec((2, LANES), lambda b, h, t: (0, 0)),
                  pl.BlockSpec((1, HEAD_DIM), lambda b, h, t: (0, 0))],
        out_specs=pl.BlockSpec((tb, width), lambda b, h, t: (b * nt + t, h)),
        out_shape=jax.ShapeDtypeStruct((bsz * s, heads * HEAD_DIM), BF16),
        scratch_shapes=[pltpu.VMEM((tb + SUBLANES, width), F32)] * 3
                       + [pltpu.VMEM((hps, HEAD_DIM, HEAD_DIM), F32),
                          pltpu.VMEM((nbat * GDN_CHUNK, HEAD_DIM), F32),
                          pltpu.VMEM((nbat * 2 * GDN_CHUNK, HEAD_DIM), BF16),
                          pltpu.VMEM((nbat * (GDN_CHUNK + HEAD_DIM), GDN_CHUNK), BF16),
                          pltpu.VMEM((nbat * SUBLANES, HEAD_DIM), F32)],
        compiler_params=_cparams("parallel", "parallel", "arbitrary"),
        name="gdn_mixer",
    )(qkvz, qkvz, qkvz, qkvz, tail, conv_w, gate_p, norm_w)


def _gdn_layer(h, hn, ssq, w_all, layer, w_tail, conv_w, a_log, dt_bias, norm_w, wo_all, ple_gain,
               bsz, s):
    width = wo_all.shape[1]
    heads = width // HEAD_DIM
    qkvz = _matmul(hn, w_all, w_t=4 * width, w_row_blk=layer, out_dtype=F32, row_ssq=ssq,
                   tm=MM_TM_WIDE)
    tail = _matmul(hn, w_tail, w_t=LANES, out_dtype=F32, row_ssq=ssq)
    cw = conv_w.astype(F32).reshape(GDN_CONV, 3, width).transpose(1, 0, 2)
    lane_pad = lambda x: jnp.pad(x.astype(F32), (heads, LANES - 2 * heads))
    gate_p = jnp.stack([lane_pad(a_log), lane_pad(dt_bias)])
    gated = _gdn_mixer(qkvz, tail, cw, gate_p, norm_w.astype(F32).reshape(1, HEAD_DIM),
                       bsz, s, heads)
    return _matmul(gated, wo_all, w_row_blk=layer, out_dtype=F32, mode="residual",
                   extras=[(h, "tile", {})], norm_gain=ple_gain)


def _ssm_prep_kernel(lam_ref, bf_ref, cf_ref, bd_ref, wz_ref, wyt_ref, laml_ref):
    ns = SSM_PACK * SSM_STATE
    lam_re, lam_im, step_log = lam_ref[0, 0:1, :], lam_ref[0, 1:2, :], lam_ref[0, 2:3, :]
    step = jnp.exp(step_log)
    mag = jnp.exp(lam_re * step)
    lb_re, lb_im = mag * jnp.cos(lam_im * step), mag * jnp.sin(lam_im * step)
    den = lam_re * lam_re + lam_im * lam_im
    num_re = lb_re - 1.0
    zoh_re = (num_re * lam_re + lb_im * lam_im) / den
    zoh_im = (lb_im * lam_re - num_re * lam_im) / den
    b_re, b_im = bf_ref[0, 0], bf_ref[0, 1]
    bb_re = zoh_re * b_re - zoh_im * b_im
    bb_im = zoh_re * b_im + zoh_im * b_re
    c_re, c_im = cf_ref[0, 0], cf_ref[0, 1]
    cc_hi, cc_lo = _split_bf16(jnp.concatenate([c_re, -c_im], axis=1))
    nt_dot = lambda x, y: lax.dot_general(x, y, (((1,), (1,)), ((), ())), preferred_element_type=F32)

    bd_ref[0, 0, LANES:2 * LANES, 0:LANES] = jnp.zeros((LANES, LANES), bd_ref.dtype)
    pw_re, pw_im = jnp.ones_like(lb_re), jnp.zeros_like(lb_re)
    for d in range(SSM_L + 1):
        if d < SSM_L:
            a_re = pw_re * bb_re - pw_im * bb_im
            a_im = pw_re * bb_im + pw_im * bb_re
            a_d = jnp.concatenate([a_re, a_im], axis=1)
            a_hi, a_lo = _split_bf16(a_d)
            blk = (nt_dot(a_hi, cc_hi) + nt_dot(a_hi, cc_lo) + nt_dot(a_lo, cc_hi)).astype(bd_ref.dtype)
            d2 = d // 2
            if d % 2 == 0:
                bd_ref[0, d2, 0:LANES, 0:LANES] = blk
                bd_ref[0, d2, LANES:2 * LANES, LANES:2 * LANES] = blk
            else:
                bd_ref[0, d2, 0:LANES, LANES:2 * LANES] = blk
                if d2 + 1 < SSM_L // 2:
                    bd_ref[0, d2 + 1, LANES:2 * LANES, 0:LANES] = blk
            tin = SSM_L - 1 - d
            wz_ref[0, tin * LANES:(tin + 1) * LANES, :] = a_d.astype(wz_ref.dtype)
        if d >= 1:
            y_re = pw_re * c_re - pw_im * c_im
            y_im = pw_re * c_im + pw_im * c_re
            wyt_ref[0, (d - 1) * LANES:d * LANES, :] = jnp.concatenate(
                [y_re, -y_im], axis=1).astype(wyt_ref.dtype)
        if d == SSM_L:
            laml_ref[0] = jnp.concatenate([pw_re, pw_im], axis=1)
        pw_re, pw_im = pw_re * lb_re - pw_im * lb_im, pw_re * lb_im + pw_im * lb_re


def _ssm_prep(lam, bfull, cfull):
    nsg = lam.shape[0]
    ns = SSM_PACK * SSM_STATE
    rows = SSM_L * LANES
    return pl.pallas_call(
        _ssm_prep_kernel,
        grid=(nsg,),
        in_specs=[pl.BlockSpec((1, 3, ns), lambda g: (g, 0, 0)),
                  pl.BlockSpec((1, 2, LANES, ns), lambda g: (g, 0, 0, 0)),
                  pl.BlockSpec((1, 2, LANES, ns), lambda g: (g, 0, 0, 0))],
        out_specs=[pl.BlockSpec((1, SSM_L // 2, 2 * LANES, 2 * LANES), lambda g: (g, 0, 0, 0)),
                   pl.BlockSpec((1, rows, 2 * ns), lambda g: (g, 0, 0)),
                   pl.BlockSpec((1, rows, 2 * ns), lambda g: (g, 0, 0)),
                   pl.BlockSpec((1, 1, 2 * ns), lambda g: (g, 0, 0))],
        out_shape=[jax.ShapeDtypeStruct((nsg, SSM_L // 2, 2 * LANES, 2 * LANES), BF16),
                   jax.ShapeDtypeStruct((nsg, rows, 2 * ns), BF16),
                   jax.ShapeDtypeStruct((nsg, rows, 2 * ns), BF16),
                   jax.ShapeDtypeStruct((nsg, 1, 2 * ns), F32)],
        compiler_params=_cparams("parallel"),
        name="ssm_prep",
    )(lam, bfull, cfull)


def _ssm_scan_kernel(u_ref, bd_ref, wz_ref, wyt_ref, laml_ref, d_ref, o_ref,
                     x_ref, z_ref, xp_ref, st_ref, *, tc):
    ns = SSM_PACK * SSM_STATE

    @pl.when(pl.program_id(2) == 0)
    def _():
        st_ref[...] = jnp.zeros_like(st_ref)

    for tau in range(SSM_L):
        x_ref[:, tau * LANES:(tau + 1) * LANES] = u_ref[pl.ds(tau, tc, stride=SSM_L), :].astype(BF16)

    z_ref[...] = jnp.dot(x_ref[...], wz_ref[0], preferred_element_type=F32)

    a_re, a_im = laml_ref[0, :, 0:ns], laml_ref[0, :, ns:2 * ns]

    def row(r, carry):
        s_re, s_im = carry
        xp_ref[pl.ds(r, 1), :] = jnp.concatenate([s_re, s_im], axis=1)
        zr = z_ref[pl.ds(r, 1), :]
        n_re = a_re * s_re - a_im * s_im + zr[:, 0:ns]
        n_im = a_re * s_im + a_im * s_re + zr[:, ns:2 * ns]
        return n_re, n_im

    s_re, s_im = lax.fori_loop(0, tc, row, (st_ref[:, 0:ns], st_ref[:, ns:2 * ns]), unroll=4)
    st_ref[...] = jnp.concatenate([s_re, s_im], axis=1)

    xprev = xp_ref[...].astype(BF16)
    pair = 2 * LANES
    for t2 in range(SSM_L // 2):
        y2 = lax.dot_general(xprev, wyt_ref[0, t2 * pair:(t2 + 1) * pair, :],
                             (((1,), (1,)), ((), ())), preferred_element_type=F32)
        for tin in range(t2 + 1):
            y2 = y2 + jnp.dot(x_ref[:, tin * pair:(tin + 1) * pair], bd_ref[0, t2 - tin],
                              preferred_element_type=F32)
        for half in range(2):
            tau = 2 * t2 + half
            y = y2[:, half * LANES:(half + 1) * LANES]
            y = y + d_ref[...] * u_ref[pl.ds(tau, tc, stride=SSM_L), :]
            y = 0.5 * y * (1.0 + jnp.tanh(math.sqrt(2.0 / math.pi) * (y + 0.044715 * (y * y * y))))
            o_ref[pl.ds(tau, tc, stride=SSM_L), :] = y


def _ssm_scan(uz, bd, wz, wyt, laml, d_skip, bsz, s):
    e = d_skip.shape[1]
    nsg = e // LANES
    ns = SSM_PACK * SSM_STATE
    tb = min(4096, s)
    nt = s // tb
    tc = tb // SSM_L
    rows = SSM_L * LANES
    return pl.pallas_call(
        functools.partial(_ssm_scan_kernel, tc=tc),
        grid=(nsg, bsz, nt),
        in_specs=[pl.BlockSpec((tb, LANES), lambda g, b, t: (b * nt + t, g)),
                  pl.BlockSpec((1, SSM_L // 2, 2 * LANES, 2 * LANES), lambda g, b, t: (g, 0, 0, 0)),
                  pl.BlockSpec((1, rows, 2 * ns), lambda g, b, t: (g, 0, 0)),
                  pl.BlockSpec((1, rows, 2 * ns), lambda g, b, t: (g, 0, 0)),
                  pl.BlockSpec((1, 1, 2 * ns), lambda g, b, t: (g, 0, 0)),
                  pl.BlockSpec((1, LANES), lambda g, b, t: (0, g))],
        out_specs=pl.BlockSpec((tb, LANES), lambda g, b, t: (b * nt + t, g)),
        out_shape=jax.ShapeDtypeStruct((uz.shape[0], e), F32),
        scratch_shapes=[pltpu.VMEM((tc, rows), BF16), pltpu.VMEM((tc, 2 * ns), F32),
                        pltpu.VMEM((tc, 2 * ns), F32), pltpu.VMEM((1, 2 * ns), F32)],
        compiler_params=_cparams("parallel", "parallel", "arbitrary"),
        name="ssm_scan",
    )(uz, bd, wz, wyt, laml, d_skip)


def _ssm_layer(h, hn, ssq, w_all, layer, lam_re, lam_im, b_re, b_im, c_re, c_im, log_step, d_skip,
               wg_all, b_glu, wo_all, ple_gain, bsz, s):
    e = wo_all.shape[1]
    groups, nstate = lam_re.shape
    assert nstate == SSM_STATE and e == groups * SSM_GROUP and groups % SSM_PACK == 0
    nsg = groups // SSM_PACK
    ns = SSM_PACK * SSM_STATE
    lam = jnp.stack([lam_re.astype(F32).reshape(nsg, ns), lam_im.astype(F32).reshape(nsg, ns),
                     jnp.repeat(log_step.astype(F32), SSM_STATE).reshape(nsg, ns)], axis=1)
    eye = jnp.eye(SSM_PACK, dtype=F32)

    def expand_b(b):
        return jnp.einsum("sgpm,gh->sgmhp", b.astype(F32).reshape(nsg, SSM_PACK, SSM_STATE, SSM_GROUP),
                          eye).reshape(nsg, LANES, ns)

    def expand_c(c):
        return jnp.einsum("sgnp,gh->sgnhp", c.astype(F32).reshape(nsg, SSM_PACK, SSM_GROUP, SSM_STATE),
                          eye).reshape(nsg, LANES, ns)

    bfull = jnp.stack([expand_b(b_re), expand_b(b_im)], axis=1)
    cfull = jnp.stack([expand_c(c_re), expand_c(c_im)], axis=1)
    bd, wz, wyt, laml = _ssm_prep(lam, bfull, cfull)

    uz = _matmul(hn, w_all, w_row_blk=layer, out_dtype=F32, row_ssq=ssq, tm=MM_TM_WIDE)
    y = _ssm_scan(uz, bd, wz, wyt, laml, d_skip.astype(F32).reshape(1, e), bsz, s)
    y2 = _matmul(y, wg_all, w_row_blk=layer, out_dtype=BF16, mode="glu", cast_a=True,
                 extras=[(y, "tile", {}), (uz, "tile", {"col_off": e}),
                         (b_glu.astype(F32).reshape(1, e), "row", {})])
    return _matmul(y2, wo_all, w_row_blk=layer, out_dtype=F32, mode="residual",
                   extras=[(h, "tile", {})], norm_gain=ple_gain)


def kernel(x, p, norm_mix, fox_w_in, fox_b_f, fox_w_out, gdn_w_in, gdn_conv, gdn_a_log, gdn_dt_bias, gdn_norm, gdn_w_out, ssm_w_in, ssm_lam_re, ssm_lam_im, ssm_b_re, ssm_b_im, ssm_c_re, ssm_c_im, ssm_log_step, ssm_d, ssm_w_glu, ssm_b_glu, ssm_w_out, norm_ple, ple_w_proj, ple_w_gate, final_norm):
    bsz, s, d = x.shape
    depth, m = p.shape[0], bsz * s
    pdim = p.shape[-1]
    fox_width, gdn_width = fox_w_out.shape[1], gdn_w_out.shape[1]
    fox_wt, gdn_wt = jnp.swapaxes(fox_w_in, 1, 2), jnp.swapaxes(gdn_w_in, 1, 2)
    pad_rows = lambda t: jnp.pad(t, ((0, 0), (0, LANES - t.shape[1]), (0, 0)))
    fox_w = _cast_bf16(fox_wt, 4 * fox_width)
    fox_tail = _cast_bf16(pad_rows(fox_wt[:, 4 * fox_width:]))
    gdn_w = _cast_bf16(gdn_wt, 4 * gdn_width)
    gdn_tail = _cast_bf16(pad_rows(gdn_wt[:, 4 * gdn_width:]))
    fox_wo, gdn_wo = _cast_bf16(fox_w_out), _cast_bf16(gdn_w_out)
    ssm_w, ssm_wg, ssm_wo = _cast_bf16(ssm_w_in), _cast_bf16(ssm_w_glu), _cast_bf16(ssm_w_out)
    ple_wg, ple_wp = _cast_bf16(ple_w_gate), _cast_bf16(ple_w_proj)
    p_bf = _cast_bf16(p.reshape(depth, m, pdim))

    h = x.reshape(m, d).astype(F32)
    hn, ssq = _rmsnorm(h, norm_mix[0], BF16), None
    for i in range(depth):
        kind, j = i % 3, i // 3
        if kind == 0:
            h, hg, hssq = _fox_layer(h, hn, ssq, fox_w, j, fox_tail[j * LANES:(j + 1) * LANES],
                                     fox_b_f[j], fox_wo, norm_ple[i], bsz, s)
        elif kind == 1:
            h, hg, hssq = _gdn_layer(h, hn, ssq, gdn_w, j, gdn_tail[j * LANES:(j + 1) * LANES],
                                     gdn_conv[j], gdn_a_log[j], gdn_dt_bias[j], gdn_norm[j], gdn_wo,
                                     norm_ple[i], bsz, s)
        else:
            h, hg, hssq = _ssm_layer(h, hn, ssq, ssm_w, j, ssm_lam_re[j], ssm_lam_im[j], ssm_b_re[j],
                                     ssm_b_im[j], ssm_c_re[j], ssm_c_im[j], ssm_log_step[j], ssm_d[j],
                                     ssm_wg, ssm_b_glu[j], ssm_wo, norm_ple[i], bsz, s)
        extras = [(h, "tile", {}), (p_bf, "rows", {"row_blk": i * (m // min(MM_TM, m))}),
                  (ple_wp, "cols", {"row_blk": i, "rows": pdim})]
        outs = _matmul(hg, ple_wg, w_row_blk=i, out_dtype=F32, mode="ple", row_ssq=hssq, extras=extras,
                       norm_gain=norm_mix[i + 1] if i + 1 < depth else None)
        h, hn, ssq = outs if i + 1 < depth else (outs, None, None)
    return _rmsnorm(h, final_norm, F32).reshape(bsz, s, d)
```

```python
import functools
import math

import jax
import jax.numpy as jnp
from jax import lax
from jax.experimental import pallas as pl
from jax.experimental.pallas import tpu as pltpu

F32 = jnp.float32
BF16 = jnp.bfloat16

LANES = 128
SUBLANES = 8
VMEM_LIMIT_BYTES = 56 * 1024 * 1024

CAST_BLOCK_ELEMS = 1 << 20
A_PREP_ROWS = 32
MM_TM = 512
MM_TM_WIDE = 1024
MM_TN = 1024
NORM_EPS = 1e-6
LOG2E = 1.4426950408889634
HEAD_DIM = 128
FOX_BLOCK = 1024
GDN_CHUNK = 64
GDN_CONV = 4
GDN_HEADS_PER_STEP = 16
GDN_TIME_BLOCK = 128
SSM_GROUP = 16
SSM_STATE = 64
SSM_PACK = LANES // SSM_GROUP
SSM_L = 16
NEG_BIG = -1e30

HIGHEST = lax.Precision.HIGHEST


def _cparams(*sem):
    return pltpu.CompilerParams(dimension_semantics=sem, vmem_limit_bytes=VMEM_LIMIT_BYTES)


def _sigmoid(x):
    return 0.5 * jnp.tanh(0.5 * x) + 0.5


def _silu(x):
    hx = 0.5 * x
    return hx + hx * jnp.tanh(hx)


def _softplus(x):
    return jnp.maximum(x, 0.0) + jnp.log(1.0 + jnp.exp(-jnp.abs(x)))


def _rmsnorm_kernel(x_ref, g_ref, o_ref):
    x = x_ref[...]
    ms = jnp.mean(x * x, axis=-1, keepdims=True)
    o_ref[...] = (x * lax.rsqrt(ms + NORM_EPS) * g_ref[...]).astype(o_ref.dtype)


def _rmsnorm(x, g, out_dtype):
    m, d = x.shape
    tm = min(256, m)
    return pl.pallas_call(
        _rmsnorm_kernel,
        grid=(m // tm,),
        in_specs=[pl.BlockSpec((tm, d), lambda i: (i, 0)),
                  pl.BlockSpec((1, d), lambda i: (0, 0))],
        out_specs=pl.BlockSpec((tm, d), lambda i: (i, 0)),
        out_shape=jax.ShapeDtypeStruct((m, d), out_dtype),
        compiler_params=_cparams("parallel"),
        name="rmsnorm",
    )(x, g.reshape(1, d).astype(F32))


def _cast_kernel(x_ref, o_ref):
    o_ref[...] = x_ref[...].astype(o_ref.dtype)


def _cast_bf16(x, rows=None):
    nl, r, c = x.shape
    rows = r if rows is None else rows
    tc = min(2048, c)
    tr = min(rows, CAST_BLOCK_ELEMS // tc)
    assert rows % tr == 0 and c % tc == 0
    out = pl.pallas_call(
        _cast_kernel,
        grid=(nl, rows // tr, c // tc),
        in_specs=[pl.BlockSpec((1, tr, tc), lambda l, i, j: (l, i, j))],
        out_specs=pl.BlockSpec((1, tr, tc), lambda l, i, j: (l, i, j)),
        out_shape=jax.ShapeDtypeStruct((nl, rows, c), BF16),
        compiler_params=_cparams("parallel", "parallel", "parallel"),
        name="cast_bf16",
    )(x)
    return out.reshape(nl * rows, c)


_N_EXTRA = {"plain": 0, "colscale": 1, "residual": 1, "ple": 3, "glu": 3}


def _mm_kernel(*refs, mode, cast_a, w_t, has_ssq, emit_norm, inv_k):
    it = iter(refs)
    a_ref, w_ref = next(it), next(it)
    ssq_ref = next(it) if has_ssq else None
    extra = [next(it) for _ in range(_N_EXTRA[mode])]
    gain_ref = next(it) if emit_norm else None
    o_ref = next(it)
    hg_ref, ssq_out_ref = (next(it), next(it)) if emit_norm else (None, None)
    if cast_a:
        a_scr = next(it)

        @pl.when(pl.program_id(1) == 0)
        def _():
            def prep(r, carry):
                rows = pl.ds(pl.multiple_of(r * A_PREP_ROWS, A_PREP_ROWS), A_PREP_ROWS)
                a_scr[rows, :] = a_ref[rows, :].astype(BF16)
                return carry

            lax.fori_loop(0, a_ref.shape[0] // A_PREP_ROWS, prep, 0)

        a = a_scr[...]
    else:
        a = a_ref[...]

    acc = lax.dot_general(a, w_ref[...], (((1,), (1 if w_t else 0,)), ((), ())),
                          preferred_element_type=F32)
    if has_ssq:
        parts = ssq_ref[...]
        tot = parts[:, 0:LANES]
        for t in range(1, parts.shape[1] // LANES):
            tot = tot + parts[:, t * LANES:(t + 1) * LANES]
        acc = acc * lax.rsqrt(tot[:, 0:1] * inv_k + NORM_EPS)
    if mode == "plain":
        out = acc
    elif mode == "colscale":
        out = acc * extra[0][...]
    elif mode == "residual":
        out = extra[0][...] + acc
    elif mode == "ple":
        res_ref, p_ref, wp_ref = extra
        emb = jnp.dot(p_ref[...], wp_ref[...], preferred_element_type=F32)
        out = res_ref[...] + _sigmoid(acc) * emb
    elif mode == "glu":
        y_ref, z_ref, b_ref = extra
        out = y_ref[...] * _sigmoid(acc + b_ref[...]) * _silu(z_ref[...])
    else:
        raise ValueError(mode)
    o_ref[...] = out.astype(o_ref.dtype)
    if emit_norm:
        hg_ref[...] = (out * gain_ref[...]).astype(hg_ref.dtype)
        ssq_out_ref[...] = jnp.broadcast_to(jnp.sum(out * out, axis=-1, keepdims=True),
                                            ssq_out_ref.shape)


def _extra_spec(arr, kind, tm, tn, row_blk=0, col_off=0, rows=None):
    if kind == "tile":
        return pl.BlockSpec((tm, tn), lambda i, j: (i + row_blk, j + col_off // tn))
    if kind == "row":
        return pl.BlockSpec((1, tn), lambda i, j: (0, j))
    if kind == "rows":
        return pl.BlockSpec((tm, arr.shape[1]), lambda i, j: (i + row_blk, 0))
    if kind == "cols":
        return pl.BlockSpec((rows, tn), lambda i, j: (row_blk, j))
    raise ValueError(kind)


def _matmul(a, w, *, out_dtype, n=None, mode="plain", extras=(), cast_a=False, row_ssq=None,
            norm_gain=None, w_t=None, w_row_blk=0, w_col_off=0, tm=MM_TM, tn=MM_TN):
    m, kdim = a.shape
    n = (w_t if w_t else w.shape[1]) if n is None else n
    tm = min(tm, m)
    tn = functools.reduce(math.gcd, [tn, n, w_col_off, w_t or 0])
    assert m % tm == 0 and tn % LANES == 0 and len(extras) == _N_EXTRA[mode]
    operands = [a, w]
    if w_t:
        first = (w_row_blk * w_t + w_col_off) // tn
        w_spec = pl.BlockSpec((tn, kdim), lambda i, j: (first + j, 0))
    else:
        w_spec = pl.BlockSpec((kdim, tn), lambda i, j: (w_row_blk, j + w_col_off // tn))
    in_specs = [pl.BlockSpec((tm, kdim), lambda i, j: (i, 0)), w_spec]
    if row_ssq is not None:
        operands.append(row_ssq)
        in_specs.append(pl.BlockSpec((tm, row_ssq.shape[1]), lambda i, j: (i, 0)))
    for arr, kind, opts in extras:
        assert opts.get("col_off", 0) % tn == 0
        operands.append(arr)
        in_specs.append(_extra_spec(arr, kind, tm, tn, **opts))
    tile = pl.BlockSpec((tm, tn), lambda i, j: (i, j))
    out_specs, out_shape = tile, jax.ShapeDtypeStruct((m, n), out_dtype)
    if norm_gain is not None:
        operands.append(norm_gain.astype(F32).reshape(1, n))
        in_specs.append(pl.BlockSpec((1, tn), lambda i, j: (0, j)))
        out_specs = [tile, tile, pl.BlockSpec((tm, LANES), lambda i, j: (i, j))]
        out_shape = [out_shape, jax.ShapeDtypeStruct((m, n), BF16),
                     jax.ShapeDtypeStruct((m, n // tn * LANES), F32)]
    scratch = [pltpu.VMEM((tm, kdim), BF16)] if cast_a else []
    return pl.pallas_call(
        functools.partial(_mm_kernel, mode=mode, cast_a=cast_a, w_t=bool(w_t),
                          has_ssq=row_ssq is not None,
                          emit_norm=norm_gain is not None, inv_k=1.0 / kdim),
        grid=(m // tm, n // tn),
        in_specs=in_specs,
        out_specs=out_specs,
        out_shape=out_shape,
        scratch_shapes=scratch,
        compiler_params=_cparams("parallel", "arbitrary"),
        name="matmul_" + mode,
    )(*operands)


def _fox_cum_kernel(f_ref, b_ref, o_ref, carry_ref, *, ts):
    @pl.when(pl.program_id(1) == 0)
    def _():
        carry_ref[...] = jnp.zeros_like(carry_ref)

    x = f_ref[...] + b_ref[...]
    log_f = jnp.minimum(x, 0.0) - jnp.log(1.0 + jnp.exp(-jnp.abs(x)))
    r = lax.broadcasted_iota(jnp.int32, (ts, ts), 0)
    c = lax.broadcasted_iota(jnp.int32, (ts, ts), 1)
    lower = (c <= r).astype(F32)
    cum = jnp.dot(lower, log_f, precision=HIGHEST, preferred_element_type=F32) + carry_ref[...]
    carry_ref[...] = cum[ts - 1:ts, :]
    o_ref[0] = cum.T


def _fox_cum(f_logit, b_f, bsz, s):
    ts = min(512, s)
    return pl.pallas_call(
        functools.partial(_fox_cum_kernel, ts=ts),
        grid=(bsz, s // ts),
        in_specs=[pl.BlockSpec((ts, LANES), lambda b, t: (b * (s // ts) + t, 0)),
                  pl.BlockSpec((1, LANES), lambda b, t: (0, 0))],
        out_specs=pl.BlockSpec((1, LANES, ts), lambda b, t: (b, 0, t)),
        out_shape=jax.ShapeDtypeStruct((bsz, LANES, s), F32),
        scratch_shapes=[pltpu.VMEM((1, LANES), F32)],
        compiler_params=_cparams("parallel", "arbitrary"),
        name="fox_cum",
    )(f_logit, b_f)


def _fox_attn_kernel(q_ref, k_ref, v_ref, c_ref, z_ref, o_ref, sa_ref, sb_ref, *, tq):
    tk = tq // 2
    i = pl.program_id(2)
    q = q_ref[...]
    c0 = c_ref[0, 0, :, pl.ds(pl.multiple_of(i * tq, tq), LANES)][:, 0:1]

    def scores(qrows, j):
        start = j * tk if isinstance(j, int) else pl.multiple_of(j * tk, tk)
        s = lax.dot_general(qrows, k_ref[pl.ds(start, tk), :], (((1,), (1,)), ((), ())),
                            preferred_element_type=F32)
        return s + (c0 - c_ref[0, 0, :, pl.ds(start, tk)]) * LOG2E

    def update(s, j, m, acc, width=1):
        rows = width * tk
        start = j * tk if isinstance(j, int) else pl.multiple_of(j * tk, tk)
        vb = jnp.concatenate([v_ref[pl.ds(start, rows), :], jnp.ones((rows, HEAD_DIM), BF16)], axis=1)
        m_new = jnp.maximum(m, jnp.max(s, axis=-1, keepdims=True))
        p = jnp.exp2(s - m_new)
        acc = jnp.exp2(m - m_new) * acc + jnp.dot(p.astype(BF16), vb, preferred_element_type=F32)
        return m_new, acc

    sa_ref[...] = scores(q, 0)

    def pair(t, carry):
        m, acc = carry
        sb_ref[...] = scores(q, 2 * t + 1)
        m, acc = update(sa_ref[...], 2 * t, m, acc)
        sa_ref[...] = scores(q, 2 * t + 2)
        return update(sb_ref[...], 2 * t + 1, m, acc)

    init = (jnp.full((tq, 1), NEG_BIG, F32), jnp.zeros((tq, 2 * HEAD_DIM), F32))
    carry = lax.fori_loop(0, i // 2, lambda t2, cr: pair(2 * t2 + 1, pair(2 * t2, cr)), init)

    def finish(a, rows):
        o = a[:, 0:HEAD_DIM] / a[:, HEAD_DIM:HEAD_DIM + 1]
        o_ref[rows, :] = (o * _silu(z_ref[rows, :])).astype(o_ref.dtype)

    def tail(carry, odd):
        m, acc = pair(i - 1, carry) if odd else carry
        r = lax.broadcasted_iota(jnp.int32, (tk, tk), 0)
        c = lax.broadcasted_iota(jnp.int32, (tk, tk), 1)
        s_bot = jnp.concatenate(
            [sa_ref[tk:tq, :], jnp.where(c <= r, scores(q[tk:tq], 2 * i + 1), NEG_BIG)], axis=1)
        _, acc_top = update(jnp.where(c <= r, sa_ref[0:tk, :], NEG_BIG), 2 * i, m[0:tk], acc[0:tk])
        _, acc_bot = update(s_bot, 2 * i, m[tk:tq], acc[tk:tq], width=2)
        finish(acc_top, slice(0, tk))
        finish(acc_bot, slice(tk, tq))

    lax.cond(i % 2 == 1, lambda cr: tail(cr, True), lambda cr: tail(cr, False), carry)


def _fox_attention(qkv, cum4, z, bsz, s, heads):
    tq = min(FOX_BLOCK, s)
    nq = s // tq
    return pl.pallas_call(
        functools.partial(_fox_attn_kernel, tq=tq),
        grid=(bsz, heads, nq),
        in_specs=[pl.BlockSpec((tq, HEAD_DIM), lambda b, h, i: (b * nq + i, h)),
                  pl.BlockSpec((s, HEAD_DIM), lambda b, h, i: (b, heads + h)),
                  pl.BlockSpec((s, HEAD_DIM), lambda b, h, i: (b, 2 * heads + h)),
                  pl.BlockSpec((1, 1, 1, s), lambda b, h, i: (b, h, 0, 0)),
                  pl.BlockSpec((tq, HEAD_DIM), lambda b, h, i: (b * nq + i, h))],
        out_specs=pl.BlockSpec((tq, HEAD_DIM), lambda b, h, i: (b * nq + i, h)),
        out_shape=jax.ShapeDtypeStruct((bsz * s, heads * HEAD_DIM), BF16),
        scratch_shapes=[pltpu.VMEM((tq, tq // 2), F32)] * 2,
        compiler_params=_cparams("parallel", "parallel", "arbitrary"),
        name="fox_attn",
    )(qkv, qkv, qkv, cum4, z)


def _fox_layer(h, hn, ssq, w_all, layer, w_tail, b_f, wo_all, ple_gain, bsz, s):
    width = wo_all.shape[1]
    heads = width // HEAD_DIM
    q_scale = jnp.concatenate([jnp.full((width,), HEAD_DIM ** -0.5 * LOG2E, F32),
                               jnp.ones((2 * width,), F32)]).reshape(1, 3 * width)
    qkv = _matmul(hn, w_all, n=3 * width, w_t=4 * width, w_row_blk=layer, out_dtype=BF16,
                  mode="colscale", extras=[(q_scale, "row", {})], row_ssq=ssq, tm=MM_TM_WIDE)
    z = _matmul(hn, w_all, n=width, w_t=4 * width, w_row_blk=layer, w_col_off=3 * width,
                out_dtype=F32, row_ssq=ssq)
    f_logit = _matmul(hn, w_tail, w_t=LANES, out_dtype=F32, row_ssq=ssq)
    b_row = jnp.pad(b_f.astype(F32), (0, LANES - heads)).reshape(1, LANES)
    cum = _fox_cum(f_logit, b_row, bsz, s)
    gated = _fox_attention(qkv, cum.reshape(bsz, LANES, 1, s), z, bsz, s, heads)
    return _matmul(gated, wo_all, w_row_blk=layer, out_dtype=F32, mode="residual",
                   extras=[(h, "tile", {})], norm_gain=ple_gain)


def _bdot(a, b):
    return jnp.einsum("bmk,bkn->bmn", a, b, preferred_element_type=F32)


def _split_bf16(x):
    hi = x.astype(BF16)
    return hi, (x - hi.astype(F32)).astype(BF16)


def _bdot3_parts(a_hi, a_lo, b_hi, b_lo):
    return _bdot(a_hi, b_hi) + _bdot(a_hi, b_lo) + _bdot(a_lo, b_hi)


def _bdot3(a, b):
    return _bdot3_parts(*_split_bf16(a), *_split_bf16(b))


def _bdot_exact_lhs(a_bf16, b):
    b_hi = b.astype(BF16)
    b_mid, b_lo = _split_bf16(b - b_hi.astype(F32))
    return _bdot(a_bf16, b_hi) + _bdot(a_bf16, b_mid) + _bdot(a_bf16, b_lo)


def _gdn_kernel(q_ref, k_ref, v_ref, z_ref, t_ref, cw_ref, gp_ref, nw_ref, o_ref,
                xq_ref, xk_ref, xv_ref, st_ref, u_ref, wq_ref, ik_ref, gl_ref, *, tb, heads, hps):
    hb = pl.program_id(1)
    c, dh = GDN_CHUNK, HEAD_DIM
    nb = tb // c
    nbat = hps * nb
    pad = SUBLANES

    @pl.when(pl.program_id(2) == 0)
    def _():
        for r in (xq_ref, xk_ref, xv_ref):
            r[0:pad, :] = jnp.zeros((pad, hps * dh), F32)
        st_ref[...] = jnp.zeros_like(st_ref)

    def conv(x_ref, xs_ref, col):
        x = x_ref[...]
        xs_ref[pad:pad + tb, :] = x
        xs = xs_ref[...]
        y = x * cw_ref[col, GDN_CONV - 1:GDN_CONV, :]
        for j in range(GDN_CONV - 1):
            shifted = pltpu.roll(xs, GDN_CONV - 1 - j, axis=0)[pad:pad + tb, :]
            y = y + shifted * cw_ref[col, j:j + 1, :]
        xs_ref[0:pad, :] = x[tb - pad:tb, :]
        return _silu(y)

    def by_chunk(x):
        w = x.shape[1] // hps
        return jnp.concatenate([x[:, g * w:(g + 1) * w].reshape(nb, c, w) for g in range(hps)], axis=0)

    def l2n(x):
        return x * lax.rsqrt(jnp.sum(x * x, axis=-1, keepdims=True) + NORM_EPS)

    q = l2n(by_chunk(conv(q_ref, xq_ref, 0))) * (dh ** -0.5)
    k = l2n(by_chunk(conv(k_ref, xk_ref, 1)))
    v = by_chunk(conv(v_ref, xv_ref, 2))

    t = t_ref[...]
    lane = lax.broadcasted_iota(jnp.int32, (tb, LANES), 1)
    beta_all = _sigmoid(t)
    g_all = -jnp.exp(gp_ref[0:1, :]) * _softplus(t + gp_ref[1:2, :])

    def pick(x, first):
        cols = [jnp.sum(jnp.where(lane == first + hb * hps + g, x, 0.0), axis=-1, keepdims=True)
                for g in range(hps)]
        return jnp.concatenate([col.reshape(nb, c, 1) for col in cols], axis=0)

    beta = pick(beta_all, 0)
    g = pick(g_all, heads)

    ri = lax.broadcasted_iota(jnp.int32, (c, c), 0)
    ci = lax.broadcasted_iota(jnp.int32, (c, c), 1)
    incl = ci <= ri
    strict = ci < ri
    eye = (ci == ri).astype(F32)
    tri = jnp.broadcast_to(incl.astype(BF16), (nbat, c, c))
    rl = lax.broadcasted_iota(jnp.int32, (c, LANES), 0)
    cl = lax.broadcasted_iota(jnp.int32, (c, LANES), 1)
    seg_mask = (rl > cl) | (cl == c)

    seg = _bdot_exact_lhs(tri, jnp.where(seg_mask, g, 0.0))
    gc = seg[:, :, c:c + 1]
    g_last = seg[:, c - 1:c, c:c + 1]
    decay = jnp.exp(jnp.where(incl, seg[:, :, 0:c], NEG_BIG))
    egc = jnp.exp(gc)
    k_beta = k * beta
    kq = jnp.einsum("bmd,bnd->bmn", jnp.concatenate([k_beta, q], axis=1).astype(BF16),
                    k.astype(BF16), preferred_element_type=F32)
    a = jnp.where(strict, kq[:, 0:c] * decay, 0.0)
    intra = kq[:, c:2 * c] * decay
    npair = nbat // 2
    pair_lanes = lambda x: jnp.concatenate(
        [x.reshape(npair, 2, c, x.shape[2])[:, 0], x.reshape(npair, 2, c, x.shape[2])[:, 1]], axis=2)
    left = lax.broadcasted_iota(jnp.int32, (c, 2 * c), 1) < c

    def block_diag(x2):
        zero = jnp.zeros_like(x2)
        return jnp.concatenate([jnp.where(left, x2, zero), jnp.where(left, zero, x2)], axis=1)

    a2 = pair_lanes(a)
    inv2 = jnp.concatenate([eye, eye], axis=1) - a2
    a_hi, a_lo = _split_bf16(a2)
    pw2 = _bdot3_parts(a_hi, a_lo, block_diag(a_hi), block_diag(a_lo))
    levels = int(math.log2(c)) - 1
    for lvl in range(levels):
        inv_hi, inv_lo = _split_bf16(inv2)
        pw_hi, pw_lo = _split_bf16(pw2)
        pd_hi, pd_lo = block_diag(pw_hi), block_diag(pw_lo)
        if lvl < levels - 1:
            both = _bdot3_parts(jnp.concatenate([inv_hi, pw_hi], axis=1),
                                jnp.concatenate([inv_lo, pw_lo], axis=1), pd_hi, pd_lo)
            inv2 = inv2 + both[:, 0:c]
            pw2 = both[:, c:2 * c]
        else:
            inv2 = inv2 + _bdot3_parts(inv_hi, inv_lo, pd_hi, pd_lo)
    rhs = jnp.concatenate([v * beta, k_beta * egc], axis=2).reshape(npair, 2, c, 2 * dh)
    zero = jnp.zeros((npair, c, 2 * dh), F32)
    rhs_bd = jnp.concatenate([jnp.concatenate([rhs[:, 0], zero], axis=2),
                              jnp.concatenate([zero, rhs[:, 1]], axis=2)], axis=1)
    uw2 = _bdot3(inv2, rhs_bd)
    uw = jnp.stack([uw2[:, :, 0:2 * dh], uw2[:, :, 2 * dh:4 * dh]], axis=1).reshape(nbat, c, 2 * dh)
    u_ref[...] = uw[:, :, 0:dh].reshape(nbat * c, dh)
    wq_ref[...] = jnp.concatenate([uw[:, :, dh:2 * dh], q * egc], axis=1).astype(BF16).reshape(
        nbat * 2 * c, dh)
    k_dec = k * jnp.exp(g_last - gc)
    for b in range(nbat):
        ik_ref[b * (c + dh):(b + 1) * (c + dh), :] = jnp.concatenate(
            [intra[b], k_dec[b].T], axis=0).astype(BF16)
    gl_ref[...] = jnp.broadcast_to(jnp.exp(g_last), (nbat, SUBLANES, dh)).reshape(nbat * SUBLANES, dh)

    def step(n, states):
        rows = lambda g, size: pl.ds(pl.multiple_of((g * nb + n) * size, size), size)
        ws = [jnp.dot(wq_ref[rows(g, 2 * c), :], states[g].astype(BF16), preferred_element_type=F32)
              for g in range(hps)]
        v_new = [u_ref[rows(g, c), :] - ws[g][0:c] for g in range(hps)]
        iv = [jnp.dot(ik_ref[pl.ds(pl.multiple_of((g * nb + n) * (c + dh), c), c + dh), :],
                      v_new[g].astype(BF16), preferred_element_type=F32) for g in range(hps)]
        new_states = []
        r0 = pl.multiple_of(n * c, c)
        for g in range(hps):
            out = ws[g][c:2 * c] + iv[g][0:c]
            gl = gl_ref[pl.ds(pl.multiple_of((g * nb + n) * SUBLANES, SUBLANES), 1), :]
            new_states.append(states[g] * gl + iv[g][c:c + dh])
            ms = jnp.mean(out * out, axis=-1, keepdims=True)
            o = out * lax.rsqrt(ms + NORM_EPS) * nw_ref[...]
            o_ref[pl.ds(r0, c), g * dh:(g + 1) * dh] = (
                o * _silu(z_ref[pl.ds(r0, c), g * dh:(g + 1) * dh])).astype(o_ref.dtype)
        return tuple(new_states)

    states = lax.fori_loop(0, nb, step, tuple(st_ref[g] for g in range(hps)))
    for g in range(hps):
        st_ref[g] = states[g]


def _gdn_mixer(qkvz, tail, conv_w, gate_p, norm_w, bsz, s, heads):
    hps = math.gcd(GDN_HEADS_PER_STEP, heads)
    tb = min(GDN_TIME_BLOCK, s)
    nt = s // tb
    hblocks = heads // hps
    width = hps * HEAD_DIM
    nbat = hps * (tb // GDN_CHUNK)
    col = lambda off: pl.BlockSpec((tb, width), lambda b, h, t: (b * nt + t, off + h))
    return pl.pallas_call(
        functools.partial(_gdn_kernel, tb=tb, heads=heads, hps=hps),
        grid=(bsz, hblocks, nt),
        in_specs=[col(0), col(hblocks), col(2 * hblocks), col(3 * hblocks),
                  pl.BlockSpec((tb, LANES), lambda b, h, t: (b * nt + t, 0)),
                  pl.BlockSpec((3, GDN_CONV, width), lambda b, h, t: (0, 0, h)),
                  pl.BlockSpec((2, LANES), lambda b, h, t: (0, 0)),
                  pl.BlockSpec((1, HEAD_DIM), lambda b, h, t: (0, 0))],
        out_specs=pl.BlockSpec((tb, width), lambda b, h, t: (b * nt + t, h)),
        out_shape=jax.ShapeDtypeStruct((bsz * s, heads * HEAD_DIM), BF16),
        scratch_shapes=[pltpu.VMEM((tb + SUBLANES, width), F32)] * 3
                       + [pltpu.VMEM((hps, HEAD_DIM, HEAD_DIM), F32),
                          pltpu.VMEM((nbat * GDN_CHUNK, HEAD_DIM), F32),
                          pltpu.VMEM((nbat * 2 * GDN_CHUNK, HEAD_DIM), BF16),
                          pltpu.VMEM((nbat * (GDN_CHUNK + HEAD_DIM), GDN_CHUNK), BF16),
                          pltpu.VMEM((nbat * SUBLANES, HEAD_DIM), F32)],
        compiler_params=_cparams("parallel", "parallel", "arbitrary"),
        name="gdn_mixer",
    )(qkvz, qkvz, qkvz, qkvz, tail, conv_w, gate_p, norm_w)


def _gdn_layer(h, hn, ssq, w_all, layer, w_tail, conv_w, a_log, dt_bias, norm_w, wo_all, ple_gain,
               bsz, s):
    width = wo_all.shape[1]
    heads = width // HEAD_DIM
    qkvz = _matmul(hn, w_all, w_t=4 * width, w_row_blk=layer, out_dtype=F32, row_ssq=ssq,
                   tm=MM_TM_WIDE)
    tail = _matmul(hn, w_tail, w_t=LANES, out_dtype=F32, row_ssq=ssq)
    cw = conv_w.astype(F32).reshape(GDN_CONV, 3, width).transpose(1, 0, 2)
    lane_pad = lambda x: jnp.pad(x.astype(F32), (heads, LANES - 2 * heads))
    gate_p = jnp.stack([lane_pad(a_log), lane_pad(dt_bias)])
    gated = _gdn_mixer(qkvz, tail, cw, gate_p, norm_w.astype(F32).reshape(1, HEAD_DIM),
                       bsz, s, heads)
    return _matmul(gated, wo_all, w_row_blk=layer, out_dtype=F32, mode="residual",
                   extras=[(h, "tile", {})], norm_gain=ple_gain)


def _ssm_prep_kernel(lam_ref, bf_ref, cf_ref, bd_ref, wz_ref, wyt_ref, laml_ref):
    ns = SSM_PACK * SSM_STATE
    lam_re, lam_im, step_log = lam_ref[0, 0:1, :], lam_ref[0, 1:2, :], lam_ref[0, 2:3, :]
    step = jnp.exp(step_log)
    mag = jnp.exp(lam_re * step)
    lb_re, lb_im = mag * jnp.cos(lam_im * step), mag * jnp.sin(lam_im * step)
    den = lam_re * lam_re + lam_im * lam_im
    num_re = lb_re - 1.0
    zoh_re = (num_re * lam_re + lb_im * lam_im) / den
    zoh_im = (lb_im * lam_re - num_re * lam_im) / den
    b_re, b_im = bf_ref[0, 0], bf_ref[0, 1]
    bb_re = zoh_re * b_re - zoh_im * b_im
    bb_im = zoh_re * b_im + zoh_im * b_re
    c_re, c_im = cf_ref[0, 0], cf_ref[0, 1]
    cc_hi, cc_lo = _split_bf16(jnp.concatenate([c_re, -c_im], axis=1))
    nt_dot = lambda x, y: lax.dot_general(x, y, (((1,), (1,)), ((), ())), preferred_element_type=F32)

    bd_ref[0, 0, LANES:2 * LANES, 0:LANES] = jnp.zeros((LANES, LANES), bd_ref.dtype)
    pw_re, pw_im = jnp.ones_like(lb_re), jnp.zeros_like(lb_re)
    for d in range(SSM_L + 1):
        if d < SSM_L:
            a_re = pw_re * bb_re - pw_im * bb_im
            a_im = pw_re * bb_im + pw_im * bb_re
            a_d = jnp.concatenate([a_re, a_im], axis=1)
            a_hi, a_lo = _split_bf16(a_d)
            blk = (nt_dot(a_hi, cc_hi) + nt_dot(a_hi, cc_lo) + nt_dot(a_lo, cc_hi)).astype(bd_ref.dtype)
            d2 = d // 2
            if d % 2 == 0:
                bd_ref[0, d2, 0:LANES, 0:LANES] = blk
                bd_ref[0, d2, LANES:2 * LANES, LANES:2 * LANES] = blk
            else:
                bd_ref[0, d2, 0:LANES, LANES:2 * LANES] = blk
                if d2 + 1 < SSM_L // 2:
                    bd_ref[0, d2 + 1, LANES:2 * LANES, 0:LANES] = blk
            tin = SSM_L - 1 - d
            wz_ref[0, tin * LANES:(tin + 1) * LANES, :] = a_d.astype(wz_ref.dtype)
        if d >= 1:
            y_re = pw_re * c_re - pw_im * c_im
            y_im = pw_re * c_im + pw_im * c_re
            wyt_ref[0, (d - 1) * LANES:d * LANES, :] = jnp.concatenate(
                [y_re, -y_im], axis=1).astype(wyt_ref.dtype)
        if d == SSM_L:
            laml_ref[0] = jnp.concatenate([pw_re, pw_im], axis=1)
        pw_re, pw_im = pw_re * lb_re - pw_im * lb_im, pw_re * lb_im + pw_im * lb_re


def _ssm_prep(lam, bfull, cfull):
    nsg = lam.shape[0]
    ns = SSM_PACK * SSM_STATE
    rows = SSM_L * LANES
    return pl.pallas_call(
        _ssm_prep_kernel,
        grid=(nsg,),
        in_specs=[pl.BlockSpec((1, 3, ns), lambda g: (g, 0, 0)),
                  pl.BlockSpec((1, 2, LANES, ns), lambda g: (g, 0, 0, 0)),
                  pl.BlockSpec((1, 2, LANES, ns), lambda g: (g, 0, 0, 0))],
        out_specs=[pl.BlockSpec((1, SSM_L // 2, 2 * LANES, 2 * LANES), lambda g: (g, 0, 0, 0)),
                   pl.BlockSpec((1, rows, 2 * ns), lambda g: (g, 0, 0)),
                   pl.BlockSpec((1, rows, 2 * ns), lambda g: (g, 0, 0)),
                   pl.BlockSpec((1, 1, 2 * ns), lambda g: (g, 0, 0))],
        out_shape=[jax.ShapeDtypeStruct((nsg, SSM_L // 2, 2 * LANES, 2 * LANES), BF16),
                   jax.ShapeDtypeStruct((nsg, rows, 2 * ns), BF16),
                   jax.ShapeDtypeStruct((nsg, rows, 2 * ns), BF16),
                   jax.ShapeDtypeStruct((nsg, 1, 2 * ns), F32)],
        compiler_params=_cparams("parallel"),
        name="ssm_prep",
    )(lam, bfull, cfull)


def _ssm_scan_kernel(u_ref, bd_ref, wz_ref, wyt_ref, laml_ref, d_ref, o_ref,
                     x_ref, z_ref, xp_ref, st_ref, *, tc):
    ns = SSM_PACK * SSM_STATE

    @pl.when(pl.program_id(2) == 0)
    def _():
        st_ref[...] = jnp.zeros_like(st_ref)

    for tau in range(SSM_L):
        x_ref[:, tau * LANES:(tau + 1) * LANES] = u_ref[pl.ds(tau, tc, stride=SSM_L), :].astype(BF16)

    z_ref[...] = jnp.dot(x_ref[...], wz_ref[0], preferred_element_type=F32)

    a_re, a_im = laml_ref[0, :, 0:ns], laml_ref[0, :, ns:2 * ns]

    def row(r, carry):
        s_re, s_im = carry
        xp_ref[pl.ds(r, 1), :] = jnp.concatenate([s_re, s_im], axis=1)
        zr = z_ref[pl.ds(r, 1), :]
        n_re = a_re * s_re - a_im * s_im + zr[:, 0:ns]
        n_im = a_re * s_im + a_im * s_re + zr[:, ns:2 * ns]
        return n_re, n_im

    s_re, s_im = lax.fori_loop(0, tc, row, (st_ref[:, 0:ns], st_ref[:, ns:2 * ns]), unroll=4)
    st_ref[...] = jnp.concatenate([s_re, s_im], axis=1)

    xprev = xp_ref[...].astype(BF16)
    pair = 2 * LANES
    for t2 in range(SSM_L // 2):
        y2 = lax.dot_general(xprev, wyt_ref[0, t2 * pair:(t2 + 1) * pair, :],
                             (((1,), (1,)), ((), ())), preferred_element_type=F32)
        for tin in range(t2 + 1):
            y2 = y2 + jnp.dot(x_ref[:, tin * pair:(tin + 1) * pair], bd_ref[0, t2 - tin],
                              preferred_element_type=F32)
        for half in range(2):
            tau = 2 * t2 + half
            y = y2[:, half * LANES:(half + 1) * LANES]
            y = y + d_ref[...] * u_ref[pl.ds(tau, tc, stride=SSM_L), :]
            y = 0.5 * y * (1.0 + jnp.tanh(math.sqrt(2.0 / math.pi) * (y + 0.044715 * (y * y * y))))
            o_ref[pl.ds(tau, tc, stride=SSM_L), :] = y


def _ssm_scan(uz, bd, wz, wyt, laml, d_skip, bsz, s):
    e = d_skip.shape[1]
    nsg = e // LANES
    ns = SSM_PACK * SSM_STATE
    tb = min(4096, s)
    nt = s // tb
    tc = tb // SSM_L
    rows = SSM_L * LANES
    return pl.pallas_call(
        functools.partial(_ssm_scan_kernel, tc=tc),
        grid=(nsg, bsz, nt),
        in_specs=[pl.BlockSpec((tb, LANES), lambda g, b, t: (b * nt + t, g)),
                  pl.BlockSpec((1, SSM_L // 2, 2 * LANES, 2 * LANES), lambda g, b, t: (g, 0, 0, 0)),
                  pl.BlockSpec((1, rows, 2 * ns), lambda g, b, t: (g, 0, 0)),
                  pl.BlockSpec((1, rows, 2 * ns), lambda g, b, t: (g, 0, 0)),
                  pl.BlockSpec((1, 1, 2 * ns), lambda g, b, t: (g, 0, 0)),
                  pl.BlockSpec((1, LANES), lambda g, b, t: (0, g))],
        out_specs=pl.BlockSpec((tb, LANES), lambda g, b, t: (b * nt + t, g)),
        out_shape=jax.ShapeDtypeStruct((uz.shape[0], e), F32),
        scratch_shapes=[pltpu.VMEM((tc, rows), BF16), pltpu.VMEM((tc, 2 * ns), F32),
                        pltpu.VMEM((tc, 2 * ns), F32), pltpu.VMEM((1, 2 * ns), F32)],
        compiler_params=_cparams("parallel", "parallel", "arbitrary"),
        name="ssm_scan",
    )(uz, bd, wz, wyt, laml, d_skip)


def _ssm_layer(h, hn, ssq, w_all, layer, lam_re, lam_im, b_re, b_im, c_re, c_im, log_step, d_skip,
               wg_all, b_glu, wo_all, ple_gain, bsz, s):
    e = wo_all.shape[1]
    groups, nstate = lam_re.shape
    assert nstate == SSM_STATE and e == groups * SSM_GROUP and groups % SSM_PACK == 0
    nsg = groups // SSM_PACK
    ns = SSM_PACK * SSM_STATE
    lam = jnp.stack([lam_re.astype(F32).reshape(nsg, ns), lam_im.astype(F32).reshape(nsg, ns),
                     jnp.repeat(log_step.astype(F32), SSM_STATE).reshape(nsg, ns)], axis=1)
    eye = jnp.eye(SSM_PACK, dtype=F32)

    def expand_b(b):
        return jnp.einsum("sgpm,gh->sgmhp", b.astype(F32).reshape(nsg, SSM_PACK, SSM_STATE, SSM_GROUP),
                          eye).reshape(nsg, LANES, ns)

    def expand_c(c):
        return jnp.einsum("sgnp,gh->sgnhp", c.astype(F32).reshape(nsg, SSM_PACK, SSM_GROUP, SSM_STATE),
                          eye).reshape(nsg, LANES, ns)

    bfull = jnp.stack([expand_b(b_re), expand_b(b_im)], axis=1)
    cfull = jnp.stack([expand_c(c_re), expand_c(c_im)], axis=1)
    bd, wz, wyt, laml = _ssm_prep(lam, bfull, cfull)

    uz = _matmul(hn, w_all, w_row_blk=layer, out_dtype=F32, row_ssq=ssq, tm=MM_TM_WIDE)
    y = _ssm_scan(uz, bd, wz, wyt, laml, d_skip.astype(F32).reshape(1, e), bsz, s)
    y2 = _matmul(y, wg_all, w_row_blk=layer, out_dtype=BF16, mode="glu", cast_a=True,
                 extras=[(y, "tile", {}), (uz, "tile", {"col_off": e}),
                         (b_glu.astype(F32).reshape(1, e), "row", {})])
    return _matmul(y2, wo_all, w_row_blk=layer, out_dtype=F32, mode="residual",
                   extras=[(h, "tile", {})], norm_gain=ple_gain)


def kernel(x, p, norm_mix, fox_w_in, fox_b_f, fox_w_out, gdn_w_in, gdn_conv, gdn_a_log, gdn_dt_bias, gdn_norm, gdn_w_out, ssm_w_in, ssm_lam_re, ssm_lam_im, ssm_b_re, ssm_b_im, ssm_c_re, ssm_c_im, ssm_log_step, ssm_d, ssm_w_glu, ssm_b_glu, ssm_w_out, norm_ple, ple_w_proj, ple_w_gate, final_norm):
    bsz, s, d = x.shape
    depth, m = p.shape[0], bsz * s
    pdim = p.shape[-1]
    fox_width, gdn_width = fox_w_out.shape[1], gdn_w_out.shape[1]
    fox_wt, gdn_wt = jnp.swapaxes(fox_w_in, 1, 2), jnp.swapaxes(gdn_w_in, 1, 2)
    pad_rows = lambda t: jnp.pad(t, ((0, 0), (0, LANES - t.shape[1]), (0, 0)))
    fox_w = _cast_bf16(fox_wt, 4 * fox_width)
    fox_tail = _cast_bf16(pad_rows(fox_wt[:, 4 * fox_width:]))
    gdn_w = _cast_bf16(gdn_wt, 4 * gdn_width)
    gdn_tail = _cast_bf16(pad_rows(gdn_wt[:, 4 * gdn_width:]))
    fox_wo, gdn_wo = _cast_bf16(fox_w_out), _cast_bf16(gdn_w_out)
    ssm_w, ssm_wg, ssm_wo = _cast_bf16(ssm_w_in), _cast_bf16(ssm_w_glu), _cast_bf16(ssm_w_out)
    ple_wg, ple_wp = _cast_bf16(ple_w_gate), _cast_bf16(ple_w_proj)
    p_bf = _cast_bf16(p.reshape(depth, m, pdim))

    h = x.reshape(m, d).astype(F32)
    hn, ssq = _rmsnorm(h, norm_mix[0], BF16), None
    for i in range(depth):
        kind, j = i % 3, i // 3
        if kind == 0:
            h, hg, hssq = _fox_layer(h, hn, ssq, fox_w, j, fox_tail[j * LANES:(j + 1) * LANES],
                                     fox_b_f[j], fox_wo, norm_ple[i], bsz, s)
        elif kind == 1:
            h, hg, hssq = _gdn_layer(h, hn, ssq, gdn_w, j, gdn_tail[j * LANES:(j + 1) * LANES],
                                     gdn_conv[j], gdn_a_log[j], gdn_dt_bias[j], gdn_norm[j], gdn_wo,
                                     norm_ple[i], bsz, s)
        else:
            h, hg, hssq = _ssm_layer(h, hn, ssq, ssm_w, j, ssm_lam_re[j], ssm_lam_im[j], ssm_b_re[j],
                                     ssm_b_im[j], ssm_c_re[j], ssm_c_im[j], ssm_log_step[j], ssm_d[j],
                                     ssm_wg, ssm_b_glu[j], ssm_wo, norm_ple[i], bsz, s)
        extras = [(h, "tile", {}), (p_bf, "rows", {"row_blk": i * (m // min(MM_TM, m))}),
                  (ple_wp, "cols", {"row_blk": i, "rows": pdim})]
        outs = _matmul(hg, ple_wg, w_row_blk=i, out_dtype=F32, mode="ple", row_ssq=hssq, extras=extras,
                       norm_gain=norm_mix[i + 1] if i + 1 < depth else None)
        h, hn, ssq = outs if i + 1 < depth else (outs, None, None)
    return _rmsnorm(h, final_norm, F32).reshape(bsz, s, d)
```

```python
import functools
import math

import jax
import jax.numpy as jnp
from jax import lax
from jax.experimental import pallas as pl
from jax.experimental.pallas import tpu as pltpu

F32 = jnp.float32
BF16 = jnp.bfloat16

LANES = 128
SUBLANES = 8
VMEM_LIMIT_BYTES = 56 * 1024 * 1024

CAST_BLOCK_ELEMS = 1 << 20
A_PREP_ROWS = 32
MM_TM = 512
MM_TM_WIDE = 1024
MM_TN = 1024
NORM_EPS = 1e-6
LOG2E = 1.4426950408889634
HEAD_DIM = 128
FOX_BLOCK = 1024
GDN_CHUNK = 64
GDN_CONV = 4
GDN_HEADS_PER_STEP = 16
GDN_TIME_BLOCK = 256
SSM_GROUP = 16
SSM_STATE = 64
SSM_PACK = LANES // SSM_GROUP
SSM_L = 16
NEG_BIG = -1e30

HIGHEST = lax.Precision.HIGHEST


def _cparams(*sem):
    return pltpu.CompilerParams(dimension_semantics=sem, vmem_limit_bytes=VMEM_LIMIT_BYTES)


def _sigmoid(x):
    return 0.5 * jnp.tanh(0.5 * x) + 0.5


def _silu(x):
    hx = 0.5 * x
    return hx + hx * jnp.tanh(hx)


def _softplus(x):
    return jnp.maximum(x, 0.0) + jnp.log(1.0 + jnp.exp(-jnp.abs(x)))


def _rmsnorm_kernel(x_ref, g_ref, o_ref):
    x = x_ref[...]
    ms = jnp.mean(x * x, axis=-1, keepdims=True)
    o_ref[...] = (x * lax.rsqrt(ms + NORM_EPS) * g_ref[...]).astype(o_ref.dtype)


def _rmsnorm(x, g, out_dtype):
    m, d = x.shape
    tm = min(256, m)
    return pl.pallas_call(
        _rmsnorm_kernel,
        grid=(m // tm,),
        in_specs=[pl.BlockSpec((tm, d), lambda i: (i, 0)),
                  pl.BlockSpec((1, d), lambda i: (0, 0))],
        out_specs=pl.BlockSpec((tm, d), lambda i: (i, 0)),
        out_shape=jax.ShapeDtypeStruct((m, d), out_dtype),
        compiler_params=_cparams("parallel"),
        name="rmsnorm",
    )(x, g.reshape(1, d).astype(F32))


def _cast_kernel(x_ref, o_ref):
    o_ref[...] = x_ref[...].astype(o_ref.dtype)


def _cast_bf16(x, rows=None):
    nl, r, c = x.shape
    rows = r if rows is None else rows
    tc = min(2048, c)
    tr = min(rows, CAST_BLOCK_ELEMS // tc)
    assert rows % tr == 0 and c % tc == 0
    out = pl.pallas_call(
        _cast_kernel,
        grid=(nl, rows // tr, c // tc),
        in_specs=[pl.BlockSpec((1, tr, tc), lambda l, i, j: (l, i, j))],
        out_specs=pl.BlockSpec((1, tr, tc), lambda l, i, j: (l, i, j)),
        out_shape=jax.ShapeDtypeStruct((nl, rows, c), BF16),
        compiler_params=_cparams("parallel", "parallel", "parallel"),
        name="cast_bf16",
    )(x)
    return out.reshape(nl * rows, c)


_N_EXTRA = {"plain": 0, "colscale": 1, "residual": 1, "ple": 3, "glu": 3}


def _mm_kernel(*refs, mode, cast_a, w_t, has_ssq, emit_norm, inv_k):
    it = iter(refs)
    a_ref, w_ref = next(it), next(it)
    ssq_ref = next(it) if has_ssq else None
    extra = [next(it) for _ in range(_N_EXTRA[mode])]
    gain_ref = next(it) if emit_norm else None
    o_ref = next(it)
    hg_ref, ssq_out_ref = (next(it), next(it)) if emit_norm else (None, None)
    if cast_a:
        a_scr = next(it)

        @pl.when(pl.program_id(1) == 0)
        def _():
            def prep(r, carry):
                rows = pl.ds(pl.multiple_of(r * A_PREP_ROWS, A_PREP_ROWS), A_PREP_ROWS)
                a_scr[rows, :] = a_ref[rows, :].astype(BF16)
                return carry

            lax.fori_loop(0, a_ref.shape[0] // A_PREP_ROWS, prep, 0)

        a = a_scr[...]
    else:
        a = a_ref[...]

    acc = lax.dot_general(a, w_ref[...], (((1,), (1 if w_t else 0,)), ((), ())),
                          preferred_element_type=F32)
    if has_ssq:
        parts = ssq_ref[...]
        tot = parts[:, 0:LANES]
        for t in range(1, parts.shape[1] // LANES):
            tot = tot + parts[:, t * LANES:(t + 1) * LANES]
        acc = acc * lax.rsqrt(tot[:, 0:1] * inv_k + NORM_EPS)
    if mode == "plain":
        out = acc
    elif mode == "colscale":
        out = acc * extra[0][...]
    elif mode == "residual":
        out = extra[0][...] + acc
    elif mode == "ple":
        res_ref, p_ref, wp_ref = extra
        emb = jnp.dot(p_ref[...], wp_ref[...], preferred_element_type=F32)
        out = res_ref[...] + _sigmoid(acc) * emb
    elif mode == "glu":
        y_ref, z_ref, b_ref = extra
        out = y_ref[...] * _sigmoid(acc + b_ref[...]) * _silu(z_ref[...])
    else:
        raise ValueError(mode)
    o_ref[...] = out.astype(o_ref.dtype)
    if emit_norm:
        hg_ref[...] = (out * gain_ref[...]).astype(hg_ref.dtype)
        ssq_out_ref[...] = jnp.broadcast_to(jnp.sum(out * out, axis=-1, keepdims=True),
                                            ssq_out_ref.shape)


def _extra_spec(arr, kind, tm, tn, row_blk=0, col_off=0, rows=None):
    if kind == "tile":
        return pl.BlockSpec((tm, tn), lambda i, j: (i + row_blk, j + col_off // tn))
    if kind == "row":
        return pl.BlockSpec((1, tn), lambda i, j: (0, j))
    if kind == "rows":
        return pl.BlockSpec((tm, arr.shape[1]), lambda i, j: (i + row_blk, 0))
    if kind == "cols":
        return pl.BlockSpec((rows, tn), lambda i, j: (row_blk, j))
    raise ValueError(kind)


def _matmul(a, w, *, out_dtype, n=None, mode="plain", extras=(), cast_a=False, row_ssq=None,
            norm_gain=None, w_t=None, w_row_blk=0, w_col_off=0, w_outer=False, tm=MM_TM, tn=MM_TN):
    m, kdim = a.shape
    n = (w_t if w_t else w.shape[1]) if n is None else n
    tm = min(tm, m)
    tn = functools.reduce(math.gcd, [tn, n, w_col_off, w_t or 0])
    assert m % tm == 0 and tn % LANES == 0 and len(extras) == _N_EXTRA[mode]
    operands = [a, w]
    if w_t:
        first = (w_row_blk * w_t + w_col_off) // tn
        w_spec = pl.BlockSpec((tn, kdim), lambda i, j: (first + j, 0))
    else:
        w_spec = pl.BlockSpec((kdim, tn), lambda i, j: (w_row_blk, j + w_col_off // tn))
    in_specs = [pl.BlockSpec((tm, kdim), lambda i, j: (i, 0)), w_spec]
    if row_ssq is not None:
        operands.append(row_ssq)
        in_specs.append(pl.BlockSpec((tm, row_ssq.shape[1]), lambda i, j: (i, 0)))
    for arr, kind, opts in extras:
        assert opts.get("col_off", 0) % tn == 0
        operands.append(arr)
        in_specs.append(_extra_spec(arr, kind, tm, tn, **opts))
    tile = pl.BlockSpec((tm, tn), lambda i, j: (i, j))
    out_specs, out_shape = tile, jax.ShapeDtypeStruct((m, n), out_dtype)
    if norm_gain is not None:
        operands.append(norm_gain.astype(F32).reshape(1, n))
        in_specs.append(pl.BlockSpec((1, tn), lambda i, j: (0, j)))
        out_specs = [tile, tile, pl.BlockSpec((tm, LANES), lambda i, j: (i, j))]
        out_shape = [out_shape, jax.ShapeDtypeStruct((m, n), BF16),
                     jax.ShapeDtypeStruct((m, n // tn * LANES), F32)]
    scratch = [pltpu.VMEM((tm, kdim), BF16)] if cast_a else []
    grid = (m // tm, n // tn)
    if w_outer:
        assert not cast_a
        flip = lambda sp: pl.BlockSpec(sp.block_shape, lambda j, i, f=sp.index_map: f(i, j))
        in_specs = [flip(sp) for sp in in_specs]
        out_specs = [flip(sp) for sp in out_specs] if norm_gain is not None else flip(out_specs)
        grid = grid[::-1]
    return pl.pallas_call(
        functools.partial(_mm_kernel, mode=mode, cast_a=cast_a, w_t=bool(w_t),
                          has_ssq=row_ssq is not None,
                          emit_norm=norm_gain is not None, inv_k=1.0 / kdim),
        grid=grid,
        in_specs=in_specs,
        out_specs=out_specs,
        out_shape=out_shape,
        scratch_shapes=scratch,
        compiler_params=_cparams("parallel", "arbitrary"),
        name="matmul_" + mode,
    )(*operands)


def _fox_cum_kernel(f_ref, b_ref, o_ref, carry_ref, *, ts):
    @pl.when(pl.program_id(1) == 0)
    def _():
        carry_ref[...] = jnp.zeros_like(carry_ref)

    x = f_ref[...] + b_ref[...]
    log_f = jnp.minimum(x, 0.0) - jnp.log(1.0 + jnp.exp(-jnp.abs(x)))
    r = lax.broadcasted_iota(jnp.int32, (ts, ts), 0)
    c = lax.broadcasted_iota(jnp.int32, (ts, ts), 1)
    lower = (c <= r).astype(F32)
    cum = jnp.dot(lower, log_f, precision=HIGHEST, preferred_element_type=F32) + carry_ref[...]
    carry_ref[...] = cum[ts - 1:ts, :]
    o_ref[0] = cum.T


def _fox_cum(f_logit, b_f, bsz, s):
    ts = min(512, s)
    return pl.pallas_call(
        functools.partial(_fox_cum_kernel, ts=ts),
        grid=(bsz, s // ts),
        in_specs=[pl.BlockSpec((ts, LANES), lambda b, t: (b * (s // ts) + t, 0)),
                  pl.BlockSpec((1, LANES), lambda b, t: (0, 0))],
        out_specs=pl.BlockSpec((1, LANES, ts), lambda b, t: (b, 0, t)),
        out_shape=jax.ShapeDtypeStruct((bsz, LANES, s), F32),
        scratch_shapes=[pltpu.VMEM((1, LANES), F32)],
        compiler_params=_cparams("parallel", "arbitrary"),
        name="fox_cum",
    )(f_logit, b_f)


def _fox_attn_kernel(q_ref, k_ref, v_ref, c_ref, z_ref, o_ref, sa_ref, sb_ref, *, tq):
    tk = tq // 2
    i = pl.program_id(2)
    q = q_ref[...]
    c0 = c_ref[0, 0, :, pl.ds(pl.multiple_of(i * tq, tq), LANES)][:, 0:1]

    def scores(qrows, j):
        start = j * tk if isinstance(j, int) else pl.multiple_of(j * tk, tk)
        s = lax.dot_general(qrows, k_ref[pl.ds(start, tk), :], (((1,), (1,)), ((), ())),
                            preferred_element_type=F32)
        return s + (c0 - c_ref[0, 0, :, pl.ds(start, tk)]) * LOG2E

    def update(s, j, m, acc, width=1):
        rows = width * tk
        start = j * tk if isinstance(j, int) else pl.multiple_of(j * tk, tk)
        vb = jnp.concatenate([v_ref[pl.ds(start, rows), :], jnp.ones((rows, HEAD_DIM), BF16)], axis=1)
        m_new = jnp.maximum(m, jnp.max(s, axis=-1, keepdims=True))
        p = jnp.exp2(s - m_new)
        acc = jnp.exp2(m - m_new) * acc + jnp.dot(p.astype(BF16), vb, preferred_element_type=F32)
        return m_new, acc

    sa_ref[...] = scores(q, 0)

    def pair(t, carry):
        m, acc = carry
        sb_ref[...] = scores(q, 2 * t + 1)
        m, acc = update(sa_ref[...], 2 * t, m, acc)
        sa_ref[...] = scores(q, 2 * t + 2)
        return update(sb_ref[...], 2 * t + 1, m, acc)

    init = (jnp.full((tq, 1), NEG_BIG, F32), jnp.zeros((tq, 2 * HEAD_DIM), F32))
    carry = lax.fori_loop(0, i // 2, lambda t2, cr: pair(2 * t2 + 1, pair(2 * t2, cr)), init)

    def finish(a, rows):
        o = a[:, 0:HEAD_DIM] / a[:, HEAD_DIM:HEAD_DIM + 1]
        o_ref[rows, :] = (o * _silu(z_ref[rows, :])).astype(o_ref.dtype)

    def tail(carry, odd):
        m, acc = pair(i - 1, carry) if odd else carry
        r = lax.broadcasted_iota(jnp.int32, (tk, tk), 0)
        c = lax.broadcasted_iota(jnp.int32, (tk, tk), 1)
        s_bot = jnp.concatenate(
            [sa_ref[tk:tq, :], jnp.where(c <= r, scores(q[tk:tq], 2 * i + 1), NEG_BIG)], axis=1)
        _, acc_top = update(jnp.where(c <= r, sa_ref[0:tk, :], NEG_BIG), 2 * i, m[0:tk], acc[0:tk])
        _, acc_bot = update(s_bot, 2 * i, m[tk:tq], acc[tk:tq], width=2)
        finish(acc_top, slice(0, tk))
        finish(acc_bot, slice(tk, tq))

    lax.cond(i % 2 == 1, lambda cr: tail(cr, True), lambda cr: tail(cr, False), carry)


def _fox_attention(qkv, cum4, z, bsz, s, heads):
    tq = min(FOX_BLOCK, s)
    nq = s // tq
    return pl.pallas_call(
        functools.partial(_fox_attn_kernel, tq=tq),
        grid=(bsz, heads, nq),
        in_specs=[pl.BlockSpec((tq, HEAD_DIM), lambda b, h, i: (b * nq + i, h)),
                  pl.BlockSpec((s, HEAD_DIM), lambda b, h, i: (b, heads + h)),
                  pl.BlockSpec((s, HEAD_DIM), lambda b, h, i: (b, 2 * heads + h)),
                  pl.BlockSpec((1, 1, 1, s), lambda b, h, i: (b, h, 0, 0)),
                  pl.BlockSpec((tq, HEAD_DIM), lambda b, h, i: (b * nq + i, h))],
        out_specs=pl.BlockSpec((tq, HEAD_DIM), lambda b, h, i: (b * nq + i, h)),
        out_shape=jax.ShapeDtypeStruct((bsz * s, heads * HEAD_DIM), BF16),
        scratch_shapes=[pltpu.VMEM((tq, tq // 2), F32)] * 2,
        compiler_params=_cparams("parallel", "parallel", "arbitrary"),
        name="fox_attn",
    )(qkv, qkv, qkv, cum4, z)


def _fox_layer(h, hn, ssq, w_all, layer, w_tail, b_f, wo_all, ple_gain, bsz, s):
    width = wo_all.shape[1]
    heads = width // HEAD_DIM
    q_scale = jnp.concatenate([jnp.full((width,), HEAD_DIM ** -0.5 * LOG2E, F32),
                               jnp.ones((2 * width,), F32)]).reshape(1, 3 * width)
    qkv = _matmul(hn, w_all, n=3 * width, w_t=4 * width, w_row_blk=layer, out_dtype=BF16,
                  mode="colscale", extras=[(q_scale, "row", {})], row_ssq=ssq, tm=MM_TM_WIDE)
    z = _matmul(hn, w_all, n=width, w_t=4 * width, w_row_blk=layer, w_col_off=3 * width,
                out_dtype=F32, row_ssq=ssq, w_outer=True, tm=MM_TM_WIDE)
    f_logit = _matmul(hn, w_tail, w_t=LANES, out_dtype=F32, row_ssq=ssq, tm=MM_TM_WIDE)
    b_row = jnp.pad(b_f.astype(F32), (0, LANES - heads)).reshape(1, LANES)
    cum = _fox_cum(f_logit, b_row, bsz, s)
    gated = _fox_attention(qkv, cum.reshape(bsz, LANES, 1, s), z, bsz, s, heads)
    return _matmul(gated, wo_all, w_row_blk=layer, out_dtype=F32, mode="residual",
                   extras=[(h, "tile", {})], norm_gain=ple_gain, w_outer=True)


def _bdot(a, b):
    return jnp.einsum("bmk,bkn->bmn", a, b, preferred_element_type=F32)


def _split_bf16(x):
    hi = x.astype(BF16)
    return hi, (x - hi.astype(F32)).astype(BF16)


def _bdot3_parts(a_hi, a_lo, b_hi, b_lo):
    return _bdot(a_hi, b_hi) + _bdot(a_hi, b_lo) + _bdot(a_lo, b_hi)


def _bdot3(a, b):
    return _bdot3_parts(*_split_bf16(a), *_split_bf16(b))


def _bdot_exact_lhs(a_bf16, b):
    b_hi = b.astype(BF16)
    b_mid, b_lo = _split_bf16(b - b_hi.astype(F32))
    return _bdot(a_bf16, b_hi) + _bdot(a_bf16, b_mid) + _bdot(a_bf16, b_lo)


def _gdn_kernel(q_ref, k_ref, v_ref, z_ref, t_ref, cw_ref, gp_ref, nw_ref, o_ref,
                xq_ref, xk_ref, xv_ref, st_ref, u_ref, wq_ref, ik_ref, gl_ref, *, tb, heads, hps):
    hb = pl.program_id(1)
    c, dh = GDN_CHUNK, HEAD_DIM
    nb = tb // c
    nbat = hps * nb
    pad = SUBLANES

    @pl.when(pl.program_id(2) == 0)
    def _():
        for r in (xq_ref, xk_ref, xv_ref):
            r[0:pad, :] = jnp.zeros((pad, hps * dh), F32)
        st_ref[...] = jnp.zeros_like(st_ref)

    def conv(x_ref, xs_ref, col):
        x = x_ref[...]
        xs_ref[pad:pad + tb, :] = x
        xs = xs_ref[...]
        y = x * cw_ref[col, GDN_CONV - 1:GDN_CONV, :]
        for j in range(GDN_CONV - 1):
            shifted = pltpu.roll(xs, GDN_CONV - 1 - j, axis=0)[pad:pad + tb, :]
            y = y + shifted * cw_ref[col, j:j + 1, :]
        xs_ref[0:pad, :] = x[tb - pad:tb, :]
        return _silu(y)

    def by_chunk(x):
        w = x.shape[1] // hps
        return jnp.concatenate([x[:, g * w:(g + 1) * w].reshape(nb, c, w) for g in range(hps)], axis=0)

    def l2n(x):
        return x * lax.rsqrt(jnp.sum(x * x, axis=-1, keepdims=True) + NORM_EPS)

    q = l2n(by_chunk(conv(q_ref, xq_ref, 0))) * (dh ** -0.5)
    k = l2n(by_chunk(conv(k_ref, xk_ref, 1)))
    v = by_chunk(conv(v_ref, xv_ref, 2))

    t = t_ref[...]
    lane = lax.broadcasted_iota(jnp.int32, (tb, LANES), 1)
    beta_all = _sigmoid(t)
    g_all = -jnp.exp(gp_ref[0:1, :]) * _softplus(t + gp_ref[1:2, :])

    def pick(x, first):
        cols = [jnp.sum(jnp.where(lane == first + hb * hps + g, x, 0.0), axis=-1, keepdims=True)
                for g in range(hps)]
        return jnp.concatenate([col.reshape(nb, c, 1) for col in cols], axis=0)

    beta = pick(beta_all, 0)
    g = pick(g_all, heads)

    ri = lax.broadcasted_iota(jnp.int32, (c, c), 0)
    ci = lax.broadcasted_iota(jnp.int32, (c, c), 1)
    incl = ci <= ri
    strict = ci < ri
    eye = (ci == ri).astype(F32)
    tri = jnp.broadcast_to(incl.astype(BF16), (nbat, c, c))
    rl = lax.broadcasted_iota(jnp.int32, (c, LANES), 0)
    cl = lax.broadcasted_iota(jnp.int32, (c, LANES), 1)
    seg_mask = (rl > cl) | (cl == c)

    seg = _bdot_exact_lhs(tri, jnp.where(seg_mask, g, 0.0))
    gc = seg[:, :, c:c + 1]
    g_last = seg[:, c - 1:c, c:c + 1]
    decay = jnp.exp(jnp.where(incl, seg[:, :, 0:c], NEG_BIG))
    egc = jnp.exp(gc)
    k_beta = k * beta
    kq = jnp.einsum("bmd,bnd->bmn", jnp.concatenate([k_beta, q], axis=1).astype(BF16),
                    k.astype(BF16), preferred_element_type=F32)
    a = jnp.where(strict, kq[:, 0:c] * decay, 0.0)
    intra = kq[:, c:2 * c] * decay
    npair = nbat // 2
    pair_lanes = lambda x: jnp.concatenate(
        [x.reshape(npair, 2, c, x.shape[2])[:, 0], x.reshape(npair, 2, c, x.shape[2])[:, 1]], axis=2)
    left = lax.broadcasted_iota(jnp.int32, (c, 2 * c), 1) < c

    def block_diag(x2):
        zero = jnp.zeros_like(x2)
        return jnp.concatenate([jnp.where(left, x2, zero), jnp.where(left, zero, x2)], axis=1)

    a2 = pair_lanes(a)
    inv2 = jnp.concatenate([eye, eye], axis=1) - a2
    a_hi, a_lo = _split_bf16(a2)
    pw2 = _bdot3_parts(a_hi, a_lo, block_diag(a_hi), block_diag(a_lo))
    levels = int(math.log2(c)) - 1
    for lvl in range(levels):
        inv_hi, inv_lo = _split_bf16(inv2)
        pw_hi, pw_lo = _split_bf16(pw2)
        pd_hi, pd_lo = block_diag(pw_hi), block_diag(pw_lo)
        if lvl < levels - 1:
            both = _bdot3_parts(jnp.concatenate([inv_hi, pw_hi], axis=1),
                                jnp.concatenate([inv_lo, pw_lo], axis=1), pd_hi, pd_lo)
            inv2 = inv2 + both[:, 0:c]
            pw2 = both[:, c:2 * c]
        else:
            inv2 = inv2 + _bdot3_parts(inv_hi, inv_lo, pd_hi, pd_lo)
    rhs = jnp.concatenate([v * beta, k_beta * egc], axis=2).reshape(npair, 2, c, 2 * dh)
    zero = jnp.zeros((npair, c, 2 * dh), F32)
    rhs_bd = jnp.concatenate([jnp.concatenate([rhs[:, 0], zero], axis=2),
                              jnp.concatenate([zero, rhs[:, 1]], axis=2)], axis=1)
    uw2 = _bdot3(inv2, rhs_bd)
    uw = jnp.stack([uw2[:, :, 0:2 * dh], uw2[:, :, 2 * dh:4 * dh]], axis=1).reshape(nbat, c, 2 * dh)
    u_ref[...] = uw[:, :, 0:dh].reshape(nbat * c, dh)
    wq_ref[...] = jnp.concatenate([uw[:, :, dh:2 * dh], q * egc], axis=1).astype(BF16).reshape(
        nbat * 2 * c, dh)
    k_dec = k * jnp.exp(g_last - gc)
    for b in range(nbat):
        ik_ref[b * (c + dh):(b + 1) * (c + dh), :] = jnp.concatenate(
            [intra[b], k_dec[b].T], axis=0).astype(BF16)
    gl_ref[...] = jnp.broadcast_to(jnp.exp(g_last), (nbat, SUBLANES, dh)).reshape(nbat * SUBLANES, dh)

    def step(n, states):
        rows = lambda g, size: pl.ds(pl.multiple_of((g * nb + n) * size, size), size)
        ws = [jnp.dot(wq_ref[rows(g, 2 * c), :], states[g].astype(BF16), preferred_element_type=F32)
              for g in range(hps)]
        v_new = [u_ref[rows(g, c), :] - ws[g][0:c] for g in range(hps)]
        iv = [jnp.dot(ik_ref[pl.ds(pl.multiple_of((g * nb + n) * (c + dh), c), c + dh), :],
                      v_new[g].astype(BF16), preferred_element_type=F32) for g in range(hps)]
        new_states = []
        r0 = pl.multiple_of(n * c, c)
        for g in range(hps):
            out = ws[g][c:2 * c] + iv[g][0:c]
            gl = gl_ref[pl.ds(pl.multiple_of((g * nb + n) * SUBLANES, SUBLANES), 1), :]
            new_states.append(states[g] * gl + iv[g][c:c + dh])
            ms = jnp.mean(out * out, axis=-1, keepdims=True)
            o = out * lax.rsqrt(ms + NORM_EPS) * nw_ref[...]
            o_ref[pl.ds(r0, c), g * dh:(g + 1) * dh] = (
                o * _silu(z_ref[pl.ds(r0, c), g * dh:(g + 1) * dh])).astype(o_ref.dtype)
        return tuple(new_states)

    states = lax.fori_loop(0, nb, step, tuple(st_ref[g] for g in range(hps)))
    for g in range(hps):
        st_ref[g] = states[g]


def _gdn_mixer(qkvz, tail, conv_w, gate_p, norm_w, bsz, s, heads):
    hps = math.gcd(GDN_HEADS_PER_STEP, heads)
    tb = min(GDN_TIME_BLOCK, s)
    nt = s // tb
    hblocks = heads // hps
    width = hps * HEAD_DIM
    nbat = hps * (tb // GDN_CHUNK)
    col = lambda off: pl.BlockSpec((tb, width), lambda b, h, t: (b * nt + t, off + h))
    return pl.pallas_call(
        functools.partial(_gdn_kernel, tb=tb, heads=heads, hps=hps),
        grid=(bsz, hblocks, nt),
        in_specs=[col(0), col(hblocks), col(2 * hblocks), col(3 * hblocks),
                  pl.BlockSpec((tb, LANES), lambda b, h, t: (b * nt + t, 0)),
                  pl.BlockSpec((3, GDN_CONV, width), lambda b, h, t: (0, 0, h)),
                  pl.BlockSpec((2, LANES), lambda b, h, t: (0, 0)),
                  pl.BlockSpec((1, HEAD_DIM), lambda b, h, t: (0, 0))],
        out_specs=pl.BlockSpec((tb, width), lambda b, h, t: (b * nt + t, h)),
        out_shape=jax.ShapeDtypeStruct((bsz * s, heads * HEAD_DIM), BF16),
        scratch_shapes=[pltpu.VMEM((tb + SUBLANES, width), F32)] * 3
                       + [pltpu.VMEM((hps, HEAD_DIM, HEAD_DIM), F32),
                          pltpu.VMEM((nbat * GDN_CHUNK, HEAD_DIM), F32),
                          pltpu.VMEM((nbat * 2 * GDN_CHUNK, HEAD_DIM), BF16),
                          pltpu.VMEM((nbat * (GDN_CHUNK + HEAD_DIM), GDN_CHUNK), BF16),
                          pltpu.VMEM((nbat * SUBLANES, HEAD_DIM), F32)],
        compiler_params=_cparams("parallel", "parallel", "arbitrary"),
        name="gdn_mixer",
    )(qkvz, qkvz, qkvz, qkvz, tail, conv_w, gate_p, norm_w)


def _gdn_layer(h, hn, ssq, w_all, layer, w_tail, conv_w, a_log, dt_bias, norm_w, wo_all, ple_gain,
               bsz, s):
    width = wo_all.shape[1]
    heads = width // HEAD_DIM
    qkvz = _matmul(hn, w_all, w_t=4 * width, w_row_blk=layer, out_dtype=F32, row_ssq=ssq,
                   tm=MM_TM_WIDE)
    tail = _matmul(hn, w_tail, w_t=LANES, out_dtype=F32, row_ssq=ssq, tm=MM_TM_WIDE)
    cw = conv_w.astype(F32).reshape(GDN_CONV, 3, width).transpose(1, 0, 2)
    lane_pad = lambda x: jnp.pad(x.astype(F32), (heads, LANES - 2 * heads))
    gate_p = jnp.stack([lane_pad(a_log), lane_pad(dt_bias)])
    gated = _gdn_mixer(qkvz, tail, cw, gate_p, norm_w.astype(F32).reshape(1, HEAD_DIM),
                       bsz, s, heads)
    return _matmul(gated, wo_all, w_row_blk=layer, out_dtype=F32, mode="residual",
                   extras=[(h, "tile", {})], norm_gain=ple_gain, w_outer=True)


def _ssm_prep_kernel(lam_ref, bf_ref, cf_ref, bd_ref, wz_ref, wyt_ref, laml_ref):
    ns = SSM_PACK * SSM_STATE
    lam_re, lam_im, step_log = lam_ref[0, 0:1, :], lam_ref[0, 1:2, :], lam_ref[0, 2:3, :]
    step = jnp.exp(step_log)
    mag = jnp.exp(lam_re * step)
    lb_re, lb_im = mag * jnp.cos(lam_im * step), mag * jnp.sin(lam_im * step)
    den = lam_re * lam_re + lam_im * lam_im
    num_re = lb_re - 1.0
    zoh_re = (num_re * lam_re + lb_im * lam_im) / den
    zoh_im = (lb_im * lam_re - num_re * lam_im) / den
    b_re, b_im = bf_ref[0, 0], bf_ref[0, 1]
    bb_re = zoh_re * b_re - zoh_im * b_im
    bb_im = zoh_re * b_im + zoh_im * b_re
    c_re, c_im = cf_ref[0, 0], cf_ref[0, 1]
    cc_hi, cc_lo = _split_bf16(jnp.concatenate([c_re, -c_im], axis=1))
    nt_dot = lambda x, y: lax.dot_general(x, y, (((1,), (1,)), ((), ())), preferred_element_type=F32)

    bd_ref[0, 0, LANES:2 * LANES, 0:LANES] = jnp.zeros((LANES, LANES), bd_ref.dtype)
    pw_re, pw_im = jnp.ones_like(lb_re), jnp.zeros_like(lb_re)
    for d in range(SSM_L + 1):
        if d < SSM_L:
            a_re = pw_re * bb_re - pw_im * bb_im
            a_im = pw_re * bb_im + pw_im * bb_re
            a_d = jnp.concatenate([a_re, a_im], axis=1)
            a_hi, a_lo = _split_bf16(a_d)
            blk = (nt_dot(a_hi, cc_hi) + nt_dot(a_hi, cc_lo) + nt_dot(a_lo, cc_hi)).astype(bd_ref.dtype)
            d2 = d // 2
            if d % 2 == 0:
                bd_ref[0, d2, 0:LANES, 0:LANES] = blk
                bd_ref[0, d2, LANES:2 * LANES, LANES:2 * LANES] = blk
            else:
                bd_ref[0, d2, 0:LANES, LANES:2 * LANES] = blk
                if d2 + 1 < SSM_L // 2:
                    bd_ref[0, d2 + 1, LANES:2 * LANES, 0:LANES] = blk
            tin = SSM_L - 1 - d
            wz_ref[0, tin * LANES:(tin + 1) * LANES, :] = a_d.astype(wz_ref.dtype)
        if d >= 1:
            y_re = pw_re * c_re - pw_im * c_im
            y_im = pw_re * c_im + pw_im * c_re
            wyt_ref[0, (d - 1) * LANES:d * LANES, :] = jnp.concatenate(
                [y_re, -y_im], axis=1).astype(wyt_ref.dtype)
        if d == SSM_L:
            laml_ref[0] = jnp.concatenate([pw_re, pw_im], axis=1)
        pw_re, pw_im = pw_re * lb_re - pw_im * lb_im, pw_re * lb_im + pw_im * lb_re


def _ssm_prep(lam, bfull, cfull):
    nsg = lam.shape[0]
    ns = SSM_PACK * SSM_STATE
    rows = SSM_L * LANES
    return pl.pallas_call(
        _ssm_prep_kernel,
        grid=(nsg,),
        in_specs=[pl.BlockSpec((1, 3, ns), lambda g: (g, 0, 0)),
                  pl.BlockSpec((1, 2, LANES, ns), lambda g: (g, 0, 0, 0)),
                  pl.BlockSpec((1, 2, LANES, ns), lambda g: (g, 0, 0, 0))],
        out_specs=[pl.BlockSpec((1, SSM_L // 2, 2 * LANES, 2 * LANES), lambda g: (g, 0, 0, 0)),
                   pl.BlockSpec((1, rows, 2 * ns), lambda g: (g, 0, 0)),
                   pl.BlockSpec((1, rows, 2 * ns), lambda g: (g, 0, 0)),
                   pl.BlockSpec((1, 1, 2 * ns), lambda g: (g, 0, 0))],
        out_shape=[jax.ShapeDtypeStruct((nsg, SSM_L // 2, 2 * LANES, 2 * LANES), BF16),
                   jax.ShapeDtypeStruct((nsg, rows, 2 * ns), BF16),
                   jax.ShapeDtypeStruct((nsg, rows, 2 * ns), BF16),
                   jax.ShapeDtypeStruct((nsg, 1, 2 * ns), F32)],
        compiler_params=_cparams("parallel"),
        name="ssm_prep",
    )(lam, bfull, cfull)


def _ssm_scan_kernel(u_ref, bd_ref, wz_ref, wyt_ref, laml_ref, d_ref, o_ref,
                     x_ref, z_ref, xp_ref, st_ref, *, tc):
    ns = SSM_PACK * SSM_STATE

    @pl.when(pl.program_id(2) == 0)
    def _():
        st_ref[...] = jnp.zeros_like(st_ref)

    for tau in range(SSM_L):
        x_ref[:, tau * LANES:(tau + 1) * LANES] = u_ref[pl.ds(tau, tc, stride=SSM_L), :].astype(BF16)

    z_ref[...] = jnp.dot(x_ref[...], wz_ref[0], preferred_element_type=F32)

    a_re, a_im = laml_ref[0, :, 0:ns], laml_ref[0, :, ns:2 * ns]

    def row(r, carry):
        s_re, s_im = carry
        xp_ref[pl.ds(r, 1), :] = jnp.concatenate([s_re, s_im], axis=1)
        zr = z_ref[pl.ds(r, 1), :]
        n_re = a_re * s_re - a_im * s_im + zr[:, 0:ns]
        n_im = a_re * s_im + a_im * s_re + zr[:, ns:2 * ns]
        return n_re, n_im

    s_re, s_im = lax.fori_loop(0, tc, row, (st_ref[:, 0:ns], st_ref[:, ns:2 * ns]), unroll=4)
    st_ref[...] = jnp.concatenate([s_re, s_im], axis=1)

    xprev = xp_ref[...].astype(BF16)
    pair = 2 * LANES
    for t2 in range(SSM_L // 2):
        y2 = lax.dot_general(xprev, wyt_ref[0, t2 * pair:(t2 + 1) * pair, :],
                             (((1,), (1,)), ((), ())), preferred_element_type=F32)
        for tin in range(t2 + 1):
            y2 = y2 + jnp.dot(x_ref[:, tin * pair:(tin + 1) * pair], bd_ref[0, t2 - tin],
                              preferred_element_type=F32)
        for half in range(2):
            tau = 2 * t2 + half
            y = y2[:, half * LANES:(half + 1) * LANES]
            y = y + d_ref[...] * u_ref[pl.ds(tau, tc, stride=SSM_L), :]
            y = 0.5 * y * (1.0 + jnp.tanh(math.sqrt(2.0 / math.pi) * (y + 0.044715 * (y * y * y))))
            o_ref[pl.ds(tau, tc, stride=SSM_L), :] = y


def _ssm_scan(uz, bd, wz, wyt, laml, d_skip, bsz, s):
    e = d_skip.shape[1]
    nsg = e // LANES
    ns = SSM_PACK * SSM_STATE
    tb = min(4096, s)
    nt = s // tb
    tc = tb // SSM_L
    rows = SSM_L * LANES
    return pl.pallas_call(
        functools.partial(_ssm_scan_kernel, tc=tc),
        grid=(nsg, bsz, nt),
        in_specs=[pl.BlockSpec((tb, LANES), lambda g, b, t: (b * nt + t, g)),
                  pl.BlockSpec((1, SSM_L // 2, 2 * LANES, 2 * LANES), lambda g, b, t: (g, 0, 0, 0)),
                  pl.BlockSpec((1, rows, 2 * ns), lambda g, b, t: (g, 0, 0)),
                  pl.BlockSpec((1, rows, 2 * ns), lambda g, b, t: (g, 0, 0)),
                  pl.BlockSpec((1, 1, 2 * ns), lambda g, b, t: (g, 0, 0)),
                  pl.BlockSpec((1, LANES), lambda g, b, t: (0, g))],
        out_specs=pl.BlockSpec((tb, LANES), lambda g, b, t: (b * nt + t, g)),
        out_shape=jax.ShapeDtypeStruct((uz.shape[0], e), F32),
        scratch_shapes=[pltpu.VMEM((tc, rows), BF16), pltpu.VMEM((tc, 2 * ns), F32),
                        pltpu.VMEM((tc, 2 * ns), F32), pltpu.VMEM((1, 2 * ns), F32)],
        compiler_params=_cparams("parallel", "parallel", "arbitrary"),
        name="ssm_scan",
    )(uz, bd, wz, wyt, laml, d_skip)


def _ssm_layer(h, hn, ssq, w_all, layer, lam_re, lam_im, b_re, b_im, c_re, c_im, log_step, d_skip,
               wg_all, b_glu, wo_all, ple_gain, bsz, s):
    e = wo_all.shape[1]
    groups, nstate = lam_re.shape
    assert nstate == SSM_STATE and e == groups * SSM_GROUP and groups % SSM_PACK == 0
    nsg = groups // SSM_PACK
    ns = SSM_PACK * SSM_STATE
    lam = jnp.stack([lam_re.astype(F32).reshape(nsg, ns), lam_im.astype(F32).reshape(nsg, ns),
                     jnp.repeat(log_step.astype(F32), SSM_STATE).reshape(nsg, ns)], axis=1)
    eye = jnp.eye(SSM_PACK, dtype=F32)

    def expand_b(b):
        return jnp.einsum("sgpm,gh->sgmhp", b.astype(F32).reshape(nsg, SSM_PACK, SSM_STATE, SSM_GROUP),
                          eye).reshape(nsg, LANES, ns)

    def expand_c(c):
        return jnp.einsum("sgnp,gh->sgnhp", c.astype(F32).reshape(nsg, SSM_PACK, SSM_GROUP, SSM_STATE),
                          eye).reshape(nsg, LANES, ns)

    bfull = jnp.stack([expand_b(b_re), expand_b(b_im)], axis=1)
    cfull = jnp.stack([expand_c(c_re), expand_c(c_im)], axis=1)
    bd, wz, wyt, laml = _ssm_prep(lam, bfull, cfull)

    uz = _matmul(hn, w_all, w_row_blk=layer, out_dtype=F32, row_ssq=ssq, tm=MM_TM_WIDE)
    y = _ssm_scan(uz, bd, wz, wyt, laml, d_skip.astype(F32).reshape(1, e), bsz, s)
    y2 = _matmul(y, wg_all, w_row_blk=layer, out_dtype=BF16, mode="glu", cast_a=True,
                 extras=[(y, "tile", {}), (uz, "tile", {"col_off": e}),
                         (b_glu.astype(F32).reshape(1, e), "row", {})])
    return _matmul(y2, wo_all, w_row_blk=layer, out_dtype=F32, mode="residual",
                   extras=[(h, "tile", {})], norm_gain=ple_gain, w_outer=True)


def kernel(x, p, norm_mix, fox_w_in, fox_b_f, fox_w_out, gdn_w_in, gdn_conv, gdn_a_log, gdn_dt_bias, gdn_norm, gdn_w_out, ssm_w_in, ssm_lam_re, ssm_lam_im, ssm_b_re, ssm_b_im, ssm_c_re, ssm_c_im, ssm_log_step, ssm_d, ssm_w_glu, ssm_b_glu, ssm_w_out, norm_ple, ple_w_proj, ple_w_gate, final_norm):
    bsz, s, d = x.shape
    depth, m = p.shape[0], bsz * s
    pdim = p.shape[-1]
    fox_width, gdn_width = fox_w_out.shape[1], gdn_w_out.shape[1]
    fox_wt, gdn_wt = jnp.swapaxes(fox_w_in, 1, 2), jnp.swapaxes(gdn_w_in, 1, 2)
    pad_rows = lambda t: jnp.pad(t, ((0, 0), (0, LANES - t.shape[1]), (0, 0)))
    fox_w = _cast_bf16(fox_wt, 4 * fox_width)
    fox_tail = _cast_bf16(pad_rows(fox_wt[:, 4 * fox_width:]))
    gdn_w = _cast_bf16(gdn_wt, 4 * gdn_width)
    gdn_tail = _cast_bf16(pad_rows(gdn_wt[:, 4 * gdn_width:]))
    fox_wo, gdn_wo = _cast_bf16(fox_w_out), _cast_bf16(gdn_w_out)
    ssm_w, ssm_wg, ssm_wo = _cast_bf16(ssm_w_in), _cast_bf16(ssm_w_glu), _cast_bf16(ssm_w_out)
    ple_wg, ple_wp = _cast_bf16(ple_w_gate), _cast_bf16(ple_w_proj)
    p_bf = _cast_bf16(p.reshape(depth, m, pdim))

    h = x.reshape(m, d).astype(F32)
    hn, ssq = _rmsnorm(h, norm_mix[0], BF16), None
    for i in range(depth):
        kind, j = i % 3, i // 3
        if kind == 0:
            h, hg, hssq = _fox_layer(h, hn, ssq, fox_w, j, fox_tail[j * LANES:(j + 1) * LANES],
                                     fox_b_f[j], fox_wo, norm_ple[i], bsz, s)
        elif kind == 1:
            h, hg, hssq = _gdn_layer(h, hn, ssq, gdn_w, j, gdn_tail[j * LANES:(j + 1) * LANES],
                                     gdn_conv[j], gdn_a_log[j], gdn_dt_bias[j], gdn_norm[j], gdn_wo,
                                     norm_ple[i], bsz, s)
        else:
            h, hg, hssq = _ssm_layer(h, hn, ssq, ssm_w, j, ssm_lam_re[j], ssm_lam_im[j], ssm_b_re[j],
                                     ssm_b_im[j], ssm_c_re[j], ssm_c_im[j], ssm_log_step[j], ssm_d[j],
                                     ssm_wg, ssm_b_glu[j], ssm_wo, norm_ple[i], bsz, s)
        extras = [(h, "tile", {}), (p_bf, "rows", {"row_blk": i * (m // min(MM_TM, m))}),
                  (ple_wp, "cols", {"row_blk": i, "rows": pdim})]
        outs = _matmul(hg, ple_wg, w_row_blk=i, out_dtype=F32, mode="ple", row_ssq=hssq, extras=extras,
                       norm_gain=norm_mix[i + 1] if i + 1 < depth else None, w_outer=True)
        h, hn, ssq = outs if i + 1 < depth else (outs, None, None)
    return _rmsnorm(h, final_norm, F32).reshape(bsz, s, d)
```

```python
import functools
import math

import jax
import jax.numpy as jnp
from jax import lax
from jax.experimental import pallas as pl
from jax.experimental.pallas import tpu as pltpu

F32 = jnp.float32
BF16 = jnp.bfloat16

LANES = 128
SUBLANES = 8
VMEM_LIMIT_BYTES = 56 * 1024 * 1024

CAST_BLOCK_ELEMS = 1 << 20
A_PREP_ROWS = 32
MM_TM = 512
MM_TM_WIDE = 1024
MM_TN = 1024
NORM_EPS = 1e-6
LOG2E = 1.4426950408889634
HEAD_DIM = 128
FOX_BLOCK = 1024
GDN_CHUNK = 64
GDN_CONV = 4
GDN_HEADS_PER_STEP = 16
GDN_TIME_BLOCK = 256
SSM_GROUP = 16
SSM_STATE = 64
SSM_PACK = LANES // SSM_GROUP
SSM_L = 16
NEG_BIG = -1e30

HIGHEST = lax.Precision.HIGHEST


def _cparams(*sem):
    return pltpu.CompilerParams(dimension_semantics=sem, vmem_limit_bytes=VMEM_LIMIT_BYTES)


def _sigmoid(x):
    return 0.5 * jnp.tanh(0.5 * x) + 0.5


def _silu(x):
    hx = 0.5 * x
    return hx + hx * jnp.tanh(hx)


def _softplus(x):
    return jnp.maximum(x, 0.0) + jnp.log(1.0 + jnp.exp(-jnp.abs(x)))


def _rmsnorm_kernel(x_ref, g_ref, o_ref):
    x = x_ref[...]
    ms = jnp.mean(x * x, axis=-1, keepdims=True)
    o_ref[...] = (x * lax.rsqrt(ms + NORM_EPS) * g_ref[...]).astype(o_ref.dtype)


def _rmsnorm(x, g, out_dtype):
    m, d = x.shape
    tm = min(256, m)
    return pl.pallas_call(
        _rmsnorm_kernel,
        grid=(m // tm,),
        in_specs=[pl.BlockSpec((tm, d), lambda i: (i, 0)),
                  pl.BlockSpec((1, d), lambda i: (0, 0))],
        out_specs=pl.BlockSpec((tm, d), lambda i: (i, 0)),
        out_shape=jax.ShapeDtypeStruct((m, d), out_dtype),
        compiler_params=_cparams("parallel"),
        name="rmsnorm",
    )(x, g.reshape(1, d).astype(F32))


def _cast_kernel(x_ref, o_ref):
    o_ref[...] = x_ref[...].astype(o_ref.dtype)


def _cast_bf16(x, rows=None):
    nl, r, c = x.shape
    rows = r if rows is None else rows
    tc = min(2048, c)
    tr = min(rows, CAST_BLOCK_ELEMS // tc)
    assert rows % tr == 0 and c % tc == 0
    out = pl.pallas_call(
        _cast_kernel,
        grid=(nl, rows // tr, c // tc),
        in_specs=[pl.BlockSpec((1, tr, tc), lambda l, i, j: (l, i, j))],
        out_specs=pl.BlockSpec((1, tr, tc), lambda l, i, j: (l, i, j)),
        out_shape=jax.ShapeDtypeStruct((nl, rows, c), BF16),
        compiler_params=_cparams("parallel", "parallel", "parallel"),
        name="cast_bf16",
    )(x)
    return out.reshape(nl * rows, c)


_N_EXTRA = {"plain": 0, "colscale": 1, "residual": 1, "ple": 3, "glu": 2}


def _mm_kernel(*refs, mode, cast_a, w_t, has_ssq, emit_norm, inv_k):
    it = iter(refs)
    a_ref, w_ref = next(it), next(it)
    ssq_ref = next(it) if has_ssq else None
    extra = [next(it) for _ in range(_N_EXTRA[mode])]
    gain_ref = next(it) if emit_norm else None
    o_ref = next(it)
    hg_ref, ssq_out_ref = (next(it), next(it)) if emit_norm else (None, None)
    if cast_a:
        a_scr = next(it)

        @pl.when(pl.program_id(1) == 0)
        def _():
            def prep(r, carry):
                rows = pl.ds(pl.multiple_of(r * A_PREP_ROWS, A_PREP_ROWS), A_PREP_ROWS)
                a_scr[rows, :] = a_ref[rows, :].astype(BF16)
                return carry

            lax.fori_loop(0, a_ref.shape[0] // A_PREP_ROWS, prep, 0)

        a = a_scr[...]
    else:
        a = a_ref[...]

    acc = lax.dot_general(a, w_ref[...], (((1,), (1 if w_t else 0,)), ((), ())),
                          preferred_element_type=F32)
    if has_ssq:
        parts = ssq_ref[...]
        tot = parts[:, 0:LANES]
        for t in range(1, parts.shape[1] // LANES):
            tot = tot + parts[:, t * LANES:(t + 1) * LANES]
        acc = acc * lax.rsqrt(tot[:, 0:1] * inv_k + NORM_EPS)
    if mode == "plain":
        out = acc
    elif mode == "colscale":
        out = acc * extra[0][...]
    elif mode == "residual":
        out = extra[0][...] + acc
    elif mode == "ple":
        res_ref, p_ref, wp_ref = extra
        emb = jnp.dot(p_ref[...], wp_ref[...], preferred_element_type=F32)
        out = res_ref[...] + _sigmoid(acc) * emb
    elif mode == "glu":
        z_ref, b_ref = extra
        tn = o_ref.shape[1]
        y = a_ref[:, pl.ds(pl.multiple_of(pl.program_id(1) * tn, tn), tn)]
        out = y * _sigmoid(acc + b_ref[...]) * _silu(z_ref[...])
    else:
        raise ValueError(mode)
    o_ref[...] = out.astype(o_ref.dtype)
    if emit_norm:
        hg_ref[...] = (out * gain_ref[...]).astype(hg_ref.dtype)
        ssq_out_ref[...] = jnp.broadcast_to(jnp.sum(out * out, axis=-1, keepdims=True),
                                            ssq_out_ref.shape)


def _extra_spec(arr, kind, tm, tn, row_blk=0, col_off=0, rows=None):
    if kind == "tile":
        return pl.BlockSpec((tm, tn), lambda i, j: (i + row_blk, j + col_off // tn))
    if kind == "row":
        return pl.BlockSpec((1, tn), lambda i, j: (0, j))
    if kind == "rows":
        return pl.BlockSpec((tm, arr.shape[1]), lambda i, j: (i + row_blk, 0))
    if kind == "cols":
        return pl.BlockSpec((rows, tn), lambda i, j: (row_blk, j))
    raise ValueError(kind)


def _matmul(a, w, *, out_dtype, n=None, mode="plain", extras=(), cast_a=False, row_ssq=None,
            norm_gain=None, w_t=None, w_row_blk=0, w_col_off=0, w_outer=False, tm=MM_TM, tn=MM_TN):
    m, kdim = a.shape
    n = (w_t if w_t else w.shape[1]) if n is None else n
    tm = min(tm, m)
    tn = functools.reduce(math.gcd, [tn, n, w_col_off, w_t or 0])
    assert m % tm == 0 and tn % LANES == 0 and len(extras) == _N_EXTRA[mode]
    operands = [a, w]
    if w_t:
        first = (w_row_blk * w_t + w_col_off) // tn
        w_spec = pl.BlockSpec((tn, kdim), lambda i, j: (first + j, 0))
    else:
        w_spec = pl.BlockSpec((kdim, tn), lambda i, j: (w_row_blk, j + w_col_off // tn))
    in_specs = [pl.BlockSpec((tm, kdim), lambda i, j: (i, 0)), w_spec]
    if row_ssq is not None:
        operands.append(row_ssq)
        in_specs.append(pl.BlockSpec((tm, row_ssq.shape[1]), lambda i, j: (i, 0)))
    for arr, kind, opts in extras:
        assert opts.get("col_off", 0) % tn == 0
        operands.append(arr)
        in_specs.append(_extra_spec(arr, kind, tm, tn, **opts))
    tile = pl.BlockSpec((tm, tn), lambda i, j: (i, j))
    out_specs, out_shape = tile, jax.ShapeDtypeStruct((m, n), out_dtype)
    if norm_gain is not None:
        operands.append(norm_gain.astype(F32).reshape(1, n))
        in_specs.append(pl.BlockSpec((1, tn), lambda i, j: (0, j)))
        out_specs = [tile, tile, pl.BlockSpec((tm, LANES), lambda i, j: (i, j))]
        out_shape = [out_shape, jax.ShapeDtypeStruct((m, n), BF16),
                     jax.ShapeDtypeStruct((m, n // tn * LANES), F32)]
    scratch = [pltpu.VMEM((tm, kdim), BF16)] if cast_a else []
    grid = (m // tm, n // tn)
    if w_outer:
        assert not cast_a
        flip = lambda sp: pl.BlockSpec(sp.block_shape, lambda j, i, f=sp.index_map: f(i, j))
        in_specs = [flip(sp) for sp in in_specs]
        out_specs = [flip(sp) for sp in out_specs] if norm_gain is not None else flip(out_specs)
        grid = grid[::-1]
    return pl.pallas_call(
        functools.partial(_mm_kernel, mode=mode, cast_a=cast_a, w_t=bool(w_t),
                          has_ssq=row_ssq is not None,
                          emit_norm=norm_gain is not None, inv_k=1.0 / kdim),
        grid=grid,
        in_specs=in_specs,
        out_specs=out_specs,
        out_shape=out_shape,
        scratch_shapes=scratch,
        compiler_params=_cparams("parallel", "arbitrary"),
        name="matmul_" + mode,
    )(*operands)


def _fox_cum_kernel(f_ref, b_ref, o_ref, carry_ref, *, ts):
    @pl.when(pl.program_id(1) == 0)
    def _():
        carry_ref[...] = jnp.zeros_like(carry_ref)

    x = f_ref[...] + b_ref[...]
    log_f = jnp.minimum(x, 0.0) - jnp.log(1.0 + jnp.exp(-jnp.abs(x)))
    r = lax.broadcasted_iota(jnp.int32, (ts, ts), 0)
    c = lax.broadcasted_iota(jnp.int32, (ts, ts), 1)
    lower = (c <= r).astype(F32)
    cum = jnp.dot(lower, log_f, precision=HIGHEST, preferred_element_type=F32) + carry_ref[...]
    carry_ref[...] = cum[ts - 1:ts, :]
    o_ref[0] = cum.T


def _fox_cum(f_logit, b_f, bsz, s):
    ts = min(512, s)
    return pl.pallas_call(
        functools.partial(_fox_cum_kernel, ts=ts),
        grid=(bsz, s // ts),
        in_specs=[pl.BlockSpec((ts, LANES), lambda b, t: (b * (s // ts) + t, 0)),
                  pl.BlockSpec((1, LANES), lambda b, t: (0, 0))],
        out_specs=pl.BlockSpec((1, LANES, ts), lambda b, t: (b, 0, t)),
        out_shape=jax.ShapeDtypeStruct((bsz, LANES, s), F32),
        scratch_shapes=[pltpu.VMEM((1, LANES), F32)],
        compiler_params=_cparams("parallel", "arbitrary"),
        name="fox_cum",
    )(f_logit, b_f)


def _fox_attn_kernel(q_ref, k_ref, v_ref, c_ref, z_ref, o_ref, sa_ref, sb_ref, *, tq):
    tk = tq // 2
    i = pl.program_id(2)
    q = q_ref[...]
    c0 = c_ref[0, 0, :, pl.ds(pl.multiple_of(i * tq, tq), LANES)][:, 0:1]

    def scores(qrows, j):
        start = j * tk if isinstance(j, int) else pl.multiple_of(j * tk, tk)
        s = lax.dot_general(qrows, k_ref[pl.ds(start, tk), :], (((1,), (1,)), ((), ())),
                            preferred_element_type=F32)
        return s + (c0 - c_ref[0, 0, :, pl.ds(start, tk)]) * LOG2E

    def update(s, j, m, acc, width=1):
        rows = width * tk
        start = j * tk if isinstance(j, int) else pl.multiple_of(j * tk, tk)
        vb = jnp.concatenate([v_ref[pl.ds(start, rows), :], jnp.ones((rows, HEAD_DIM), BF16)], axis=1)
        m_new = jnp.maximum(m, jnp.max(s, axis=-1, keepdims=True))
        p = jnp.exp2(s - m_new)
        acc = jnp.exp2(m - m_new) * acc + jnp.dot(p.astype(BF16), vb, preferred_element_type=F32)
        return m_new, acc

    sa_ref[...] = scores(q, 0)

    def pair(t, carry):
        m, acc = carry
        sb_ref[...] = scores(q, 2 * t + 1)
        m, acc = update(sa_ref[...], 2 * t, m, acc)
        sa_ref[...] = scores(q, 2 * t + 2)
        return update(sb_ref[...], 2 * t + 1, m, acc)

    init = (jnp.full((tq, 1), NEG_BIG, F32), jnp.zeros((tq, 2 * HEAD_DIM), F32))
    carry = lax.fori_loop(0, i // 2, lambda t2, cr: pair(2 * t2 + 1, pair(2 * t2, cr)), init)

    def finish(a, rows):
        o = a[:, 0:HEAD_DIM] / a[:, HEAD_DIM:HEAD_DIM + 1]
        o_ref[rows, :] = (o * _silu(z_ref[rows, :])).astype(o_ref.dtype)

    def tail(carry, odd):
        m, acc = pair(i - 1, carry) if odd else carry
        r = lax.broadcasted_iota(jnp.int32, (tk, tk), 0)
        c = lax.broadcasted_iota(jnp.int32, (tk, tk), 1)
        s_bot = jnp.concatenate(
            [sa_ref[tk:tq, :], jnp.where(c <= r, scores(q[tk:tq], 2 * i + 1), NEG_BIG)], axis=1)
        _, acc_top = update(jnp.where(c <= r, sa_ref[0:tk, :], NEG_BIG), 2 * i, m[0:tk], acc[0:tk])
        _, acc_bot = update(s_bot, 2 * i, m[tk:tq], acc[tk:tq], width=2)
        finish(acc_top, slice(0, tk))
        finish(acc_bot, slice(tk, tq))

    lax.cond(i % 2 == 1, lambda cr: tail(cr, True), lambda cr: tail(cr, False), carry)


def _fox_attention(qkv, cum4, z, bsz, s, heads):
    tq = min(FOX_BLOCK, s)
    nq = s // tq
    return pl.pallas_call(
        functools.partial(_fox_attn_kernel, tq=tq),
        grid=(bsz, heads, nq),
        in_specs=[pl.BlockSpec((tq, HEAD_DIM), lambda b, h, i: (b * nq + i, h)),
                  pl.BlockSpec((s, HEAD_DIM), lambda b, h, i: (b, heads + h)),
                  pl.BlockSpec((s, HEAD_DIM), lambda b, h, i: (b, 2 * heads + h)),
                  pl.BlockSpec((1, 1, 1, s), lambda b, h, i: (b, h, 0, 0)),
                  pl.BlockSpec((tq, HEAD_DIM), lambda b, h, i: (b * nq + i, h))],
        out_specs=pl.BlockSpec((tq, HEAD_DIM), lambda b, h, i: (b * nq + i, h)),
        out_shape=jax.ShapeDtypeStruct((bsz * s, heads * HEAD_DIM), BF16),
        scratch_shapes=[pltpu.VMEM((tq, tq // 2), F32)] * 2,
        compiler_params=_cparams("parallel", "parallel", "arbitrary"),
        name="fox_attn",
    )(qkv, qkv, qkv, cum4, z)


def _fox_layer(h, hn, ssq, w_all, layer, w_tail, b_f, wo_all, ple_gain, bsz, s):
    width = wo_all.shape[1]
    heads = width // HEAD_DIM
    q_scale = jnp.concatenate([jnp.full((width,), HEAD_DIM ** -0.5 * LOG2E, F32),
                               jnp.ones((2 * width,), F32)]).reshape(1, 3 * width)
    qkv = _matmul(hn, w_all, n=3 * width, w_t=4 * width, w_row_blk=layer, out_dtype=BF16,
                  mode="colscale", extras=[(q_scale, "row", {})], row_ssq=ssq, tm=MM_TM_WIDE)
    z = _matmul(hn, w_all, n=width, w_t=4 * width, w_row_blk=layer, w_col_off=3 * width,
                out_dtype=F32, row_ssq=ssq, w_outer=True, tm=MM_TM_WIDE)
    f_logit = _matmul(hn, w_tail, w_t=LANES, out_dtype=F32, row_ssq=ssq, tm=MM_TM_WIDE)
    b_row = jnp.pad(b_f.astype(F32), (0, LANES - heads)).reshape(1, LANES)
    cum = _fox_cum(f_logit, b_row, bsz, s)
    gated = _fox_attention(qkv, cum.reshape(bsz, LANES, 1, s), z, bsz, s, heads)
    return _matmul(gated, wo_all, w_row_blk=layer, out_dtype=F32, mode="residual",
                   extras=[(h, "tile", {})], norm_gain=ple_gain, w_outer=True)


def _bdot(a, b):
    return jnp.einsum("bmk,bkn->bmn", a, b, preferred_element_type=F32)


def _split_bf16(x):
    hi = x.astype(BF16)
    return hi, (x - hi.astype(F32)).astype(BF16)


def _bdot3_parts(a_hi, a_lo, b_hi, b_lo):
    return _bdot(a_hi, b_hi) + _bdot(a_hi, b_lo) + _bdot(a_lo, b_hi)


def _bdot3(a, b):
    return _bdot3_parts(*_split_bf16(a), *_split_bf16(b))


def _bdot_exact_lhs(a_bf16, b):
    b_hi = b.astype(BF16)
    b_mid, b_lo = _split_bf16(b - b_hi.astype(F32))
    return _bdot(a_bf16, b_hi) + _bdot(a_bf16, b_mid) + _bdot(a_bf16, b_lo)


def _gdn_kernel(q_ref, k_ref, v_ref, z_ref, t_ref, cw_ref, gp_ref, nw_ref, o_ref,
                xq_ref, xk_ref, xv_ref, st_ref, u_ref, wq_ref, ik_ref, gl_ref, *, tb, heads, hps):
    hb = pl.program_id(1)
    c, dh = GDN_CHUNK, HEAD_DIM
    nb = tb // c
    nbat = hps * nb
    pad = SUBLANES

    @pl.when(pl.program_id(2) == 0)
    def _():
        for r in (xq_ref, xk_ref, xv_ref):
            r[0:pad, :] = jnp.zeros((pad, hps * dh), F32)
        st_ref[...] = jnp.zeros_like(st_ref)

    def conv(x_ref, xs_ref, col):
        x = x_ref[...]
        xs_ref[pad:pad + tb, :] = x
        xs = xs_ref[...]
        y = x * cw_ref[col, GDN_CONV - 1:GDN_CONV, :]
        for j in range(GDN_CONV - 1):
            shifted = pltpu.roll(xs, GDN_CONV - 1 - j, axis=0)[pad:pad + tb, :]
            y = y + shifted * cw_ref[col, j:j + 1, :]
        xs_ref[0:pad, :] = x[tb - pad:tb, :]
        return _silu(y)

    def by_chunk(x):
        w = x.shape[1] // hps
        return jnp.concatenate([x[:, g * w:(g + 1) * w].reshape(nb, c, w) for g in range(hps)], axis=0)

    def l2n(x):
        return x * lax.rsqrt(jnp.sum(x * x, axis=-1, keepdims=True) + NORM_EPS)

    q = l2n(by_chunk(conv(q_ref, xq_ref, 0))) * (dh ** -0.5)
    k = l2n(by_chunk(conv(k_ref, xk_ref, 1)))
    v = by_chunk(conv(v_ref, xv_ref, 2))

    t = t_ref[...]
    lane = lax.broadcasted_iota(jnp.int32, (tb, LANES), 1)
    beta_all = _sigmoid(t)
    g_all = -jnp.exp(gp_ref[0:1, :]) * _softplus(t + gp_ref[1:2, :])

    def pick(x, first):
        cols = [jnp.sum(jnp.where(lane == first + hb * hps + g, x, 0.0), axis=-1, keepdims=True)
                for g in range(hps)]
        return jnp.concatenate([col.reshape(nb, c, 1) for col in cols], axis=0)

    beta = pick(beta_all, 0)
    g = pick(g_all, heads)

    ri = lax.broadcasted_iota(jnp.int32, (c, c), 0)
    ci = lax.broadcasted_iota(jnp.int32, (c, c), 1)
    incl = ci <= ri
    strict = ci < ri
    eye = (ci == ri).astype(F32)
    tri = jnp.broadcast_to(incl.astype(BF16), (nbat, c, c))
    rl = lax.broadcasted_iota(jnp.int32, (c, LANES), 0)
    cl = lax.broadcasted_iota(jnp.int32, (c, LANES), 1)
    seg_mask = (rl > cl) | (cl == c)

    seg = _bdot_exact_lhs(tri, jnp.where(seg_mask, g, 0.0))
    gc = seg[:, :, c:c + 1]
    g_last = seg[:, c - 1:c, c:c + 1]
    decay = jnp.exp(jnp.where(incl, seg[:, :, 0:c], NEG_BIG))
    egc = jnp.exp(gc)
    k_beta = k * beta
    kq = jnp.einsum("bmd,bnd->bmn", jnp.concatenate([k_beta, q], axis=1).astype(BF16),
                    k.astype(BF16), preferred_element_type=F32)
    a = jnp.where(strict, kq[:, 0:c] * decay, 0.0)
    intra = kq[:, c:2 * c] * decay
    npair = nbat // 2
    pair_lanes = lambda x: jnp.concatenate(
        [x.reshape(npair, 2, c, x.shape[2])[:, 0], x.reshape(npair, 2, c, x.shape[2])[:, 1]], axis=2)
    left = lax.broadcasted_iota(jnp.int32, (c, 2 * c), 1) < c

    def block_diag(x2):
        zero = jnp.zeros_like(x2)
        return jnp.concatenate([jnp.where(left, x2, zero), jnp.where(left, zero, x2)], axis=1)

    a2 = pair_lanes(a)
    inv2 = jnp.concatenate([eye, eye], axis=1) - a2
    a_hi, a_lo = _split_bf16(a2)
    pw2 = _bdot3_parts(a_hi, a_lo, block_diag(a_hi), block_diag(a_lo))
    levels = int(math.log2(c)) - 1
    for lvl in range(levels):
        inv_hi, inv_lo = _split_bf16(inv2)
        pw_hi, pw_lo = _split_bf16(pw2)
        pd_hi, pd_lo = block_diag(pw_hi), block_diag(pw_lo)
        if lvl < levels - 1:
            both = _bdot3_parts(jnp.concatenate([inv_hi, pw_hi], axis=1),
                                jnp.concatenate([inv_lo, pw_lo], axis=1), pd_hi, pd_lo)
            inv2 = inv2 + both[:, 0:c]
            pw2 = both[:, c:2 * c]
        else:
            inv2 = inv2 + _bdot3_parts(inv_hi, inv_lo, pd_hi, pd_lo)
    rhs = jnp.concatenate([v * beta, k_beta * egc], axis=2).reshape(npair, 2, c, 2 * dh)
    zero = jnp.zeros((npair, c, 2 * dh), F32)
    rhs_bd = jnp.concatenate([jnp.concatenate([rhs[:, 0], zero], axis=2),
                              jnp.concatenate([zero, rhs[:, 1]], axis=2)], axis=1)
    uw2 = _bdot3(inv2, rhs_bd)
    uw = jnp.stack([uw2[:, :, 0:2 * dh], uw2[:, :, 2 * dh:4 * dh]], axis=1).reshape(nbat, c, 2 * dh)
    u_ref[...] = uw[:, :, 0:dh].reshape(nbat * c, dh)
    wq_ref[...] = jnp.concatenate([uw[:, :, dh:2 * dh], q * egc], axis=1).astype(BF16).reshape(
        nbat * 2 * c, dh)
    k_dec = k * jnp.exp(g_last - gc)
    for b in range(nbat):
        ik_ref[b * (c + dh):(b + 1) * (c + dh), :] = jnp.concatenate(
            [intra[b], k_dec[b].T], axis=0).astype(BF16)
    gl_ref[...] = jnp.broadcast_to(jnp.exp(g_last), (nbat, SUBLANES, dh)).reshape(nbat * SUBLANES, dh)

    def step(n, states):
        rows = lambda g, size: pl.ds(pl.multiple_of((g * nb + n) * size, size), size)
        ws = [jnp.dot(wq_ref[rows(g, 2 * c), :], states[g].astype(BF16), preferred_element_type=F32)
              for g in range(hps)]
        v_new = [u_ref[rows(g, c), :] - ws[g][0:c] for g in range(hps)]
        iv = [jnp.dot(ik_ref[pl.ds(pl.multiple_of((g * nb + n) * (c + dh), c), c + dh), :],
                      v_new[g].astype(BF16), preferred_element_type=F32) for g in range(hps)]
        new_states = []
        r0 = pl.multiple_of(n * c, c)
        for g in range(hps):
            out = ws[g][c:2 * c] + iv[g][0:c]
            gl = gl_ref[pl.ds(pl.multiple_of((g * nb + n) * SUBLANES, SUBLANES), 1), :]
            new_states.append(states[g] * gl + iv[g][c:c + dh])
            ms = jnp.mean(out * out, axis=-1, keepdims=True)
            o = out * lax.rsqrt(ms + NORM_EPS) * nw_ref[...]
            o_ref[pl.ds(r0, c), g * dh:(g + 1) * dh] = (
                o * _silu(z_ref[pl.ds(r0, c), g * dh:(g + 1) * dh])).astype(o_ref.dtype)
        return tuple(new_states)

    states = lax.fori_loop(0, nb, step, tuple(st_ref[g] for g in range(hps)))
    for g in range(hps):
        st_ref[g] = states[g]


def _gdn_mixer(qkvz, tail, conv_w, gate_p, norm_w, bsz, s, heads):
    hps = math.gcd(GDN_HEADS_PER_STEP, heads)
    tb = min(GDN_TIME_BLOCK, s)
    nt = s // tb
    hblocks = heads // hps
    width = hps * HEAD_DIM
    nbat = hps * (tb // GDN_CHUNK)
    col = lambda off: pl.BlockSpec((tb, width), lambda b, h, t: (b * nt + t, off + h))
    return pl.pallas_call(
        functools.partial(_gdn_kernel, tb=tb, heads=heads, hps=hps),
        grid=(bsz, hblocks, nt),
        in_specs=[col(0), col(hblocks), col(2 * hblocks), col(3 * hblocks),
                  pl.BlockSpec((tb, LANES), lambda b, h, t: (b * nt + t, 0)),
                  pl.BlockSpec((3, GDN_CONV, width), lambda b, h, t: (0, 0, h)),
                  pl.BlockSpec((2, LANES), lambda b, h, t: (0, 0)),
                  pl.BlockSpec((1, HEAD_DIM), lambda b, h, t: (0, 0))],
        out_specs=pl.BlockSpec((tb, width), lambda b, h, t: (b * nt + t, h)),
        out_shape=jax.ShapeDtypeStruct((bsz * s, heads * HEAD_DIM), BF16),
        scratch_shapes=[pltpu.VMEM((tb + SUBLANES, width), F32)] * 3
                       + [pltpu.VMEM((hps, HEAD_DIM, HEAD_DIM), F32),
                          pltpu.VMEM((nbat * GDN_CHUNK, HEAD_DIM), F32),
                          pltpu.VMEM((nbat * 2 * GDN_CHUNK, HEAD_DIM), BF16),
                          pltpu.VMEM((nbat * (GDN_CHUNK + HEAD_DIM), GDN_CHUNK), BF16),
                          pltpu.VMEM((nbat * SUBLANES, HEAD_DIM), F32)],
        compiler_params=_cparams("parallel", "parallel", "arbitrary"),
        name="gdn_mixer",
    )(qkvz, qkvz, qkvz, qkvz, tail, conv_w, gate_p, norm_w)


def _gdn_layer(h, hn, ssq, w_all, layer, w_tail, conv_w, a_log, dt_bias, norm_w, wo_all, ple_gain,
               bsz, s):
    width = wo_all.shape[1]
    heads = width // HEAD_DIM
    qkvz = _matmul(hn, w_all, w_t=4 * width, w_row_blk=layer, out_dtype=F32, row_ssq=ssq,
                   tm=MM_TM_WIDE)
    tail = _matmul(hn, w_tail, w_t=LANES, out_dtype=F32, row_ssq=ssq, tm=MM_TM_WIDE)
    cw = conv_w.astype(F32).reshape(GDN_CONV, 3, width).transpose(1, 0, 2)
    lane_pad = lambda x: jnp.pad(x.astype(F32), (heads, LANES - 2 * heads))
    gate_p = jnp.stack([lane_pad(a_log), lane_pad(dt_bias)])
    gated = _gdn_mixer(qkvz, tail, cw, gate_p, norm_w.astype(F32).reshape(1, HEAD_DIM),
                       bsz, s, heads)
    return _matmul(gated, wo_all, w_row_blk=layer, out_dtype=F32, mode="residual",
                   extras=[(h, "tile", {})], norm_gain=ple_gain, w_outer=True)


def _ssm_prep_kernel(lam_ref, bf_ref, cf_ref, bd_ref, wz_ref, wyt_ref, laml_ref):
    ns = SSM_PACK * SSM_STATE
    lam_re, lam_im, step_log = lam_ref[0, 0:1, :], lam_ref[0, 1:2, :], lam_ref[0, 2:3, :]
    step = jnp.exp(step_log)
    mag = jnp.exp(lam_re * step)
    lb_re, lb_im = mag * jnp.cos(lam_im * step), mag * jnp.sin(lam_im * step)
    den = lam_re * lam_re + lam_im * lam_im
    num_re = lb_re - 1.0
    zoh_re = (num_re * lam_re + lb_im * lam_im) / den
    zoh_im = (lb_im * lam_re - num_re * lam_im) / den
    b_re, b_im = bf_ref[0, 0], bf_ref[0, 1]
    bb_re = zoh_re * b_re - zoh_im * b_im
    bb_im = zoh_re * b_im + zoh_im * b_re
    c_re, c_im = cf_ref[0, 0], cf_ref[0, 1]
    cc_hi, cc_lo = _split_bf16(jnp.concatenate([c_re, -c_im], axis=1))
    nt_dot = lambda x, y: lax.dot_general(x, y, (((1,), (1,)), ((), ())), preferred_element_type=F32)

    bd_ref[0, 0, LANES:2 * LANES, 0:LANES] = jnp.zeros((LANES, LANES), bd_ref.dtype)
    pw_re, pw_im = jnp.ones_like(lb_re), jnp.zeros_like(lb_re)
    for d in range(SSM_L + 1):
        if d < SSM_L:
            a_re = pw_re * bb_re - pw_im * bb_im
            a_im = pw_re * bb_im + pw_im * bb_re
            a_d = jnp.concatenate([a_re, a_im], axis=1)
            a_hi, a_lo = _split_bf16(a_d)
            blk = (nt_dot(a_hi, cc_hi) + nt_dot(a_hi, cc_lo) + nt_dot(a_lo, cc_hi)).astype(bd_ref.dtype)
            d2 = d // 2
            if d % 2 == 0:
                bd_ref[0, d2, 0:LANES, 0:LANES] = blk
                bd_ref[0, d2, LANES:2 * LANES, LANES:2 * LANES] = blk
            else:
                bd_ref[0, d2, 0:LANES, LANES:2 * LANES] = blk
                if d2 + 1 < SSM_L // 2:
                    bd_ref[0, d2 + 1, LANES:2 * LANES, 0:LANES] = blk
            tin = SSM_L - 1 - d
            wz_ref[0, tin * LANES:(tin + 1) * LANES, :] = a_d.astype(wz_ref.dtype)
        if d >= 1:
            y_re = pw_re * c_re - pw_im * c_im
            y_im = pw_re * c_im + pw_im * c_re
            wyt_ref[0, (d - 1) * LANES:d * LANES, :] = jnp.concatenate(
                [y_re, -y_im], axis=1).astype(wyt_ref.dtype)
        if d == SSM_L:
            laml_ref[0] = jnp.concatenate([pw_re, pw_im], axis=1)
        pw_re, pw_im = pw_re * lb_re - pw_im * lb_im, pw_re * lb_im + pw_im * lb_re


def _ssm_prep(lam, bfull, cfull):
    nsg = lam.shape[0]
    ns = SSM_PACK * SSM_STATE
    rows = SSM_L * LANES
    return pl.pallas_call(
        _ssm_prep_kernel,
        grid=(nsg,),
        in_specs=[pl.BlockSpec((1, 3, ns), lambda g: (g, 0, 0)),
                  pl.BlockSpec((1, 2, LANES, ns), lambda g: (g, 0, 0, 0)),
                  pl.BlockSpec((1, 2, LANES, ns), lambda g: (g, 0, 0, 0))],
        out_specs=[pl.BlockSpec((1, SSM_L // 2, 2 * LANES, 2 * LANES), lambda g: (g, 0, 0, 0)),
                   pl.BlockSpec((1, rows, 2 * ns), lambda g: (g, 0, 0)),
                   pl.BlockSpec((1, rows, 2 * ns), lambda g: (g, 0, 0)),
                   pl.BlockSpec((1, 1, 2 * ns), lambda g: (g, 0, 0))],
        out_shape=[jax.ShapeDtypeStruct((nsg, SSM_L // 2, 2 * LANES, 2 * LANES), BF16),
                   jax.ShapeDtypeStruct((nsg, rows, 2 * ns), BF16),
                   jax.ShapeDtypeStruct((nsg, rows, 2 * ns), BF16),
                   jax.ShapeDtypeStruct((nsg, 1, 2 * ns), F32)],
        compiler_params=_cparams("parallel"),
        name="ssm_prep",
    )(lam, bfull, cfull)


def _ssm_scan_kernel(u_ref, bd_ref, wz_ref, wyt_ref, laml_ref, d_ref, o_ref,
                     x_ref, z_ref, xp_ref, st_ref, *, tc):
    ns = SSM_PACK * SSM_STATE

    @pl.when(pl.program_id(2) == 0)
    def _():
        st_ref[...] = jnp.zeros_like(st_ref)

    for tau in range(SSM_L):
        x_ref[:, tau * LANES:(tau + 1) * LANES] = u_ref[pl.ds(tau, tc, stride=SSM_L), :].astype(BF16)

    z_ref[...] = jnp.dot(x_ref[...], wz_ref[0], preferred_element_type=F32)

    a_re, a_im = laml_ref[0, :, 0:ns], laml_ref[0, :, ns:2 * ns]

    def row(r, carry):
        s_re, s_im = carry
        xp_ref[pl.ds(r, 1), :] = jnp.concatenate([s_re, s_im], axis=1)
        zr = z_ref[pl.ds(r, 1), :]
        n_re = a_re * s_re - a_im * s_im + zr[:, 0:ns]
        n_im = a_re * s_im + a_im * s_re + zr[:, ns:2 * ns]
        return n_re, n_im

    s_re, s_im = lax.fori_loop(0, tc, row, (st_ref[:, 0:ns], st_ref[:, ns:2 * ns]), unroll=4)
    st_ref[...] = jnp.concatenate([s_re, s_im], axis=1)

    xprev = xp_ref[...].astype(BF16)
    pair = 2 * LANES
    for t2 in range(SSM_L // 2):
        y2 = lax.dot_general(xprev, wyt_ref[0, t2 * pair:(t2 + 1) * pair, :],
                             (((1,), (1,)), ((), ())), preferred_element_type=F32)
        for tin in range(t2 + 1):
            y2 = y2 + jnp.dot(x_ref[:, tin * pair:(tin + 1) * pair], bd_ref[0, t2 - tin],
                              preferred_element_type=F32)
        for half in range(2):
            tau = 2 * t2 + half
            y = y2[:, half * LANES:(half + 1) * LANES]
            y = y + d_ref[...] * u_ref[pl.ds(tau, tc, stride=SSM_L), :]
            y = 0.5 * y * (1.0 + jnp.tanh(math.sqrt(2.0 / math.pi) * (y + 0.044715 * (y * y * y))))
            o_ref[pl.ds(tau, tc, stride=SSM_L), :] = y


def _ssm_scan(uz, bd, wz, wyt, laml, d_skip, bsz, s):
    e = d_skip.shape[1]
    nsg = e // LANES
    ns = SSM_PACK * SSM_STATE
    tb = min(8192, s)
    nt = s // tb
    tc = tb // SSM_L
    rows = SSM_L * LANES
    return pl.pallas_call(
        functools.partial(_ssm_scan_kernel, tc=tc),
        grid=(nsg, bsz, nt),
        in_specs=[pl.BlockSpec((tb, LANES), lambda g, b, t: (b * nt + t, g)),
                  pl.BlockSpec((1, SSM_L // 2, 2 * LANES, 2 * LANES), lambda g, b, t: (g, 0, 0, 0)),
                  pl.BlockSpec((1, rows, 2 * ns), lambda g, b, t: (g, 0, 0)),
                  pl.BlockSpec((1, rows, 2 * ns), lambda g, b, t: (g, 0, 0)),
                  pl.BlockSpec((1, 1, 2 * ns), lambda g, b, t: (g, 0, 0)),
                  pl.BlockSpec((1, LANES), lambda g, b, t: (0, g))],
        out_specs=pl.BlockSpec((tb, LANES), lambda g, b, t: (b * nt + t, g)),
        out_shape=jax.ShapeDtypeStruct((uz.shape[0], e), F32),
        scratch_shapes=[pltpu.VMEM((tc, rows), BF16), pltpu.VMEM((tc, 2 * ns), F32),
                        pltpu.VMEM((tc, 2 * ns), F32), pltpu.VMEM((1, 2 * ns), F32)],
        compiler_params=_cparams("parallel", "parallel", "arbitrary"),
        name="ssm_scan",
    )(uz, bd, wz, wyt, laml, d_skip)


def _ssm_layer(h, hn, ssq, w_all, layer, lam_re, lam_im, b_re, b_im, c_re, c_im, log_step, d_skip,
               wg_all, b_glu, wo_all, ple_gain, bsz, s):
    e = wo_all.shape[1]
    groups, nstate = lam_re.shape
    assert nstate == SSM_STATE and e == groups * SSM_GROUP and groups % SSM_PACK == 0
    nsg = groups // SSM_PACK
    ns = SSM_PACK * SSM_STATE
    lam = jnp.stack([lam_re.astype(F32).reshape(nsg, ns), lam_im.astype(F32).reshape(nsg, ns),
                     jnp.repeat(log_step.astype(F32), SSM_STATE).reshape(nsg, ns)], axis=1)
    eye = jnp.eye(SSM_PACK, dtype=F32)

    def expand_b(b):
        return jnp.einsum("sgpm,gh->sgmhp", b.astype(F32).reshape(nsg, SSM_PACK, SSM_STATE, SSM_GROUP),
                          eye).reshape(nsg, LANES, ns)

    def expand_c(c):
        return jnp.einsum("sgnp,gh->sgnhp", c.astype(F32).reshape(nsg, SSM_PACK, SSM_GROUP, SSM_STATE),
                          eye).reshape(nsg, LANES, ns)

    bfull = jnp.stack([expand_b(b_re), expand_b(b_im)], axis=1)
    cfull = jnp.stack([expand_c(c_re), expand_c(c_im)], axis=1)
    bd, wz, wyt, laml = _ssm_prep(lam, bfull, cfull)

    uz = _matmul(hn, w_all, w_row_blk=layer, out_dtype=F32, row_ssq=ssq, tm=MM_TM_WIDE)
    y = _ssm_scan(uz, bd, wz, wyt, laml, d_skip.astype(F32).reshape(1, e), bsz, s)
    y2 = _matmul(y, wg_all, w_row_blk=layer, out_dtype=BF16, mode="glu", cast_a=True,
                 extras=[(uz, "tile", {"col_off": e}), (b_glu.astype(F32).reshape(1, e), "row", {})])
    return _matmul(y2, wo_all, w_row_blk=layer, out_dtype=F32, mode="residual",
                   extras=[(h, "tile", {})], norm_gain=ple_gain, w_outer=True)


def kernel(x, p, norm_mix, fox_w_in, fox_b_f, fox_w_out, gdn_w_in, gdn_conv, gdn_a_log, gdn_dt_bias, gdn_norm, gdn_w_out, ssm_w_in, ssm_lam_re, ssm_lam_im, ssm_b_re, ssm_b_im, ssm_c_re, ssm_c_im, ssm_log_step, ssm_d, ssm_w_glu, ssm_b_glu, ssm_w_out, norm_ple, ple_w_proj, ple_w_gate, final_norm):
    bsz, s, d = x.shape
    depth, m = p.shape[0], bsz * s
    pdim = p.shape[-1]
    fox_width, gdn_width = fox_w_out.shape[1], gdn_w_out.shape[1]
    fox_wt, gdn_wt = jnp.swapaxes(fox_w_in, 1, 2), jnp.swapaxes(gdn_w_in, 1, 2)
    pad_rows = lambda t: jnp.pad(t, ((0, 0), (0, LANES - t.shape[1]), (0, 0)))
    fox_w = _cast_bf16(fox_wt, 4 * fox_width)
    fox_tail = _cast_bf16(pad_rows(fox_wt[:, 4 * fox_width:]))
    gdn_w = _cast_bf16(gdn_wt, 4 * gdn_width)
    gdn_tail = _cast_bf16(pad_rows(gdn_wt[:, 4 * gdn_width:]))
    fox_wo, gdn_wo = _cast_bf16(fox_w_out), _cast_bf16(gdn_w_out)
    ssm_w, ssm_wg, ssm_wo = _cast_bf16(ssm_w_in), _cast_bf16(ssm_w_glu), _cast_bf16(ssm_w_out)
    ple_wg, ple_wp = _cast_bf16(ple_w_gate), _cast_bf16(ple_w_proj)
    p_bf = _cast_bf16(p.reshape(depth, m, pdim))

    h = x.reshape(m, d).astype(F32)
    hn, ssq = _rmsnorm(h, norm_mix[0], BF16), None
    for i in range(depth):
        kind, j = i % 3, i // 3
        if kind == 0:
            h, hg, hssq = _fox_layer(h, hn, ssq, fox_w, j, fox_tail[j * LANES:(j + 1) * LANES],
                                     fox_b_f[j], fox_wo, norm_ple[i], bsz, s)
        elif kind == 1:
            h, hg, hssq = _gdn_layer(h, hn, ssq, gdn_w, j, gdn_tail[j * LANES:(j + 1) * LANES],
                                     gdn_conv[j], gdn_a_log[j], gdn_dt_bias[j], gdn_norm[j], gdn_wo,
                                     norm_ple[i], bsz, s)
        else:
            h, hg, hssq = _ssm_layer(h, hn, ssq, ssm_w, j, ssm_lam_re[j], ssm_lam_im[j], ssm_b_re[j],
                                     ssm_b_im[j], ssm_c_re[j], ssm_c_im[j], ssm_log_step[j], ssm_d[j],
                                     ssm_wg, ssm_b_glu[j], ssm_wo, norm_ple[i], bsz, s)
        extras = [(h, "tile", {}), (p_bf, "rows", {"row_blk": i * (m // min(MM_TM, m))}),
                  (ple_wp, "cols", {"row_blk": i, "rows": pdim})]
        outs = _matmul(hg, ple_wg, w_row_blk=i, out_dtype=F32, mode="ple", row_ssq=hssq, extras=extras,
                       norm_gain=norm_mix[i + 1] if i + 1 < depth else None, w_outer=True)
        h, hn, ssq = outs if i + 1 < depth else (outs, None, None)
    return _rmsnorm(h, final_norm, F32).reshape(bsz, s, d)
```

```python
import functools
import math

import jax
import jax.numpy as jnp
from jax import lax
from jax.experimental import pallas as pl
from jax.experimental.pallas import tpu as pltpu

F32 = jnp.float32
BF16 = jnp.bfloat16

LANES = 128
SUBLANES = 8
VMEM_LIMIT_BYTES = 56 * 1024 * 1024

CAST_BLOCK_ELEMS = 1 << 20
A_PREP_ROWS = 32
MM_TM = 512
MM_TM_WIDE = 1024
MM_TN = 1024
NORM_EPS = 1e-6
LOG2E = 1.4426950408889634
HEAD_DIM = 128
FOX_BLOCK = 1024
GDN_CHUNK = 64
GDN_CONV = 4
GDN_HEADS_PER_STEP = 16
GDN_TIME_BLOCK = 256
SSM_GROUP = 16
SSM_STATE = 64
SSM_PACK = LANES // SSM_GROUP
SSM_L = 16
NEG_BIG = -1e30

HIGHEST = lax.Precision.HIGHEST


def _cparams(*sem):
    return pltpu.CompilerParams(dimension_semantics=sem, vmem_limit_bytes=VMEM_LIMIT_BYTES)


def _sigmoid(x):
    return 0.5 * jnp.tanh(0.5 * x) + 0.5


def _silu(x):
    hx = 0.5 * x
    return hx + hx * jnp.tanh(hx)


def _softplus(x):
    return jnp.maximum(x, 0.0) + jnp.log(1.0 + jnp.exp(-jnp.abs(x)))


def _rmsnorm_kernel(x_ref, g_ref, o_ref):
    x = x_ref[...]
    ms = jnp.mean(x * x, axis=-1, keepdims=True)
    o_ref[...] = (x * lax.rsqrt(ms + NORM_EPS) * g_ref[...]).astype(o_ref.dtype)


def _rmsnorm(x, g, out_dtype):
    m, d = x.shape
    tm = min(512, m)
    return pl.pallas_call(
        _rmsnorm_kernel,
        grid=(m // tm,),
        in_specs=[pl.BlockSpec((tm, d), lambda i: (i, 0)),
                  pl.BlockSpec((1, d), lambda i: (0, 0))],
        out_specs=pl.BlockSpec((tm, d), lambda i: (i, 0)),
        out_shape=jax.ShapeDtypeStruct((m, d), out_dtype),
        compiler_params=_cparams("parallel"),
        name="rmsnorm",
    )(x, g.reshape(1, d).astype(F32))


def _cast_kernel(x_ref, o_ref):
    o_ref[...] = x_ref[...].astype(o_ref.dtype)


def _cast_bf16(x, rows=None):
    nl, r, c = x.shape
    rows = r if rows is None else rows
    tc = min(2048, c)
    tr = min(rows, CAST_BLOCK_ELEMS // tc)
    assert rows % tr == 0 and c % tc == 0
    out = pl.pallas_call(
        _cast_kernel,
        grid=(nl, rows // tr, c // tc),
        in_specs=[pl.BlockSpec((1, tr, tc), lambda l, i, j: (l, i, j))],
        out_specs=pl.BlockSpec((1, tr, tc), lambda l, i, j: (l, i, j)),
        out_shape=jax.ShapeDtypeStruct((nl, rows, c), BF16),
        compiler_params=_cparams("parallel", "parallel", "parallel"),
        name="cast_bf16",
    )(x)
    return out.reshape(nl * rows, c)


_N_EXTRA = {"plain": 0, "colscale": 1, "residual": 1, "ple": 3, "glu": 2}


def _mm_kernel(*refs, mode, cast_a, w_t, has_ssq, emit_norm, inv_k):
    it = iter(refs)
    a_ref, w_ref = next(it), next(it)
    ssq_ref = next(it) if has_ssq else None
    extra = [next(it) for _ in range(_N_EXTRA[mode])]
    gain_ref = next(it) if emit_norm else None
    o_ref = next(it)
    hg_ref, ssq_out_ref = (next(it), next(it)) if emit_norm else (None, None)
    if cast_a:
        a_scr = next(it)

        @pl.when(pl.program_id(1) == 0)
        def _():
            def prep(r, carry):
                rows = pl.ds(pl.multiple_of(r * A_PREP_ROWS, A_PREP_ROWS), A_PREP_ROWS)
                a_scr[rows, :] = a_ref[rows, :].astype(BF16)
                return carry

            lax.fori_loop(0, a_ref.shape[0] // A_PREP_ROWS, prep, 0)

        a = a_scr[...]
    else:
        a = a_ref[...]

    acc = lax.dot_general(a, w_ref[...], (((1,), (1 if w_t else 0,)), ((), ())),
                          preferred_element_type=F32)
    if has_ssq:
        parts = ssq_ref[...]
        tot = parts[:, 0:LANES]
        for t in range(1, parts.shape[1] // LANES):
            tot = tot + parts[:, t * LANES:(t + 1) * LANES]
        acc = acc * lax.rsqrt(tot[:, 0:1] * inv_k + NORM_EPS)
    if mode == "plain":
        out = acc
    elif mode == "colscale":
        out = acc * extra[0][...]
    elif mode == "residual":
        out = extra[0][...] + acc
    elif mode == "ple":
        res_ref, p_ref, wp_ref = extra
        emb = jnp.dot(p_ref[...].astype(BF16), wp_ref[...], preferred_element_type=F32)
        out = res_ref[...] + _sigmoid(acc) * emb
    elif mode == "glu":
        z_ref, b_ref = extra
        tn = o_ref.shape[1]
        y = a_ref[:, pl.ds(pl.multiple_of(pl.program_id(1) * tn, tn), tn)]
        out = y * _sigmoid(acc + b_ref[...]) * _silu(z_ref[...])
    else:
        raise ValueError(mode)
    o_ref[...] = out.astype(o_ref.dtype)
    if emit_norm:
        hg_ref[...] = (out * gain_ref[...]).astype(hg_ref.dtype)
        ssq_out_ref[...] = jnp.broadcast_to(jnp.sum(out * out, axis=-1, keepdims=True),
                                            ssq_out_ref.shape)


def _extra_spec(arr, kind, tm, tn, row_blk=0, col_off=0, rows=None):
    if kind == "tile":
        return pl.BlockSpec((tm, tn), lambda i, j: (i + row_blk, j + col_off // tn))
    if kind == "row":
        return pl.BlockSpec((1, tn), lambda i, j: (0, j))
    if kind == "rows":
        return pl.BlockSpec((tm, arr.shape[1]), lambda i, j: (i + row_blk, 0))
    if kind == "cols":
        return pl.BlockSpec((rows, tn), lambda i, j: (row_blk, j))
    raise ValueError(kind)


def _matmul(a, w, *, out_dtype, n=None, mode="plain", extras=(), cast_a=False, row_ssq=None,
            norm_gain=None, w_t=None, w_row_blk=0, w_col_off=0, w_outer=False, tm=MM_TM, tn=MM_TN):
    m, kdim = a.shape
    n = (w_t if w_t else w.shape[1]) if n is None else n
    tm = min(tm, m)
    tn = functools.reduce(math.gcd, [tn, n, w_col_off, w_t or 0])
    assert m % tm == 0 and tn % LANES == 0 and len(extras) == _N_EXTRA[mode]
    operands = [a, w]
    if w_t:
        first = (w_row_blk * w_t + w_col_off) // tn
        w_spec = pl.BlockSpec((tn, kdim), lambda i, j: (first + j, 0))
    else:
        w_spec = pl.BlockSpec((kdim, tn), lambda i, j: (w_row_blk, j + w_col_off // tn))
    in_specs = [pl.BlockSpec((tm, kdim), lambda i, j: (i, 0)), w_spec]
    if row_ssq is not None:
        operands.append(row_ssq)
        in_specs.append(pl.BlockSpec((tm, row_ssq.shape[1]), lambda i, j: (i, 0)))
    for arr, kind, opts in extras:
        assert opts.get("col_off", 0) % tn == 0
        operands.append(arr)
        in_specs.append(_extra_spec(arr, kind, tm, tn, **opts))
    tile = pl.BlockSpec((tm, tn), lambda i, j: (i, j))
    out_specs, out_shape = tile, jax.ShapeDtypeStruct((m, n), out_dtype)
    if norm_gain is not None:
        operands.append(norm_gain.astype(F32).reshape(1, n))
        in_specs.append(pl.BlockSpec((1, tn), lambda i, j: (0, j)))
        out_specs = [tile, tile, pl.BlockSpec((tm, LANES), lambda i, j: (i, j))]
        out_shape = [out_shape, jax.ShapeDtypeStruct((m, n), BF16),
                     jax.ShapeDtypeStruct((m, n // tn * LANES), F32)]
    scratch = [pltpu.VMEM((tm, kdim), BF16)] if cast_a else []
    grid = (m // tm, n // tn)
    if w_outer:
        assert not cast_a
        flip = lambda sp: pl.BlockSpec(sp.block_shape, lambda j, i, f=sp.index_map: f(i, j))
        in_specs = [flip(sp) for sp in in_specs]
        out_specs = [flip(sp) for sp in out_specs] if norm_gain is not None else flip(out_specs)
        grid = grid[::-1]
    return pl.pallas_call(
        functools.partial(_mm_kernel, mode=mode, cast_a=cast_a, w_t=bool(w_t),
                          has_ssq=row_ssq is not None,
                          emit_norm=norm_gain is not None, inv_k=1.0 / kdim),
        grid=grid,
        in_specs=in_specs,
        out_specs=out_specs,
        out_shape=out_shape,
        scratch_shapes=scratch,
        compiler_params=_cparams("parallel", "arbitrary"),
        name="matmul_" + mode,
    )(*operands)


def _fox_cum_kernel(f_ref, b_ref, o_ref, carry_ref, *, ts):
    @pl.when(pl.program_id(1) == 0)
    def _():
        carry_ref[...] = jnp.zeros_like(carry_ref)

    x = f_ref[...] + b_ref[...]
    log_f = jnp.minimum(x, 0.0) - jnp.log(1.0 + jnp.exp(-jnp.abs(x)))
    r = lax.broadcasted_iota(jnp.int32, (ts, ts), 0)
    c = lax.broadcasted_iota(jnp.int32, (ts, ts), 1)
    lower = (c <= r).astype(F32)
    cum = jnp.dot(lower, log_f, precision=HIGHEST, preferred_element_type=F32) + carry_ref[...]
    carry_ref[...] = cum[ts - 1:ts, :]
    o_ref[0] = cum.T


def _fox_cum(f_logit, b_f, bsz, s):
    ts = min(1024, s)
    return pl.pallas_call(
        functools.partial(_fox_cum_kernel, ts=ts),
        grid=(bsz, s // ts),
        in_specs=[pl.BlockSpec((ts, LANES), lambda b, t: (b * (s // ts) + t, 0)),
                  pl.BlockSpec((1, LANES), lambda b, t: (0, 0))],
        out_specs=pl.BlockSpec((1, LANES, ts), lambda b, t: (b, 0, t)),
        out_shape=jax.ShapeDtypeStruct((bsz, LANES, s), F32),
        scratch_shapes=[pltpu.VMEM((1, LANES), F32)],
        compiler_params=_cparams("parallel", "arbitrary"),
        name="fox_cum",
    )(f_logit, b_f)


def _fox_attn_kernel(q_ref, k_ref, v_ref, c_ref, z_ref, o_ref, sa_ref, sb_ref, *, tq):
    tk = tq // 2
    i = pl.program_id(2)
    q = q_ref[...]
    c0 = c_ref[0, 0, :, pl.ds(pl.multiple_of(i * tq, tq), LANES)][:, 0:1]

    def scores(qrows, j):
        start = j * tk if isinstance(j, int) else pl.multiple_of(j * tk, tk)
        s = lax.dot_general(qrows, k_ref[pl.ds(start, tk), :], (((1,), (1,)), ((), ())),
                            preferred_element_type=F32)
        return s + (c0 - c_ref[0, 0, :, pl.ds(start, tk)]) * LOG2E

    def update(s, j, m, acc, width=1):
        rows = width * tk
        start = j * tk if isinstance(j, int) else pl.multiple_of(j * tk, tk)
        vb = jnp.concatenate([v_ref[pl.ds(start, rows), :], jnp.ones((rows, HEAD_DIM), BF16)], axis=1)
        m_new = jnp.maximum(m, jnp.max(s, axis=-1, keepdims=True))
        p = jnp.exp2(s - m_new)
        acc = jnp.exp2(m - m_new) * acc + jnp.dot(p.astype(BF16), vb, preferred_element_type=F32)
        return m_new, acc

    sa_ref[...] = scores(q, 0)

    def pair(t, carry):
        m, acc = carry
        sb_ref[...] = scores(q, 2 * t + 1)
        m, acc = update(sa_ref[...], 2 * t, m, acc)
        sa_ref[...] = scores(q, 2 * t + 2)
        return update(sb_ref[...], 2 * t + 1, m, acc)

    init = (jnp.full((tq, 1), NEG_BIG, F32), jnp.zeros((tq, 2 * HEAD_DIM), F32))
    carry = lax.fori_loop(0, i // 2, lambda t2, cr: pair(2 * t2 + 1, pair(2 * t2, cr)), init)

    def finish(a, rows):
        o = a[:, 0:HEAD_DIM] / a[:, HEAD_DIM:HEAD_DIM + 1]
        o_ref[rows, :] = (o * _silu(z_ref[rows, :])).astype(o_ref.dtype)

    def tail(carry, odd):
        m, acc = pair(i - 1, carry) if odd else carry
        r = lax.broadcasted_iota(jnp.int32, (tk, tk), 0)
        c = lax.broadcasted_iota(jnp.int32, (tk, tk), 1)
        s_bot = jnp.concatenate(
            [sa_ref[tk:tq, :], jnp.where(c <= r, scores(q[tk:tq], 2 * i + 1), NEG_BIG)], axis=1)
        _, acc_top = update(jnp.where(c <= r, sa_ref[0:tk, :], NEG_BIG), 2 * i, m[0:tk], acc[0:tk])
        _, acc_bot = update(s_bot, 2 * i, m[tk:tq], acc[tk:tq], width=2)
        finish(acc_top, slice(0, tk))
        finish(acc_bot, slice(tk, tq))

    lax.cond(i % 2 == 1, lambda cr: tail(cr, True), lambda cr: tail(cr, False), carry)


def _fox_attention(qkv, cum4, z, bsz, s, heads):
    tq = min(FOX_BLOCK, s)
    nq = s // tq
    return pl.pallas_call(
        functools.partial(_fox_attn_kernel, tq=tq),
        grid=(bsz, heads, nq),
        in_specs=[pl.BlockSpec((tq, HEAD_DIM), lambda b, h, i: (b * nq + i, h)),
                  pl.BlockSpec((s, HEAD_DIM), lambda b, h, i: (b, heads + h)),
                  pl.BlockSpec((s, HEAD_DIM), lambda b, h, i: (b, 2 * heads + h)),
                  pl.BlockSpec((1, 1, 1, s), lambda b, h, i: (b, h, 0, 0)),
                  pl.BlockSpec((tq, HEAD_DIM), lambda b, h, i: (b * nq + i, h))],
        out_specs=pl.BlockSpec((tq, HEAD_DIM), lambda b, h, i: (b * nq + i, h)),
        out_shape=jax.ShapeDtypeStruct((bsz * s, heads * HEAD_DIM), BF16),
        scratch_shapes=[pltpu.VMEM((tq, tq // 2), F32)] * 2,
        compiler_params=_cparams("parallel", "parallel", "arbitrary"),
        name="fox_attn",
    )(qkv, qkv, qkv, cum4, z)


def _fox_layer(h, hn, ssq, w_all, layer, w_tail, b_f, wo_all, ple_gain, bsz, s):
    width = wo_all.shape[1]
    heads = width // HEAD_DIM
    q_scale = jnp.concatenate([jnp.full((width,), HEAD_DIM ** -0.5 * LOG2E, F32),
                               jnp.ones((2 * width,), F32)]).reshape(1, 3 * width)
    qkv = _matmul(hn, w_all, n=3 * width, w_t=4 * width, w_row_blk=layer, out_dtype=BF16,
                  mode="colscale", extras=[(q_scale, "row", {})], row_ssq=ssq, tm=MM_TM_WIDE)
    z = _matmul(hn, w_all, n=width, w_t=4 * width, w_row_blk=layer, w_col_off=3 * width,
                out_dtype=F32, row_ssq=ssq, w_outer=True, tm=MM_TM_WIDE)
    f_logit = _matmul(hn, w_tail, w_t=LANES, out_dtype=F32, row_ssq=ssq, tm=MM_TM_WIDE)
    b_row = jnp.pad(b_f.astype(F32), (0, LANES - heads)).reshape(1, LANES)
    cum = _fox_cum(f_logit, b_row, bsz, s)
    gated = _fox_attention(qkv, cum.reshape(bsz, LANES, 1, s), z, bsz, s, heads)
    return _matmul(gated, wo_all, w_row_blk=layer, out_dtype=F32, mode="residual",
                   extras=[(h, "tile", {})], norm_gain=ple_gain, w_outer=True)


def _bdot(a, b):
    return jnp.einsum("bmk,bkn->bmn", a, b, preferred_element_type=F32)


def _split_bf16(x):
    hi = x.astype(BF16)
    return hi, (x - hi.astype(F32)).astype(BF16)


def _bdot3_parts(a_hi, a_lo, b_hi, b_lo):
    return _bdot(a_hi, b_hi) + _bdot(a_hi, b_lo) + _bdot(a_lo, b_hi)


def _bdot3(a, b):
    return _bdot3_parts(*_split_bf16(a), *_split_bf16(b))


def _bdot_exact_lhs(a_bf16, b):
    b_hi = b.astype(BF16)
    b_mid, b_lo = _split_bf16(b - b_hi.astype(F32))
    return _bdot(a_bf16, b_hi) + _bdot(a_bf16, b_mid) + _bdot(a_bf16, b_lo)


def _gdn_kernel(q_ref, k_ref, v_ref, z_ref, t_ref, cw_ref, gp_ref, nw_ref, o_ref,
                xq_ref, xk_ref, xv_ref, st_ref, u_ref, wq_ref, ik_ref, gl_ref, *, tb, heads, hps):
    hb = pl.program_id(1)
    c, dh = GDN_CHUNK, HEAD_DIM
    nb = tb // c
    nbat = hps * nb
    pad = SUBLANES

    @pl.when(pl.program_id(2) == 0)
    def _():
        for r in (xq_ref, xk_ref, xv_ref):
            r[0:pad, :] = jnp.zeros((pad, hps * dh), F32)
        st_ref[...] = jnp.zeros_like(st_ref)

    def conv(x_ref, xs_ref, col):
        x = x_ref[...]
        xs_ref[pad:pad + tb, :] = x
        xs = xs_ref[...]
        y = x * cw_ref[col, GDN_CONV - 1:GDN_CONV, :]
        for j in range(GDN_CONV - 1):
            shifted = pltpu.roll(xs, GDN_CONV - 1 - j, axis=0)[pad:pad + tb, :]
            y = y + shifted * cw_ref[col, j:j + 1, :]
        xs_ref[0:pad, :] = x[tb - pad:tb, :]
        return _silu(y)

    def by_chunk(x):
        w = x.shape[1] // hps
        return jnp.concatenate([x[:, g * w:(g + 1) * w].reshape(nb, c, w) for g in range(hps)], axis=0)

    def l2n(x):
        return x * lax.rsqrt(jnp.sum(x * x, axis=-1, keepdims=True) + NORM_EPS)

    q = l2n(by_chunk(conv(q_ref, xq_ref, 0))) * (dh ** -0.5)
    k = l2n(by_chunk(conv(k_ref, xk_ref, 1)))
    v = by_chunk(conv(v_ref, xv_ref, 2))

    t = t_ref[...]
    lane = lax.broadcasted_iota(jnp.int32, (tb, LANES), 1)
    beta_all = _sigmoid(t)
    g_all = -jnp.exp(gp_ref[0:1, :]) * _softplus(t + gp_ref[1:2, :])

    def pick(x, first):
        cols = [jnp.sum(jnp.where(lane == first + hb * hps + g, x, 0.0), axis=-1, keepdims=True)
                for g in range(hps)]
        return jnp.concatenate([col.reshape(nb, c, 1) for col in cols], axis=0)

    beta = pick(beta_all, 0)
    g = pick(g_all, heads)

    ri = lax.broadcasted_iota(jnp.int32, (c, c), 0)
    ci = lax.broadcasted_iota(jnp.int32, (c, c), 1)
    incl = ci <= ri
    strict = ci < ri
    eye = (ci == ri).astype(F32)
    tri = jnp.broadcast_to(incl.astype(BF16), (nbat, c, c))
    rl = lax.broadcasted_iota(jnp.int32, (c, LANES), 0)
    cl = lax.broadcasted_iota(jnp.int32, (c, LANES), 1)
    seg_mask = (rl > cl) | (cl == c)

    seg = _bdot_exact_lhs(tri, jnp.where(seg_mask, g, 0.0))
    gc = seg[:, :, c:c + 1]
    g_last = seg[:, c - 1:c, c:c + 1]
    decay = jnp.exp(jnp.where(incl, seg[:, :, 0:c], NEG_BIG))
    egc = jnp.exp(gc)
    k_beta = k * beta
    kq = jnp.einsum("bmd,bnd->bmn", jnp.concatenate([k_beta, q], axis=1).astype(BF16),
                    k.astype(BF16), preferred_element_type=F32)
    a = jnp.where(strict, kq[:, 0:c] * decay, 0.0)
    intra = kq[:, c:2 * c] * decay
    npair = nbat // 2
    pair_lanes = lambda x: jnp.concatenate(
        [x.reshape(npair, 2, c, x.shape[2])[:, 0], x.reshape(npair, 2, c, x.shape[2])[:, 1]], axis=2)
    left = lax.broadcasted_iota(jnp.int32, (c, 2 * c), 1) < c

    def block_diag(x2):
        zero = jnp.zeros_like(x2)
        return jnp.concatenate([jnp.where(left, x2, zero), jnp.where(left, zero, x2)], axis=1)

    a2 = pair_lanes(a)
    inv2 = jnp.concatenate([eye, eye], axis=1) - a2
    a_hi, a_lo = _split_bf16(a2)
    pw2 = _bdot3_parts(a_hi, a_lo, block_diag(a_hi), block_diag(a_lo))
    levels = int(math.log2(c)) - 1
    for lvl in range(levels):
        inv_hi, inv_lo = _split_bf16(inv2)
        pw_hi, pw_lo = _split_bf16(pw2)
        pd_hi, pd_lo = block_diag(pw_hi), block_diag(pw_lo)
        if lvl < levels - 1:
            both = _bdot3_parts(jnp.concatenate([inv_hi, pw_hi], axis=1),
                                jnp.concatenate([inv_lo, pw_lo], axis=1), pd_hi, pd_lo)
            inv2 = inv2 + both[:, 0:c]
            pw2 = both[:, c:2 * c]
        else:
            inv2 = inv2 + _bdot3_parts(inv_hi, inv_lo, pd_hi, pd_lo)
    rhs = jnp.concatenate([v * beta, k_beta * egc], axis=2).reshape(npair, 2, c, 2 * dh)
    zero = jnp.zeros((npair, c, 2 * dh), F32)
    rhs_bd = jnp.concatenate([jnp.concatenate([rhs[:, 0], zero], axis=2),
                              jnp.concatenate([zero, rhs[:, 1]], axis=2)], axis=1)
    uw2 = _bdot3(inv2, rhs_bd)
    uw = jnp.stack([uw2[:, :, 0:2 * dh], uw2[:, :, 2 * dh:4 * dh]], axis=1).reshape(nbat, c, 2 * dh)
    u_ref[...] = uw[:, :, 0:dh].reshape(nbat * c, dh)
    wq_ref[...] = jnp.concatenate([uw[:, :, dh:2 * dh], q * egc], axis=1).astype(BF16).reshape(
        nbat * 2 * c, dh)
    k_dec = k * jnp.exp(g_last - gc)
    for b in range(nbat):
        ik_ref[b * (c + dh):(b + 1) * (c + dh), :] = jnp.concatenate(
            [intra[b], k_dec[b].T], axis=0).astype(BF16)
    gl_ref[...] = jnp.broadcast_to(jnp.exp(g_last), (nbat, SUBLANES, dh)).reshape(nbat * SUBLANES, dh)

    def step(n, states):
        rows = lambda g, size: pl.ds(pl.multiple_of((g * nb + n) * size, size), size)
        ws = [jnp.dot(wq_ref[rows(g, 2 * c), :], states[g].astype(BF16), preferred_element_type=F32)
              for g in range(hps)]
        v_new = [u_ref[rows(g, c), :] - ws[g][0:c] for g in range(hps)]
        iv = [jnp.dot(ik_ref[pl.ds(pl.multiple_of((g * nb + n) * (c + dh), c), c + dh), :],
                      v_new[g].astype(BF16), preferred_element_type=F32) for g in range(hps)]
        new_states = []
        r0 = pl.multiple_of(n * c, c)
        for g in range(hps):
            out = ws[g][c:2 * c] + iv[g][0:c]
            gl = gl_ref[pl.ds(pl.multiple_of((g * nb + n) * SUBLANES, SUBLANES), 1), :]
            new_states.append(states[g] * gl + iv[g][c:c + dh])
            ms = jnp.mean(out * out, axis=-1, keepdims=True)
            o = out * lax.rsqrt(ms + NORM_EPS) * nw_ref[...]
            o_ref[pl.ds(r0, c), g * dh:(g + 1) * dh] = (
                o * _silu(z_ref[pl.ds(r0, c), g * dh:(g + 1) * dh])).astype(o_ref.dtype)
        return tuple(new_states)

    states = lax.fori_loop(0, nb, step, tuple(st_ref[g] for g in range(hps)))
    for g in range(hps):
        st_ref[g] = states[g]


def _gdn_mixer(qkvz, tail, conv_w, gate_p, norm_w, bsz, s, heads):
    hps = math.gcd(GDN_HEADS_PER_STEP, heads)
    tb = min(GDN_TIME_BLOCK, s)
    nt = s // tb
    hblocks = heads // hps
    width = hps * HEAD_DIM
    nbat = hps * (tb // GDN_CHUNK)
    col = lambda off: pl.BlockSpec((tb, width), lambda b, h, t: (b * nt + t, off + h))
    return pl.pallas_call(
        functools.partial(_gdn_kernel, tb=tb, heads=heads, hps=hps),
        grid=(bsz, hblocks, nt),
        in_specs=[col(0), col(hblocks), col(2 * hblocks), col(3 * hblocks),
                  pl.BlockSpec((tb, LANES), lambda b, h, t: (b * nt + t, 0)),
                  pl.BlockSpec((3, GDN_CONV, width), lambda b, h, t: (0, 0, h)),
                  pl.BlockSpec((2, LANES), lambda b, h, t: (0, 0)),
                  pl.BlockSpec((1, HEAD_DIM), lambda b, h, t: (0, 0))],
        out_specs=pl.BlockSpec((tb, width), lambda b, h, t: (b * nt + t, h)),
        out_shape=jax.ShapeDtypeStruct((bsz * s, heads * HEAD_DIM), BF16),
        scratch_shapes=[pltpu.VMEM((tb + SUBLANES, width), F32)] * 3
                       + [pltpu.VMEM((hps, HEAD_DIM, HEAD_DIM), F32),
                          pltpu.VMEM((nbat * GDN_CHUNK, HEAD_DIM), F32),
                          pltpu.VMEM((nbat * 2 * GDN_CHUNK, HEAD_DIM), BF16),
                          pltpu.VMEM((nbat * (GDN_CHUNK + HEAD_DIM), GDN_CHUNK), BF16),
                          pltpu.VMEM((nbat * SUBLANES, HEAD_DIM), F32)],
        compiler_params=_cparams("parallel", "parallel", "arbitrary"),
        name="gdn_mixer",
    )(qkvz, qkvz, qkvz, qkvz, tail, conv_w, gate_p, norm_w)


def _gdn_layer(h, hn, ssq, w_all, layer, w_tail, conv_w, a_log, dt_bias, norm_w, wo_all, ple_gain,
               bsz, s):
    width = wo_all.shape[1]
    heads = width // HEAD_DIM
    qkvz = _matmul(hn, w_all, w_t=4 * width, w_row_blk=layer, out_dtype=F32, row_ssq=ssq,
                   tm=MM_TM_WIDE)
    tail = _matmul(hn, w_tail, w_t=LANES, out_dtype=F32, row_ssq=ssq, tm=MM_TM_WIDE)
    cw = conv_w.astype(F32).reshape(GDN_CONV, 3, width).transpose(1, 0, 2)
    lane_pad = lambda x: jnp.pad(x.astype(F32), (heads, LANES - 2 * heads))
    gate_p = jnp.stack([lane_pad(a_log), lane_pad(dt_bias)])
    gated = _gdn_mixer(qkvz, tail, cw, gate_p, norm_w.astype(F32).reshape(1, HEAD_DIM),
                       bsz, s, heads)
    return _matmul(gated, wo_all, w_row_blk=layer, out_dtype=F32, mode="residual",
                   extras=[(h, "tile", {})], norm_gain=ple_gain, w_outer=True)


def _ssm_prep_kernel(lam_ref, bf_ref, cf_ref, bd_ref, wz_ref, wyt_ref, laml_ref):
    ns = SSM_PACK * SSM_STATE
    lam_re, lam_im, step_log = lam_ref[0, 0:1, :], lam_ref[0, 1:2, :], lam_ref[0, 2:3, :]
    step = jnp.exp(step_log)
    mag = jnp.exp(lam_re * step)
    lb_re, lb_im = mag * jnp.cos(lam_im * step), mag * jnp.sin(lam_im * step)
    den = lam_re * lam_re + lam_im * lam_im
    num_re = lb_re - 1.0
    zoh_re = (num_re * lam_re + lb_im * lam_im) / den
    zoh_im = (lb_im * lam_re - num_re * lam_im) / den
    b_re, b_im = bf_ref[0, 0], bf_ref[0, 1]
    bb_re = zoh_re * b_re - zoh_im * b_im
    bb_im = zoh_re * b_im + zoh_im * b_re
    c_re, c_im = cf_ref[0, 0], cf_ref[0, 1]
    cc_hi, cc_lo = _split_bf16(jnp.concatenate([c_re, -c_im], axis=1))
    nt_dot = lambda x, y: lax.dot_general(x, y, (((1,), (1,)), ((), ())), preferred_element_type=F32)

    bd_ref[0, 0, LANES:2 * LANES, 0:LANES] = jnp.zeros((LANES, LANES), bd_ref.dtype)
    pw_re, pw_im = jnp.ones_like(lb_re), jnp.zeros_like(lb_re)
    for d in range(SSM_L + 1):
        if d < SSM_L:
            a_re = pw_re * bb_re - pw_im * bb_im
            a_im = pw_re * bb_im + pw_im * bb_re
            a_d = jnp.concatenate([a_re, a_im], axis=1)
            a_hi, a_lo = _split_bf16(a_d)
            blk = (nt_dot(a_hi, cc_hi) + nt_dot(a_hi, cc_lo) + nt_dot(a_lo, cc_hi)).astype(bd_ref.dtype)
            d2 = d // 2
            if d % 2 == 0:
                bd_ref[0, d2, 0:LANES, 0:LANES] = blk
                bd_ref[0, d2, LANES:2 * LANES, LANES:2 * LANES] = blk
            else:
                bd_ref[0, d2, 0:LANES, LANES:2 * LANES] = blk
                if d2 + 1 < SSM_L // 2:
                    bd_ref[0, d2 + 1, LANES:2 * LANES, 0:LANES] = blk
            tin = SSM_L - 1 - d
            wz_ref[0, tin * LANES:(tin + 1) * LANES, :] = a_d.astype(wz_ref.dtype)
        if d >= 1:
            y_re = pw_re * c_re - pw_im * c_im
            y_im = pw_re * c_im + pw_im * c_re
            wyt_ref[0, (d - 1) * LANES:d * LANES, :] = jnp.concatenate(
                [y_re, -y_im], axis=1).astype(wyt_ref.dtype)
        if d == SSM_L:
            laml_ref[0] = jnp.concatenate([pw_re, pw_im], axis=1)
        pw_re, pw_im = pw_re * lb_re - pw_im * lb_im, pw_re * lb_im + pw_im * lb_re


def _ssm_prep(lam, bfull, cfull):
    nsg = lam.shape[0]
    ns = SSM_PACK * SSM_STATE
    rows = SSM_L * LANES
    return pl.pallas_call(
        _ssm_prep_kernel,
        grid=(nsg,),
        in_specs=[pl.BlockSpec((1, 3, ns), lambda g: (g, 0, 0)),
                  pl.BlockSpec((1, 2, LANES, ns), lambda g: (g, 0, 0, 0)),
                  pl.BlockSpec((1, 2, LANES, ns), lambda g: (g, 0, 0, 0))],
        out_specs=[pl.BlockSpec((1, SSM_L // 2, 2 * LANES, 2 * LANES), lambda g: (g, 0, 0, 0)),
                   pl.BlockSpec((1, rows, 2 * ns), lambda g: (g, 0, 0)),
                   pl.BlockSpec((1, rows, 2 * ns), lambda g: (g, 0, 0)),
                   pl.BlockSpec((1, 1, 2 * ns), lambda g: (g, 0, 0))],
        out_shape=[jax.ShapeDtypeStruct((nsg, SSM_L // 2, 2 * LANES, 2 * LANES), BF16),
                   jax.ShapeDtypeStruct((nsg, rows, 2 * ns), BF16),
                   jax.ShapeDtypeStruct((nsg, rows, 2 * ns), BF16),
                   jax.ShapeDtypeStruct((nsg, 1, 2 * ns), F32)],
        compiler_params=_cparams("parallel"),
        name="ssm_prep",
    )(lam, bfull, cfull)


def _ssm_scan_kernel(u_ref, bd_ref, wz_ref, wyt_ref, laml_ref, d_ref, o_ref,
                     x_ref, z_ref, xp_ref, st_ref, *, tc):
    ns = SSM_PACK * SSM_STATE

    @pl.when(pl.program_id(2) == 0)
    def _():
        st_ref[...] = jnp.zeros_like(st_ref)

    for tau in range(SSM_L):
        x_ref[:, tau * LANES:(tau + 1) * LANES] = u_ref[pl.ds(tau, tc, stride=SSM_L), :].astype(BF16)

    z_ref[...] = jnp.dot(x_ref[...], wz_ref[0], preferred_element_type=F32)

    a_re, a_im = laml_ref[0, :, 0:ns], laml_ref[0, :, ns:2 * ns]

    def row(r, carry):
        s_re, s_im = carry
        xp_ref[pl.ds(r, 1), :] = jnp.concatenate([s_re, s_im], axis=1)
        zr = z_ref[pl.ds(r, 1), :]
        n_re = a_re * s_re - a_im * s_im + zr[:, 0:ns]
        n_im = a_re * s_im + a_im * s_re + zr[:, ns:2 * ns]
        return n_re, n_im

    s_re, s_im = lax.fori_loop(0, tc, row, (st_ref[:, 0:ns], st_ref[:, ns:2 * ns]), unroll=4)
    st_ref[...] = jnp.concatenate([s_re, s_im], axis=1)

    xprev = xp_ref[...].astype(BF16)
    pair = 2 * LANES
    for t2 in range(SSM_L // 2):
        y2 = lax.dot_general(xprev, wyt_ref[0, t2 * pair:(t2 + 1) * pair, :],
                             (((1,), (1,)), ((), ())), preferred_element_type=F32)
        for tin in range(t2 + 1):
            y2 = y2 + jnp.dot(x_ref[:, tin * pair:(tin + 1) * pair], bd_ref[0, t2 - tin],
                              preferred_element_type=F32)
        for half in range(2):
            tau = 2 * t2 + half
            y = y2[:, half * LANES:(half + 1) * LANES]
            y = y + d_ref[...] * u_ref[pl.ds(tau, tc, stride=SSM_L), :]
            y = 0.5 * y * (1.0 + jnp.tanh(math.sqrt(2.0 / math.pi) * (y + 0.044715 * (y * y * y))))
            o_ref[pl.ds(tau, tc, stride=SSM_L), :] = y


def _ssm_scan(uz, bd, wz, wyt, laml, d_skip, bsz, s):
    e = d_skip.shape[1]
    nsg = e // LANES
    ns = SSM_PACK * SSM_STATE
    tb = min(8192, s)
    nt = s // tb
    tc = tb // SSM_L
    rows = SSM_L * LANES
    return pl.pallas_call(
        functools.partial(_ssm_scan_kernel, tc=tc),
        grid=(nsg, bsz, nt),
        in_specs=[pl.BlockSpec((tb, LANES), lambda g, b, t: (b * nt + t, g)),
                  pl.BlockSpec((1, SSM_L // 2, 2 * LANES, 2 * LANES), lambda g, b, t: (g, 0, 0, 0)),
                  pl.BlockSpec((1, rows, 2 * ns), lambda g, b, t: (g, 0, 0)),
                  pl.BlockSpec((1, rows, 2 * ns), lambda g, b, t: (g, 0, 0)),
                  pl.BlockSpec((1, 1, 2 * ns), lambda g, b, t: (g, 0, 0)),
                  pl.BlockSpec((1, LANES), lambda g, b, t: (0, g))],
        out_specs=pl.BlockSpec((tb, LANES), lambda g, b, t: (b * nt + t, g)),
        out_shape=jax.ShapeDtypeStruct((uz.shape[0], e), F32),
        scratch_shapes=[pltpu.VMEM((tc, rows), BF16), pltpu.VMEM((tc, 2 * ns), F32),
                        pltpu.VMEM((tc, 2 * ns), F32), pltpu.VMEM((1, 2 * ns), F32)],
        compiler_params=_cparams("parallel", "parallel", "arbitrary"),
        name="ssm_scan",
    )(uz, bd, wz, wyt, laml, d_skip)


def _ssm_layer(h, hn, ssq, w_all, layer, lam_re, lam_im, b_re, b_im, c_re, c_im, log_step, d_skip,
               wg_all, b_glu, wo_all, ple_gain, bsz, s):
    e = wo_all.shape[1]
    groups, nstate = lam_re.shape
    assert nstate == SSM_STATE and e == groups * SSM_GROUP and groups % SSM_PACK == 0
    nsg = groups // SSM_PACK
    ns = SSM_PACK * SSM_STATE
    lam = jnp.stack([lam_re.astype(F32).reshape(nsg, ns), lam_im.astype(F32).reshape(nsg, ns),
                     jnp.repeat(log_step.astype(F32), SSM_STATE).reshape(nsg, ns)], axis=1)
    eye = jnp.eye(SSM_PACK, dtype=F32)

    def expand_b(b):
        return jnp.einsum("sgpm,gh->sgmhp", b.astype(F32).reshape(nsg, SSM_PACK, SSM_STATE, SSM_GROUP),
                          eye).reshape(nsg, LANES, ns)

    def expand_c(c):
        return jnp.einsum("sgnp,gh->sgnhp", c.astype(F32).reshape(nsg, SSM_PACK, SSM_GROUP, SSM_STATE),
                          eye).reshape(nsg, LANES, ns)

    bfull = jnp.stack([expand_b(b_re), expand_b(b_im)], axis=1)
    cfull = jnp.stack([expand_c(c_re), expand_c(c_im)], axis=1)
    bd, wz, wyt, laml = _ssm_prep(lam, bfull, cfull)

    uz = _matmul(hn, w_all, w_row_blk=layer, out_dtype=F32, row_ssq=ssq, tm=MM_TM_WIDE)
    y = _ssm_scan(uz, bd, wz, wyt, laml, d_skip.astype(F32).reshape(1, e), bsz, s)
    y2 = _matmul(y, wg_all, w_row_blk=layer, out_dtype=BF16, mode="glu", cast_a=True,
                 extras=[(uz, "tile", {"col_off": e}), (b_glu.astype(F32).reshape(1, e), "row", {})])
    return _matmul(y2, wo_all, w_row_blk=layer, out_dtype=F32, mode="residual",
                   extras=[(h, "tile", {})], norm_gain=ple_gain, w_outer=True)


def kernel(x, p, norm_mix, fox_w_in, fox_b_f, fox_w_out, gdn_w_in, gdn_conv, gdn_a_log, gdn_dt_bias, gdn_norm, gdn_w_out, ssm_w_in, ssm_lam_re, ssm_lam_im, ssm_b_re, ssm_b_im, ssm_c_re, ssm_c_im, ssm_log_step, ssm_d, ssm_w_glu, ssm_b_glu, ssm_w_out, norm_ple, ple_w_proj, ple_w_gate, final_norm):
    bsz, s, d = x.shape
    depth, m = p.shape[0], bsz * s
    pdim = p.shape[-1]
    fox_width, gdn_width = fox_w_out.shape[1], gdn_w_out.shape[1]
    fox_wt, gdn_wt = jnp.swapaxes(fox_w_in, 1, 2), jnp.swapaxes(gdn_w_in, 1, 2)
    pad_rows = lambda t: jnp.pad(t, ((0, 0), (0, LANES - t.shape[1]), (0, 0)))
    fox_w = _cast_bf16(fox_wt, 4 * fox_width)
    fox_tail = _cast_bf16(pad_rows(fox_wt[:, 4 * fox_width:]))
    gdn_w = _cast_bf16(gdn_wt, 4 * gdn_width)
    gdn_tail = _cast_bf16(pad_rows(gdn_wt[:, 4 * gdn_width:]))
    fox_wo, gdn_wo = _cast_bf16(fox_w_out), _cast_bf16(gdn_w_out)
    ssm_w, ssm_wg, ssm_wo = _cast_bf16(ssm_w_in), _cast_bf16(ssm_w_glu), _cast_bf16(ssm_w_out)
    ple_wg, ple_wp = _cast_bf16(ple_w_gate), _cast_bf16(ple_w_proj)
    p_rows = p.reshape(depth * m, pdim)

    h = x.reshape(m, d).astype(F32)
    hn, ssq = _rmsnorm(h, norm_mix[0], BF16), None
    for i in range(depth):
        kind, j = i % 3, i // 3
        if kind == 0:
            h, hg, hssq = _fox_layer(h, hn, ssq, fox_w, j, fox_tail[j * LANES:(j + 1) * LANES],
                                     fox_b_f[j], fox_wo, norm_ple[i], bsz, s)
        elif kind == 1:
            h, hg, hssq = _gdn_layer(h, hn, ssq, gdn_w, j, gdn_tail[j * LANES:(j + 1) * LANES],
                                     gdn_conv[j], gdn_a_log[j], gdn_dt_bias[j], gdn_norm[j], gdn_wo,
                                     norm_ple[i], bsz, s)
        else:
            h, hg, hssq = _ssm_layer(h, hn, ssq, ssm_w, j, ssm_lam_re[j], ssm_lam_im[j], ssm_b_re[j],
                                     ssm_b_im[j], ssm_c_re[j], ssm_c_im[j], ssm_log_step[j], ssm_d[j],
                                     ssm_wg, ssm_b_glu[j], ssm_wo, norm_ple[i], bsz, s)
        extras = [(h, "tile", {}), (p_rows, "rows", {"row_blk": i * (m // min(MM_TM, m))}),
                  (ple_wp, "cols", {"row_blk": i, "rows": pdim})]
        outs = _matmul(hg, ple_wg, w_row_blk=i, out_dtype=F32, mode="ple", row_ssq=hssq, extras=extras,
                       norm_gain=norm_mix[i + 1] if i + 1 < depth else None, w_outer=True)
        h, hn, ssq = outs if i + 1 < depth else (outs, None, None)
    return _rmsnorm(h, final_norm, F32).reshape(bsz, s, d)
```

```python
import functools
import math

import jax
import jax.numpy as jnp
from jax import lax
from jax.experimental import pallas as pl
from jax.experimental.pallas import tpu as pltpu

F32 = jnp.float32
BF16 = jnp.bfloat16

LANES = 128
SUBLANES = 8
VMEM_LIMIT_BYTES = 56 * 1024 * 1024

CAST_BLOCK_ELEMS = 1 << 20
A_PREP_ROWS = 32
MM_TM = 512
MM_TM_WIDE = 1024
MM_TN = 1024
NORM_EPS = 1e-6
LOG2E = 1.4426950408889634
HEAD_DIM = 128
FOX_BLOCK = 1024
GDN_CHUNK = 64
GDN_CONV = 4
GDN_HEADS_PER_STEP = 16
GDN_TIME_BLOCK = 256
SSM_GROUP = 16
SSM_STATE = 64
SSM_PACK = LANES // SSM_GROUP
SSM_L = 16
NEG_BIG = -1e30

HIGHEST = lax.Precision.HIGHEST


def _cparams(*sem):
    return pltpu.CompilerParams(dimension_semantics=sem, vmem_limit_bytes=VMEM_LIMIT_BYTES)


def _sigmoid(x):
    return 0.5 * jnp.tanh(0.5 * x) + 0.5


def _silu(x):
    hx = 0.5 * x
    return hx + hx * jnp.tanh(hx)


def _softplus(x):
    return jnp.maximum(x, 0.0) + jnp.log(1.0 + jnp.exp(-jnp.abs(x)))


def _rmsnorm_kernel(x_ref, g_ref, o_ref):
    x = x_ref[...]
    ms = jnp.mean(x * x, axis=-1, keepdims=True)
    o_ref[...] = (x * lax.rsqrt(ms + NORM_EPS) * g_ref[...]).astype(o_ref.dtype)


def _rmsnorm(x, g, out_dtype):
    m, d = x.shape
    tm = min(512, m)
    return pl.pallas_call(
        _rmsnorm_kernel,
        grid=(m // tm,),
        in_specs=[pl.BlockSpec((tm, d), lambda i: (i, 0)),
                  pl.BlockSpec((1, d), lambda i: (0, 0))],
        out_specs=pl.BlockSpec((tm, d), lambda i: (i, 0)),
        out_shape=jax.ShapeDtypeStruct((m, d), out_dtype),
        compiler_params=_cparams("parallel"),
        name="rmsnorm",
    )(x, g.reshape(1, d).astype(F32))


def _cast_kernel(x_ref, o_ref):
    o_ref[...] = x_ref[...].astype(o_ref.dtype)


def _cast_bf16(x, rows=None):
    nl, r, c = x.shape
    rows = r if rows is None else rows
    tc = min(2048, c)
    tr = min(rows, CAST_BLOCK_ELEMS // tc)
    assert rows % tr == 0 and c % tc == 0
    out = pl.pallas_call(
        _cast_kernel,
        grid=(nl, rows // tr, c // tc),
        in_specs=[pl.BlockSpec((1, tr, tc), lambda l, i, j: (l, i, j))],
        out_specs=pl.BlockSpec((1, tr, tc), lambda l, i, j: (l, i, j)),
        out_shape=jax.ShapeDtypeStruct((nl, rows, c), BF16),
        compiler_params=_cparams("parallel", "parallel", "parallel"),
        name="cast_bf16",
    )(x)
    return out.reshape(nl * rows, c)


_N_EXTRA = {"plain": 0, "colscale": 1, "residual": 1, "ple": 3, "glu": 2}


def _mm_kernel(*refs, mode, cast_a, w_t, has_ssq, emit_norm, inv_k):
    it = iter(refs)
    a_ref, w_ref = next(it), next(it)
    ssq_ref = next(it) if has_ssq else None
    extra = [next(it) for _ in range(_N_EXTRA[mode])]
    gain_ref = next(it) if emit_norm else None
    o_ref = next(it)
    hg_ref, ssq_out_ref = (next(it), next(it)) if emit_norm else (None, None)
    if cast_a:
        a_scr = next(it)

        @pl.when(pl.program_id(1) == 0)
        def _():
            def prep(r, carry):
                rows = pl.ds(pl.multiple_of(r * A_PREP_ROWS, A_PREP_ROWS), A_PREP_ROWS)
                a_scr[rows, :] = a_ref[rows, :].astype(BF16)
                return carry

            lax.fori_loop(0, a_ref.shape[0] // A_PREP_ROWS, prep, 0)

        a = a_scr[...]
    else:
        a = a_ref[...]

    acc = lax.dot_general(a, w_ref[...], (((1,), (1 if w_t else 0,)), ((), ())),
                          preferred_element_type=F32)
    if has_ssq:
        parts = ssq_ref[...]
        tot = parts[:, 0:LANES]
        for t in range(1, parts.shape[1] // LANES):
            tot = tot + parts[:, t * LANES:(t + 1) * LANES]
        acc = acc * lax.rsqrt(tot[:, 0:1] * inv_k + NORM_EPS)
    if mode == "plain":
        out = acc
    elif mode == "colscale":
        out = acc * extra[0][...]
    elif mode == "residual":
        out = extra[0][...] + acc
    elif mode == "ple":
        res_ref, p_ref, wp_ref = extra
        emb = jnp.dot(p_ref[...].astype(BF16), wp_ref[...], preferred_element_type=F32)
        out = res_ref[...] + _sigmoid(acc) * emb
    elif mode == "glu":
        z_ref, b_ref = extra
        tn = o_ref.shape[1]
        y = a_ref[:, pl.ds(pl.multiple_of(pl.program_id(1) * tn, tn), tn)]
        out = y * _sigmoid(acc + b_ref[...]) * _silu(z_ref[...])
    else:
        raise ValueError(mode)
    o_ref[...] = out.astype(o_ref.dtype)
    if emit_norm:
        hg_ref[...] = (out * gain_ref[...]).astype(hg_ref.dtype)
        ssq_out_ref[...] = jnp.broadcast_to(jnp.sum(out * out, axis=-1, keepdims=True),
                                            ssq_out_ref.shape)


def _extra_spec(arr, kind, tm, tn, row_blk=0, col_off=0, rows=None):
    if kind == "tile":
        return pl.BlockSpec((tm, tn), lambda i, j: (i + row_blk, j + col_off // tn))
    if kind == "row":
        return pl.BlockSpec((1, tn), lambda i, j: (0, j))
    if kind == "rows":
        return pl.BlockSpec((tm, arr.shape[1]), lambda i, j: (i + row_blk, 0))
    if kind == "cols":
        return pl.BlockSpec((rows, tn), lambda i, j: (row_blk, j))
    raise ValueError(kind)


def _matmul(a, w, *, out_dtype, n=None, mode="plain", extras=(), cast_a=False, row_ssq=None,
            norm_gain=None, w_t=None, w_row_blk=0, w_col_off=0, w_outer=False, tm=MM_TM, tn=MM_TN):
    m, kdim = a.shape
    n = (w_t if w_t else w.shape[1]) if n is None else n
    tm = min(tm, m)
    tn = functools.reduce(math.gcd, [tn, n, w_col_off, w_t or 0])
    assert m % tm == 0 and tn % LANES == 0 and len(extras) == _N_EXTRA[mode]
    operands = [a, w]
    if w_t:
        first = (w_row_blk * w_t + w_col_off) // tn
        w_spec = pl.BlockSpec((tn, kdim), lambda i, j: (first + j, 0))
    else:
        w_spec = pl.BlockSpec((kdim, tn), lambda i, j: (w_row_blk, j + w_col_off // tn))
    in_specs = [pl.BlockSpec((tm, kdim), lambda i, j: (i, 0)), w_spec]
    if row_ssq is not None:
        operands.append(row_ssq)
        in_specs.append(pl.BlockSpec((tm, row_ssq.shape[1]), lambda i, j: (i, 0)))
    for arr, kind, opts in extras:
        assert opts.get("col_off", 0) % tn == 0
        operands.append(arr)
        in_specs.append(_extra_spec(arr, kind, tm, tn, **opts))
    tile = pl.BlockSpec((tm, tn), lambda i, j: (i, j))
    out_specs, out_shape = tile, jax.ShapeDtypeStruct((m, n), out_dtype)
    if norm_gain is not None:
        operands.append(norm_gain.astype(F32).reshape(1, n))
        in_specs.append(pl.BlockSpec((1, tn), lambda i, j: (0, j)))
        out_specs = [tile, tile, pl.BlockSpec((tm, LANES), lambda i, j: (i, j))]
        out_shape = [out_shape, jax.ShapeDtypeStruct((m, n), BF16),
                     jax.ShapeDtypeStruct((m, n // tn * LANES), F32)]
    scratch = [pltpu.VMEM((tm, kdim), BF16)] if cast_a else []
    grid = (m // tm, n // tn)
    if w_outer:
        assert not cast_a
        flip = lambda sp: pl.BlockSpec(sp.block_shape, lambda j, i, f=sp.index_map: f(i, j))
        in_specs = [flip(sp) for sp in in_specs]
        out_specs = [flip(sp) for sp in out_specs] if norm_gain is not None else flip(out_specs)
        grid = grid[::-1]
    return pl.pallas_call(
        functools.partial(_mm_kernel, mode=mode, cast_a=cast_a, w_t=bool(w_t),
                          has_ssq=row_ssq is not None,
                          emit_norm=norm_gain is not None, inv_k=1.0 / kdim),
        grid=grid,
        in_specs=in_specs,
        out_specs=out_specs,
        out_shape=out_shape,
        scratch_shapes=scratch,
        compiler_params=_cparams("parallel", "arbitrary"),
        name="matmul_" + mode,
    )(*operands)


def _fox_cum_kernel(f_ref, b_ref, o_ref, carry_ref, *, ts):
    @pl.when(pl.program_id(1) == 0)
    def _():
        carry_ref[...] = jnp.zeros_like(carry_ref)

    x = f_ref[...] + b_ref[...]
    log_f = jnp.minimum(x, 0.0) - jnp.log(1.0 + jnp.exp(-jnp.abs(x)))
    r = lax.broadcasted_iota(jnp.int32, (ts, ts), 0)
    c = lax.broadcasted_iota(jnp.int32, (ts, ts), 1)
    lower = (c <= r).astype(F32)
    cum = jnp.dot(lower, log_f, precision=HIGHEST, preferred_element_type=F32) + carry_ref[...]
    carry_ref[...] = cum[ts - 1:ts, :]
    o_ref[0] = cum.T


def _fox_cum(f_logit, b_f, bsz, s):
    ts = min(1024, s)
    return pl.pallas_call(
        functools.partial(_fox_cum_kernel, ts=ts),
        grid=(bsz, s // ts),
        in_specs=[pl.BlockSpec((ts, LANES), lambda b, t: (b * (s // ts) + t, 0)),
                  pl.BlockSpec((1, LANES), lambda b, t: (0, 0))],
        out_specs=pl.BlockSpec((1, LANES, ts), lambda b, t: (b, 0, t)),
        out_shape=jax.ShapeDtypeStruct((bsz, LANES, s), F32),
        scratch_shapes=[pltpu.VMEM((1, LANES), F32)],
        compiler_params=_cparams("parallel", "arbitrary"),
        name="fox_cum",
    )(f_logit, b_f)


def _fox_attn_kernel(q_ref, k_ref, v_ref, c_ref, z_ref, o_ref, sa_ref, sb_ref, *, tq):
    tk = tq // 2
    i = pl.program_id(2)
    q = q_ref[...]
    c0 = c_ref[0, 0, :, pl.ds(pl.multiple_of(i * tq, tq), LANES)][:, 0:1]

    def scores(qrows, j):
        start = j * tk if isinstance(j, int) else pl.multiple_of(j * tk, tk)
        s = lax.dot_general(qrows, k_ref[pl.ds(start, tk), :], (((1,), (1,)), ((), ())),
                            preferred_element_type=F32)
        return s + (c0 - c_ref[0, 0, :, pl.ds(start, tk)]) * LOG2E

    def update(s, j, m, acc, width=1):
        rows = width * tk
        start = j * tk if isinstance(j, int) else pl.multiple_of(j * tk, tk)
        vb = jnp.concatenate([v_ref[pl.ds(start, rows), :], jnp.ones((rows, HEAD_DIM), BF16)], axis=1)
        m_new = jnp.maximum(m, jnp.max(s, axis=-1, keepdims=True))
        p = jnp.exp2(s - m_new)
        acc = jnp.exp2(m - m_new) * acc + jnp.dot(p.astype(BF16), vb, preferred_element_type=F32)
        return m_new, acc

    sa_ref[...] = scores(q, 0)

    def pair(t, carry):
        m, acc = carry
        sb_ref[...] = scores(q, 2 * t + 1)
        m, acc = update(sa_ref[...], 2 * t, m, acc)
        sa_ref[...] = scores(q, 2 * t + 2)
        return update(sb_ref[...], 2 * t + 1, m, acc)

    init = (jnp.full((tq, 1), NEG_BIG, F32), jnp.zeros((tq, 2 * HEAD_DIM), F32))
    carry = lax.fori_loop(
        0, i // 3, lambda t3, cr: pair(3 * t3 + 2, pair(3 * t3 + 1, pair(3 * t3, cr))), init)

    def finish(a, rows):
        o = a[:, 0:HEAD_DIM] / a[:, HEAD_DIM:HEAD_DIM + 1]
        o_ref[rows, :] = (o * _silu(z_ref[rows, :])).astype(o_ref.dtype)

    def tail(carry, left):
        for n in range(left):
            carry = pair(i - left + n, carry)
        m, acc = carry
        r = lax.broadcasted_iota(jnp.int32, (tk, tk), 0)
        c = lax.broadcasted_iota(jnp.int32, (tk, tk), 1)
        s_bot = jnp.concatenate(
            [sa_ref[tk:tq, :], jnp.where(c <= r, scores(q[tk:tq], 2 * i + 1), NEG_BIG)], axis=1)
        _, acc_top = update(jnp.where(c <= r, sa_ref[0:tk, :], NEG_BIG), 2 * i, m[0:tk], acc[0:tk])
        _, acc_bot = update(s_bot, 2 * i, m[tk:tq], acc[tk:tq], width=2)
        finish(acc_top, slice(0, tk))
        finish(acc_bot, slice(tk, tq))

    left = i % 3
    lax.cond(left == 0, lambda cr: tail(cr, 0),
             lambda cr: lax.cond(left == 1, lambda c2: tail(c2, 1), lambda c2: tail(c2, 2), cr), carry)


def _fox_attention(qkv, cum4, z, bsz, s, heads):
    tq = min(FOX_BLOCK, s)
    nq = s // tq
    return pl.pallas_call(
        functools.partial(_fox_attn_kernel, tq=tq),
        grid=(bsz, heads, nq),
        in_specs=[pl.BlockSpec((tq, HEAD_DIM), lambda b, h, i: (b * nq + i, h)),
                  pl.BlockSpec((s, HEAD_DIM), lambda b, h, i: (b, heads + h)),
                  pl.BlockSpec((s, HEAD_DIM), lambda b, h, i: (b, 2 * heads + h)),
                  pl.BlockSpec((1, 1, 1, s), lambda b, h, i: (b, h, 0, 0)),
                  pl.BlockSpec((tq, HEAD_DIM), lambda b, h, i: (b * nq + i, h))],
        out_specs=pl.BlockSpec((tq, HEAD_DIM), lambda b, h, i: (b * nq + i, h)),
        out_shape=jax.ShapeDtypeStruct((bsz * s, heads * HEAD_DIM), BF16),
        scratch_shapes=[pltpu.VMEM((tq, tq // 2), F32)] * 2,
        compiler_params=_cparams("parallel", "parallel", "arbitrary"),
        name="fox_attn",
    )(qkv, qkv, qkv, cum4, z)


def _fox_layer(h, hn, ssq, w_all, layer, w_tail, b_f, wo_all, ple_gain, bsz, s):
    width = wo_all.shape[1]
    heads = width // HEAD_DIM
    q_scale = jnp.concatenate([jnp.full((width,), HEAD_DIM ** -0.5 * LOG2E, F32),
                               jnp.ones((2 * width,), F32)]).reshape(1, 3 * width)
    qkv = _matmul(hn, w_all, n=3 * width, w_t=4 * width, w_row_blk=layer, out_dtype=BF16,
                  mode="colscale", extras=[(q_scale, "row", {})], row_ssq=ssq, tm=MM_TM_WIDE)
    z = _matmul(hn, w_all, n=width, w_t=4 * width, w_row_blk=layer, w_col_off=3 * width,
                out_dtype=F32, row_ssq=ssq, w_outer=True, tm=MM_TM_WIDE)
    f_logit = _matmul(hn, w_tail, w_t=LANES, out_dtype=F32, row_ssq=ssq, tm=MM_TM_WIDE)
    b_row = jnp.pad(b_f.astype(F32), (0, LANES - heads)).reshape(1, LANES)
    cum = _fox_cum(f_logit, b_row, bsz, s)
    gated = _fox_attention(qkv, cum.reshape(bsz, LANES, 1, s), z, bsz, s, heads)
    return _matmul(gated, wo_all, w_row_blk=layer, out_dtype=F32, mode="residual",
                   extras=[(h, "tile", {})], norm_gain=ple_gain, w_outer=True)


def _bdot(a, b):
    return jnp.einsum("bmk,bkn->bmn", a, b, preferred_element_type=F32)


def _split_bf16(x):
    hi = x.astype(BF16)
    return hi, (x - hi.astype(F32)).astype(BF16)


def _bdot3_parts(a_hi, a_lo, b_hi, b_lo):
    return _bdot(a_hi, b_hi) + _bdot(a_hi, b_lo) + _bdot(a_lo, b_hi)


def _bdot3(a, b):
    return _bdot3_parts(*_split_bf16(a), *_split_bf16(b))


def _bdot_exact_lhs(a_bf16, b):
    b_hi = b.astype(BF16)
    b_mid, b_lo = _split_bf16(b - b_hi.astype(F32))
    return _bdot(a_bf16, b_hi) + _bdot(a_bf16, b_mid) + _bdot(a_bf16, b_lo)


def _gdn_kernel(q_ref, k_ref, v_ref, z_ref, t_ref, cw_ref, gp_ref, nw_ref, o_ref,
                xq_ref, xk_ref, xv_ref, st_ref, u_ref, wq_ref, ik_ref, gl_ref, *, tb, heads, hps):
    hb = pl.program_id(1)
    c, dh = GDN_CHUNK, HEAD_DIM
    nb = tb // c
    nbat = hps * nb
    pad = SUBLANES

    @pl.when(pl.program_id(2) == 0)
    def _():
        for r in (xq_ref, xk_ref, xv_ref):
            r[0:pad, :] = jnp.zeros((pad, hps * dh), F32)
        st_ref[...] = jnp.zeros_like(st_ref)

    def conv(x_ref, xs_ref, col):
        x = x_ref[...]
        xs_ref[pad:pad + tb, :] = x
        xs = xs_ref[...]
        y = x * cw_ref[col, GDN_CONV - 1:GDN_CONV, :]
        for j in range(GDN_CONV - 1):
            shifted = pltpu.roll(xs, GDN_CONV - 1 - j, axis=0)[pad:pad + tb, :]
            y = y + shifted * cw_ref[col, j:j + 1, :]
        xs_ref[0:pad, :] = x[tb - pad:tb, :]
        return _silu(y)

    def by_chunk(x):
        w = x.shape[1] // hps
        return jnp.concatenate([x[:, g * w:(g + 1) * w].reshape(nb, c, w) for g in range(hps)], axis=0)

    def l2n(x):
        return x * lax.rsqrt(jnp.sum(x * x, axis=-1, keepdims=True) + NORM_EPS)

    q = l2n(by_chunk(conv(q_ref, xq_ref, 0))) * (dh ** -0.5)
    k = l2n(by_chunk(conv(k_ref, xk_ref, 1)))
    v = by_chunk(conv(v_ref, xv_ref, 2))

    t = t_ref[...]
    lane = lax.broadcasted_iota(jnp.int32, (tb, LANES), 1)
    beta_all = _sigmoid(t)
    g_all = -jnp.exp(gp_ref[0:1, :]) * _softplus(t + gp_ref[1:2, :])

    def pick(x, first):
        cols = [jnp.sum(jnp.where(lane == first + hb * hps + g, x, 0.0), axis=-1, keepdims=True)
                for g in range(hps)]
        return jnp.concatenate([col.reshape(nb, c, 1) for col in cols], axis=0)

    beta = pick(beta_all, 0)
    g = pick(g_all, heads)

    ri = lax.broadcasted_iota(jnp.int32, (c, c), 0)
    ci = lax.broadcasted_iota(jnp.int32, (c, c), 1)
    incl = ci <= ri
    strict = ci < ri
    eye = (ci == ri).astype(F32)
    tri = jnp.broadcast_to(incl.astype(BF16), (nbat, c, c))
    rl = lax.broadcasted_iota(jnp.int32, (c, LANES), 0)
    cl = lax.broadcasted_iota(jnp.int32, (c, LANES), 1)
    seg_mask = (rl > cl) | (cl == c)

    seg = _bdot_exact_lhs(tri, jnp.where(seg_mask, g, 0.0))
    gc = seg[:, :, c:c + 1]
    g_last = seg[:, c - 1:c, c:c + 1]
    decay = jnp.exp(jnp.where(incl, seg[:, :, 0:c], NEG_BIG))
    egc = jnp.exp(gc)
    k_beta = k * beta
    kq = jnp.einsum("bmd,bnd->bmn", jnp.concatenate([k_beta, q], axis=1).astype(BF16),
                    k.astype(BF16), preferred_element_type=F32)
    a = jnp.where(strict, kq[:, 0:c] * decay, 0.0)
    intra = kq[:, c:2 * c] * decay
    npair = nbat // 2
    pair_lanes = lambda x: jnp.concatenate(
        [x.reshape(npair, 2, c, x.shape[2])[:, 0], x.reshape(npair, 2, c, x.shape[2])[:, 1]], axis=2)
    left = lax.broadcasted_iota(jnp.int32, (c, 2 * c), 1) < c

    def block_diag(x2):
        zero = jnp.zeros_like(x2)
        return jnp.concatenate([jnp.where(left, x2, zero), jnp.where(left, zero, x2)], axis=1)

    a2 = pair_lanes(a)
    inv2 = jnp.concatenate([eye, eye], axis=1) - a2
    a_hi, a_lo = _split_bf16(a2)
    pw2 = _bdot3_parts(a_hi, a_lo, block_diag(a_hi), block_diag(a_lo))
    levels = int(math.log2(c)) - 1
    for lvl in range(levels):
        inv_hi, inv_lo = _split_bf16(inv2)
        pw_hi, pw_lo = _split_bf16(pw2)
        pd_hi, pd_lo = block_diag(pw_hi), block_diag(pw_lo)
        if lvl < levels - 1:
            both = _bdot3_parts(jnp.concatenate([inv_hi, pw_hi], axis=1),
                                jnp.concatenate([inv_lo, pw_lo], axis=1), pd_hi, pd_lo)
            inv2 = inv2 + both[:, 0:c]
            pw2 = both[:, c:2 * c]
        else:
            inv2 = inv2 + _bdot3_parts(inv_hi, inv_lo, pd_hi, pd_lo)
    rhs = jnp.concatenate([v * beta, k_beta * egc], axis=2).reshape(npair, 2, c, 2 * dh)
    zero = jnp.zeros((npair, c, 2 * dh), F32)
    rhs_bd = jnp.concatenate([jnp.concatenate([rhs[:, 0], zero], axis=2),
                              jnp.concatenate([zero, rhs[:, 1]], axis=2)], axis=1)
    uw2 = _bdot3(inv2, rhs_bd)
    uw = jnp.stack([uw2[:, :, 0:2 * dh], uw2[:, :, 2 * dh:4 * dh]], axis=1).reshape(nbat, c, 2 * dh)
    u_ref[...] = uw[:, :, 0:dh].reshape(nbat * c, dh)
    wq_ref[...] = jnp.concatenate([uw[:, :, dh:2 * dh], q * egc], axis=1).astype(BF16).reshape(
        nbat * 2 * c, dh)
    k_dec = k * jnp.exp(g_last - gc)
    for b in range(nbat):
        ik_ref[b * (c + dh):(b + 1) * (c + dh), :] = jnp.concatenate(
            [intra[b], k_dec[b].T], axis=0).astype(BF16)
    gl_ref[...] = jnp.broadcast_to(jnp.exp(g_last), (nbat, SUBLANES, dh)).reshape(nbat * SUBLANES, dh)

    def step(n, states):
        rows = lambda g, size: pl.ds(pl.multiple_of((g * nb + n) * size, size), size)
        ws = [jnp.dot(wq_ref[rows(g, 2 * c), :], states[g].astype(BF16), preferred_element_type=F32)
              for g in range(hps)]
        v_new = [u_ref[rows(g, c), :] - ws[g][0:c] for g in range(hps)]
        iv = [jnp.dot(ik_ref[pl.ds(pl.multiple_of((g * nb + n) * (c + dh), c), c + dh), :],
                      v_new[g].astype(BF16), preferred_element_type=F32) for g in range(hps)]
        new_states = []
        r0 = pl.multiple_of(n * c, c)
        for g in range(hps):
            out = ws[g][c:2 * c] + iv[g][0:c]
            gl = gl_ref[pl.ds(pl.multiple_of((g * nb + n) * SUBLANES, SUBLANES), 1), :]
            new_states.append(states[g] * gl + iv[g][c:c + dh])
            ms = jnp.mean(out * out, axis=-1, keepdims=True)
            o = out * lax.rsqrt(ms + NORM_EPS) * nw_ref[...]
            o_ref[pl.ds(r0, c), g * dh:(g + 1) * dh] = (
                o * _silu(z_ref[pl.ds(r0, c), g * dh:(g + 1) * dh])).astype(o_ref.dtype)
        return tuple(new_states)

    states = lax.fori_loop(0, nb, step, tuple(st_ref[g] for g in range(hps)))
    for g in range(hps):
        st_ref[g] = states[g]


def _gdn_mixer(qkvz, tail, conv_w, gate_p, norm_w, bsz, s, heads):
    hps = math.gcd(GDN_HEADS_PER_STEP, heads)
    tb = min(GDN_TIME_BLOCK, s)
    nt = s // tb
    hblocks = heads // hps
    width = hps * HEAD_DIM
    nbat = hps * (tb // GDN_CHUNK)
    col = lambda off: pl.BlockSpec((tb, width), lambda b, h, t: (b * nt + t, off + h))
    return pl.pallas_call(
        functools.partial(_gdn_kernel, tb=tb, heads=heads, hps=hps),
        grid=(bsz, hblocks, nt),
        in_specs=[col(0), col(hblocks), col(2 * hblocks), col(3 * hblocks),
                  pl.BlockSpec((tb, LANES), lambda b, h, t: (b * nt + t, 0)),
                  pl.BlockSpec((3, GDN_CONV, width), lambda b, h, t: (0, 0, h)),
                  pl.BlockSpec((2, LANES), lambda b, h, t: (0, 0)),
                  pl.BlockSpec((1, HEAD_DIM), lambda b, h, t: (0, 0))],
        out_specs=pl.BlockSpec((tb, width), lambda b, h, t: (b * nt + t, h)),
        out_shape=jax.ShapeDtypeStruct((bsz * s, heads * HEAD_DIM), BF16),
        scratch_shapes=[pltpu.VMEM((tb + SUBLANES, width), F32)] * 3
                       + [pltpu.VMEM((hps, HEAD_DIM, HEAD_DIM), F32),
                          pltpu.VMEM((nbat * GDN_CHUNK, HEAD_DIM), F32),
                          pltpu.VMEM((nbat * 2 * GDN_CHUNK, HEAD_DIM), BF16),
                          pltpu.VMEM((nbat * (GDN_CHUNK + HEAD_DIM), GDN_CHUNK), BF16),
                          pltpu.VMEM((nbat * SUBLANES, HEAD_DIM), F32)],
        compiler_params=_cparams("parallel", "parallel", "arbitrary"),
        name="gdn_mixer",
    )(qkvz, qkvz, qkvz, qkvz, tail, conv_w, gate_p, norm_w)


def _gdn_layer(h, hn, ssq, w_all, layer, w_tail, conv_w, a_log, dt_bias, norm_w, wo_all, ple_gain,
               bsz, s):
    width = wo_all.shape[1]
    heads = width // HEAD_DIM
    qkvz = _matmul(hn, w_all, w_t=4 * width, w_row_blk=layer, out_dtype=F32, row_ssq=ssq,
                   tm=MM_TM_WIDE)
    tail = _matmul(hn, w_tail, w_t=LANES, out_dtype=F32, row_ssq=ssq, tm=MM_TM_WIDE)
    cw = conv_w.astype(F32).reshape(GDN_CONV, 3, width).transpose(1, 0, 2)
    lane_pad = lambda x: jnp.pad(x.astype(F32), (heads, LANES - 2 * heads))
    gate_p = jnp.stack([lane_pad(a_log), lane_pad(dt_bias)])
    gated = _gdn_mixer(qkvz, tail, cw, gate_p, norm_w.astype(F32).reshape(1, HEAD_DIM),
                       bsz, s, heads)
    return _matmul(gated, wo_all, w_row_blk=layer, out_dtype=F32, mode="residual",
                   extras=[(h, "tile", {})], norm_gain=ple_gain, w_outer=True)


def _ssm_prep_kernel(lam_ref, bf_ref, cf_ref, bd_ref, wz_ref, wyt_ref, laml_ref):
    ns = SSM_PACK * SSM_STATE
    lam_re, lam_im, step_log = lam_ref[0, 0:1, :], lam_ref[0, 1:2, :], lam_ref[0, 2:3, :]
    step = jnp.exp(step_log)
    mag = jnp.exp(lam_re * step)
    lb_re, lb_im = mag * jnp.cos(lam_im * step), mag * jnp.sin(lam_im * step)
    den = lam_re * lam_re + lam_im * lam_im
    num_re = lb_re - 1.0
    zoh_re = (num_re * lam_re + lb_im * lam_im) / den
    zoh_im = (lb_im * lam_re - num_re * lam_im) / den
    b_re, b_im = bf_ref[0, 0], bf_ref[0, 1]
    bb_re = zoh_re * b_re - zoh_im * b_im
    bb_im = zoh_re * b_im + zoh_im * b_re
    c_re, c_im = cf_ref[0, 0], cf_ref[0, 1]
    cc_hi, cc_lo = _split_bf16(jnp.concatenate([c_re, -c_im], axis=1))
    nt_dot = lambda x, y: lax.dot_general(x, y, (((1,), (1,)), ((), ())), preferred_element_type=F32)

    bd_ref[0, 0, LANES:2 * LANES, 0:LANES] = jnp.zeros((LANES, LANES), bd_ref.dtype)
    pw_re, pw_im = jnp.ones_like(lb_re), jnp.zeros_like(lb_re)
    for d in range(SSM_L + 1):
        if d < SSM_L:
            a_re = pw_re * bb_re - pw_im * bb_im
            a_im = pw_re * bb_im + pw_im * bb_re
            a_d = jnp.concatenate([a_re, a_im], axis=1)
            a_hi, a_lo = _split_bf16(a_d)
            blk = (nt_dot(a_hi, cc_hi) + nt_dot(a_hi, cc_lo) + nt_dot(a_lo, cc_hi)).astype(bd_ref.dtype)
            d2 = d // 2
            if d % 2 == 0:
                bd_ref[0, d2, 0:LANES, 0:LANES] = blk
                bd_ref[0, d2, LANES:2 * LANES, LANES:2 * LANES] = blk
            else:
                bd_ref[0, d2, 0:LANES, LANES:2 * LANES] = blk
                if d2 + 1 < SSM_L // 2:
                    bd_ref[0, d2 + 1, LANES:2 * LANES, 0:LANES] = blk
            tin = SSM_L - 1 - d
            wz_ref[0, tin * LANES:(tin + 1) * LANES, :] = a_d.astype(wz_ref.dtype)
        if d >= 1:
            y_re = pw_re * c_re - pw_im * c_im
            y_im = pw_re * c_im + pw_im * c_re
            wyt_ref[0, (d - 1) * LANES:d * LANES, :] = jnp.concatenate(
                [y_re, -y_im], axis=1).astype(wyt_ref.dtype)
        if d == SSM_L:
            laml_ref[0] = jnp.concatenate([pw_re, pw_im], axis=1)
        pw_re, pw_im = pw_re * lb_re - pw_im * lb_im, pw_re * lb_im + pw_im * lb_re


def _ssm_prep(lam, bfull, cfull):
    nsg = lam.shape[0]
    ns = SSM_PACK * SSM_STATE
    rows = SSM_L * LANES
    return pl.pallas_call(
        _ssm_prep_kernel,
        grid=(nsg,),
        in_specs=[pl.BlockSpec((1, 3, ns), lambda g: (g, 0, 0)),
                  pl.BlockSpec((1, 2, LANES, ns), lambda g: (g, 0, 0, 0)),
                  pl.BlockSpec((1, 2, LANES, ns), lambda g: (g, 0, 0, 0))],
        out_specs=[pl.BlockSpec((1, SSM_L // 2, 2 * LANES, 2 * LANES), lambda g: (g, 0, 0, 0)),
                   pl.BlockSpec((1, rows, 2 * ns), lambda g: (g, 0, 0)),
                   pl.BlockSpec((1, rows, 2 * ns), lambda g: (g, 0, 0)),
                   pl.BlockSpec((1, 1, 2 * ns), lambda g: (g, 0, 0))],
        out_shape=[jax.ShapeDtypeStruct((nsg, SSM_L // 2, 2 * LANES, 2 * LANES), BF16),
                   jax.ShapeDtypeStruct((nsg, rows, 2 * ns), BF16),
                   jax.ShapeDtypeStruct((nsg, rows, 2 * ns), BF16),
                   jax.ShapeDtypeStruct((nsg, 1, 2 * ns), F32)],
        compiler_params=_cparams("parallel"),
        name="ssm_prep",
    )(lam, bfull, cfull)


def _ssm_scan_kernel(u_ref, bd_ref, wz_ref, wyt_ref, laml_ref, d_ref, o_ref,
                     x_ref, z_ref, xp_ref, st_ref, *, tc):
    ns = SSM_PACK * SSM_STATE

    @pl.when(pl.program_id(2) == 0)
    def _():
        st_ref[...] = jnp.zeros_like(st_ref)

    for tau in range(SSM_L):
        x_ref[:, tau * LANES:(tau + 1) * LANES] = u_ref[pl.ds(tau, tc, stride=SSM_L), :].astype(BF16)

    z_ref[...] = jnp.dot(x_ref[...], wz_ref[0], preferred_element_type=F32)

    a_re, a_im = laml_ref[0, :, 0:ns], laml_ref[0, :, ns:2 * ns]

    def row(r, carry):
        s_re, s_im = carry
        xp_ref[pl.ds(r, 1), :] = jnp.concatenate([s_re, s_im], axis=1)
        zr = z_ref[pl.ds(r, 1), :]
        n_re = a_re * s_re - a_im * s_im + zr[:, 0:ns]
        n_im = a_re * s_im + a_im * s_re + zr[:, ns:2 * ns]
        return n_re, n_im

    s_re, s_im = lax.fori_loop(0, tc, row, (st_ref[:, 0:ns], st_ref[:, ns:2 * ns]), unroll=4)
    st_ref[...] = jnp.concatenate([s_re, s_im], axis=1)

    xprev = xp_ref[...].astype(BF16)
    pair = 2 * LANES
    for t2 in range(SSM_L // 2):
        y2 = lax.dot_general(xprev, wyt_ref[0, t2 * pair:(t2 + 1) * pair, :],
                             (((1,), (1,)), ((), ())), preferred_element_type=F32)
        for tin in range(t2 + 1):
            y2 = y2 + jnp.dot(x_ref[:, tin * pair:(tin + 1) * pair], bd_ref[0, t2 - tin],
                              preferred_element_type=F32)
        for half in range(2):
            tau = 2 * t2 + half
            y = y2[:, half * LANES:(half + 1) * LANES]
            y = y + d_ref[...] * u_ref[pl.ds(tau, tc, stride=SSM_L), :]
            y = 0.5 * y * (1.0 + jnp.tanh(math.sqrt(2.0 / math.pi) * (y + 0.044715 * (y * y * y))))
            o_ref[pl.ds(tau, tc, stride=SSM_L), :] = y


def _ssm_scan(uz, bd, wz, wyt, laml, d_skip, bsz, s):
    e = d_skip.shape[1]
    nsg = e // LANES
    ns = SSM_PACK * SSM_STATE
    tb = min(8192, s)
    nt = s // tb
    tc = tb // SSM_L
    rows = SSM_L * LANES
    return pl.pallas_call(
        functools.partial(_ssm_scan_kernel, tc=tc),
        grid=(nsg, bsz, nt),
        in_specs=[pl.BlockSpec((tb, LANES), lambda g, b, t: (b * nt + t, g)),
                  pl.BlockSpec((1, SSM_L // 2, 2 * LANES, 2 * LANES), lambda g, b, t: (g, 0, 0, 0)),
                  pl.BlockSpec((1, rows, 2 * ns), lambda g, b, t: (g, 0, 0)),
                  pl.BlockSpec((1, rows, 2 * ns), lambda g, b, t: (g, 0, 0)),
                  pl.BlockSpec((1, 1, 2 * ns), lambda g, b, t: (g, 0, 0)),
                  pl.BlockSpec((1, LANES), lambda g, b, t: (0, g))],
        out_specs=pl.BlockSpec((tb, LANES), lambda g, b, t: (b * nt + t, g)),
        out_shape=jax.ShapeDtypeStruct((uz.shape[0], e), F32),
        scratch_shapes=[pltpu.VMEM((tc, rows), BF16), pltpu.VMEM((tc, 2 * ns), F32),
                        pltpu.VMEM((tc, 2 * ns), F32), pltpu.VMEM((1, 2 * ns), F32)],
        compiler_params=_cparams("parallel", "parallel", "arbitrary"),
        name="ssm_scan",
    )(uz, bd, wz, wyt, laml, d_skip)


def _ssm_layer(h, hn, ssq, w_all, layer, lam_re, lam_im, b_re, b_im, c_re, c_im, log_step, d_skip,
               wg_all, b_glu, wo_all, ple_gain, bsz, s):
    e = wo_all.shape[1]
    groups, nstate = lam_re.shape
    assert nstate == SSM_STATE and e == groups * SSM_GROUP and groups % SSM_PACK == 0
    nsg = groups // SSM_PACK
    ns = SSM_PACK * SSM_STATE
    lam = jnp.stack([lam_re.astype(F32).reshape(nsg, ns), lam_im.astype(F32).reshape(nsg, ns),
                     jnp.repeat(log_step.astype(F32), SSM_STATE).reshape(nsg, ns)], axis=1)
    eye = jnp.eye(SSM_PACK, dtype=F32)

    def expand_b(b):
        return jnp.einsum("sgpm,gh->sgmhp", b.astype(F32).reshape(nsg, SSM_PACK, SSM_STATE, SSM_GROUP),
                          eye).reshape(nsg, LANES, ns)

    def expand_c(c):
        return jnp.einsum("sgnp,gh->sgnhp", c.astype(F32).reshape(nsg, SSM_PACK, SSM_GROUP, SSM_STATE),
                          eye).reshape(nsg, LANES, ns)

    bfull = jnp.stack([expand_b(b_re), expand_b(b_im)], axis=1)
    cfull = jnp.stack([expand_c(c_re), expand_c(c_im)], axis=1)
    bd, wz, wyt, laml = _ssm_prep(lam, bfull, cfull)

    uz = _matmul(hn, w_all, w_row_blk=layer, out_dtype=F32, row_ssq=ssq, tm=MM_TM_WIDE)
    y = _ssm_scan(uz, bd, wz, wyt, laml, d_skip.astype(F32).reshape(1, e), bsz, s)
    y2 = _matmul(y, wg_all, w_row_blk=layer, out_dtype=BF16, mode="glu", cast_a=True,
                 extras=[(uz, "tile", {"col_off": e}), (b_glu.astype(F32).reshape(1, e), "row", {})])
    return _matmul(y2, wo_all, w_row_blk=layer, out_dtype=F32, mode="residual",
                   extras=[(h, "tile", {})], norm_gain=ple_gain, w_outer=True)


def kernel(x, p, norm_mix, fox_w_in, fox_b_f, fox_w_out, gdn_w_in, gdn_conv, gdn_a_log, gdn_dt_bias, gdn_norm, gdn_w_out, ssm_w_in, ssm_lam_re, ssm_lam_im, ssm_b_re, ssm_b_im, ssm_c_re, ssm_c_im, ssm_log_step, ssm_d, ssm_w_glu, ssm_b_glu, ssm_w_out, norm_ple, ple_w_proj, ple_w_gate, final_norm):
    bsz, s, d = x.shape
    depth, m = p.shape[0], bsz * s
    pdim = p.shape[-1]
    fox_width, gdn_width = fox_w_out.shape[1], gdn_w_out.shape[1]
    fox_wt, gdn_wt = jnp.swapaxes(fox_w_in, 1, 2), jnp.swapaxes(gdn_w_in, 1, 2)
    pad_rows = lambda t: jnp.pad(t, ((0, 0), (0, LANES - t.shape[1]), (0, 0)))
    fox_w = _cast_bf16(fox_wt, 4 * fox_width)
    fox_tail = _cast_bf16(pad_rows(fox_wt[:, 4 * fox_width:]))
    gdn_w = _cast_bf16(gdn_wt, 4 * gdn_width)
    gdn_tail = _cast_bf16(pad_rows(gdn_wt[:, 4 * gdn_width:]))
    fox_wo, gdn_wo = _cast_bf16(fox_w_out), _cast_bf16(gdn_w_out)
    ssm_w, ssm_wg, ssm_wo = _cast_bf16(ssm_w_in), _cast_bf16(ssm_w_glu), _cast_bf16(ssm_w_out)
    ple_wg, ple_wp = _cast_bf16(ple_w_gate), _cast_bf16(ple_w_proj)
    p_rows = p.reshape(depth * m, pdim)

    h = x.reshape(m, d).astype(F32)
    hn, ssq = _rmsnorm(h, norm_mix[0], BF16), None
    for i in range(depth):
        kind, j = i % 3, i // 3
        if kind == 0:
            h, hg, hssq = _fox_layer(h, hn, ssq, fox_w, j, fox_tail[j * LANES:(j + 1) * LANES],
                                     fox_b_f[j], fox_wo, norm_ple[i], bsz, s)
        elif kind == 1:
            h, hg, hssq = _gdn_layer(h, hn, ssq, gdn_w, j, gdn_tail[j * LANES:(j + 1) * LANES],
                                     gdn_conv[j], gdn_a_log[j], gdn_dt_bias[j], gdn_norm[j], gdn_wo,
                                     norm_ple[i], bsz, s)
        else:
            h, hg, hssq = _ssm_layer(h, hn, ssq, ssm_w, j, ssm_lam_re[j], ssm_lam_im[j], ssm_b_re[j],
                                     ssm_b_im[j], ssm_c_re[j], ssm_c_im[j], ssm_log_step[j], ssm_d[j],
                                     ssm_wg, ssm_b_glu[j], ssm_wo, norm_ple[i], bsz, s)
        extras = [(h, "tile", {}), (p_rows, "rows", {"row_blk": i * (m // min(MM_TM, m))}),
                  (ple_wp, "cols", {"row_blk": i, "rows": pdim})]
        outs = _matmul(hg, ple_wg, w_row_blk=i, out_dtype=F32, mode="ple", row_ssq=hssq, extras=extras,
                       norm_gain=norm_mix[i + 1] if i + 1 < depth else None, w_outer=True)
        h, hn, ssq = outs if i + 1 < depth else (outs, None, None)
    return _rmsnorm(h, final_norm, F32).reshape(bsz, s, d)
```

```python
import functools
import math

import jax
import jax.numpy as jnp
from jax import lax
from jax.experimental import pallas as pl
from jax.experimental.pallas import tpu as pltpu

F32 = jnp.float32
BF16 = jnp.bfloat16

LANES = 128
SUBLANES = 8
VMEM_LIMIT_BYTES = 56 * 1024 * 1024

CAST_BLOCK_ELEMS = 1 << 20
A_PREP_ROWS = 32
MM_TM = 512
MM_TM_WIDE = 1024
MM_TN = 1024
NORM_EPS = 1e-6
LOG2E = 1.4426950408889634
HEAD_DIM = 128
FOX_BLOCK = 1024
FOX_PAIRS_PER_TRIP = 4
GDN_CHUNK = 64
GDN_CONV = 4
GDN_HEADS_PER_STEP = 16
GDN_TIME_BLOCK = 256
SSM_GROUP = 16
SSM_STATE = 64
SSM_PACK = LANES // SSM_GROUP
SSM_L = 16
NEG_BIG = -1e30

HIGHEST = lax.Precision.HIGHEST


def _cparams(*sem):
    return pltpu.CompilerParams(dimension_semantics=sem, vmem_limit_bytes=VMEM_LIMIT_BYTES)


def _sigmoid(x):
    return 0.5 * jnp.tanh(0.5 * x) + 0.5


def _silu(x):
    hx = 0.5 * x
    return hx + hx * jnp.tanh(hx)


def _softplus(x):
    return jnp.maximum(x, 0.0) + jnp.log(1.0 + jnp.exp(-jnp.abs(x)))


def _rmsnorm_kernel(x_ref, g_ref, o_ref):
    x = x_ref[...]
    ms = jnp.mean(x * x, axis=-1, keepdims=True)
    o_ref[...] = (x * lax.rsqrt(ms + NORM_EPS) * g_ref[...]).astype(o_ref.dtype)


def _rmsnorm(x, g, out_dtype):
    m, d = x.shape
    tm = min(512, m)
    return pl.pallas_call(
        _rmsnorm_kernel,
        grid=(m // tm,),
        in_specs=[pl.BlockSpec((tm, d), lambda i: (i, 0)),
                  pl.BlockSpec((1, d), lambda i: (0, 0))],
        out_specs=pl.BlockSpec((tm, d), lambda i: (i, 0)),
        out_shape=jax.ShapeDtypeStruct((m, d), out_dtype),
        compiler_params=_cparams("parallel"),
        name="rmsnorm",
    )(x, g.reshape(1, d).astype(F32))


def _cast_kernel(x_ref, o_ref):
    o_ref[...] = x_ref[...].astype(o_ref.dtype)


def _cast_bf16(x, rows=None):
    nl, r, c = x.shape
    rows = r if rows is None else rows
    tc = min(2048, c)
    tr = min(rows, CAST_BLOCK_ELEMS // tc)
    assert rows % tr == 0 and c % tc == 0
    out = pl.pallas_call(
        _cast_kernel,
        grid=(nl, rows // tr, c // tc),
        in_specs=[pl.BlockSpec((1, tr, tc), lambda l, i, j: (l, i, j))],
        out_specs=pl.BlockSpec((1, tr, tc), lambda l, i, j: (l, i, j)),
        out_shape=jax.ShapeDtypeStruct((nl, rows, c), BF16),
        compiler_params=_cparams("parallel", "parallel", "parallel"),
        name="cast_bf16",
    )(x)
    return out.reshape(nl * rows, c)


_N_EXTRA = {"plain": 0, "colscale": 1, "residual": 1, "ple": 3, "glu": 2}


def _mm_kernel(*refs, mode, cast_a, w_t, has_ssq, emit_norm, inv_k):
    it = iter(refs)
    a_ref, w_ref = next(it), next(it)
    ssq_ref = next(it) if has_ssq else None
    extra = [next(it) for _ in range(_N_EXTRA[mode])]
    gain_ref = next(it) if emit_norm else None
    o_ref = next(it)
    hg_ref, ssq_out_ref = (next(it), next(it)) if emit_norm else (None, None)
    if cast_a:
        a_scr = next(it)

        @pl.when(pl.program_id(1) == 0)
        def _():
            def prep(r, carry):
                rows = pl.ds(pl.multiple_of(r * A_PREP_ROWS, A_PREP_ROWS), A_PREP_ROWS)
                a_scr[rows, :] = a_ref[rows, :].astype(BF16)
                return carry

            lax.fori_loop(0, a_ref.shape[0] // A_PREP_ROWS, prep, 0)

        a = a_scr[...]
    else:
        a = a_ref[...]

    acc = lax.dot_general(a, w_ref[...], (((1,), (1 if w_t else 0,)), ((), ())),
                          preferred_element_type=F32)
    if has_ssq:
        parts = ssq_ref[...]
        tot = parts[:, 0:LANES]
        for t in range(1, parts.shape[1] // LANES):
            tot = tot + parts[:, t * LANES:(t + 1) * LANES]
        acc = acc * lax.rsqrt(tot[:, 0:1] * inv_k + NORM_EPS)
    if mode == "plain":
        out = acc
    elif mode == "colscale":
        out = acc * extra[0][...]
    elif mode == "residual":
        out = extra[0][...] + acc
    elif mode == "ple":
        res_ref, p_ref, wp_ref = extra
        emb = jnp.dot(p_ref[...].astype(BF16), wp_ref[...], preferred_element_type=F32)
        out = res_ref[...] + _sigmoid(acc) * emb
    elif mode == "glu":
        z_ref, b_ref = extra
        tn = o_ref.shape[1]
        y = a_ref[:, pl.ds(pl.multiple_of(pl.program_id(1) * tn, tn), tn)]
        out = y * _sigmoid(acc + b_ref[...]) * _silu(z_ref[...])
    else:
        raise ValueError(mode)
    o_ref[...] = out.astype(o_ref.dtype)
    if emit_norm:
        hg_ref[...] = (out * gain_ref[...]).astype(hg_ref.dtype)
        ssq_out_ref[...] = jnp.broadcast_to(jnp.sum(out * out, axis=-1, keepdims=True),
                                            ssq_out_ref.shape)


def _extra_spec(arr, kind, tm, tn, row_blk=0, col_off=0, rows=None):
    if kind == "tile":
        return pl.BlockSpec((tm, tn), lambda i, j: (i + row_blk, j + col_off // tn))
    if kind == "row":
        return pl.BlockSpec((1, tn), lambda i, j: (0, j))
    if kind == "rows":
        return pl.BlockSpec((tm, arr.shape[1]), lambda i, j: (i + row_blk, 0))
    if kind == "cols":
        return pl.BlockSpec((rows, tn), lambda i, j: (row_blk, j))
    raise ValueError(kind)


def _matmul(a, w, *, out_dtype, n=None, mode="plain", extras=(), cast_a=False, row_ssq=None,
            norm_gain=None, w_t=None, w_row_blk=0, w_col_off=0, w_outer=False, tm=MM_TM, tn=MM_TN):
    m, kdim = a.shape
    n = (w_t if w_t else w.shape[1]) if n is None else n
    tm = min(tm, m)
    tn = functools.reduce(math.gcd, [tn, n, w_col_off, w_t or 0])
    assert m % tm == 0 and tn % LANES == 0 and len(extras) == _N_EXTRA[mode]
    operands = [a, w]
    if w_t:
        first = (w_row_blk * w_t + w_col_off) // tn
        w_spec = pl.BlockSpec((tn, kdim), lambda i, j: (first + j, 0))
    else:
        w_spec = pl.BlockSpec((kdim, tn), lambda i, j: (w_row_blk, j + w_col_off // tn))
    in_specs = [pl.BlockSpec((tm, kdim), lambda i, j: (i, 0)), w_spec]
    if row_ssq is not None:
        operands.append(row_ssq)
        in_specs.append(pl.BlockSpec((tm, row_ssq.shape[1]), lambda i, j: (i, 0)))
    for arr, kind, opts in extras:
        assert opts.get("col_off", 0) % tn == 0
        operands.append(arr)
        in_specs.append(_extra_spec(arr, kind, tm, tn, **opts))
    tile = pl.BlockSpec((tm, tn), lambda i, j: (i, j))
    out_specs, out_shape = tile, jax.ShapeDtypeStruct((m, n), out_dtype)
    if norm_gain is not None:
        operands.append(norm_gain.astype(F32).reshape(1, n))
        in_specs.append(pl.BlockSpec((1, tn), lambda i, j: (0, j)))
        out_specs = [tile, tile, pl.BlockSpec((tm, LANES), lambda i, j: (i, j))]
        out_shape = [out_shape, jax.ShapeDtypeStruct((m, n), BF16),
                     jax.ShapeDtypeStruct((m, n // tn * LANES), F32)]
    scratch = [pltpu.VMEM((tm, kdim), BF16)] if cast_a else []
    grid = (m // tm, n // tn)
    if w_outer:
        assert not cast_a
        flip = lambda sp: pl.BlockSpec(sp.block_shape, lambda j, i, f=sp.index_map: f(i, j))
        in_specs = [flip(sp) for sp in in_specs]
        out_specs = [flip(sp) for sp in out_specs] if norm_gain is not None else flip(out_specs)
        grid = grid[::-1]
    return pl.pallas_call(
        functools.partial(_mm_kernel, mode=mode, cast_a=cast_a, w_t=bool(w_t),
                          has_ssq=row_ssq is not None,
                          emit_norm=norm_gain is not None, inv_k=1.0 / kdim),
        grid=grid,
        in_specs=in_specs,
        out_specs=out_specs,
        out_shape=out_shape,
        scratch_shapes=scratch,
        compiler_params=_cparams("parallel", "arbitrary"),
        name="matmul_" + mode,
    )(*operands)


def _fox_cum_kernel(f_ref, b_ref, o_ref, carry_ref, *, ts):
    @pl.when(pl.program_id(1) == 0)
    def _():
        carry_ref[...] = jnp.zeros_like(carry_ref)

    x = f_ref[...] + b_ref[...]
    log_f = jnp.minimum(x, 0.0) - jnp.log(1.0 + jnp.exp(-jnp.abs(x)))
    r = lax.broadcasted_iota(jnp.int32, (ts, ts), 0)
    c = lax.broadcasted_iota(jnp.int32, (ts, ts), 1)
    lower = (c <= r).astype(F32)
    cum = jnp.dot(lower, log_f, precision=HIGHEST, preferred_element_type=F32) + carry_ref[...]
    carry_ref[...] = cum[ts - 1:ts, :]
    o_ref[0] = cum.T


def _fox_cum(f_logit, b_f, bsz, s):
    ts = min(1024, s)
    return pl.pallas_call(
        functools.partial(_fox_cum_kernel, ts=ts),
        grid=(bsz, s // ts),
        in_specs=[pl.BlockSpec((ts, LANES), lambda b, t: (b * (s // ts) + t, 0)),
                  pl.BlockSpec((1, LANES), lambda b, t: (0, 0))],
        out_specs=pl.BlockSpec((1, LANES, ts), lambda b, t: (b, 0, t)),
        out_shape=jax.ShapeDtypeStruct((bsz, LANES, s), F32),
        scratch_shapes=[pltpu.VMEM((1, LANES), F32)],
        compiler_params=_cparams("parallel", "arbitrary"),
        name="fox_cum",
    )(f_logit, b_f)


def _fox_attn_kernel(q_ref, k_ref, v_ref, c_ref, z_ref, o_ref, sa_ref, sb_ref, *, tq):
    tk = tq // 2
    i = pl.program_id(2)
    q = q_ref[...]
    c0 = c_ref[0, 0, :, pl.ds(pl.multiple_of(i * tq, tq), LANES)][:, 0:1]

    def scores(qrows, j):
        start = j * tk if isinstance(j, int) else pl.multiple_of(j * tk, tk)
        s = lax.dot_general(qrows, k_ref[pl.ds(start, tk), :], (((1,), (1,)), ((), ())),
                            preferred_element_type=F32)
        return s + (c0 - c_ref[0, 0, :, pl.ds(start, tk)]) * LOG2E

    def update(s, j, m, acc, width=1):
        rows = width * tk
        start = j * tk if isinstance(j, int) else pl.multiple_of(j * tk, tk)
        vb = jnp.concatenate([v_ref[pl.ds(start, rows), :], jnp.ones((rows, HEAD_DIM), BF16)], axis=1)
        m_new = jnp.maximum(m, jnp.max(s, axis=-1, keepdims=True))
        p = jnp.exp2(s - m_new)
        acc = jnp.exp2(m - m_new) * acc + jnp.dot(p.astype(BF16), vb, preferred_element_type=F32)
        return m_new, acc

    sa_ref[...] = scores(q, 0)

    def pair(t, carry):
        m, acc = carry
        sb_ref[...] = scores(q, 2 * t + 1)
        m, acc = update(sa_ref[...], 2 * t, m, acc)
        sa_ref[...] = scores(q, 2 * t + 2)
        return update(sb_ref[...], 2 * t + 1, m, acc)

    def trip(t, carry):
        for n in range(FOX_PAIRS_PER_TRIP):
            carry = pair(FOX_PAIRS_PER_TRIP * t + n, carry)
        return carry

    init = (jnp.full((tq, 1), NEG_BIG, F32), jnp.zeros((tq, 2 * HEAD_DIM), F32))
    carry = lax.fori_loop(0, i // FOX_PAIRS_PER_TRIP, trip, init)

    def finish(a, rows):
        o = a[:, 0:HEAD_DIM] / a[:, HEAD_DIM:HEAD_DIM + 1]
        o_ref[rows, :] = (o * _silu(z_ref[rows, :])).astype(o_ref.dtype)

    def tail(carry, left):
        for n in range(left):
            carry = pair(i - left + n, carry)
        m, acc = carry
        r = lax.broadcasted_iota(jnp.int32, (tk, tk), 0)
        c = lax.broadcasted_iota(jnp.int32, (tk, tk), 1)
        s_bot = jnp.concatenate(
            [sa_ref[tk:tq, :], jnp.where(c <= r, scores(q[tk:tq], 2 * i + 1), NEG_BIG)], axis=1)
        _, acc_top = update(jnp.where(c <= r, sa_ref[0:tk, :], NEG_BIG), 2 * i, m[0:tk], acc[0:tk])
        _, acc_bot = update(s_bot, 2 * i, m[tk:tq], acc[tk:tq], width=2)
        finish(acc_top, slice(0, tk))
        finish(acc_bot, slice(tk, tq))

    def dispatch(carry, lo, hi):
        if hi - lo == 1:
            tail(carry, lo)
        else:
            mid = (lo + hi) // 2
            lax.cond(i % FOX_PAIRS_PER_TRIP < mid, lambda cr: dispatch(cr, lo, mid),
                     lambda cr: dispatch(cr, mid, hi), carry)

    dispatch(carry, 0, FOX_PAIRS_PER_TRIP)


def _fox_attention(qkv, cum4, z, bsz, s, heads):
    tq = min(FOX_BLOCK, s)
    nq = s // tq
    return pl.pallas_call(
        functools.partial(_fox_attn_kernel, tq=tq),
        grid=(bsz, heads, nq),
        in_specs=[pl.BlockSpec((tq, HEAD_DIM), lambda b, h, i: (b * nq + i, h)),
                  pl.BlockSpec((s, HEAD_DIM), lambda b, h, i: (b, heads + h)),
                  pl.BlockSpec((s, HEAD_DIM), lambda b, h, i: (b, 2 * heads + h)),
                  pl.BlockSpec((1, 1, 1, s), lambda b, h, i: (b, h, 0, 0)),
                  pl.BlockSpec((tq, HEAD_DIM), lambda b, h, i: (b * nq + i, h))],
        out_specs=pl.BlockSpec((tq, HEAD_DIM), lambda b, h, i: (b * nq + i, h)),
        out_shape=jax.ShapeDtypeStruct((bsz * s, heads * HEAD_DIM), BF16),
        scratch_shapes=[pltpu.VMEM((tq, tq // 2), F32)] * 2,
        compiler_params=_cparams("parallel", "parallel", "arbitrary"),
        name="fox_attn",
    )(qkv, qkv, qkv, cum4, z)


def _fox_layer(h, hn, ssq, w_all, layer, w_tail, b_f, wo_all, ple_gain, bsz, s):
    width = wo_all.shape[1]
    heads = width // HEAD_DIM
    q_scale = jnp.concatenate([jnp.full((width,), HEAD_DIM ** -0.5 * LOG2E, F32),
                               jnp.ones((2 * width,), F32)]).reshape(1, 3 * width)
    qkv = _matmul(hn, w_all, n=3 * width, w_t=4 * width, w_row_blk=layer, out_dtype=BF16,
                  mode="colscale", extras=[(q_scale, "row", {})], row_ssq=ssq, tm=MM_TM_WIDE)
    z = _matmul(hn, w_all, n=width, w_t=4 * width, w_row_blk=layer, w_col_off=3 * width,
                out_dtype=F32, row_ssq=ssq, w_outer=True, tm=MM_TM_WIDE)
    f_logit = _matmul(hn, w_tail, w_t=LANES, out_dtype=F32, row_ssq=ssq, tm=MM_TM_WIDE)
    b_row = jnp.pad(b_f.astype(F32), (0, LANES - heads)).reshape(1, LANES)
    cum = _fox_cum(f_logit, b_row, bsz, s)
    gated = _fox_attention(qkv, cum.reshape(bsz, LANES, 1, s), z, bsz, s, heads)
    return _matmul(gated, wo_all, w_row_blk=layer, out_dtype=F32, mode="residual",
                   extras=[(h, "tile", {})], norm_gain=ple_gain, w_outer=True)


def _bdot(a, b):
    return jnp.einsum("bmk,bkn->bmn", a, b, preferred_element_type=F32)


def _split_bf16(x):
    hi = x.astype(BF16)
    return hi, (x - hi.astype(F32)).astype(BF16)


def _bdot3_parts(a_hi, a_lo, b_hi, b_lo):
    return _bdot(a_hi, b_hi) + _bdot(a_hi, b_lo) + _bdot(a_lo, b_hi)


def _bdot3(a, b):
    return _bdot3_parts(*_split_bf16(a), *_split_bf16(b))


def _bdot_exact_lhs(a_bf16, b):
    b_hi = b.astype(BF16)
    b_mid, b_lo = _split_bf16(b - b_hi.astype(F32))
    return _bdot(a_bf16, b_hi) + _bdot(a_bf16, b_mid) + _bdot(a_bf16, b_lo)


def _gdn_kernel(q_ref, k_ref, v_ref, z_ref, t_ref, cw_ref, gp_ref, nw_ref, o_ref,
                xq_ref, xk_ref, xv_ref, st_ref, u_ref, wq_ref, ik_ref, gl_ref, *, tb, heads, hps):
    hb = pl.program_id(1)
    c, dh = GDN_CHUNK, HEAD_DIM
    nb = tb // c
    nbat = hps * nb
    pad = SUBLANES

    @pl.when(pl.program_id(2) == 0)
    def _():
        for r in (xq_ref, xk_ref, xv_ref):
            r[0:pad, :] = jnp.zeros((pad, hps * dh), F32)
        st_ref[...] = jnp.zeros_like(st_ref)

    def conv(x_ref, xs_ref, col):
        x = x_ref[...]
        xs_ref[pad:pad + tb, :] = x
        xs = xs_ref[...]
        y = x * cw_ref[col, GDN_CONV - 1:GDN_CONV, :]
        for j in range(GDN_CONV - 1):
            shifted = pltpu.roll(xs, GDN_CONV - 1 - j, axis=0)[pad:pad + tb, :]
            y = y + shifted * cw_ref[col, j:j + 1, :]
        xs_ref[0:pad, :] = x[tb - pad:tb, :]
        return _silu(y)

    def by_chunk(x):
        w = x.shape[1] // hps
        return jnp.concatenate([x[:, g * w:(g + 1) * w].reshape(nb, c, w) for g in range(hps)], axis=0)

    def l2n(x):
        return x * lax.rsqrt(jnp.sum(x * x, axis=-1, keepdims=True) + NORM_EPS)

    q = l2n(by_chunk(conv(q_ref, xq_ref, 0))) * (dh ** -0.5)
    k = l2n(by_chunk(conv(k_ref, xk_ref, 1)))
    v = by_chunk(conv(v_ref, xv_ref, 2))

    t = t_ref[...]
    lane = lax.broadcasted_iota(jnp.int32, (tb, LANES), 1)
    beta_all = _sigmoid(t)
    g_all = -jnp.exp(gp_ref[0:1, :]) * _softplus(t + gp_ref[1:2, :])

    def pick(x, first):
        cols = [jnp.sum(jnp.where(lane == first + hb * hps + g, x, 0.0), axis=-1, keepdims=True)
                for g in range(hps)]
        return jnp.concatenate([col.reshape(nb, c, 1) for col in cols], axis=0)

    beta = pick(beta_all, 0)
    g = pick(g_all, heads)

    ri = lax.broadcasted_iota(jnp.int32, (c, c), 0)
    ci = lax.broadcasted_iota(jnp.int32, (c, c), 1)
    incl = ci <= ri
    strict = ci < ri
    eye = (ci == ri).astype(F32)
    tri = jnp.broadcast_to(incl.astype(BF16), (nbat, c, c))
    rl = lax.broadcasted_iota(jnp.int32, (c, LANES), 0)
    cl = lax.broadcasted_iota(jnp.int32, (c, LANES), 1)
    seg_mask = (rl > cl) | (cl == c)

    seg = _bdot_exact_lhs(tri, jnp.where(seg_mask, g, 0.0))
    gc = seg[:, :, c:c + 1]
    g_last = seg[:, c - 1:c, c:c + 1]
    decay = jnp.exp(jnp.where(incl, seg[:, :, 0:c], NEG_BIG))
    egc = jnp.exp(gc)
    k_beta = k * beta
    kq = jnp.einsum("bmd,bnd->bmn", jnp.concatenate([k_beta, q], axis=1).astype(BF16),
                    k.astype(BF16), preferred_element_type=F32)
    a = jnp.where(strict, kq[:, 0:c] * decay, 0.0)
    intra = kq[:, c:2 * c] * decay
    npair = nbat // 2
    pair_lanes = lambda x: jnp.concatenate(
        [x.reshape(npair, 2, c, x.shape[2])[:, 0], x.reshape(npair, 2, c, x.shape[2])[:, 1]], axis=2)
    left = lax.broadcasted_iota(jnp.int32, (c, 2 * c), 1) < c

    def block_diag(x2):
        zero = jnp.zeros_like(x2)
        return jnp.concatenate([jnp.where(left, x2, zero), jnp.where(left, zero, x2)], axis=1)

    a2 = pair_lanes(a)
    inv2 = jnp.concatenate([eye, eye], axis=1) - a2
    a_hi, a_lo = _split_bf16(a2)
    pw2 = _bdot3_parts(a_hi, a_lo, block_diag(a_hi), block_diag(a_lo))
    levels = int(math.log2(c)) - 1
    for lvl in range(levels):
        inv_hi, inv_lo = _split_bf16(inv2)
        pw_hi, pw_lo = _split_bf16(pw2)
        pd_hi, pd_lo = block_diag(pw_hi), block_diag(pw_lo)
        if lvl < levels - 1:
            both = _bdot3_parts(jnp.concatenate([inv_hi, pw_hi], axis=1),
                                jnp.concatenate([inv_lo, pw_lo], axis=1), pd_hi, pd_lo)
            inv2 = inv2 + both[:, 0:c]
            pw2 = both[:, c:2 * c]
        else:
            inv2 = inv2 + _bdot3_parts(inv_hi, inv_lo, pd_hi, pd_lo)
    rhs = jnp.concatenate([v * beta, k_beta * egc], axis=2).reshape(npair, 2, c, 2 * dh)
    zero = jnp.zeros((npair, c, 2 * dh), F32)
    rhs_bd = jnp.concatenate([jnp.concatenate([rhs[:, 0], zero], axis=2),
                              jnp.concatenate([zero, rhs[:, 1]], axis=2)], axis=1)
    uw2 = _bdot3(inv2, rhs_bd)
    uw = jnp.stack([uw2[:, :, 0:2 * dh], uw2[:, :, 2 * dh:4 * dh]], axis=1).reshape(nbat, c, 2 * dh)
    u_ref[...] = uw[:, :, 0:dh].reshape(nbat * c, dh)
    wq_ref[...] = jnp.concatenate([uw[:, :, dh:2 * dh], q * egc], axis=1).astype(BF16).reshape(
        nbat * 2 * c, dh)
    k_dec = k * jnp.exp(g_last - gc)
    for b in range(nbat):
        ik_ref[b * (c + dh):(b + 1) * (c + dh), :] = jnp.concatenate(
            [intra[b], k_dec[b].T], axis=0).astype(BF16)
    gl_ref[...] = jnp.broadcast_to(jnp.exp(g_last), (nbat, SUBLANES, dh)).reshape(nbat * SUBLANES, dh)

    def step(n, states):
        rows = lambda g, size: pl.ds(pl.multiple_of((g * nb + n) * size, size), size)
        ws = [jnp.dot(wq_ref[rows(g, 2 * c), :], states[g].astype(BF16), preferred_element_type=F32)
              for g in range(hps)]
        v_new = [u_ref[rows(g, c), :] - ws[g][0:c] for g in range(hps)]
        iv = [jnp.dot(ik_ref[pl.ds(pl.multiple_of((g * nb + n) * (c + dh), c), c + dh), :],
                      v_new[g].astype(BF16), preferred_element_type=F32) for g in range(hps)]
        new_states = []
        r0 = pl.multiple_of(n * c, c)
        for g in range(hps):
            out = ws[g][c:2 * c] + iv[g][0:c]
            gl = gl_ref[pl.ds(pl.multiple_of((g * nb + n) * SUBLANES, SUBLANES), 1), :]
            new_states.append(states[g] * gl + iv[g][c:c + dh])
            ms = jnp.mean(out * out, axis=-1, keepdims=True)
            o = out * lax.rsqrt(ms + NORM_EPS) * nw_ref[...]
            o_ref[pl.ds(r0, c), g * dh:(g + 1) * dh] = (
                o * _silu(z_ref[pl.ds(r0, c), g * dh:(g + 1) * dh])).astype(o_ref.dtype)
        return tuple(new_states)

    states = lax.fori_loop(0, nb, step, tuple(st_ref[g] for g in range(hps)))
    for g in range(hps):
        st_ref[g] = states[g]


def _gdn_mixer(qkvz, tail, conv_w, gate_p, norm_w, bsz, s, heads):
    hps = math.gcd(GDN_HEADS_PER_STEP, heads)
    tb = min(GDN_TIME_BLOCK, s)
    nt = s // tb
    hblocks = heads // hps
    width = hps * HEAD_DIM
    nbat = hps * (tb // GDN_CHUNK)
    col = lambda off: pl.BlockSpec((tb, width), lambda b, h, t: (b * nt + t, off + h))
    return pl.pallas_call(
        functools.partial(_gdn_kernel, tb=tb, heads=heads, hps=hps),
        grid=(bsz, hblocks, nt),
        in_specs=[col(0), col(hblocks), col(2 * hblocks), col(3 * hblocks),
                  pl.BlockSpec((tb, LANES), lambda b, h, t: (b * nt + t, 0)),
                  pl.BlockSpec((3, GDN_CONV, width), lambda b, h, t: (0, 0, h)),
                  pl.BlockSpec((2, LANES), lambda b, h, t: (0, 0)),
                  pl.BlockSpec((1, HEAD_DIM), lambda b, h, t: (0, 0))],
        out_specs=pl.BlockSpec((tb, width), lambda b, h, t: (b * nt + t, h)),
        out_shape=jax.ShapeDtypeStruct((bsz * s, heads * HEAD_DIM), BF16),
        scratch_shapes=[pltpu.VMEM((tb + SUBLANES, width), F32)] * 3
                       + [pltpu.VMEM((hps, HEAD_DIM, HEAD_DIM), F32),
                          pltpu.VMEM((nbat * GDN_CHUNK, HEAD_DIM), F32),
                          pltpu.VMEM((nbat * 2 * GDN_CHUNK, HEAD_DIM), BF16),
                          pltpu.VMEM((nbat * (GDN_CHUNK + HEAD_DIM), GDN_CHUNK), BF16),
                          pltpu.VMEM((nbat * SUBLANES, HEAD_DIM), F32)],
        compiler_params=_cparams("parallel", "parallel", "arbitrary"),
        name="gdn_mixer",
    )(qkvz, qkvz, qkvz, qkvz, tail, conv_w, gate_p, norm_w)


def _gdn_layer(h, hn, ssq, w_all, layer, w_tail, conv_w, a_log, dt_bias, norm_w, wo_all, ple_gain,
               bsz, s):
    width = wo_all.shape[1]
    heads = width // HEAD_DIM
    qkvz = _matmul(hn, w_all, w_t=4 * width, w_row_blk=layer, out_dtype=F32, row_ssq=ssq,
                   tm=MM_TM_WIDE)
    tail = _matmul(hn, w_tail, w_t=LANES, out_dtype=F32, row_ssq=ssq, tm=MM_TM_WIDE)
    cw = conv_w.astype(F32).reshape(GDN_CONV, 3, width).transpose(1, 0, 2)
    lane_pad = lambda x: jnp.pad(x.astype(F32), (heads, LANES - 2 * heads))
    gate_p = jnp.stack([lane_pad(a_log), lane_pad(dt_bias)])
    gated = _gdn_mixer(qkvz, tail, cw, gate_p, norm_w.astype(F32).reshape(1, HEAD_DIM),
                       bsz, s, heads)
    return _matmul(gated, wo_all, w_row_blk=layer, out_dtype=F32, mode="residual",
                   extras=[(h, "tile", {})], norm_gain=ple_gain, w_outer=True)


def _ssm_prep_kernel(lam_ref, bf_ref, cf_ref, bd_ref, wz_ref, wyt_ref, laml_ref):
    ns = SSM_PACK * SSM_STATE
    lam_re, lam_im, step_log = lam_ref[0, 0:1, :], lam_ref[0, 1:2, :], lam_ref[0, 2:3, :]
    step = jnp.exp(step_log)
    mag = jnp.exp(lam_re * step)
    lb_re, lb_im = mag * jnp.cos(lam_im * step), mag * jnp.sin(lam_im * step)
    den = lam_re * lam_re + lam_im * lam_im
    num_re = lb_re - 1.0
    zoh_re = (num_re * lam_re + lb_im * lam_im) / den
    zoh_im = (lb_im * lam_re - num_re * lam_im) / den
    b_re, b_im = bf_ref[0, 0], bf_ref[0, 1]
    bb_re = zoh_re * b_re - zoh_im * b_im
    bb_im = zoh_re * b_im + zoh_im * b_re
    c_re, c_im = cf_ref[0, 0], cf_ref[0, 1]
    cc_hi, cc_lo = _split_bf16(jnp.concatenate([c_re, -c_im], axis=1))
    nt_dot = lambda x, y: lax.dot_general(x, y, (((1,), (1,)), ((), ())), preferred_element_type=F32)

    bd_ref[0, 0, LANES:2 * LANES, 0:LANES] = jnp.zeros((LANES, LANES), bd_ref.dtype)
    pw_re, pw_im = jnp.ones_like(lb_re), jnp.zeros_like(lb_re)
    for d in range(SSM_L + 1):
        if d < SSM_L:
            a_re = pw_re * bb_re - pw_im * bb_im
            a_im = pw_re * bb_im + pw_im * bb_re
            a_d = jnp.concatenate([a_re, a_im], axis=1)
            a_hi, a_lo = _split_bf16(a_d)
            blk = (nt_dot(a_hi, cc_hi) + nt_dot(a_hi, cc_lo) + nt_dot(a_lo, cc_hi)).astype(bd_ref.dtype)
            d2 = d // 2
            if d % 2 == 0:
                bd_ref[0, d2, 0:LANES, 0:LANES] = blk
                bd_ref[0, d2, LANES:2 * LANES, LANES:2 * LANES] = blk
            else:
                bd_ref[0, d2, 0:LANES, LANES:2 * LANES] = blk
                if d2 + 1 < SSM_L // 2:
                    bd_ref[0, d2 + 1, LANES:2 * LANES, 0:LANES] = blk
            tin = SSM_L - 1 - d
            wz_ref[0, tin * LANES:(tin + 1) * LANES, :] = a_d.astype(wz_ref.dtype)
        if d >= 1:
            y_re = pw_re * c_re - pw_im * c_im
            y_im = pw_re * c_im + pw_im * c_re
            wyt_ref[0, (d - 1) * LANES:d * LANES, :] = jnp.concatenate(
                [y_re, -y_im], axis=1).astype(wyt_ref.dtype)
        if d == SSM_L:
            laml_ref[0] = jnp.concatenate([pw_re, pw_im], axis=1)
        pw_re, pw_im = pw_re * lb_re - pw_im * lb_im, pw_re * lb_im + pw_im * lb_re


def _ssm_prep(lam, bfull, cfull):
    nsg = lam.shape[0]
    ns = SSM_PACK * SSM_STATE
    rows = SSM_L * LANES
    return pl.pallas_call(
        _ssm_prep_kernel,
        grid=(nsg,),
        in_specs=[pl.BlockSpec((1, 3, ns), lambda g: (g, 0, 0)),
                  pl.BlockSpec((1, 2, LANES, ns), lambda g: (g, 0, 0, 0)),
                  pl.BlockSpec((1, 2, LANES, ns), lambda g: (g, 0, 0, 0))],
        out_specs=[pl.BlockSpec((1, SSM_L // 2, 2 * LANES, 2 * LANES), lambda g: (g, 0, 0, 0)),
                   pl.BlockSpec((1, rows, 2 * ns), lambda g: (g, 0, 0)),
                   pl.BlockSpec((1, rows, 2 * ns), lambda g: (g, 0, 0)),
                   pl.BlockSpec((1, 1, 2 * ns), lambda g: (g, 0, 0))],
        out_shape=[jax.ShapeDtypeStruct((nsg, SSM_L // 2, 2 * LANES, 2 * LANES), BF16),
                   jax.ShapeDtypeStruct((nsg, rows, 2 * ns), BF16),
                   jax.ShapeDtypeStruct((nsg, rows, 2 * ns), BF16),
                   jax.ShapeDtypeStruct((nsg, 1, 2 * ns), F32)],
        compiler_params=_cparams("parallel"),
        name="ssm_prep",
    )(lam, bfull, cfull)


def _ssm_scan_kernel(u_ref, bd_ref, wz_ref, wyt_ref, laml_ref, d_ref, o_ref,
                     x_ref, z_ref, xp_ref, st_ref, *, tc):
    ns = SSM_PACK * SSM_STATE

    @pl.when(pl.program_id(2) == 0)
    def _():
        st_ref[...] = jnp.zeros_like(st_ref)

    for tau in range(SSM_L):
        x_ref[:, tau * LANES:(tau + 1) * LANES] = u_ref[pl.ds(tau, tc, stride=SSM_L), :].astype(BF16)

    z_ref[...] = jnp.dot(x_ref[...], wz_ref[0], preferred_element_type=F32)

    a_re, a_im = laml_ref[0, :, 0:ns], laml_ref[0, :, ns:2 * ns]

    def row(r, carry):
        s_re, s_im = carry
        xp_ref[pl.ds(r, 1), :] = jnp.concatenate([s_re, s_im], axis=1)
        zr = z_ref[pl.ds(r, 1), :]
        n_re = a_re * s_re - a_im * s_im + zr[:, 0:ns]
        n_im = a_re * s_im + a_im * s_re + zr[:, ns:2 * ns]
        return n_re, n_im

    s_re, s_im = lax.fori_loop(0, tc, row, (st_ref[:, 0:ns], st_ref[:, ns:2 * ns]), unroll=4)
    st_ref[...] = jnp.concatenate([s_re, s_im], axis=1)

    xprev = xp_ref[...].astype(BF16)
    pair = 2 * LANES
    for t2 in range(SSM_L // 2):
        y2 = lax.dot_general(xprev, wyt_ref[0, t2 * pair:(t2 + 1) * pair, :],
                             (((1,), (1,)), ((), ())), preferred_element_type=F32)
        for tin in range(t2 + 1):
            y2 = y2 + jnp.dot(x_ref[:, tin * pair:(tin + 1) * pair], bd_ref[0, t2 - tin],
                              preferred_element_type=F32)
        for half in range(2):
            tau = 2 * t2 + half
            y = y2[:, half * LANES:(half + 1) * LANES]
            y = y + d_ref[...] * u_ref[pl.ds(tau, tc, stride=SSM_L), :]
            y = 0.5 * y * (1.0 + jnp.tanh(math.sqrt(2.0 / math.pi) * (y + 0.044715 * (y * y * y))))
            o_ref[pl.ds(tau, tc, stride=SSM_L), :] = y


def _ssm_scan(uz, bd, wz, wyt, laml, d_skip, bsz, s):
    e = d_skip.shape[1]
    nsg = e // LANES
    ns = SSM_PACK * SSM_STATE
    tb = min(8192, s)
    nt = s // tb
    tc = tb // SSM_L
    rows = SSM_L * LANES
    return pl.pallas_call(
        functools.partial(_ssm_scan_kernel, tc=tc),
        grid=(nsg, bsz, nt),
        in_specs=[pl.BlockSpec((tb, LANES), lambda g, b, t: (b * nt + t, g)),
                  pl.BlockSpec((1, SSM_L // 2, 2 * LANES, 2 * LANES), lambda g, b, t: (g, 0, 0, 0)),
                  pl.BlockSpec((1, rows, 2 * ns), lambda g, b, t: (g, 0, 0)),
                  pl.BlockSpec((1, rows, 2 * ns), lambda g, b, t: (g, 0, 0)),
                  pl.BlockSpec((1, 1, 2 * ns), lambda g, b, t: (g, 0, 0)),
                  pl.BlockSpec((1, LANES), lambda g, b, t: (0, g))],
        out_specs=pl.BlockSpec((tb, LANES), lambda g, b, t: (b * nt + t, g)),
        out_shape=jax.ShapeDtypeStruct((uz.shape[0], e), F32),
        scratch_shapes=[pltpu.VMEM((tc, rows), BF16), pltpu.VMEM((tc, 2 * ns), F32),
                        pltpu.VMEM((tc, 2 * ns), F32), pltpu.VMEM((1, 2 * ns), F32)],
        compiler_params=_cparams("parallel", "parallel", "arbitrary"),
        name="ssm_scan",
    )(uz, bd, wz, wyt, laml, d_skip)


def _ssm_layer(h, hn, ssq, w_all, layer, lam_re, lam_im, b_re, b_im, c_re, c_im, log_step, d_skip,
               wg_all, b_glu, wo_all, ple_gain, bsz, s):
    e = wo_all.shape[1]
    groups, nstate = lam_re.shape
    assert nstate == SSM_STATE and e == groups * SSM_GROUP and groups % SSM_PACK == 0
    nsg = groups // SSM_PACK
    ns = SSM_PACK * SSM_STATE
    lam = jnp.stack([lam_re.astype(F32).reshape(nsg, ns), lam_im.astype(F32).reshape(nsg, ns),
                     jnp.repeat(log_step.astype(F32), SSM_STATE).reshape(nsg, ns)], axis=1)
    eye = jnp.eye(SSM_PACK, dtype=F32)

    def expand_b(b):
        return jnp.einsum("sgpm,gh->sgmhp", b.astype(F32).reshape(nsg, SSM_PACK, SSM_STATE, SSM_GROUP),
                          eye).reshape(nsg, LANES, ns)

    def expand_c(c):
        return jnp.einsum("sgnp,gh->sgnhp", c.astype(F32).reshape(nsg, SSM_PACK, SSM_GROUP, SSM_STATE),
                          eye).reshape(nsg, LANES, ns)

    bfull = jnp.stack([expand_b(b_re), expand_b(b_im)], axis=1)
    cfull = jnp.stack([expand_c(c_re), expand_c(c_im)], axis=1)
    bd, wz, wyt, laml = _ssm_prep(lam, bfull, cfull)

    uz = _matmul(hn, w_all, w_row_blk=layer, out_dtype=F32, row_ssq=ssq, tm=MM_TM_WIDE)
    y = _ssm_scan(uz, bd, wz, wyt, laml, d_skip.astype(F32).reshape(1, e), bsz, s)
    y2 = _matmul(y, wg_all, w_row_blk=layer, out_dtype=BF16, mode="glu", cast_a=True,
                 extras=[(uz, "tile", {"col_off": e}), (b_glu.astype(F32).reshape(1, e), "row", {})])
    return _matmul(y2, wo_all, w_row_blk=layer, out_dtype=F32, mode="residual",
                   extras=[(h, "tile", {})], norm_gain=ple_gain, w_outer=True)


def kernel(x, p, norm_mix, fox_w_in, fox_b_f, fox_w_out, gdn_w_in, gdn_conv, gdn_a_log, gdn_dt_bias, gdn_norm, gdn_w_out, ssm_w_in, ssm_lam_re, ssm_lam_im, ssm_b_re, ssm_b_im, ssm_c_re, ssm_c_im, ssm_log_step, ssm_d, ssm_w_glu, ssm_b_glu, ssm_w_out, norm_ple, ple_w_proj, ple_w_gate, final_norm):
    bsz, s, d = x.shape
    depth, m = p.shape[0], bsz * s
    pdim = p.shape[-1]
    fox_width, gdn_width = fox_w_out.shape[1], gdn_w_out.shape[1]
    fox_wt, gdn_wt = jnp.swapaxes(fox_w_in, 1, 2), jnp.swapaxes(gdn_w_in, 1, 2)
    pad_rows = lambda t: jnp.pad(t, ((0, 0), (0, LANES - t.shape[1]), (0, 0)))
    fox_w = _cast_bf16(fox_wt, 4 * fox_width)
    fox_tail = _cast_bf16(pad_rows(fox_wt[:, 4 * fox_width:]))
    gdn_w = _cast_bf16(gdn_wt, 4 * gdn_width)
    gdn_tail = _cast_bf16(pad_rows(gdn_wt[:, 4 * gdn_width:]))
    fox_wo, gdn_wo = _cast_bf16(fox_w_out), _cast_bf16(gdn_w_out)
    ssm_w, ssm_wg, ssm_wo = _cast_bf16(ssm_w_in), _cast_bf16(ssm_w_glu), _cast_bf16(ssm_w_out)
    ple_wg, ple_wp = _cast_bf16(ple_w_gate), _cast_bf16(ple_w_proj)
    p_rows = p.reshape(depth * m, pdim)

    h = x.reshape(m, d).astype(F32)
    hn, ssq = _rmsnorm(h, norm_mix[0], BF16), None
    for i in range(depth):
        kind, j = i % 3, i // 3
        if kind == 0:
            h, hg, hssq = _fox_layer(h, hn, ssq, fox_w, j, fox_tail[j * LANES:(j + 1) * LANES],
                                     fox_b_f[j], fox_wo, norm_ple[i], bsz, s)
        elif kind == 1:
            h, hg, hssq = _gdn_layer(h, hn, ssq, gdn_w, j, gdn_tail[j * LANES:(j + 1) * LANES],
                                     gdn_conv[j], gdn_a_log[j], gdn_dt_bias[j], gdn_norm[j], gdn_wo,
                                     norm_ple[i], bsz, s)
        else:
            h, hg, hssq = _ssm_layer(h, hn, ssq, ssm_w, j, ssm_lam_re[j], ssm_lam_im[j], ssm_b_re[j],
                                     ssm_b_im[j], ssm_c_re[j], ssm_c_im[j], ssm_log_step[j], ssm_d[j],
                                     ssm_wg, ssm_b_glu[j], ssm_wo, norm_ple[i], bsz, s)
        extras = [(h, "tile", {}), (p_rows, "rows", {"row_blk": i * (m // min(MM_TM, m))}),
                  (ple_wp, "cols", {"row_blk": i, "rows": pdim})]
        outs = _matmul(hg, ple_wg, w_row_blk=i, out_dtype=F32, mode="ple", row_ssq=hssq, extras=extras,
                       norm_gain=norm_mix[i + 1] if i + 1 < depth else None, w_outer=True)
        h, hn, ssq = outs if i + 1 < depth else (outs, None, None)
    return _rmsnorm(h, final_norm, F32).reshape(bsz, s, d)
```

```python
import functools
import math

import jax
import jax.numpy as jnp
from jax import lax
from jax.experimental import pallas as pl
from jax.experimental.pallas import tpu as pltpu

F32 = jnp.float32
BF16 = jnp.bfloat16

LANES = 128
SUBLANES = 8
VMEM_LIMIT_BYTES = 56 * 1024 * 1024

CAST_BLOCK_ELEMS = 1 << 20
A_PREP_ROWS = 32
MM_TM = 512
MM_TM_WIDE = 1024
MM_TN = 1024
NORM_EPS = 1e-6
LOG2E = 1.4426950408889634
HEAD_DIM = 128
FOX_BLOCK = 1024
FOX_PAIRS_PER_TRIP = 8
GDN_CHUNK = 64
GDN_CONV = 4
GDN_HEADS_PER_STEP = 16
GDN_TIME_BLOCK = 256
SSM_GROUP = 16
SSM_STATE = 64
SSM_PACK = LANES // SSM_GROUP
SSM_L = 16
NEG_BIG = -1e30

HIGHEST = lax.Precision.HIGHEST


def _cparams(*sem):
    return pltpu.CompilerParams(dimension_semantics=sem, vmem_limit_bytes=VMEM_LIMIT_BYTES)


def _sigmoid(x):
    return 0.5 * jnp.tanh(0.5 * x) + 0.5


def _silu(x):
    hx = 0.5 * x
    return hx + hx * jnp.tanh(hx)


def _softplus(x):
    return jnp.maximum(x, 0.0) + jnp.log(1.0 + jnp.exp(-jnp.abs(x)))


def _rmsnorm_kernel(x_ref, g_ref, o_ref):
    x = x_ref[...]
    ms = jnp.mean(x * x, axis=-1, keepdims=True)
    o_ref[...] = (x * lax.rsqrt(ms + NORM_EPS) * g_ref[...]).astype(o_ref.dtype)


def _rmsnorm(x, g, out_dtype):
    m, d = x.shape
    tm = min(512, m)
    return pl.pallas_call(
        _rmsnorm_kernel,
        grid=(m // tm,),
        in_specs=[pl.BlockSpec((tm, d), lambda i: (i, 0)),
                  pl.BlockSpec((1, d), lambda i: (0, 0))],
        out_specs=pl.BlockSpec((tm, d), lambda i: (i, 0)),
        out_shape=jax.ShapeDtypeStruct((m, d), out_dtype),
        compiler_params=_cparams("parallel"),
        name="rmsnorm",
    )(x, g.reshape(1, d).astype(F32))


def _cast_kernel(x_ref, o_ref):
    o_ref[...] = x_ref[...].astype(o_ref.dtype)


def _cast_bf16(x, rows=None):
    nl, r, c = x.shape
    rows = r if rows is None else rows
    tc = min(2048, c)
    tr = min(rows, CAST_BLOCK_ELEMS // tc)
    assert rows % tr == 0 and c % tc == 0
    out = pl.pallas_call(
        _cast_kernel,
        grid=(nl, rows // tr, c // tc),
        in_specs=[pl.BlockSpec((1, tr, tc), lambda l, i, j: (l, i, j))],
        out_specs=pl.BlockSpec((1, tr, tc), lambda l, i, j: (l, i, j)),
        out_shape=jax.ShapeDtypeStruct((nl, rows, c), BF16),
        compiler_params=_cparams("parallel", "parallel", "parallel"),
        name="cast_bf16",
    )(x)
    return out.reshape(nl * rows, c)


_N_EXTRA = {"plain": 0, "colscale": 1, "residual": 1, "ple": 3, "glu": 2}


def _mm_kernel(*refs, mode, cast_a, w_t, has_ssq, emit_norm, inv_k):
    it = iter(refs)
    a_ref, w_ref = next(it), next(it)
    ssq_ref = next(it) if has_ssq else None
    extra = [next(it) for _ in range(_N_EXTRA[mode])]
    gain_ref = next(it) if emit_norm else None
    o_ref = next(it)
    hg_ref, ssq_out_ref = (next(it), next(it)) if emit_norm else (None, None)
    if cast_a:
        a_scr = next(it)

        @pl.when(pl.program_id(1) == 0)
        def _():
            def prep(r, carry):
                rows = pl.ds(pl.multiple_of(r * A_PREP_ROWS, A_PREP_ROWS), A_PREP_ROWS)
                a_scr[rows, :] = a_ref[rows, :].astype(BF16)
                return carry

            lax.fori_loop(0, a_ref.shape[0] // A_PREP_ROWS, prep, 0)

        a = a_scr[...]
    else:
        a = a_ref[...]

    acc = lax.dot_general(a, w_ref[...], (((1,), (1 if w_t else 0,)), ((), ())),
                          preferred_element_type=F32)
    if has_ssq:
        parts = ssq_ref[...]
        tot = parts[:, 0:LANES]
        for t in range(1, parts.shape[1] // LANES):
            tot = tot + parts[:, t * LANES:(t + 1) * LANES]
        acc = acc * lax.rsqrt(tot[:, 0:1] * inv_k + NORM_EPS)
    if mode == "plain":
        out = acc
    elif mode == "colscale":
        out = acc * extra[0][...]
    elif mode == "residual":
        out = extra[0][...] + acc
    elif mode == "ple":
        res_ref, p_ref, wp_ref = extra
        emb = jnp.dot(p_ref[...].astype(BF16), wp_ref[...], preferred_element_type=F32)
        out = res_ref[...] + _sigmoid(acc) * emb
    elif mode == "glu":
        z_ref, b_ref = extra
        tn = o_ref.shape[1]
        y = a_ref[:, pl.ds(pl.multiple_of(pl.program_id(1) * tn, tn), tn)]
        out = y * _sigmoid(acc + b_ref[...]) * _silu(z_ref[...])
    else:
        raise ValueError(mode)
    o_ref[...] = out.astype(o_ref.dtype)
    if emit_norm:
        hg_ref[...] = (out * gain_ref[...]).astype(hg_ref.dtype)
        ssq_out_ref[...] = jnp.broadcast_to(jnp.sum(out * out, axis=-1, keepdims=True),
                                            ssq_out_ref.shape)


def _extra_spec(arr, kind, tm, tn, row_blk=0, col_off=0, rows=None):
    if kind == "tile":
        return pl.BlockSpec((tm, tn), lambda i, j: (i + row_blk, j + col_off // tn))
    if kind == "row":
        return pl.BlockSpec((1, tn), lambda i, j: (0, j))
    if kind == "rows":
        return pl.BlockSpec((tm, arr.shape[1]), lambda i, j: (i + row_blk, 0))
    if kind == "cols":
        return pl.BlockSpec((rows, tn), lambda i, j: (row_blk, j))
    raise ValueError(kind)


def _matmul(a, w, *, out_dtype, n=None, mode="plain", extras=(), cast_a=False, row_ssq=None,
            norm_gain=None, w_t=None, w_row_blk=0, w_col_off=0, w_outer=False, tm=MM_TM, tn=MM_TN):
    m, kdim = a.shape
    n = (w_t if w_t else w.shape[1]) if n is None else n
    tm = min(tm, m)
    tn = functools.reduce(math.gcd, [tn, n, w_col_off, w_t or 0])
    assert m % tm == 0 and tn % LANES == 0 and len(extras) == _N_EXTRA[mode]
    operands = [a, w]
    if w_t:
        first = (w_row_blk * w_t + w_col_off) // tn
        w_spec = pl.BlockSpec((tn, kdim), lambda i, j: (first + j, 0))
    else:
        w_spec = pl.BlockSpec((kdim, tn), lambda i, j: (w_row_blk, j + w_col_off // tn))
    in_specs = [pl.BlockSpec((tm, kdim), lambda i, j: (i, 0)), w_spec]
    if row_ssq is not None:
        operands.append(row_ssq)
        in_specs.append(pl.BlockSpec((tm, row_ssq.shape[1]), lambda i, j: (i, 0)))
    for arr, kind, opts in extras:
        assert opts.get("col_off", 0) % tn == 0
        operands.append(arr)
        in_specs.append(_extra_spec(arr, kind, tm, tn, **opts))
    tile = pl.BlockSpec((tm, tn), lambda i, j: (i, j))
    out_specs, out_shape = tile, jax.ShapeDtypeStruct((m, n), out_dtype)
    if norm_gain is not None:
        operands.append(norm_gain.astype(F32).reshape(1, n))
        in_specs.append(pl.BlockSpec((1, tn), lambda i, j: (0, j)))
        out_specs = [tile, tile, pl.BlockSpec((tm, LANES), lambda i, j: (i, j))]
        out_shape = [out_shape, jax.ShapeDtypeStruct((m, n), BF16),
                     jax.ShapeDtypeStruct((m, n // tn * LANES), F32)]
    scratch = [pltpu.VMEM((tm, kdim), BF16)] if cast_a else []
    grid = (m // tm, n // tn)
    if w_outer:
        assert not cast_a
        flip = lambda sp: pl.BlockSpec(sp.block_shape, lambda j, i, f=sp.index_map: f(i, j))
        in_specs = [flip(sp) for sp in in_specs]
        out_specs = [flip(sp) for sp in out_specs] if norm_gain is not None else flip(out_specs)
        grid = grid[::-1]
    return pl.pallas_call(
        functools.partial(_mm_kernel, mode=mode, cast_a=cast_a, w_t=bool(w_t),
                          has_ssq=row_ssq is not None,
                          emit_norm=norm_gain is not None, inv_k=1.0 / kdim),
        grid=grid,
        in_specs=in_specs,
        out_specs=out_specs,
        out_shape=out_shape,
        scratch_shapes=scratch,
        compiler_params=_cparams("parallel", "arbitrary"),
        name="matmul_" + mode,
    )(*operands)


def _fox_cum_kernel(f_ref, b_ref, o_ref, carry_ref, *, ts):
    @pl.when(pl.program_id(1) == 0)
    def _():
        carry_ref[...] = jnp.zeros_like(carry_ref)

    x = f_ref[...] + b_ref[...]
    log_f = jnp.minimum(x, 0.0) - jnp.log(1.0 + jnp.exp(-jnp.abs(x)))
    r = lax.broadcasted_iota(jnp.int32, (ts, ts), 0)
    c = lax.broadcasted_iota(jnp.int32, (ts, ts), 1)
    lower = (c <= r).astype(F32)
    cum = jnp.dot(lower, log_f, precision=HIGHEST, preferred_element_type=F32) + carry_ref[...]
    carry_ref[...] = cum[ts - 1:ts, :]
    o_ref[0] = cum.T


def _fox_cum(f_logit, b_f, bsz, s):
    ts = min(1024, s)
    return pl.pallas_call(
        functools.partial(_fox_cum_kernel, ts=ts),
        grid=(bsz, s // ts),
        in_specs=[pl.BlockSpec((ts, LANES), lambda b, t: (b * (s // ts) + t, 0)),
                  pl.BlockSpec((1, LANES), lambda b, t: (0, 0))],
        out_specs=pl.BlockSpec((1, LANES, ts), lambda b, t: (b, 0, t)),
        out_shape=jax.ShapeDtypeStruct((bsz, LANES, s), F32),
        scratch_shapes=[pltpu.VMEM((1, LANES), F32)],
        compiler_params=_cparams("parallel", "arbitrary"),
        name="fox_cum",
    )(f_logit, b_f)


def _fox_attn_kernel(q_ref, k_ref, v_ref, c_ref, z_ref, o_ref, sa_ref, sb_ref, *, tq, nq):
    tk = tq // 2
    i = pl.program_id(2)
    q = q_ref[...]
    c0 = c_ref[0, 0, :, pl.ds(pl.multiple_of(i * tq, tq), LANES)][:, 0:1]

    def scores(qrows, j):
        start = j * tk if isinstance(j, int) else pl.multiple_of(j * tk, tk)
        s = lax.dot_general(qrows, k_ref[pl.ds(start, tk), :], (((1,), (1,)), ((), ())),
                            preferred_element_type=F32)
        return s + (c0 - c_ref[0, 0, :, pl.ds(start, tk)]) * LOG2E

    def update(s, j, m, acc, width=1):
        rows = width * tk
        start = j * tk if isinstance(j, int) else pl.multiple_of(j * tk, tk)
        vb = jnp.concatenate([v_ref[pl.ds(start, rows), :], jnp.ones((rows, HEAD_DIM), BF16)], axis=1)
        m_new = jnp.maximum(m, jnp.max(s, axis=-1, keepdims=True))
        p = jnp.exp2(s - m_new)
        acc = jnp.exp2(m - m_new) * acc + jnp.dot(p.astype(BF16), vb, preferred_element_type=F32)
        return m_new, acc

    sa_ref[...] = scores(q, 0)

    def pair(t, carry):
        m, acc = carry
        sb_ref[...] = scores(q, 2 * t + 1)
        m, acc = update(sa_ref[...], 2 * t, m, acc)
        sa_ref[...] = scores(q, 2 * t + 2)
        return update(sb_ref[...], 2 * t + 1, m, acc)

    def trip(t, carry):
        for n in range(FOX_PAIRS_PER_TRIP):
            carry = pair(FOX_PAIRS_PER_TRIP * t + n, carry)
        return carry

    init = (jnp.full((tq, 1), NEG_BIG, F32), jnp.zeros((tq, 2 * HEAD_DIM), F32))
    carry = lax.fori_loop(0, i // FOX_PAIRS_PER_TRIP, trip, init) if nq > FOX_PAIRS_PER_TRIP else init

    def finish(a, rows):
        o = a[:, 0:HEAD_DIM] / a[:, HEAD_DIM:HEAD_DIM + 1]
        o_ref[rows, :] = (o * _silu(z_ref[rows, :])).astype(o_ref.dtype)

    def tail(carry, left):
        for n in range(left):
            carry = pair(i - left + n, carry)
        m, acc = carry
        r = lax.broadcasted_iota(jnp.int32, (tk, tk), 0)
        c = lax.broadcasted_iota(jnp.int32, (tk, tk), 1)
        s_bot = jnp.concatenate(
            [sa_ref[tk:tq, :], jnp.where(c <= r, scores(q[tk:tq], 2 * i + 1), NEG_BIG)], axis=1)
        _, acc_top = update(jnp.where(c <= r, sa_ref[0:tk, :], NEG_BIG), 2 * i, m[0:tk], acc[0:tk])
        _, acc_bot = update(s_bot, 2 * i, m[tk:tq], acc[tk:tq], width=2)
        finish(acc_top, slice(0, tk))
        finish(acc_bot, slice(tk, tq))

    def dispatch(carry, lo, hi):
        if hi - lo == 1:
            tail(carry, lo)
        else:
            mid = (lo + hi) // 2
            lax.cond(i % FOX_PAIRS_PER_TRIP < mid, lambda cr: dispatch(cr, lo, mid),
                     lambda cr: dispatch(cr, mid, hi), carry)

    dispatch(carry, 0, FOX_PAIRS_PER_TRIP)


def _fox_attention(qkv, cum4, z, bsz, s, heads):
    tq = min(FOX_BLOCK, s)
    nq = s // tq
    return pl.pallas_call(
        functools.partial(_fox_attn_kernel, tq=tq, nq=nq),
        grid=(bsz, heads, nq),
        in_specs=[pl.BlockSpec((tq, HEAD_DIM), lambda b, h, i: (b * nq + i, h)),
                  pl.BlockSpec((s, HEAD_DIM), lambda b, h, i: (b, heads + h)),
                  pl.BlockSpec((s, HEAD_DIM), lambda b, h, i: (b, 2 * heads + h)),
                  pl.BlockSpec((1, 1, 1, s), lambda b, h, i: (b, h, 0, 0)),
                  pl.BlockSpec((tq, HEAD_DIM), lambda b, h, i: (b * nq + i, h))],
        out_specs=pl.BlockSpec((tq, HEAD_DIM), lambda b, h, i: (b * nq + i, h)),
        out_shape=jax.ShapeDtypeStruct((bsz * s, heads * HEAD_DIM), BF16),
        scratch_shapes=[pltpu.VMEM((tq, tq // 2), F32)] * 2,
        compiler_params=_cparams("parallel", "parallel", "arbitrary"),
        name="fox_attn",
    )(qkv, qkv, qkv, cum4, z)


def _fox_layer(h, hn, ssq, w_all, layer, w_tail, b_f, wo_all, ple_gain, bsz, s):
    width = wo_all.shape[1]
    heads = width // HEAD_DIM
    q_scale = jnp.concatenate([jnp.full((width,), HEAD_DIM ** -0.5 * LOG2E, F32),
                               jnp.ones((2 * width,), F32)]).reshape(1, 3 * width)
    qkv = _matmul(hn, w_all, n=3 * width, w_t=4 * width, w_row_blk=layer, out_dtype=BF16,
                  mode="colscale", extras=[(q_scale, "row", {})], row_ssq=ssq, tm=MM_TM_WIDE)
    z = _matmul(hn, w_all, n=width, w_t=4 * width, w_row_blk=layer, w_col_off=3 * width,
                out_dtype=F32, row_ssq=ssq, w_outer=True, tm=MM_TM_WIDE)
    f_logit = _matmul(hn, w_tail, w_t=LANES, out_dtype=F32, row_ssq=ssq, tm=MM_TM_WIDE)
    b_row = jnp.pad(b_f.astype(F32), (0, LANES - heads)).reshape(1, LANES)
    cum = _fox_cum(f_logit, b_row, bsz, s)
    gated = _fox_attention(qkv, cum.reshape(bsz, LANES, 1, s), z, bsz, s, heads)
    return _matmul(gated, wo_all, w_row_blk=layer, out_dtype=F32, mode="residual",
                   extras=[(h, "tile", {})], norm_gain=ple_gain, w_outer=True)


def _bdot(a, b):
    return jnp.einsum("bmk,bkn->bmn", a, b, preferred_element_type=F32)


def _split_bf16(x):
    hi = x.astype(BF16)
    return hi, (x - hi.astype(F32)).astype(BF16)


def _bdot3_parts(a_hi, a_lo, b_hi, b_lo):
    return _bdot(a_hi, b_hi) + _bdot(a_hi, b_lo) + _bdot(a_lo, b_hi)


def _bdot3(a, b):
    return _bdot3_parts(*_split_bf16(a), *_split_bf16(b))


def _bdot_exact_lhs(a_bf16, b):
    b_hi = b.astype(BF16)
    b_mid, b_lo = _split_bf16(b - b_hi.astype(F32))
    return _bdot(a_bf16, b_hi) + _bdot(a_bf16, b_mid) + _bdot(a_bf16, b_lo)


def _gdn_kernel(q_ref, k_ref, v_ref, z_ref, t_ref, cw_ref, gp_ref, nw_ref, o_ref,
                xq_ref, xk_ref, xv_ref, st_ref, u_ref, wq_ref, ik_ref, gl_ref, *, tb, heads, hps):
    hb = pl.program_id(1)
    c, dh = GDN_CHUNK, HEAD_DIM
    nb = tb // c
    nbat = hps * nb
    pad = SUBLANES

    @pl.when(pl.program_id(2) == 0)
    def _():
        for r in (xq_ref, xk_ref, xv_ref):
            r[0:pad, :] = jnp.zeros((pad, hps * dh), F32)
        st_ref[...] = jnp.zeros_like(st_ref)

    def conv(x_ref, xs_ref, col):
        x = x_ref[...]
        xs_ref[pad:pad + tb, :] = x
        xs = xs_ref[...]
        y = x * cw_ref[col, GDN_CONV - 1:GDN_CONV, :]
        for j in range(GDN_CONV - 1):
            shifted = pltpu.roll(xs, GDN_CONV - 1 - j, axis=0)[pad:pad + tb, :]
            y = y + shifted * cw_ref[col, j:j + 1, :]
        xs_ref[0:pad, :] = x[tb - pad:tb, :]
        return _silu(y)

    def by_chunk(x):
        w = x.shape[1] // hps
        return jnp.concatenate([x[:, g * w:(g + 1) * w].reshape(nb, c, w) for g in range(hps)], axis=0)

    def l2n(x):
        return x * lax.rsqrt(jnp.sum(x * x, axis=-1, keepdims=True) + NORM_EPS)

    q = l2n(by_chunk(conv(q_ref, xq_ref, 0))) * (dh ** -0.5)
    k = l2n(by_chunk(conv(k_ref, xk_ref, 1)))
    v = by_chunk(conv(v_ref, xv_ref, 2))

    t = t_ref[...]
    lane = lax.broadcasted_iota(jnp.int32, (tb, LANES), 1)
    beta_all = _sigmoid(t)
    g_all = -jnp.exp(gp_ref[0:1, :]) * _softplus(t + gp_ref[1:2, :])

    def pick(x, first):
        cols = [jnp.sum(jnp.where(lane == first + hb * hps + g, x, 0.0), axis=-1, keepdims=True)
                for g in range(hps)]
        return jnp.concatenate([col.reshape(nb, c, 1) for col in cols], axis=0)

    beta = pick(beta_all, 0)
    g = pick(g_all, heads)

    ri = lax.broadcasted_iota(jnp.int32, (c, c), 0)
    ci = lax.broadcasted_iota(jnp.int32, (c, c), 1)
    incl = ci <= ri
    strict = ci < ri
    eye = (ci == ri).astype(F32)
    tri = jnp.broadcast_to(incl.astype(BF16), (nbat, c, c))
    rl = lax.broadcasted_iota(jnp.int32, (c, LANES), 0)
    cl = lax.broadcasted_iota(jnp.int32, (c, LANES), 1)
    seg_mask = (rl > cl) | (cl == c)

    seg = _bdot_exact_lhs(tri, jnp.where(seg_mask, g, 0.0))
    gc = seg[:, :, c:c + 1]
    g_last = seg[:, c - 1:c, c:c + 1]
    decay = jnp.exp(jnp.where(incl, seg[:, :, 0:c], NEG_BIG))
    egc = jnp.exp(gc)
    k_beta = k * beta
    kq = jnp.einsum("bmd,bnd->bmn", jnp.concatenate([k_beta, q], axis=1).astype(BF16),
                    k.astype(BF16), preferred_element_type=F32)
    a = jnp.where(strict, kq[:, 0:c] * decay, 0.0)
    intra = kq[:, c:2 * c] * decay
    npair = nbat // 2
    pair_lanes = lambda x: jnp.concatenate(
        [x.reshape(npair, 2, c, x.shape[2])[:, 0], x.reshape(npair, 2, c, x.shape[2])[:, 1]], axis=2)
    left = lax.broadcasted_iota(jnp.int32, (c, 2 * c), 1) < c

    def block_diag(x2):
        zero = jnp.zeros_like(x2)
        return jnp.concatenate([jnp.where(left, x2, zero), jnp.where(left, zero, x2)], axis=1)

    a2 = pair_lanes(a)
    inv2 = jnp.concatenate([eye, eye], axis=1) - a2
    a_hi, a_lo = _split_bf16(a2)
    pw2 = _bdot3_parts(a_hi, a_lo, block_diag(a_hi), block_diag(a_lo))
    levels = int(math.log2(c)) - 1
    for lvl in range(levels):
        inv_hi, inv_lo = _split_bf16(inv2)
        pw_hi, pw_lo = _split_bf16(pw2)
        pd_hi, pd_lo = block_diag(pw_hi), block_diag(pw_lo)
        if lvl < levels - 1:
            both = _bdot3_parts(jnp.concatenate([inv_hi, pw_hi], axis=1),
                                jnp.concatenate([inv_lo, pw_lo], axis=1), pd_hi, pd_lo)
            inv2 = inv2 + both[:, 0:c]
            pw2 = both[:, c:2 * c]
        else:
            inv2 = inv2 + _bdot3_parts(inv_hi, inv_lo, pd_hi, pd_lo)
    rhs = jnp.concatenate([v * beta, k_beta * egc], axis=2).reshape(npair, 2, c, 2 * dh)
    zero = jnp.zeros((npair, c, 2 * dh), F32)
    rhs_bd = jnp.concatenate([jnp.concatenate([rhs[:, 0], zero], axis=2),
                              jnp.concatenate([zero, rhs[:, 1]], axis=2)], axis=1)
    uw2 = _bdot3(inv2, rhs_bd)
    uw = jnp.stack([uw2[:, :, 0:2 * dh], uw2[:, :, 2 * dh:4 * dh]], axis=1).reshape(nbat, c, 2 * dh)
    u_ref[...] = uw[:, :, 0:dh].reshape(nbat * c, dh)
    wq_ref[...] = jnp.concatenate([uw[:, :, dh:2 * dh], q * egc], axis=1).astype(BF16).reshape(
        nbat * 2 * c, dh)
    k_dec = k * jnp.exp(g_last - gc)
    for b in range(nbat):
        ik_ref[b * (c + dh):(b + 1) * (c + dh), :] = jnp.concatenate(
            [intra[b], k_dec[b].T], axis=0).astype(BF16)
    gl_ref[...] = jnp.broadcast_to(jnp.exp(g_last), (nbat, SUBLANES, dh)).reshape(nbat * SUBLANES, dh)

    def step(n, states):
        rows = lambda g, size: pl.ds(pl.multiple_of((g * nb + n) * size, size), size)
        ws = [jnp.dot(wq_ref[rows(g, 2 * c), :], states[g].astype(BF16), preferred_element_type=F32)
              for g in range(hps)]
        v_new = [u_ref[rows(g, c), :] - ws[g][0:c] for g in range(hps)]
        iv = [jnp.dot(ik_ref[pl.ds(pl.multiple_of((g * nb + n) * (c + dh), c), c + dh), :],
                      v_new[g].astype(BF16), preferred_element_type=F32) for g in range(hps)]
        new_states = []
        r0 = pl.multiple_of(n * c, c)
        for g in range(hps):
            out = ws[g][c:2 * c] + iv[g][0:c]
            gl = gl_ref[pl.ds(pl.multiple_of((g * nb + n) * SUBLANES, SUBLANES), 1), :]
            new_states.append(states[g] * gl + iv[g][c:c + dh])
            ms = jnp.mean(out * out, axis=-1, keepdims=True)
            o = out * lax.rsqrt(ms + NORM_EPS) * nw_ref[...]
            o_ref[pl.ds(r0, c), g * dh:(g + 1) * dh] = (
                o * _silu(z_ref[pl.ds(r0, c), g * dh:(g + 1) * dh])).astype(o_ref.dtype)
        return tuple(new_states)

    states = lax.fori_loop(0, nb, step, tuple(st_ref[g] for g in range(hps)))
    for g in range(hps):
        st_ref[g] = states[g]


def _gdn_mixer(qkvz, tail, conv_w, gate_p, norm_w, bsz, s, heads):
    hps = math.gcd(GDN_HEADS_PER_STEP, heads)
    tb = min(GDN_TIME_BLOCK, s)
    nt = s // tb
    hblocks = heads // hps
    width = hps * HEAD_DIM
    nbat = hps * (tb // GDN_CHUNK)
    col = lambda off: pl.BlockSpec((tb, width), lambda b, h, t: (b * nt + t, off + h))
    return pl.pallas_call(
        functools.partial(_gdn_kernel, tb=tb, heads=heads, hps=hps),
        grid=(bsz, hblocks, nt),
        in_specs=[col(0), col(hblocks), col(2 * hblocks), col(3 * hblocks),
                  pl.BlockSpec((tb, LANES), lambda b, h, t: (b * nt + t, 0)),
                  pl.BlockSpec((3, GDN_CONV, width), lambda b, h, t: (0, 0, h)),
                  pl.BlockSpec((2, LANES), lambda b, h, t: (0, 0)),
                  pl.BlockSpec((1, HEAD_DIM), lambda b, h, t: (0, 0))],
        out_specs=pl.BlockSpec((tb, width), lambda b, h, t: (b * nt + t, h)),
        out_shape=jax.ShapeDtypeStruct((bsz * s, heads * HEAD_DIM), BF16),
        scratch_shapes=[pltpu.VMEM((tb + SUBLANES, width), F32)] * 3
                       + [pltpu.VMEM((hps, HEAD_DIM, HEAD_DIM), F32),
                          pltpu.VMEM((nbat * GDN_CHUNK, HEAD_DIM), F32),
                          pltpu.VMEM((nbat * 2 * GDN_CHUNK, HEAD_DIM), BF16),
                          pltpu.VMEM((nbat * (GDN_CHUNK + HEAD_DIM), GDN_CHUNK), BF16),
                          pltpu.VMEM((nbat * SUBLANES, HEAD_DIM), F32)],
        compiler_params=_cparams("parallel", "parallel", "arbitrary"),
        name="gdn_mixer",
    )(qkvz, qkvz, qkvz, qkvz, tail, conv_w, gate_p, norm_w)


def _gdn_layer(h, hn, ssq, w_all, layer, w_tail, conv_w, a_log, dt_bias, norm_w, wo_all, ple_gain,
               bsz, s):
    width = wo_all.shape[1]
    heads = width // HEAD_DIM
    qkvz = _matmul(hn, w_all, w_t=4 * width, w_row_blk=layer, out_dtype=F32, row_ssq=ssq,
                   tm=MM_TM_WIDE)
    tail = _matmul(hn, w_tail, w_t=LANES, out_dtype=F32, row_ssq=ssq, tm=MM_TM_WIDE)
    cw = conv_w.astype(F32).reshape(GDN_CONV, 3, width).transpose(1, 0, 2)
    lane_pad = lambda x: jnp.pad(x.astype(F32), (heads, LANES - 2 * heads))
    gate_p = jnp.stack([lane_pad(a_log), lane_pad(dt_bias)])
    gated = _gdn_mixer(qkvz, tail, cw, gate_p, norm_w.astype(F32).reshape(1, HEAD_DIM),
                       bsz, s, heads)
    return _matmul(gated, wo_all, w_row_blk=layer, out_dtype=F32, mode="residual",
                   extras=[(h, "tile", {})], norm_gain=ple_gain, w_outer=True)


def _ssm_prep_kernel(lam_ref, bf_ref, cf_ref, bd_ref, wz_ref, wyt_ref, laml_ref):
    ns = SSM_PACK * SSM_STATE
    lam_re, lam_im, step_log = lam_ref[0, 0:1, :], lam_ref[0, 1:2, :], lam_ref[0, 2:3, :]
    step = jnp.exp(step_log)
    mag = jnp.exp(lam_re * step)
    lb_re, lb_im = mag * jnp.cos(lam_im * step), mag * jnp.sin(lam_im * step)
    den = lam_re * lam_re + lam_im * lam_im
    num_re = lb_re - 1.0
    zoh_re = (num_re * lam_re + lb_im * lam_im) / den
    zoh_im = (lb_im * lam_re - num_re * lam_im) / den
    b_re, b_im = bf_ref[0, 0], bf_ref[0, 1]
    bb_re = zoh_re * b_re - zoh_im * b_im
    bb_im = zoh_re * b_im + zoh_im * b_re
    c_re, c_im = cf_ref[0, 0], cf_ref[0, 1]
    cc_hi, cc_lo = _split_bf16(jnp.concatenate([c_re, -c_im], axis=1))
    nt_dot = lambda x, y: lax.dot_general(x, y, (((1,), (1,)), ((), ())), preferred_element_type=F32)

    bd_ref[0, 0, LANES:2 * LANES, 0:LANES] = jnp.zeros((LANES, LANES), bd_ref.dtype)
    pw_re, pw_im = jnp.ones_like(lb_re), jnp.zeros_like(lb_re)
    for d in range(SSM_L + 1):
        if d < SSM_L:
            a_re = pw_re * bb_re - pw_im * bb_im
            a_im = pw_re * bb_im + pw_im * bb_re
            a_d = jnp.concatenate([a_re, a_im], axis=1)
            a_hi, a_lo = _split_bf16(a_d)
            blk = (nt_dot(a_hi, cc_hi) + nt_dot(a_hi, cc_lo) + nt_dot(a_lo, cc_hi)).astype(bd_ref.dtype)
            d2 = d // 2
            if d % 2 == 0:
                bd_ref[0, d2, 0:LANES, 0:LANES] = blk
                bd_ref[0, d2, LANES:2 * LANES, LANES:2 * LANES] = blk
            else:
                bd_ref[0, d2, 0:LANES, LANES:2 * LANES] = blk
                if d2 + 1 < SSM_L // 2:
                    bd_ref[0, d2 + 1, LANES:2 * LANES, 0:LANES] = blk
            tin = SSM_L - 1 - d
            wz_ref[0, tin * LANES:(tin + 1) * LANES, :] = a_d.astype(wz_ref.dtype)
        if d >= 1:
            y_re = pw_re * c_re - pw_im * c_im
            y_im = pw_re * c_im + pw_im * c_re
            wyt_ref[0, (d - 1) * LANES:d * LANES, :] = jnp.concatenate(
                [y_re, -y_im], axis=1).astype(wyt_ref.dtype)
        if d == SSM_L:
            laml_ref[0] = jnp.concatenate([pw_re, pw_im], axis=1)
        pw_re, pw_im = pw_re * lb_re - pw_im * lb_im, pw_re * lb_im + pw_im * lb_re


def _ssm_prep(lam, bfull, cfull):
    nsg = lam.shape[0]
    ns = SSM_PACK * SSM_STATE
    rows = SSM_L * LANES
    return pl.pallas_call(
        _ssm_prep_kernel,
        grid=(nsg,),
        in_specs=[pl.BlockSpec((1, 3, ns), lambda g: (g, 0, 0)),
                  pl.BlockSpec((1, 2, LANES, ns), lambda g: (g, 0, 0, 0)),
                  pl.BlockSpec((1, 2, LANES, ns), lambda g: (g, 0, 0, 0))],
        out_specs=[pl.BlockSpec((1, SSM_L // 2, 2 * LANES, 2 * LANES), lambda g: (g, 0, 0, 0)),
                   pl.BlockSpec((1, rows, 2 * ns), lambda g: (g, 0, 0)),
                   pl.BlockSpec((1, rows, 2 * ns), lambda g: (g, 0, 0)),
                   pl.BlockSpec((1, 1, 2 * ns), lambda g: (g, 0, 0))],
        out_shape=[jax.ShapeDtypeStruct((nsg, SSM_L // 2, 2 * LANES, 2 * LANES), BF16),
                   jax.ShapeDtypeStruct((nsg, rows, 2 * ns), BF16),
                   jax.ShapeDtypeStruct((nsg, rows, 2 * ns), BF16),
                   jax.ShapeDtypeStruct((nsg, 1, 2 * ns), F32)],
        compiler_params=_cparams("parallel"),
        name="ssm_prep",
    )(lam, bfull, cfull)


def _ssm_scan_kernel(u_ref, bd_ref, wz_ref, wyt_ref, laml_ref, d_ref, o_ref,
                     x_ref, z_ref, xp_ref, st_ref, *, tc):
    ns = SSM_PACK * SSM_STATE

    @pl.when(pl.program_id(2) == 0)
    def _():
        st_ref[...] = jnp.zeros_like(st_ref)

    for tau in range(SSM_L):
        x_ref[:, tau * LANES:(tau + 1) * LANES] = u_ref[pl.ds(tau, tc, stride=SSM_L), :].astype(BF16)

    z_ref[...] = jnp.dot(x_ref[...], wz_ref[0], preferred_element_type=F32)

    a_re, a_im = laml_ref[0, :, 0:ns], laml_ref[0, :, ns:2 * ns]

    def row(r, carry):
        s_re, s_im = carry
        xp_ref[pl.ds(r, 1), :] = jnp.concatenate([s_re, s_im], axis=1)
        zr = z_ref[pl.ds(r, 1), :]
        n_re = a_re * s_re - a_im * s_im + zr[:, 0:ns]
        n_im = a_re * s_im + a_im * s_re + zr[:, ns:2 * ns]
        return n_re, n_im

    s_re, s_im = lax.fori_loop(0, tc, row, (st_ref[:, 0:ns], st_ref[:, ns:2 * ns]), unroll=4)
    st_ref[...] = jnp.concatenate([s_re, s_im], axis=1)

    xprev = xp_ref[...].astype(BF16)
    pair = 2 * LANES
    for t2 in range(SSM_L // 2):
        y2 = lax.dot_general(xprev, wyt_ref[0, t2 * pair:(t2 + 1) * pair, :],
                             (((1,), (1,)), ((), ())), preferred_element_type=F32)
        for tin in range(t2 + 1):
            y2 = y2 + jnp.dot(x_ref[:, tin * pair:(tin + 1) * pair], bd_ref[0, t2 - tin],
                              preferred_element_type=F32)
        for half in range(2):
            tau = 2 * t2 + half
            y = y2[:, half * LANES:(half + 1) * LANES]
            y = y + d_ref[...] * u_ref[pl.ds(tau, tc, stride=SSM_L), :]
            y = 0.5 * y * (1.0 + jnp.tanh(math.sqrt(2.0 / math.pi) * (y + 0.044715 * (y * y * y))))
            o_ref[pl.ds(tau, tc, stride=SSM_L), :] = y


def _ssm_scan(uz, bd, wz, wyt, laml, d_skip, bsz, s):
    e = d_skip.shape[1]
    nsg = e // LANES
    ns = SSM_PACK * SSM_STATE
    tb = min(8192, s)
    nt = s // tb
    tc = tb // SSM_L
    rows = SSM_L * LANES
    return pl.pallas_call(
        functools.partial(_ssm_scan_kernel, tc=tc),
        grid=(nsg, bsz, nt),
        in_specs=[pl.BlockSpec((tb, LANES), lambda g, b, t: (b * nt + t, g)),
                  pl.BlockSpec((1, SSM_L // 2, 2 * LANES, 2 * LANES), lambda g, b, t: (g, 0, 0, 0)),
                  pl.BlockSpec((1, rows, 2 * ns), lambda g, b, t: (g, 0, 0)),
                  pl.BlockSpec((1, rows, 2 * ns), lambda g, b, t: (g, 0, 0)),
                  pl.BlockSpec((1, 1, 2 * ns), lambda g, b, t: (g, 0, 0)),
                  pl.BlockSpec((1, LANES), lambda g, b, t: (0, g))],
        out_specs=pl.BlockSpec((tb, LANES), lambda g, b, t: (b * nt + t, g)),
        out_shape=jax.ShapeDtypeStruct((uz.shape[0], e), F32),
        scratch_shapes=[pltpu.VMEM((tc, rows), BF16), pltpu.VMEM((tc, 2 * ns), F32),
                        pltpu.VMEM((tc, 2 * ns), F32), pltpu.VMEM((1, 2 * ns), F32)],
        compiler_params=_cparams("parallel", "parallel", "arbitrary"),
        name="ssm_scan",
    )(uz, bd, wz, wyt, laml, d_skip)


def _ssm_layer(h, hn, ssq, w_all, layer, lam_re, lam_im, b_re, b_im, c_re, c_im, log_step, d_skip,
               wg_all, b_glu, wo_all, ple_gain, bsz, s):
    e = wo_all.shape[1]
    groups, nstate = lam_re.shape
    assert nstate == SSM_STATE and e == groups * SSM_GROUP and groups % SSM_PACK == 0
    nsg = groups // SSM_PACK
    ns = SSM_PACK * SSM_STATE
    lam = jnp.stack([lam_re.astype(F32).reshape(nsg, ns), lam_im.astype(F32).reshape(nsg, ns),
                     jnp.repeat(log_step.astype(F32), SSM_STATE).reshape(nsg, ns)], axis=1)
    eye = jnp.eye(SSM_PACK, dtype=F32)

    def expand_b(b):
        return jnp.einsum("sgpm,gh->sgmhp", b.astype(F32).reshape(nsg, SSM_PACK, SSM_STATE, SSM_GROUP),
                          eye).reshape(nsg, LANES, ns)

    def expand_c(c):
        return jnp.einsum("sgnp,gh->sgnhp", c.astype(F32).reshape(nsg, SSM_PACK, SSM_GROUP, SSM_STATE),
                          eye).reshape(nsg, LANES, ns)

    bfull = jnp.stack([expand_b(b_re), expand_b(b_im)], axis=1)
    cfull = jnp.stack([expand_c(c_re), expand_c(c_im)], axis=1)
    bd, wz, wyt, laml = _ssm_prep(lam, bfull, cfull)

    uz = _matmul(hn, w_all, w_row_blk=layer, out_dtype=F32, row_ssq=ssq, tm=MM_TM_WIDE)
    y = _ssm_scan(uz, bd, wz, wyt, laml, d_skip.astype(F32).reshape(1, e), bsz, s)
    y2 = _matmul(y, wg_all, w_row_blk=layer, out_dtype=BF16, mode="glu", cast_a=True,
                 extras=[(uz, "tile", {"col_off": e}), (b_glu.astype(F32).reshape(1, e), "row", {})])
    return _matmul(y2, wo_all, w_row_blk=layer, out_dtype=F32, mode="residual",
                   extras=[(h, "tile", {})], norm_gain=ple_gain, w_outer=True)


def kernel(x, p, norm_mix, fox_w_in, fox_b_f, fox_w_out, gdn_w_in, gdn_conv, gdn_a_log, gdn_dt_bias, gdn_norm, gdn_w_out, ssm_w_in, ssm_lam_re, ssm_lam_im, ssm_b_re, ssm_b_im, ssm_c_re, ssm_c_im, ssm_log_step, ssm_d, ssm_w_glu, ssm_b_glu, ssm_w_out, norm_ple, ple_w_proj, ple_w_gate, final_norm):
    bsz, s, d = x.shape
    depth, m = p.shape[0], bsz * s
    pdim = p.shape[-1]
    fox_width, gdn_width = fox_w_out.shape[1], gdn_w_out.shape[1]
    fox_wt, gdn_wt = jnp.swapaxes(fox_w_in, 1, 2), jnp.swapaxes(gdn_w_in, 1, 2)
    pad_rows = lambda t: jnp.pad(t, ((0, 0), (0, LANES - t.shape[1]), (0, 0)))
    fox_w = _cast_bf16(fox_wt, 4 * fox_width)
    fox_tail = _cast_bf16(pad_rows(fox_wt[:, 4 * fox_width:]))
    gdn_w = _cast_bf16(gdn_wt, 4 * gdn_width)
    gdn_tail = _cast_bf16(pad_rows(gdn_wt[:, 4 * gdn_width:]))
    fox_wo, gdn_wo = _cast_bf16(fox_w_out), _cast_bf16(gdn_w_out)
    ssm_w, ssm_wg, ssm_wo = _cast_bf16(ssm_w_in), _cast_bf16(ssm_w_glu), _cast_bf16(ssm_w_out)
    ple_wg, ple_wp = _cast_bf16(ple_w_gate), _cast_bf16(ple_w_proj)
    p_rows = p.reshape(depth * m, pdim)

    h = x.reshape(m, d).astype(F32)
    hn, ssq = _rmsnorm(h, norm_mix[0], BF16), None
    for i in range(depth):
        kind, j = i % 3, i // 3
        if kind == 0:
            h, hg, hssq = _fox_layer(h, hn, ssq, fox_w, j, fox_tail[j * LANES:(j + 1) * LANES],
                                     fox_b_f[j], fox_wo, norm_ple[i], bsz, s)
        elif kind == 1:
            h, hg, hssq = _gdn_layer(h, hn, ssq, gdn_w, j, gdn_tail[j * LANES:(j + 1) * LANES],
                                     gdn_conv[j], gdn_a_log[j], gdn_dt_bias[j], gdn_norm[j], gdn_wo,
                                     norm_ple[i], bsz, s)
        else:
            h, hg, hssq = _ssm_layer(h, hn, ssq, ssm_w, j, ssm_lam_re[j], ssm_lam_im[j], ssm_b_re[j],
                                     ssm_b_im[j], ssm_c_re[j], ssm_c_im[j], ssm_log_step[j], ssm_d[j],
                                     ssm_wg, ssm_b_glu[j], ssm_wo, norm_ple[i], bsz, s)
        extras = [(h, "tile", {}), (p_rows, "rows", {"row_blk": i * (m // min(MM_TM, m))}),
                  (ple_wp, "cols", {"row_blk": i, "rows": pdim})]
        outs = _matmul(hg, ple_wg, w_row_blk=i, out_dtype=F32, mode="ple", row_ssq=hssq, extras=extras,
                       norm_gain=norm_mix[i + 1] if i + 1 < depth else None, w_outer=True)
        h, hn, ssq = outs if i + 1 < depth else (outs, None, None)
    return _rmsnorm(h, final_norm, F32).reshape(bsz, s, d)
```

```python
import functools
import math

import jax
import jax.numpy as jnp
from jax import lax
from jax.experimental import pallas as pl
from jax.experimental.pallas import tpu as pltpu

F32 = jnp.float32
BF16 = jnp.bfloat16

LANES = 128
SUBLANES = 8
VMEM_LIMIT_BYTES = 56 * 1024 * 1024

CAST_BLOCK_ELEMS = 1 << 20
A_PREP_ROWS = 32
MM_TM = 512
MM_TM_WIDE = 1024
MM_TN = 1024
NORM_EPS = 1e-6
LOG2E = 1.4426950408889634
HEAD_DIM = 128
FOX_BLOCK = 1024
FOX_PAIRS_PER_TRIP = 5
GDN_CHUNK = 64
GDN_CONV = 4
GDN_HEADS_PER_STEP = 16
GDN_TIME_BLOCK = 256
SSM_GROUP = 16
SSM_STATE = 64
SSM_PACK = LANES // SSM_GROUP
SSM_L = 16
NEG_BIG = -1e30

HIGHEST = lax.Precision.HIGHEST


def _cparams(*sem):
    return pltpu.CompilerParams(dimension_semantics=sem, vmem_limit_bytes=VMEM_LIMIT_BYTES)


def _sigmoid(x):
    return 0.5 * jnp.tanh(0.5 * x) + 0.5


def _silu(x):
    hx = 0.5 * x
    return hx + hx * jnp.tanh(hx)


def _softplus(x):
    return jnp.maximum(x, 0.0) + jnp.log(1.0 + jnp.exp(-jnp.abs(x)))


def _rmsnorm_kernel(x_ref, g_ref, o_ref):
    x = x_ref[...]
    ms = jnp.mean(x * x, axis=-1, keepdims=True)
    o_ref[...] = (x * lax.rsqrt(ms + NORM_EPS) * g_ref[...]).astype(o_ref.dtype)


def _rmsnorm(x, g, out_dtype):
    m, d = x.shape
    tm = min(512, m)
    return pl.pallas_call(
        _rmsnorm_kernel,
        grid=(m // tm,),
        in_specs=[pl.BlockSpec((tm, d), lambda i: (i, 0)),
                  pl.BlockSpec((1, d), lambda i: (0, 0))],
        out_specs=pl.BlockSpec((tm, d), lambda i: (i, 0)),
        out_shape=jax.ShapeDtypeStruct((m, d), out_dtype),
        compiler_params=_cparams("parallel"),
        name="rmsnorm",
    )(x, g.reshape(1, d).astype(F32))


def _cast_kernel(x_ref, o_ref):
    o_ref[...] = x_ref[...].astype(o_ref.dtype)


def _cast_bf16(x, rows=None):
    nl, r, c = x.shape
    rows = r if rows is None else rows
    tc = min(2048, c)
    tr = min(rows, CAST_BLOCK_ELEMS // tc)
    assert rows % tr == 0 and c % tc == 0
    out = pl.pallas_call(
        _cast_kernel,
        grid=(nl, rows // tr, c // tc),
        in_specs=[pl.BlockSpec((1, tr, tc), lambda l, i, j: (l, i, j))],
        out_specs=pl.BlockSpec((1, tr, tc), lambda l, i, j: (l, i, j)),
        out_shape=jax.ShapeDtypeStruct((nl, rows, c), BF16),
        compiler_params=_cparams("parallel", "parallel", "parallel"),
        name="cast_bf16",
    )(x)
    return out.reshape(nl * rows, c)


_N_EXTRA = {"plain": 0, "colscale": 1, "residual": 1, "ple": 3, "glu": 2}


def _mm_kernel(*refs, mode, cast_a, w_t, has_ssq, emit_norm, inv_k):
    it = iter(refs)
    a_ref, w_ref = next(it), next(it)
    ssq_ref = next(it) if has_ssq else None
    extra = [next(it) for _ in range(_N_EXTRA[mode])]
    gain_ref = next(it) if emit_norm else None
    o_ref = next(it)
    hg_ref, ssq_out_ref = (next(it), next(it)) if emit_norm else (None, None)
    if cast_a:
        a_scr = next(it)

        @pl.when(pl.program_id(1) == 0)
        def _():
            def prep(r, carry):
                rows = pl.ds(pl.multiple_of(r * A_PREP_ROWS, A_PREP_ROWS), A_PREP_ROWS)
                a_scr[rows, :] = a_ref[rows, :].astype(BF16)
                return carry

            lax.fori_loop(0, a_ref.shape[0] // A_PREP_ROWS, prep, 0)

        a = a_scr[...]
    else:
        a = a_ref[...]

    acc = lax.dot_general(a, w_ref[...], (((1,), (1 if w_t else 0,)), ((), ())),
                          preferred_element_type=F32)
    if has_ssq:
        parts = ssq_ref[...]
        tot = parts[:, 0:LANES]
        for t in range(1, parts.shape[1] // LANES):
            tot = tot + parts[:, t * LANES:(t + 1) * LANES]
        acc = acc * lax.rsqrt(tot[:, 0:1] * inv_k + NORM_EPS)
    if mode == "plain":
        out = acc
    elif mode == "colscale":
        out = acc * extra[0][...]
    elif mode == "residual":
        out = extra[0][...] + acc
    elif mode == "ple":
        res_ref, p_ref, wp_ref = extra
        emb = jnp.dot(p_ref[...].astype(BF16), wp_ref[...], preferred_element_type=F32)
        out = res_ref[...] + _sigmoid(acc) * emb
    elif mode == "glu":
        z_ref, b_ref = extra
        tn = o_ref.shape[1]
        y = a_ref[:, pl.ds(pl.multiple_of(pl.program_id(1) * tn, tn), tn)]
        out = y * _sigmoid(acc + b_ref[...]) * _silu(z_ref[...])
    else:
        raise ValueError(mode)
    o_ref[...] = out.astype(o_ref.dtype)
    if emit_norm:
        hg_ref[...] = (out * gain_ref[...]).astype(hg_ref.dtype)
        ssq_out_ref[...] = jnp.broadcast_to(jnp.sum(out * out, axis=-1, keepdims=True),
                                            ssq_out_ref.shape)


def _extra_spec(arr, kind, tm, tn, row_blk=0, col_off=0, rows=None):
    if kind == "tile":
        return pl.BlockSpec((tm, tn), lambda i, j: (i + row_blk, j + col_off // tn))
    if kind == "row":
        return pl.BlockSpec((1, tn), lambda i, j: (0, j))
    if kind == "rows":
        return pl.BlockSpec((tm, arr.shape[1]), lambda i, j: (i + row_blk, 0))
    if kind == "cols":
        return pl.BlockSpec((rows, tn), lambda i, j: (row_blk, j))
    raise ValueError(kind)


def _matmul(a, w, *, out_dtype, n=None, mode="plain", extras=(), cast_a=False, row_ssq=None,
            norm_gain=None, w_t=None, w_row_blk=0, w_col_off=0, w_outer=False, tm=MM_TM, tn=MM_TN):
    m, kdim = a.shape
    n = (w_t if w_t else w.shape[1]) if n is None else n
    tm = min(tm, m)
    tn = functools.reduce(math.gcd, [tn, n, w_col_off, w_t or 0])
    assert m % tm == 0 and tn % LANES == 0 and len(extras) == _N_EXTRA[mode]
    operands = [a, w]
    if w_t:
        first = (w_row_blk * w_t + w_col_off) // tn
        w_spec = pl.BlockSpec((tn, kdim), lambda i, j: (first + j, 0))
    else:
        w_spec = pl.BlockSpec((kdim, tn), lambda i, j: (w_row_blk, j + w_col_off // tn))
    in_specs = [pl.BlockSpec((tm, kdim), lambda i, j: (i, 0)), w_spec]
    if row_ssq is not None:
        operands.append(row_ssq)
        in_specs.append(pl.BlockSpec((tm, row_ssq.shape[1]), lambda i, j: (i, 0)))
    for arr, kind, opts in extras:
        assert opts.get("col_off", 0) % tn == 0
        operands.append(arr)
        in_specs.append(_extra_spec(arr, kind, tm, tn, **opts))
    tile = pl.BlockSpec((tm, tn), lambda i, j: (i, j))
    out_specs, out_shape = tile, jax.ShapeDtypeStruct((m, n), out_dtype)
    if norm_gain is not None:
        operands.append(norm_gain.astype(F32).reshape(1, n))
        in_specs.append(pl.BlockSpec((1, tn), lambda i, j: (0, j)))
        out_specs = [tile, tile, pl.BlockSpec((tm, LANES), lambda i, j: (i, j))]
        out_shape = [out_shape, jax.ShapeDtypeStruct((m, n), BF16),
                     jax.ShapeDtypeStruct((m, n // tn * LANES), F32)]
    scratch = [pltpu.VMEM((tm, kdim), BF16)] if cast_a else []
    grid = (m // tm, n // tn)
    if w_outer:
        assert not cast_a
        flip = lambda sp: pl.BlockSpec(sp.block_shape, lambda j, i, f=sp.index_map: f(i, j))
        in_specs = [flip(sp) for sp in in_specs]
        out_specs = [flip(sp) for sp in out_specs] if norm_gain is not None else flip(out_specs)
        grid = grid[::-1]
    return pl.pallas_call(
        functools.partial(_mm_kernel, mode=mode, cast_a=cast_a, w_t=bool(w_t),
                          has_ssq=row_ssq is not None,
                          emit_norm=norm_gain is not None, inv_k=1.0 / kdim),
        grid=grid,
        in_specs=in_specs,
        out_specs=out_specs,
        out_shape=out_shape,
        scratch_shapes=scratch,
        compiler_params=_cparams("parallel", "arbitrary"),
        name="matmul_" + mode,
    )(*operands)


def _fox_cum_kernel(f_ref, b_ref, o_ref, carry_ref, *, ts):
    @pl.when(pl.program_id(1) == 0)
    def _():
        carry_ref[...] = jnp.zeros_like(carry_ref)

    x = f_ref[...] + b_ref[...]
    log_f = jnp.minimum(x, 0.0) - jnp.log(1.0 + jnp.exp(-jnp.abs(x)))
    r = lax.broadcasted_iota(jnp.int32, (ts, ts), 0)
    c = lax.broadcasted_iota(jnp.int32, (ts, ts), 1)
    lower = (c <= r).astype(F32)
    cum = jnp.dot(lower, log_f, precision=HIGHEST, preferred_element_type=F32) + carry_ref[...]
    carry_ref[...] = cum[ts - 1:ts, :]
    o_ref[0] = cum.T


def _fox_cum(f_logit, b_f, bsz, s):
    ts = min(1024, s)
    return pl.pallas_call(
        functools.partial(_fox_cum_kernel, ts=ts),
        grid=(bsz, s // ts),
        in_specs=[pl.BlockSpec((ts, LANES), lambda b, t: (b * (s // ts) + t, 0)),
                  pl.BlockSpec((1, LANES), lambda b, t: (0, 0))],
        out_specs=pl.BlockSpec((1, LANES, ts), lambda b, t: (b, 0, t)),
        out_shape=jax.ShapeDtypeStruct((bsz, LANES, s), F32),
        scratch_shapes=[pltpu.VMEM((1, LANES), F32)],
        compiler_params=_cparams("parallel", "arbitrary"),
        name="fox_cum",
    )(f_logit, b_f)


def _fox_attn_kernel(q_ref, k_ref, v_ref, c_ref, z_ref, o_ref, sa_ref, sb_ref, *, tq):
    tk = tq // 2
    i = pl.program_id(2)
    q = q_ref[...]
    c0 = c_ref[0, 0, :, pl.ds(pl.multiple_of(i * tq, tq), LANES)][:, 0:1]

    def scores(qrows, j):
        start = j * tk if isinstance(j, int) else pl.multiple_of(j * tk, tk)
        s = lax.dot_general(qrows, k_ref[pl.ds(start, tk), :], (((1,), (1,)), ((), ())),
                            preferred_element_type=F32)
        return s + (c0 - c_ref[0, 0, :, pl.ds(start, tk)]) * LOG2E

    def update(s, j, m, acc, width=1):
        rows = width * tk
        start = j * tk if isinstance(j, int) else pl.multiple_of(j * tk, tk)
        vb = jnp.concatenate([v_ref[pl.ds(start, rows), :], jnp.ones((rows, HEAD_DIM), BF16)], axis=1)
        m_new = jnp.maximum(m, jnp.max(s, axis=-1, keepdims=True))
        p = jnp.exp2(s - m_new)
        acc = jnp.exp2(m - m_new) * acc + jnp.dot(p.astype(BF16), vb, preferred_element_type=F32)
        return m_new, acc

    sa_ref[...] = scores(q, 0)

    def pair(t, carry):
        m, acc = carry
        sb_ref[...] = scores(q, 2 * t + 1)
        m, acc = update(sa_ref[...], 2 * t, m, acc)
        sa_ref[...] = scores(q, 2 * t + 2)
        return update(sb_ref[...], 2 * t + 1, m, acc)

    def trip(t, carry):
        for n in range(FOX_PAIRS_PER_TRIP):
            carry = pair(FOX_PAIRS_PER_TRIP * t + n, carry)
        return carry

    init = (jnp.full((tq, 1), NEG_BIG, F32), jnp.zeros((tq, 2 * HEAD_DIM), F32))
    carry = lax.fori_loop(0, i // FOX_PAIRS_PER_TRIP, trip, init)

    def finish(a, rows):
        o = a[:, 0:HEAD_DIM] / a[:, HEAD_DIM:HEAD_DIM + 1]
        o_ref[rows, :] = (o * _silu(z_ref[rows, :])).astype(o_ref.dtype)

    def tail(carry, left):
        for n in range(left):
            carry = pair(i - left + n, carry)
        m, acc = carry
        r = lax.broadcasted_iota(jnp.int32, (tk, tk), 0)
        c = lax.broadcasted_iota(jnp.int32, (tk, tk), 1)
        s_bot = jnp.concatenate(
            [sa_ref[tk:tq, :], jnp.where(c <= r, scores(q[tk:tq], 2 * i + 1), NEG_BIG)], axis=1)
        _, acc_top = update(jnp.where(c <= r, sa_ref[0:tk, :], NEG_BIG), 2 * i, m[0:tk], acc[0:tk])
        _, acc_bot = update(s_bot, 2 * i, m[tk:tq], acc[tk:tq], width=2)
        finish(acc_top, slice(0, tk))
        finish(acc_bot, slice(tk, tq))

    def dispatch(carry, lo, hi):
        if hi - lo == 1:
            tail(carry, lo)
        else:
            mid = (lo + hi) // 2
            lax.cond(i % FOX_PAIRS_PER_TRIP < mid, lambda cr: dispatch(cr, lo, mid),
                     lambda cr: dispatch(cr, mid, hi), carry)

    dispatch(carry, 0, FOX_PAIRS_PER_TRIP)


def _fox_attention(qkv, cum4, z, bsz, s, heads):
    tq = min(FOX_BLOCK, s)
    nq = s // tq
    return pl.pallas_call(
        functools.partial(_fox_attn_kernel, tq=tq),
        grid=(bsz, heads, nq),
        in_specs=[pl.BlockSpec((tq, HEAD_DIM), lambda b, h, i: (b * nq + i, h)),
                  pl.BlockSpec((s, HEAD_DIM), lambda b, h, i: (b, heads + h)),
                  pl.BlockSpec((s, HEAD_DIM), lambda b, h, i: (b, 2 * heads + h)),
                  pl.BlockSpec((1, 1, 1, s), lambda b, h, i: (b, h, 0, 0)),
                  pl.BlockSpec((tq, HEAD_DIM), lambda b, h, i: (b * nq + i, h))],
        out_specs=pl.BlockSpec((tq, HEAD_DIM), lambda b, h, i: (b * nq + i, h)),
        out_shape=jax.ShapeDtypeStruct((bsz * s, heads * HEAD_DIM), BF16),
        scratch_shapes=[pltpu.VMEM((tq, tq // 2), F32)] * 2,
        compiler_params=_cparams("parallel", "parallel", "arbitrary"),
        name="fox_attn",
    )(qkv, qkv, qkv, cum4, z)


def _fox_layer(h, hn, ssq, w_all, layer, w_tail, b_f, wo_all, ple_gain, bsz, s):
    width = wo_all.shape[1]
    heads = width // HEAD_DIM
    q_scale = jnp.concatenate([jnp.full((width,), HEAD_DIM ** -0.5 * LOG2E, F32),
                               jnp.ones((2 * width,), F32)]).reshape(1, 3 * width)
    qkv = _matmul(hn, w_all, n=3 * width, w_t=4 * width, w_row_blk=layer, out_dtype=BF16,
                  mode="colscale", extras=[(q_scale, "row", {})], row_ssq=ssq, tm=MM_TM_WIDE)
    z = _matmul(hn, w_all, n=width, w_t=4 * width, w_row_blk=layer, w_col_off=3 * width,
                out_dtype=F32, row_ssq=ssq, w_outer=True, tm=MM_TM_WIDE)
    f_logit = _matmul(hn, w_tail, w_t=LANES, out_dtype=F32, row_ssq=ssq, tm=MM_TM_WIDE)
    b_row = jnp.pad(b_f.astype(F32), (0, LANES - heads)).reshape(1, LANES)
    cum = _fox_cum(f_logit, b_row, bsz, s)
    gated = _fox_attention(qkv, cum.reshape(bsz, LANES, 1, s), z, bsz, s, heads)
    return _matmul(gated, wo_all, w_row_blk=layer, out_dtype=F32, mode="residual",
                   extras=[(h, "tile", {})], norm_gain=ple_gain, w_outer=True)


def _bdot(a, b):
    return jnp.einsum("bmk,bkn->bmn", a, b, preferred_element_type=F32)


def _split_bf16(x):
    hi = x.astype(BF16)
    return hi, (x - hi.astype(F32)).astype(BF16)


def _bdot3_parts(a_hi, a_lo, b_hi, b_lo):
    return _bdot(a_hi, b_hi) + _bdot(a_hi, b_lo) + _bdot(a_lo, b_hi)


def _bdot3(a, b):
    return _bdot3_parts(*_split_bf16(a), *_split_bf16(b))


def _bdot_exact_lhs(a_bf16, b):
    b_hi = b.astype(BF16)
    b_mid, b_lo = _split_bf16(b - b_hi.astype(F32))
    return _bdot(a_bf16, b_hi) + _bdot(a_bf16, b_mid) + _bdot(a_bf16, b_lo)


def _gdn_kernel(q_ref, k_ref, v_ref, z_ref, t_ref, cw_ref, gp_ref, nw_ref, o_ref,
                xq_ref, xk_ref, xv_ref, st_ref, u_ref, wq_ref, ik_ref, gl_ref, *, tb, heads, hps):
    hb = pl.program_id(1)
    c, dh = GDN_CHUNK, HEAD_DIM
    nb = tb // c
    nbat = hps * nb
    pad = SUBLANES

    @pl.when(pl.program_id(2) == 0)
    def _():
        for r in (xq_ref, xk_ref, xv_ref):
            r[0:pad, :] = jnp.zeros((pad, hps * dh), F32)
        st_ref[...] = jnp.zeros_like(st_ref)

    def conv(x_ref, xs_ref, col):
        x = x_ref[...]
        xs_ref[pad:pad + tb, :] = x
        xs = xs_ref[...]
        y = x * cw_ref[col, GDN_CONV - 1:GDN_CONV, :]
        for j in range(GDN_CONV - 1):
            shifted = pltpu.roll(xs, GDN_CONV - 1 - j, axis=0)[pad:pad + tb, :]
            y = y + shifted * cw_ref[col, j:j + 1, :]
        xs_ref[0:pad, :] = x[tb - pad:tb, :]
        return _silu(y)

    def by_chunk(x):
        w = x.shape[1] // hps
        return jnp.concatenate([x[:, g * w:(g + 1) * w].reshape(nb, c, w) for g in range(hps)], axis=0)

    def l2n(x):
        return x * lax.rsqrt(jnp.sum(x * x, axis=-1, keepdims=True) + NORM_EPS)

    q = l2n(by_chunk(conv(q_ref, xq_ref, 0))) * (dh ** -0.5)
    k = l2n(by_chunk(conv(k_ref, xk_ref, 1)))
    v = by_chunk(conv(v_ref, xv_ref, 2))

    t = t_ref[...]
    lane = lax.broadcasted_iota(jnp.int32, (tb, LANES), 1)
    beta_all = _sigmoid(t)
    g_all = -jnp.exp(gp_ref[0:1, :]) * _softplus(t + gp_ref[1:2, :])

    def pick(x, first):
        cols = [jnp.sum(jnp.where(lane == first + hb * hps + g, x, 0.0), axis=-1, keepdims=True)
                for g in range(hps)]
        return jnp.concatenate([col.reshape(nb, c, 1) for col in cols], axis=0)

    beta = pick(beta_all, 0)
    g = pick(g_all, heads)

    ri = lax.broadcasted_iota(jnp.int32, (c, c), 0)
    ci = lax.broadcasted_iota(jnp.int32, (c, c), 1)
    incl = ci <= ri
    strict = ci < ri
    eye = (ci == ri).astype(F32)
    tri = jnp.broadcast_to(incl.astype(BF16), (nbat, c, c))
    rl = lax.broadcasted_iota(jnp.int32, (c, LANES), 0)
    cl = lax.broadcasted_iota(jnp.int32, (c, LANES), 1)
    seg_mask = (rl > cl) | (cl == c)

    seg = _bdot_exact_lhs(tri, jnp.where(seg_mask, g, 0.0))
    gc = seg[:, :, c:c + 1]
    g_last = seg[:, c - 1:c, c:c + 1]
    decay = jnp.exp(jnp.where(incl, seg[:, :, 0:c], NEG_BIG))
    egc = jnp.exp(gc)
    k_beta = k * beta
    kq = jnp.einsum("bmd,bnd->bmn", jnp.concatenate([k_beta, q], axis=1).astype(BF16),
                    k.astype(BF16), preferred_element_type=F32)
    a = jnp.where(strict, kq[:, 0:c] * decay, 0.0)
    intra = kq[:, c:2 * c] * decay
    npair = nbat // 2
    pair_lanes = lambda x: jnp.concatenate(
        [x.reshape(npair, 2, c, x.shape[2])[:, 0], x.reshape(npair, 2, c, x.shape[2])[:, 1]], axis=2)
    left = lax.broadcasted_iota(jnp.int32, (c, 2 * c), 1) < c

    def block_diag(x2):
        zero = jnp.zeros_like(x2)
        return jnp.concatenate([jnp.where(left, x2, zero), jnp.where(left, zero, x2)], axis=1)

    a2 = pair_lanes(a)
    inv2 = jnp.concatenate([eye, eye], axis=1) - a2
    a_hi, a_lo = _split_bf16(a2)
    pw2 = _bdot3_parts(a_hi, a_lo, block_diag(a_hi), block_diag(a_lo))
    levels = int(math.log2(c)) - 1
    for lvl in range(levels):
        inv_hi, inv_lo = _split_bf16(inv2)
        pw_hi, pw_lo = _split_bf16(pw2)
        pd_hi, pd_lo = block_diag(pw_hi), block_diag(pw_lo)
        if lvl < levels - 1:
            both = _bdot3_parts(jnp.concatenate([inv_hi, pw_hi], axis=1),
                                jnp.concatenate([inv_lo, pw_lo], axis=1), pd_hi, pd_lo)
            inv2 = inv2 + both[:, 0:c]
            pw2 = both[:, c:2 * c]
        else:
            inv2 = inv2 + _bdot3_parts(inv_hi, inv_lo, pd_hi, pd_lo)
    rhs = jnp.concatenate([v * beta, k_beta * egc], axis=2).reshape(npair, 2, c, 2 * dh)
    zero = jnp.zeros((npair, c, 2 * dh), F32)
    rhs_bd = jnp.concatenate([jnp.concatenate([rhs[:, 0], zero], axis=2),
                              jnp.concatenate([zero, rhs[:, 1]], axis=2)], axis=1)
    uw2 = _bdot3(inv2, rhs_bd)
    uw = jnp.stack([uw2[:, :, 0:2 * dh], uw2[:, :, 2 * dh:4 * dh]], axis=1).reshape(nbat, c, 2 * dh)
    u_ref[...] = uw[:, :, 0:dh].reshape(nbat * c, dh)
    wq_ref[...] = jnp.concatenate([uw[:, :, dh:2 * dh], q * egc], axis=1).astype(BF16).reshape(
        nbat * 2 * c, dh)
    k_dec = k * jnp.exp(g_last - gc)
    for b in range(nbat):
        ik_ref[b * (c + dh):(b + 1) * (c + dh), :] = jnp.concatenate(
            [intra[b], k_dec[b].T], axis=0).astype(BF16)
    gl_ref[...] = jnp.broadcast_to(jnp.exp(g_last), (nbat, SUBLANES, dh)).reshape(nbat * SUBLANES, dh)

    def step(n, states):
        rows = lambda g, size: pl.ds(pl.multiple_of((g * nb + n) * size, size), size)
        ws = [jnp.dot(wq_ref[rows(g, 2 * c), :], states[g].astype(BF16), preferred_element_type=F32)
              for g in range(hps)]
        v_new = [u_ref[rows(g, c), :] - ws[g][0:c] for g in range(hps)]
        iv = [jnp.dot(ik_ref[pl.ds(pl.multiple_of((g * nb + n) * (c + dh), c), c + dh), :],
                      v_new[g].astype(BF16), preferred_element_type=F32) for g in range(hps)]
        new_states = []
        r0 = pl.multiple_of(n * c, c)
        for g in range(hps):
            out = ws[g][c:2 * c] + iv[g][0:c]
            gl = gl_ref[pl.ds(pl.multiple_of((g * nb + n) * SUBLANES, SUBLANES), 1), :]
            new_states.append(states[g] * gl + iv[g][c:c + dh])
            ms = jnp.mean(out * out, axis=-1, keepdims=True)
            o = out * lax.rsqrt(ms + NORM_EPS) * nw_ref[...]
            o_ref[pl.ds(r0, c), g * dh:(g + 1) * dh] = (
                o * _silu(z_ref[pl.ds(r0, c), g * dh:(g + 1) * dh])).astype(o_ref.dtype)
        return tuple(new_states)

    states = lax.fori_loop(0, nb, step, tuple(st_ref[g] for g in range(hps)))
    for g in range(hps):
        st_ref[g] = states[g]


def _gdn_mixer(qkvz, tail, conv_w, gate_p, norm_w, bsz, s, heads):
    hps = math.gcd(GDN_HEADS_PER_STEP, heads)
    tb = min(GDN_TIME_BLOCK, s)
    nt = s // tb
    hblocks = heads // hps
    width = hps * HEAD_DIM
    nbat = hps * (tb // GDN_CHUNK)
    col = lambda off: pl.BlockSpec((tb, width), lambda b, h, t: (b * nt + t, off + h))
    return pl.pallas_call(
        functools.partial(_gdn_kernel, tb=tb, heads=heads, hps=hps),
        grid=(bsz, hblocks, nt),
        in_specs=[col(0), col(hblocks), col(2 * hblocks), col(3 * hblocks),
                  pl.BlockSpec((tb, LANES), lambda b, h, t: (b * nt + t, 0)),
                  pl.BlockSpec((3, GDN_CONV, width), lambda b, h, t: (0, 0, h)),
                  pl.BlockSpec((2, LANES), lambda b, h, t: (0, 0)),
                  pl.BlockSpec((1, HEAD_DIM), lambda b, h, t: (0, 0))],
        out_specs=pl.BlockSpec((tb, width), lambda b, h, t: (b * nt + t, h)),
        out_shape=jax.ShapeDtypeStruct((bsz * s, heads * HEAD_DIM), BF16),
        scratch_shapes=[pltpu.VMEM((tb + SUBLANES, width), F32)] * 3
                       + [pltpu.VMEM((hps, HEAD_DIM, HEAD_DIM), F32),
                          pltpu.VMEM((nbat * GDN_CHUNK, HEAD_DIM), F32),
                          pltpu.VMEM((nbat * 2 * GDN_CHUNK, HEAD_DIM), BF16),
                          pltpu.VMEM((nbat * (GDN_CHUNK + HEAD_DIM), GDN_CHUNK), BF16),
                          pltpu.VMEM((nbat * SUBLANES, HEAD_DIM), F32)],
        compiler_params=_cparams("parallel", "parallel", "arbitrary"),
        name="gdn_mixer",
    )(qkvz, qkvz, qkvz, qkvz, tail, conv_w, gate_p, norm_w)


def _gdn_layer(h, hn, ssq, w_all, layer, w_tail, conv_w, a_log, dt_bias, norm_w, wo_all, ple_gain,
               bsz, s):
    width = wo_all.shape[1]
    heads = width // HEAD_DIM
    qkvz = _matmul(hn, w_all, w_t=4 * width, w_row_blk=layer, out_dtype=F32, row_ssq=ssq,
                   tm=MM_TM_WIDE)
    tail = _matmul(hn, w_tail, w_t=LANES, out_dtype=F32, row_ssq=ssq, tm=MM_TM_WIDE)
    cw = conv_w.astype(F32).reshape(GDN_CONV, 3, width).transpose(1, 0, 2)
    lane_pad = lambda x: jnp.pad(x.astype(F32), (heads, LANES - 2 * heads))
    gate_p = jnp.stack([lane_pad(a_log), lane_pad(dt_bias)])
    gated = _gdn_mixer(qkvz, tail, cw, gate_p, norm_w.astype(F32).reshape(1, HEAD_DIM),
                       bsz, s, heads)
    return _matmul(gated, wo_all, w_row_blk=layer, out_dtype=F32, mode="residual",
                   extras=[(h, "tile", {})], norm_gain=ple_gain, w_outer=True)


def _ssm_prep_kernel(lam_ref, bf_ref, cf_ref, bd_ref, wz_ref, wyt_ref, laml_ref):
    ns = SSM_PACK * SSM_STATE
    lam_re, lam_im, step_log = lam_ref[0, 0:1, :], lam_ref[0, 1:2, :], lam_ref[0, 2:3, :]
    step = jnp.exp(step_log)
    mag = jnp.exp(lam_re * step)
    lb_re, lb_im = mag * jnp.cos(lam_im * step), mag * jnp.sin(lam_im * step)
    den = lam_re * lam_re + lam_im * lam_im
    num_re = lb_re - 1.0
    zoh_re = (num_re * lam_re + lb_im * lam_im) / den
    zoh_im = (lb_im * lam_re - num_re * lam_im) / den
    b_re, b_im = bf_ref[0, 0], bf_ref[0, 1]
    bb_re = zoh_re * b_re - zoh_im * b_im
    bb_im = zoh_re * b_im + zoh_im * b_re
    c_re, c_im = cf_ref[0, 0], cf_ref[0, 1]
    cc_hi, cc_lo = _split_bf16(jnp.concatenate([c_re, -c_im], axis=1))
    nt_dot = lambda x, y: lax.dot_general(x, y, (((1,), (1,)), ((), ())), preferred_element_type=F32)

    bd_ref[0, 0, LANES:2 * LANES, 0:LANES] = jnp.zeros((LANES, LANES), bd_ref.dtype)
    pw_re, pw_im = jnp.ones_like(lb_re), jnp.zeros_like(lb_re)
    for d in range(SSM_L + 1):
        if d < SSM_L:
            a_re = pw_re * bb_re - pw_im * bb_im
            a_im = pw_re * bb_im + pw_im * bb_re
            a_d = jnp.concatenate([a_re, a_im], axis=1)
            a_hi, a_lo = _split_bf16(a_d)
            blk = (nt_dot(a_hi, cc_hi) + nt_dot(a_hi, cc_lo) + nt_dot(a_lo, cc_hi)).astype(bd_ref.dtype)
            d2 = d // 2
            if d % 2 == 0:
                bd_ref[0, d2, 0:LANES, 0:LANES] = blk
                bd_ref[0, d2, LANES:2 * LANES, LANES:2 * LANES] = blk
            else:
                bd_ref[0, d2, 0:LANES, LANES:2 * LANES] = blk
                if d2 + 1 < SSM_L // 2:
                    bd_ref[0, d2 + 1, LANES:2 * LANES, 0:LANES] = blk
            tin = SSM_L - 1 - d
            wz_ref[0, tin * LANES:(tin + 1) * LANES, :] = a_d.astype(wz_ref.dtype)
        if d >= 1:
            y_re = pw_re * c_re - pw_im * c_im
            y_im = pw_re * c_im + pw_im * c_re
            wyt_ref[0, (d - 1) * LANES:d * LANES, :] = jnp.concatenate(
                [y_re, -y_im], axis=1).astype(wyt_ref.dtype)
        if d == SSM_L:
            laml_ref[0] = jnp.concatenate([pw_re, pw_im], axis=1)
        pw_re, pw_im = pw_re * lb_re - pw_im * lb_im, pw_re * lb_im + pw_im * lb_re


def _ssm_prep(lam, bfull, cfull):
    nsg = lam.shape[0]
    ns = SSM_PACK * SSM_STATE
    rows = SSM_L * LANES
    return pl.pallas_call(
        _ssm_prep_kernel,
        grid=(nsg,),
        in_specs=[pl.BlockSpec((1, 3, ns), lambda g: (g, 0, 0)),
                  pl.BlockSpec((1, 2, LANES, ns), lambda g: (g, 0, 0, 0)),
                  pl.BlockSpec((1, 2, LANES, ns), lambda g: (g, 0, 0, 0))],
        out_specs=[pl.BlockSpec((1, SSM_L // 2, 2 * LANES, 2 * LANES), lambda g: (g, 0, 0, 0)),
                   pl.BlockSpec((1, rows, 2 * ns), lambda g: (g, 0, 0)),
                   pl.BlockSpec((1, rows, 2 * ns), lambda g: (g, 0, 0)),
                   pl.BlockSpec((1, 1, 2 * ns), lambda g: (g, 0, 0))],
        out_shape=[jax.ShapeDtypeStruct((nsg, SSM_L // 2, 2 * LANES, 2 * LANES), BF16),
                   jax.ShapeDtypeStruct((nsg, rows, 2 * ns), BF16),
                   jax.ShapeDtypeStruct((nsg, rows, 2 * ns), BF16),
                   jax.ShapeDtypeStruct((nsg, 1, 2 * ns), F32)],
        compiler_params=_cparams("parallel"),
        name="ssm_prep",
    )(lam, bfull, cfull)


def _ssm_scan_kernel(u_ref, bd_ref, wz_ref, wyt_ref, laml_ref, d_ref, o_ref,
                     x_ref, z_ref, xp_ref, st_ref, *, tc):
    ns = SSM_PACK * SSM_STATE

    @pl.when(pl.program_id(2) == 0)
    def _():
        st_ref[...] = jnp.zeros_like(st_ref)

    for tau in range(SSM_L):
        x_ref[:, tau * LANES:(tau + 1) * LANES] = u_ref[pl.ds(tau, tc, stride=SSM_L), :].astype(BF16)

    z_ref[...] = jnp.dot(x_ref[...], wz_ref[0], preferred_element_type=F32)

    a_re, a_im = laml_ref[0, :, 0:ns], laml_ref[0, :, ns:2 * ns]

    def row(r, carry):
        s_re, s_im = carry
        xp_ref[pl.ds(r, 1), :] = jnp.concatenate([s_re, s_im], axis=1)
        zr = z_ref[pl.ds(r, 1), :]
        n_re = a_re * s_re - a_im * s_im + zr[:, 0:ns]
        n_im = a_re * s_im + a_im * s_re + zr[:, ns:2 * ns]
        return n_re, n_im

    s_re, s_im = lax.fori_loop(0, tc, row, (st_ref[:, 0:ns], st_ref[:, ns:2 * ns]), unroll=4)
    st_ref[...] = jnp.concatenate([s_re, s_im], axis=1)

    xprev = xp_ref[...].astype(BF16)
    pair = 2 * LANES
    for t2 in range(SSM_L // 2):
        y2 = lax.dot_general(xprev, wyt_ref[0, t2 * pair:(t2 + 1) * pair, :],
                             (((1,), (1,)), ((), ())), preferred_element_type=F32)
        for tin in range(t2 + 1):
            y2 = y2 + jnp.dot(x_ref[:, tin * pair:(tin + 1) * pair], bd_ref[0, t2 - tin],
                              preferred_element_type=F32)
        for half in range(2):
            tau = 2 * t2 + half
            y = y2[:, half * LANES:(half + 1) * LANES]
            y = y + d_ref[...] * u_ref[pl.ds(tau, tc, stride=SSM_L), :]
            y = 0.5 * y * (1.0 + jnp.tanh(math.sqrt(2.0 / math.pi) * (y + 0.044715 * (y * y * y))))
            o_ref[pl.ds(tau, tc, stride=SSM_L), :] = y


def _ssm_scan(uz, bd, wz, wyt, laml, d_skip, bsz, s):
    e = d_skip.shape[1]
    nsg = e // LANES
    ns = SSM_PACK * SSM_STATE
    tb = min(8192, s)
    nt = s // tb
    tc = tb // SSM_L
    rows = SSM_L * LANES
    return pl.pallas_call(
        functools.partial(_ssm_scan_kernel, tc=tc),
        grid=(nsg, bsz, nt),
        in_specs=[pl.BlockSpec((tb, LANES), lambda g, b, t: (b * nt + t, g)),
                  pl.BlockSpec((1, SSM_L // 2, 2 * LANES, 2 * LANES), lambda g, b, t: (g, 0, 0, 0)),
                  pl.BlockSpec((1, rows, 2 * ns), lambda g, b, t: (g, 0, 0)),
                  pl.BlockSpec((1, rows, 2 * ns), lambda g, b, t: (g, 0, 0)),
                  pl.BlockSpec((1, 1, 2 * ns), lambda g, b, t: (g, 0, 0)),
                  pl.BlockSpec((1, LANES), lambda g, b, t: (0, g))],
        out_specs=pl.BlockSpec((tb, LANES), lambda g, b, t: (b * nt + t, g)),
        out_shape=jax.ShapeDtypeStruct((uz.shape[0], e), F32),
        scratch_shapes=[pltpu.VMEM((tc, rows), BF16), pltpu.VMEM((tc, 2 * ns), F32),
                        pltpu.VMEM((tc, 2 * ns), F32), pltpu.VMEM((1, 2 * ns), F32)],
        compiler_params=_cparams("parallel", "parallel", "arbitrary"),
        name="ssm_scan",
    )(uz, bd, wz, wyt, laml, d_skip)


def _ssm_layer(h, hn, ssq, w_all, layer, lam_re, lam_im, b_re, b_im, c_re, c_im, log_step, d_skip,
               wg_all, b_glu, wo_all, ple_gain, bsz, s):
    e = wo_all.shape[1]
    groups, nstate = lam_re.shape
    assert nstate == SSM_STATE and e == groups * SSM_GROUP and groups % SSM_PACK == 0
    nsg = groups // SSM_PACK
    ns = SSM_PACK * SSM_STATE
    lam = jnp.stack([lam_re.astype(F32).reshape(nsg, ns), lam_im.astype(F32).reshape(nsg, ns),
                     jnp.repeat(log_step.astype(F32), SSM_STATE).reshape(nsg, ns)], axis=1)
    eye = jnp.eye(SSM_PACK, dtype=F32)

    def expand_b(b):
        return jnp.einsum("sgpm,gh->sgmhp", b.astype(F32).reshape(nsg, SSM_PACK, SSM_STATE, SSM_GROUP),
                          eye).reshape(nsg, LANES, ns)

    def expand_c(c):
        return jnp.einsum("sgnp,gh->sgnhp", c.astype(F32).reshape(nsg, SSM_PACK, SSM_GROUP, SSM_STATE),
                          eye).reshape(nsg, LANES, ns)

    bfull = jnp.stack([expand_b(b_re), expand_b(b_im)], axis=1)
    cfull = jnp.stack([expand_c(c_re), expand_c(c_im)], axis=1)
    bd, wz, wyt, laml = _ssm_prep(lam, bfull, cfull)

    uz = _matmul(hn, w_all, w_row_blk=layer, out_dtype=F32, row_ssq=ssq, tm=MM_TM_WIDE)
    y = _ssm_scan(uz, bd, wz, wyt, laml, d_skip.astype(F32).reshape(1, e), bsz, s)
    y2 = _matmul(y, wg_all, w_row_blk=layer, out_dtype=BF16, mode="glu", cast_a=True,
                 extras=[(uz, "tile", {"col_off": e}), (b_glu.astype(F32).reshape(1, e), "row", {})])
    return _matmul(y2, wo_all, w_row_blk=layer, out_dtype=F32, mode="residual",
                   extras=[(h, "tile", {})], norm_gain=ple_gain, w_outer=True)


def kernel(x, p, norm_mix, fox_w_in, fox_b_f, fox_w_out, gdn_w_in, gdn_conv, gdn_a_log, gdn_dt_bias, gdn_norm, gdn_w_out, ssm_w_in, ssm_lam_re, ssm_lam_im, ssm_b_re, ssm_b_im, ssm_c_re, ssm_c_im, ssm_log_step, ssm_d, ssm_w_glu, ssm_b_glu, ssm_w_out, norm_ple, ple_w_proj, ple_w_gate, final_norm):
    bsz, s, d = x.shape
    depth, m = p.shape[0], bsz * s
    pdim = p.shape[-1]
    fox_width, gdn_width = fox_w_out.shape[1], gdn_w_out.shape[1]
    fox_wt, gdn_wt = jnp.swapaxes(fox_w_in, 1, 2), jnp.swapaxes(gdn_w_in, 1, 2)
    pad_rows = lambda t: jnp.pad(t, ((0, 0), (0, LANES - t.shape[1]), (0, 0)))
    fox_w = _cast_bf16(fox_wt, 4 * fox_width)
    fox_tail = _cast_bf16(pad_rows(fox_wt[:, 4 * fox_width:]))
    gdn_w = _cast_bf16(gdn_wt, 4 * gdn_width)
    gdn_tail = _cast_bf16(pad_rows(gdn_wt[:, 4 * gdn_width:]))
    fox_wo, gdn_wo = _cast_bf16(fox_w_out), _cast_bf16(gdn_w_out)
    ssm_w, ssm_wg, ssm_wo = _cast_bf16(ssm_w_in), _cast_bf16(ssm_w_glu), _cast_bf16(ssm_w_out)
    ple_wg, ple_wp = _cast_bf16(ple_w_gate), _cast_bf16(ple_w_proj)
    p_rows = p.reshape(depth * m, pdim)

    h = x.reshape(m, d).astype(F32)
    hn, ssq = _rmsnorm(h, norm_mix[0], BF16), None
    for i in range(depth):
        kind, j = i % 3, i // 3
        if kind == 0:
            h, hg, hssq = _fox_layer(h, hn, ssq, fox_w, j, fox_tail[j * LANES:(j + 1) * LANES],
                                     fox_b_f[j], fox_wo, norm_ple[i], bsz, s)
        elif kind == 1:
            h, hg, hssq = _gdn_layer(h, hn, ssq, gdn_w, j, gdn_tail[j * LANES:(j + 1) * LANES],
                                     gdn_conv[j], gdn_a_log[j], gdn_dt_bias[j], gdn_norm[j], gdn_wo,
                                     norm_ple[i], bsz, s)
        else:
            h, hg, hssq = _ssm_layer(h, hn, ssq, ssm_w, j, ssm_lam_re[j], ssm_lam_im[j], ssm_b_re[j],
                                     ssm_b_im[j], ssm_c_re[j], ssm_c_im[j], ssm_log_step[j], ssm_d[j],
                                     ssm_wg, ssm_b_glu[j], ssm_wo, norm_ple[i], bsz, s)
        extras = [(h, "tile", {}), (p_rows, "rows", {"row_blk": i * (m // min(MM_TM, m))}),
                  (ple_wp, "cols", {"row_blk": i, "rows": pdim})]
        outs = _matmul(hg, ple_wg, w_row_blk=i, out_dtype=F32, mode="ple", row_ssq=hssq, extras=extras,
                       norm_gain=norm_mix[i + 1] if i + 1 < depth else None, w_outer=True)
        h, hn, ssq = outs if i + 1 < depth else (outs, None, None)
    return _rmsnorm(h, final_norm, F32).reshape(bsz, s, d)
```

```python
import functools
import math

import jax
import jax.numpy as jnp
from jax import lax
from jax.experimental import pallas as pl
from jax.experimental.pallas import tpu as pltpu

F32 = jnp.float32
BF16 = jnp.bfloat16

LANES = 128
SUBLANES = 8
VMEM_LIMIT_BYTES = 56 * 1024 * 1024

CAST_BLOCK_ELEMS = 1 << 20
A_PREP_ROWS = 32
MM_TM = 512
MM_TM_WIDE = 1024
MM_TN = 1024
NORM_EPS = 1e-6
LOG2E = 1.4426950408889634
HEAD_DIM = 128
FOX_BLOCK = 1024
FOX_PAIRS_PER_TRIP = 6
GDN_CHUNK = 64
GDN_CONV = 4
GDN_HEADS_PER_STEP = 16
GDN_TIME_BLOCK = 256
SSM_GROUP = 16
SSM_STATE = 64
SSM_PACK = LANES // SSM_GROUP
SSM_L = 16
NEG_BIG = -1e30

HIGHEST = lax.Precision.HIGHEST


def _cparams(*sem):
    return pltpu.CompilerParams(dimension_semantics=sem, vmem_limit_bytes=VMEM_LIMIT_BYTES)


def _sigmoid(x):
    return 0.5 * jnp.tanh(0.5 * x) + 0.5


def _silu(x):
    hx = 0.5 * x
    return hx + hx * jnp.tanh(hx)


def _softplus(x):
    return jnp.maximum(x, 0.0) + jnp.log(1.0 + jnp.exp(-jnp.abs(x)))


def _rmsnorm_kernel(x_ref, g_ref, o_ref):
    x = x_ref[...]
    ms = jnp.mean(x * x, axis=-1, keepdims=True)
    o_ref[...] = (x * lax.rsqrt(ms + NORM_EPS) * g_ref[...]).astype(o_ref.dtype)


def _rmsnorm(x, g, out_dtype):
    m, d = x.shape
    tm = min(512, m)
    return pl.pallas_call(
        _rmsnorm_kernel,
        grid=(m // tm,),
        in_specs=[pl.BlockSpec((tm, d), lambda i: (i, 0)),
                  pl.BlockSpec((1, d), lambda i: (0, 0))],
        out_specs=pl.BlockSpec((tm, d), lambda i: (i, 0)),
        out_shape=jax.ShapeDtypeStruct((m, d), out_dtype),
        compiler_params=_cparams("parallel"),
        name="rmsnorm",
    )(x, g.reshape(1, d).astype(F32))


def _cast_kernel(x_ref, o_ref):
    o_ref[...] = x_ref[...].astype(o_ref.dtype)


def _cast_bf16(x, rows=None):
    nl, r, c = x.shape
    rows = r if rows is None else rows
    tc = min(2048, c)
    tr = min(rows, CAST_BLOCK_ELEMS // tc)
    assert rows % tr == 0 and c % tc == 0
    out = pl.pallas_call(
        _cast_kernel,
        grid=(nl, rows // tr, c // tc),
        in_specs=[pl.BlockSpec((1, tr, tc), lambda l, i, j: (l, i, j))],
        out_specs=pl.BlockSpec((1, tr, tc), lambda l, i, j: (l, i, j)),
        out_shape=jax.ShapeDtypeStruct((nl, rows, c), BF16),
        compiler_params=_cparams("parallel", "parallel", "parallel"),
        name="cast_bf16",
    )(x)
    return out.reshape(nl * rows, c)


_N_EXTRA = {"plain": 0, "colscale": 1, "residual": 1, "ple": 3, "glu": 2}


def _mm_kernel(*refs, mode, cast_a, w_t, has_ssq, emit_norm, inv_k):
    it = iter(refs)
    a_ref, w_ref = next(it), next(it)
    ssq_ref = next(it) if has_ssq else None
    extra = [next(it) for _ in range(_N_EXTRA[mode])]
    gain_ref = next(it) if emit_norm else None
    o_ref = next(it)
    hg_ref, ssq_out_ref = (next(it), next(it)) if emit_norm else (None, None)
    if cast_a:
        a_scr = next(it)

        @pl.when(pl.program_id(1) == 0)
        def _():
            def prep(r, carry):
                rows = pl.ds(pl.multiple_of(r * A_PREP_ROWS, A_PREP_ROWS), A_PREP_ROWS)
                a_scr[rows, :] = a_ref[rows, :].astype(BF16)
                return carry

            lax.fori_loop(0, a_ref.shape[0] // A_PREP_ROWS, prep, 0)

        a = a_scr[...]
    else:
        a = a_ref[...]

    acc = lax.dot_general(a, w_ref[...], (((1,), (1 if w_t else 0,)), ((), ())),
                          preferred_element_type=F32)
    if has_ssq:
        parts = ssq_ref[...]
        tot = parts[:, 0:LANES]
        for t in range(1, parts.shape[1] // LANES):
            tot = tot + parts[:, t * LANES:(t + 1) * LANES]
        acc = acc * lax.rsqrt(tot[:, 0:1] * inv_k + NORM_EPS)
    if mode == "plain":
        out = acc
    elif mode == "colscale":
        out = acc * extra[0][...]
    elif mode == "residual":
        out = extra[0][...] + acc
    elif mode == "ple":
        res_ref, p_ref, wp_ref = extra
        emb = jnp.dot(p_ref[...].astype(BF16), wp_ref[...], preferred_element_type=F32)
        out = res_ref[...] + _sigmoid(acc) * emb
    elif mode == "glu":
        z_ref, b_ref = extra
        tn = o_ref.shape[1]
        y = a_ref[:, pl.ds(pl.multiple_of(pl.program_id(1) * tn, tn), tn)]
        out = y * _sigmoid(acc + b_ref[...]) * _silu(z_ref[...])
    else:
        raise ValueError(mode)
    o_ref[...] = out.astype(o_ref.dtype)
    if emit_norm:
        hg_ref[...] = (out * gain_ref[...]).astype(hg_ref.dtype)
        ssq_out_ref[...] = jnp.broadcast_to(jnp.sum(out * out, axis=-1, keepdims=True),
                                            ssq_out_ref.shape)


def _extra_spec(arr, kind, tm, tn, row_blk=0, col_off=0, rows=None):
    if kind == "tile":
        return pl.BlockSpec((tm, tn), lambda i, j: (i + row_blk, j + col_off // tn))
    if kind == "row":
        return pl.BlockSpec((1, tn), lambda i, j: (0, j))
    if kind == "rows":
        return pl.BlockSpec((tm, arr.shape[1]), lambda i, j: (i + row_blk, 0))
    if kind == "cols":
        return pl.BlockSpec((rows, tn), lambda i, j: (row_blk, j))
    raise ValueError(kind)


def _matmul(a, w, *, out_dtype, n=None, mode="plain", extras=(), cast_a=False, row_ssq=None,
            norm_gain=None, w_t=None, w_row_blk=0, w_col_off=0, w_outer=False, tm=MM_TM, tn=MM_TN):
    m, kdim = a.shape
    n = (w_t if w_t else w.shape[1]) if n is None else n
    tm = min(tm, m)
    tn = functools.reduce(math.gcd, [tn, n, w_col_off, w_t or 0])
    assert m % tm == 0 and tn % LANES == 0 and len(extras) == _N_EXTRA[mode]
    operands = [a, w]
    if w_t:
        first = (w_row_blk * w_t + w_col_off) // tn
        w_spec = pl.BlockSpec((tn, kdim), lambda i, j: (first + j, 0))
    else:
        w_spec = pl.BlockSpec((kdim, tn), lambda i, j: (w_row_blk, j + w_col_off // tn))
    in_specs = [pl.BlockSpec((tm, kdim), lambda i, j: (i, 0)), w_spec]
    if row_ssq is not None:
        operands.append(row_ssq)
        in_specs.append(pl.BlockSpec((tm, row_ssq.shape[1]), lambda i, j: (i, 0)))
    for arr, kind, opts in extras:
        assert opts.get("col_off", 0) % tn == 0
        operands.append(arr)
        in_specs.append(_extra_spec(arr, kind, tm, tn, **opts))
    tile = pl.BlockSpec((tm, tn), lambda i, j: (i, j))
    out_specs, out_shape = tile, jax.ShapeDtypeStruct((m, n), out_dtype)
    if norm_gain is not None:
        operands.append(norm_gain.astype(F32).reshape(1, n))
        in_specs.append(pl.BlockSpec((1, tn), lambda i, j: (0, j)))
        out_specs = [tile, tile, pl.BlockSpec((tm, LANES), lambda i, j: (i, j))]
        out_shape = [out_shape, jax.ShapeDtypeStruct((m, n), BF16),
                     jax.ShapeDtypeStruct((m, n // tn * LANES), F32)]
    scratch = [pltpu.VMEM((tm, kdim), BF16)] if cast_a else []
    grid = (m // tm, n // tn)
    if w_outer:
        assert not cast_a
        flip = lambda sp: pl.BlockSpec(sp.block_shape, lambda j, i, f=sp.index_map: f(i, j))
        in_specs = [flip(sp) for sp in in_specs]
        out_specs = [flip(sp) for sp in out_specs] if norm_gain is not None else flip(out_specs)
        grid = grid[::-1]
    return pl.pallas_call(
        functools.partial(_mm_kernel, mode=mode, cast_a=cast_a, w_t=bool(w_t),
                          has_ssq=row_ssq is not None,
                          emit_norm=norm_gain is not None, inv_k=1.0 / kdim),
        grid=grid,
        in_specs=in_specs,
        out_specs=out_specs,
        out_shape=out_shape,
        scratch_shapes=scratch,
        compiler_params=_cparams("parallel", "arbitrary"),
        name="matmul_" + mode,
    )(*operands)


def _fox_cum_kernel(f_ref, b_ref, o_ref, carry_ref, *, ts):
    @pl.when(pl.program_id(1) == 0)
    def _():
        carry_ref[...] = jnp.zeros_like(carry_ref)

    x = f_ref[...] + b_ref[...]
    log_f = jnp.minimum(x, 0.0) - jnp.log(1.0 + jnp.exp(-jnp.abs(x)))
    r = lax.broadcasted_iota(jnp.int32, (ts, ts), 0)
    c = lax.broadcasted_iota(jnp.int32, (ts, ts), 1)
    lower = (c <= r).astype(F32)
    cum = jnp.dot(lower, log_f, precision=HIGHEST, preferred_element_type=F32) + carry_ref[...]
    carry_ref[...] = cum[ts - 1:ts, :]
    o_ref[0] = cum.T


def _fox_cum(f_logit, b_f, bsz, s):
    ts = min(1024, s)
    return pl.pallas_call(
        functools.partial(_fox_cum_kernel, ts=ts),
        grid=(bsz, s // ts),
        in_specs=[pl.BlockSpec((ts, LANES), lambda b, t: (b * (s // ts) + t, 0)),
                  pl.BlockSpec((1, LANES), lambda b, t: (0, 0))],
        out_specs=pl.BlockSpec((1, LANES, ts), lambda b, t: (b, 0, t)),
        out_shape=jax.ShapeDtypeStruct((bsz, LANES, s), F32),
        scratch_shapes=[pltpu.VMEM((1, LANES), F32)],
        compiler_params=_cparams("parallel", "arbitrary"),
        name="fox_cum",
    )(f_logit, b_f)


def _fox_attn_kernel(q_ref, k_ref, v_ref, c_ref, z_ref, o_ref, sa_ref, sb_ref, *, tq):
    tk = tq // 2
    i = pl.program_id(2)
    q = q_ref[...]
    c0 = c_ref[0, 0, :, pl.ds(pl.multiple_of(i * tq, tq), LANES)][:, 0:1]

    def scores(qrows, j):
        start = j * tk if isinstance(j, int) else pl.multiple_of(j * tk, tk)
        s = lax.dot_general(qrows, k_ref[pl.ds(start, tk), :], (((1,), (1,)), ((), ())),
                            preferred_element_type=F32)
        return s + (c0 - c_ref[0, 0, :, pl.ds(start, tk)]) * LOG2E

    def update(s, j, m, acc, width=1):
        rows = width * tk
        start = j * tk if isinstance(j, int) else pl.multiple_of(j * tk, tk)
        vb = jnp.concatenate([v_ref[pl.ds(start, rows), :], jnp.ones((rows, HEAD_DIM), BF16)], axis=1)
        m_new = jnp.maximum(m, jnp.max(s, axis=-1, keepdims=True))
        p = jnp.exp2(s - m_new)
        acc = jnp.exp2(m - m_new) * acc + jnp.dot(p.astype(BF16), vb, preferred_element_type=F32)
        return m_new, acc

    sa_ref[...] = scores(q, 0)

    def pair(t, carry):
        m, acc = carry
        sb_ref[...] = scores(q, 2 * t + 1)
        m, acc = update(sa_ref[...], 2 * t, m, acc)
        sa_ref[...] = scores(q, 2 * t + 2)
        return update(sb_ref[...], 2 * t + 1, m, acc)

    def trip(t, carry):
        for n in range(FOX_PAIRS_PER_TRIP):
            carry = pair(FOX_PAIRS_PER_TRIP * t + n, carry)
        return carry

    init = (jnp.full((tq, 1), NEG_BIG, F32), jnp.zeros((tq, 2 * HEAD_DIM), F32))
    carry = lax.fori_loop(0, i // FOX_PAIRS_PER_TRIP, trip, init)

    def finish(a, rows):
        o = a[:, 0:HEAD_DIM] / a[:, HEAD_DIM:HEAD_DIM + 1]
        o_ref[rows, :] = (o * _silu(z_ref[rows, :])).astype(o_ref.dtype)

    def tail(carry, left):
        for n in range(left):
            carry = pair(i - left + n, carry)
        m, acc = carry
        r = lax.broadcasted_iota(jnp.int32, (tk, tk), 0)
        c = lax.broadcasted_iota(jnp.int32, (tk, tk), 1)
        s_bot = jnp.concatenate(
            [sa_ref[tk:tq, :], jnp.where(c <= r, scores(q[tk:tq], 2 * i + 1), NEG_BIG)], axis=1)
        _, acc_top = update(jnp.where(c <= r, sa_ref[0:tk, :], NEG_BIG), 2 * i, m[0:tk], acc[0:tk])
        _, acc_bot = update(s_bot, 2 * i, m[tk:tq], acc[tk:tq], width=2)
        finish(acc_top, slice(0, tk))
        finish(acc_bot, slice(tk, tq))

    def dispatch(carry, lo, hi):
        if hi - lo == 1:
            tail(carry, lo)
        else:
            mid = (lo + hi) // 2
            lax.cond(i % FOX_PAIRS_PER_TRIP < mid, lambda cr: dispatch(cr, lo, mid),
                     lambda cr: dispatch(cr, mid, hi), carry)

    dispatch(carry, 0, FOX_PAIRS_PER_TRIP)


def _fox_attention(qkv, cum4, z, bsz, s, heads):
    tq = min(FOX_BLOCK, s)
    nq = s // tq
    return pl.pallas_call(
        functools.partial(_fox_attn_kernel, tq=tq),
        grid=(bsz, heads, nq),
        in_specs=[pl.BlockSpec((tq, HEAD_DIM), lambda b, h, i: (b * nq + i, h)),
                  pl.BlockSpec((s, HEAD_DIM), lambda b, h, i: (b, heads + h)),
                  pl.BlockSpec((s, HEAD_DIM), lambda b, h, i: (b, 2 * heads + h)),
                  pl.BlockSpec((1, 1, 1, s), lambda b, h, i: (b, h, 0, 0)),
                  pl.BlockSpec((tq, HEAD_DIM), lambda b, h, i: (b * nq + i, h))],
        out_specs=pl.BlockSpec((tq, HEAD_DIM), lambda b, h, i: (b * nq + i, h)),
        out_shape=jax.ShapeDtypeStruct((bsz * s, heads * HEAD_DIM), BF16),
        scratch_shapes=[pltpu.VMEM((tq, tq // 2), F32)] * 2,
        compiler_params=_cparams("parallel", "parallel", "arbitrary"),
        name="fox_attn",
    )(qkv, qkv, qkv, cum4, z)


def _fox_layer(h, hn, ssq, w_all, layer, w_tail, b_f, wo_all, ple_gain, bsz, s):
    width = wo_all.shape[1]
    heads = width // HEAD_DIM
    q_scale = jnp.concatenate([jnp.full((width,), HEAD_DIM ** -0.5 * LOG2E, F32),
                               jnp.ones((2 * width,), F32)]).reshape(1, 3 * width)
    qkv = _matmul(hn, w_all, n=3 * width, w_t=4 * width, w_row_blk=layer, out_dtype=BF16,
                  mode="colscale", extras=[(q_scale, "row", {})], row_ssq=ssq, tm=MM_TM_WIDE)
    z = _matmul(hn, w_all, n=width, w_t=4 * width, w_row_blk=layer, w_col_off=3 * width,
                out_dtype=F32, row_ssq=ssq, w_outer=True, tm=MM_TM_WIDE)
    f_logit = _matmul(hn, w_tail, w_t=LANES, out_dtype=F32, row_ssq=ssq, tm=MM_TM_WIDE)
    b_row = jnp.pad(b_f.astype(F32), (0, LANES - heads)).reshape(1, LANES)
    cum = _fox_cum(f_logit, b_row, bsz, s)
    gated = _fox_attention(qkv, cum.reshape(bsz, LANES, 1, s), z, bsz, s, heads)
    return _matmul(gated, wo_all, w_row_blk=layer, out_dtype=F32, mode="residual",
                   extras=[(h, "tile", {})], norm_gain=ple_gain, w_outer=True)


def _bdot(a, b):
    return jnp.einsum("bmk,bkn->bmn", a, b, preferred_element_type=F32)


def _split_bf16(x):
    hi = x.astype(BF16)
    return hi, (x - hi.astype(F32)).astype(BF16)


def _bdot3_parts(a_hi, a_lo, b_hi, b_lo):
    return _bdot(a_hi, b_hi) + _bdot(a_hi, b_lo) + _bdot(a_lo, b_hi)


def _bdot3(a, b):
    return _bdot3_parts(*_split_bf16(a), *_split_bf16(b))


def _bdot_exact_lhs(a_bf16, b):
    b_hi = b.astype(BF16)
    b_mid, b_lo = _split_bf16(b - b_hi.astype(F32))
    return _bdot(a_bf16, b_hi) + _bdot(a_bf16, b_mid) + _bdot(a_bf16, b_lo)


def _gdn_kernel(q_ref, k_ref, v_ref, z_ref, t_ref, cw_ref, gp_ref, nw_ref, o_ref,
                xq_ref, xk_ref, xv_ref, st_ref, u_ref, wq_ref, ik_ref, gl_ref, *, tb, heads, hps):
    hb = pl.program_id(1)
    c, dh = GDN_CHUNK, HEAD_DIM
    nb = tb // c
    nbat = hps * nb
    pad = SUBLANES

    @pl.when(pl.program_id(2) == 0)
    def _():
        for r in (xq_ref, xk_ref, xv_ref):
            r[0:pad, :] = jnp.zeros((pad, hps * dh), F32)
        st_ref[...] = jnp.zeros_like(st_ref)

    def conv(x_ref, xs_ref, col):
        x = x_ref[...]
        xs_ref[pad:pad + tb, :] = x
        xs = xs_ref[...]
        y = x * cw_ref[col, GDN_CONV - 1:GDN_CONV, :]
        for j in range(GDN_CONV - 1):
            shifted = pltpu.roll(xs, GDN_CONV - 1 - j, axis=0)[pad:pad + tb, :]
            y = y + shifted * cw_ref[col, j:j + 1, :]
        xs_ref[0:pad, :] = x[tb - pad:tb, :]
        return _silu(y)

    def by_chunk(x):
        w = x.shape[1] // hps
        return jnp.concatenate([x[:, g * w:(g + 1) * w].reshape(nb, c, w) for g in range(hps)], axis=0)

    def l2n(x):
        return x * lax.rsqrt(jnp.sum(x * x, axis=-1, keepdims=True) + NORM_EPS)

    q = l2n(by_chunk(conv(q_ref, xq_ref, 0))) * (dh ** -0.5)
    k = l2n(by_chunk(conv(k_ref, xk_ref, 1)))
    v = by_chunk(conv(v_ref, xv_ref, 2))

    t = t_ref[...]
    lane = lax.broadcasted_iota(jnp.int32, (tb, LANES), 1)
    beta_all = _sigmoid(t)
    g_all = -jnp.exp(gp_ref[0:1, :]) * _softplus(t + gp_ref[1:2, :])

    def pick(x, first):
        cols = [jnp.sum(jnp.where(lane == first + hb * hps + g, x, 0.0), axis=-1, keepdims=True)
                for g in range(hps)]
        return jnp.concatenate([col.reshape(nb, c, 1) for col in cols], axis=0)

    beta = pick(beta_all, 0)
    g = pick(g_all, heads)

    ri = lax.broadcasted_iota(jnp.int32, (c, c), 0)
    ci = lax.broadcasted_iota(jnp.int32, (c, c), 1)
    incl = ci <= ri
    strict = ci < ri
    eye = (ci == ri).astype(F32)
    tri = jnp.broadcast_to(incl.astype(BF16), (nbat, c, c))
    rl = lax.broadcasted_iota(jnp.int32, (c, LANES), 0)
    cl = lax.broadcasted_iota(jnp.int32, (c, LANES), 1)
    seg_mask = (rl > cl) | (cl == c)

    seg = _bdot_exact_lhs(tri, jnp.where(seg_mask, g, 0.0))
    gc = seg[:, :, c:c + 1]
    g_last = seg[:, c - 1:c, c:c + 1]
    decay = jnp.exp(jnp.where(incl, seg[:, :, 0:c], NEG_BIG))
    egc = jnp.exp(gc)
    k_beta = k * beta
    kq = jnp.einsum("bmd,bnd->bmn", jnp.concatenate([k_beta, q], axis=1).astype(BF16),
                    k.astype(BF16), preferred_element_type=F32)
    a = jnp.where(strict, kq[:, 0:c] * decay, 0.0)
    intra = kq[:, c:2 * c] * decay
    npair = nbat // 2
    pair_lanes = lambda x: jnp.concatenate(
        [x.reshape(npair, 2, c, x.shape[2])[:, 0], x.reshape(npair, 2, c, x.shape[2])[:, 1]], axis=2)
    left = lax.broadcasted_iota(jnp.int32, (c, 2 * c), 1) < c

    def block_diag(x2):
        zero = jnp.zeros_like(x2)
        return jnp.concatenate([jnp.where(left, x2, zero), jnp.where(left, zero, x2)], axis=1)

    a2 = pair_lanes(a)
    inv2 = jnp.concatenate([eye, eye], axis=1) - a2
    a_hi, a_lo = _split_bf16(a2)
    pw2 = _bdot3_parts(a_hi, a_lo, block_diag(a_hi), block_diag(a_lo))
    levels = int(math.log2(c)) - 1
    for lvl in range(levels):
        inv_hi, inv_lo = _split_bf16(inv2)
        pw_hi, pw_lo = _split_bf16(pw2)
        pd_hi, pd_lo = block_diag(pw_hi), block_diag(pw_lo)
        if lvl < levels - 1:
            both = _bdot3_parts(jnp.concatenate([inv_hi, pw_hi], axis=1),
                                jnp.concatenate([inv_lo, pw_lo], axis=1), pd_hi, pd_lo)
            inv2 = inv2 + both[:, 0:c]
            pw2 = both[:, c:2 * c]
        else:
            inv2 = inv2 + _bdot3_parts(inv_hi, inv_lo, pd_hi, pd_lo)
    rhs = jnp.concatenate([v * beta, k_beta * egc], axis=2).reshape(npair, 2, c, 2 * dh)
    zero = jnp.zeros((npair, c, 2 * dh), F32)
    rhs_bd = jnp.concatenate([jnp.concatenate([rhs[:, 0], zero], axis=2),
                              jnp.concatenate([zero, rhs[:, 1]], axis=2)], axis=1)
    uw2 = _bdot3(inv2, rhs_bd)
    uw = jnp.stack([uw2[:, :, 0:2 * dh], uw2[:, :, 2 * dh:4 * dh]], axis=1).reshape(nbat, c, 2 * dh)
    u_ref[...] = uw[:, :, 0:dh].reshape(nbat * c, dh)
    wq_ref[...] = jnp.concatenate([uw[:, :, dh:2 * dh], q * egc], axis=1).astype(BF16).reshape(
        nbat * 2 * c, dh)
    k_dec = k * jnp.exp(g_last - gc)
    for b in range(nbat):
        ik_ref[b * (c + dh):(b + 1) * (c + dh), :] = jnp.concatenate(
            [intra[b], k_dec[b].T], axis=0).astype(BF16)
    gl_ref[...] = jnp.broadcast_to(jnp.exp(g_last), (nbat, SUBLANES, dh)).reshape(nbat * SUBLANES, dh)

    def step(n, states):
        rows = lambda g, size: pl.ds(pl.multiple_of((g * nb + n) * size, size), size)
        ws = [jnp.dot(wq_ref[rows(g, 2 * c), :], states[g].astype(BF16), preferred_element_type=F32)
              for g in range(hps)]
        v_new = [u_ref[rows(g, c), :] - ws[g][0:c] for g in range(hps)]
        iv = [jnp.dot(ik_ref[pl.ds(pl.multiple_of((g * nb + n) * (c + dh), c), c + dh), :],
                      v_new[g].astype(BF16), preferred_element_type=F32) for g in range(hps)]
        new_states = []
        r0 = pl.multiple_of(n * c, c)
        for g in range(hps):
            out = ws[g][c:2 * c] + iv[g][0:c]
            gl = gl_ref[pl.ds(pl.multiple_of((g * nb + n) * SUBLANES, SUBLANES), 1), :]
            new_states.append(states[g] * gl + iv[g][c:c + dh])
            ms = jnp.mean(out * out, axis=-1, keepdims=True)
            o = out * lax.rsqrt(ms + NORM_EPS) * nw_ref[...]
            o_ref[pl.ds(r0, c), g * dh:(g + 1) * dh] = (
                o * _silu(z_ref[pl.ds(r0, c), g * dh:(g + 1) * dh])).astype(o_ref.dtype)
        return tuple(new_states)

    states = lax.fori_loop(0, nb, step, tuple(st_ref[g] for g in range(hps)))
    for g in range(hps):
        st_ref[g] = states[g]


def _gdn_mixer(qkvz, tail, conv_w, gate_p, norm_w, bsz, s, heads):
    hps = math.gcd(GDN_HEADS_PER_STEP, heads)
    tb = min(GDN_TIME_BLOCK, s)
    nt = s // tb
    hblocks = heads // hps
    width = hps * HEAD_DIM
    nbat = hps * (tb // GDN_CHUNK)
    col = lambda off: pl.BlockSpec((tb, width), lambda b, h, t: (b * nt + t, off + h))
    return pl.pallas_call(
        functools.partial(_gdn_kernel, tb=tb, heads=heads, hps=hps),
        grid=(bsz, hblocks, nt),
        in_specs=[col(0), col(hblocks), col(2 * hblocks), col(3 * hblocks),
                  pl.BlockSpec((tb, LANES), lambda b, h, t: (b * nt + t, 0)),
                  pl.BlockSpec((3, GDN_CONV, width), lambda b, h, t: (0, 0, h)),
                  pl.BlockSpec((2, LANES), lambda b, h, t: (0, 0)),
                  pl.BlockSpec((1, HEAD_DIM), lambda b, h, t: (0, 0))],
        out_specs=pl.BlockSpec((tb, width), lambda b, h, t: (b * nt + t, h)),
        out_shape=jax.ShapeDtypeStruct((bsz * s, heads * HEAD_DIM), BF16),
        scratch_shapes=[pltpu.VMEM((tb + SUBLANES, width), F32)] * 3
                       + [pltpu.VMEM((hps, HEAD_DIM, HEAD_DIM), F32),
                          pltpu.VMEM((nbat * GDN_CHUNK, HEAD_DIM), F32),
                          pltpu.VMEM((nbat * 2 * GDN_CHUNK, HEAD_DIM), BF16),
                          pltpu.VMEM((nbat * (GDN_CHUNK + HEAD_DIM), GDN_CHUNK), BF16),
                          pltpu.VMEM((nbat * SUBLANES, HEAD_DIM), F32)],
        compiler_params=_cparams("parallel", "parallel", "arbitrary"),
        name="gdn_mixer",
    )(qkvz, qkvz, qkvz, qkvz, tail, conv_w, gate_p, norm_w)


def _gdn_layer(h, hn, ssq, w_all, layer, w_tail, conv_w, a_log, dt_bias, norm_w, wo_all, ple_gain,
               bsz, s):
    width = wo_all.shape[1]
    heads = width // HEAD_DIM
    qkvz = _matmul(hn, w_all, w_t=4 * width, w_row_blk=layer, out_dtype=F32, row_ssq=ssq,
                   tm=MM_TM_WIDE)
    tail = _matmul(hn, w_tail, w_t=LANES, out_dtype=F32, row_ssq=ssq, tm=MM_TM_WIDE)
    cw = conv_w.astype(F32).reshape(GDN_CONV, 3, width).transpose(1, 0, 2)
    lane_pad = lambda x: jnp.pad(x.astype(F32), (heads, LANES - 2 * heads))
    gate_p = jnp.stack([lane_pad(a_log), lane_pad(dt_bias)])
    gated = _gdn_mixer(qkvz, tail, cw, gate_p, norm_w.astype(F32).reshape(1, HEAD_DIM),
                       bsz, s, heads)
    return _matmul(gated, wo_all, w_row_blk=layer, out_dtype=F32, mode="residual",
                   extras=[(h, "tile", {})], norm_gain=ple_gain, w_outer=True)


def _ssm_prep_kernel(lam_ref, bf_ref, cf_ref, bd_ref, wz_ref, wyt_ref, laml_ref):
    ns = SSM_PACK * SSM_STATE
    lam_re, lam_im, step_log = lam_ref[0, 0:1, :], lam_ref[0, 1:2, :], lam_ref[0, 2:3, :]
    step = jnp.exp(step_log)
    mag = jnp.exp(lam_re * step)
    lb_re, lb_im = mag * jnp.cos(lam_im * step), mag * jnp.sin(lam_im * step)
    den = lam_re * lam_re + lam_im * lam_im
    num_re = lb_re - 1.0
    zoh_re = (num_re * lam_re + lb_im * lam_im) / den
    zoh_im = (lb_im * lam_re - num_re * lam_im) / den
    b_re, b_im = bf_ref[0, 0], bf_ref[0, 1]
    bb_re = zoh_re * b_re - zoh_im * b_im
    bb_im = zoh_re * b_im + zoh_im * b_re
    c_re, c_im = cf_ref[0, 0], cf_ref[0, 1]
    cc_hi, cc_lo = _split_bf16(jnp.concatenate([c_re, -c_im], axis=1))
    nt_dot = lambda x, y: lax.dot_general(x, y, (((1,), (1,)), ((), ())), preferred_element_type=F32)

    bd_ref[0, 0, LANES:2 * LANES, 0:LANES] = jnp.zeros((LANES, LANES), bd_ref.dtype)
    pw_re, pw_im = jnp.ones_like(lb_re), jnp.zeros_like(lb_re)
    for d in range(SSM_L + 1):
        if d < SSM_L:
            a_re = pw_re * bb_re - pw_im * bb_im
            a_im = pw_re * bb_im + pw_im * bb_re
            a_d = jnp.concatenate([a_re, a_im], axis=1)
            a_hi, a_lo = _split_bf16(a_d)
            blk = (nt_dot(a_hi, cc_hi) + nt_dot(a_hi, cc_lo) + nt_dot(a_lo, cc_hi)).astype(bd_ref.dtype)
            d2 = d // 2
            if d % 2 == 0:
                bd_ref[0, d2, 0:LANES, 0:LANES] = blk
                bd_ref[0, d2, LANES:2 * LANES, LANES:2 * LANES] = blk
            else:
                bd_ref[0, d2, 0:LANES, LANES:2 * LANES] = blk
                if d2 + 1 < SSM_L // 2:
                    bd_ref[0, d2 + 1, LANES:2 * LANES, 0:LANES] = blk
            tin = SSM_L - 1 - d
            wz_ref[0, tin * LANES:(tin + 1) * LANES, :] = a_d.astype(wz_ref.dtype)
        if d >= 1:
            y_re = pw_re * c_re - pw_im * c_im
            y_im = pw_re * c_im + pw_im * c_re
            wyt_ref[0, (d - 1) * LANES:d * LANES, :] = jnp.concatenate(
                [y_re, -y_im], axis=1).astype(wyt_ref.dtype)
        if d == SSM_L:
            laml_ref[0] = jnp.concatenate([pw_re, pw_im], axis=1)
        pw_re, pw_im = pw_re * lb_re - pw_im * lb_im, pw_re * lb_im + pw_im * lb_re


def _ssm_prep(lam, bfull, cfull):
    nsg = lam.shape[0]
    ns = SSM_PACK * SSM_STATE
    rows = SSM_L * LANES
    return pl.pallas_call(
        _ssm_prep_kernel,
        grid=(nsg,),
        in_specs=[pl.BlockSpec((1, 3, ns), lambda g: (g, 0, 0)),
                  pl.BlockSpec((1, 2, LANES, ns), lambda g: (g, 0, 0, 0)),
                  pl.BlockSpec((1, 2, LANES, ns), lambda g: (g, 0, 0, 0))],
        out_specs=[pl.BlockSpec((1, SSM_L // 2, 2 * LANES, 2 * LANES), lambda g: (g, 0, 0, 0)),
                   pl.BlockSpec((1, rows, 2 * ns), lambda g: (g, 0, 0)),
                   pl.BlockSpec((1, rows, 2 * ns), lambda g: (g, 0, 0)),
                   pl.BlockSpec((1, 1, 2 * ns), lambda g: (g, 0, 0))],
        out_shape=[jax.ShapeDtypeStruct((nsg, SSM_L // 2, 2 * LANES, 2 * LANES), BF16),
                   jax.ShapeDtypeStruct((nsg, rows, 2 * ns), BF16),
                   jax.ShapeDtypeStruct((nsg, rows, 2 * ns), BF16),
                   jax.ShapeDtypeStruct((nsg, 1, 2 * ns), F32)],
        compiler_params=_cparams("parallel"),
        name="ssm_prep",
    )(lam, bfull, cfull)


def _ssm_scan_kernel(u_ref, bd_ref, wz_ref, wyt_ref, laml_ref, d_ref, o_ref,
                     x_ref, z_ref, xp_ref, st_ref, *, tc):
    ns = SSM_PACK * SSM_STATE

    @pl.when(pl.program_id(2) == 0)
    def _():
        st_ref[...] = jnp.zeros_like(st_ref)

    for tau in range(SSM_L):
        x_ref[:, tau * LANES:(tau + 1) * LANES] = u_ref[pl.ds(tau, tc, stride=SSM_L), :].astype(BF16)

    z_ref[...] = jnp.dot(x_ref[...], wz_ref[0], preferred_element_type=F32)

    a_re, a_im = laml_ref[0, :, 0:ns], laml_ref[0, :, ns:2 * ns]

    def row(r, carry):
        s_re, s_im = carry
        xp_ref[pl.ds(r, 1), :] = jnp.concatenate([s_re, s_im], axis=1)
        zr = z_ref[pl.ds(r, 1), :]
        n_re = a_re * s_re - a_im * s_im + zr[:, 0:ns]
        n_im = a_re * s_im + a_im * s_re + zr[:, ns:2 * ns]
        return n_re, n_im

    s_re, s_im = lax.fori_loop(0, tc, row, (st_ref[:, 0:ns], st_ref[:, ns:2 * ns]), unroll=4)
    st_ref[...] = jnp.concatenate([s_re, s_im], axis=1)

    xprev = xp_ref[...].astype(BF16)
    pair = 2 * LANES
    for t2 in range(SSM_L // 2):
        y2 = lax.dot_general(xprev, wyt_ref[0, t2 * pair:(t2 + 1) * pair, :],
                             (((1,), (1,)), ((), ())), preferred_element_type=F32)
        for tin in range(t2 + 1):
            y2 = y2 + jnp.dot(x_ref[:, tin * pair:(tin + 1) * pair], bd_ref[0, t2 - tin],
                              preferred_element_type=F32)
        for half in range(2):
            tau = 2 * t2 + half
            y = y2[:, half * LANES:(half + 1) * LANES]
            y = y + d_ref[...] * u_ref[pl.ds(tau, tc, stride=SSM_L), :]
            y = 0.5 * y * (1.0 + jnp.tanh(math.sqrt(2.0 / math.pi) * (y + 0.044715 * (y * y * y))))
            o_ref[pl.ds(tau, tc, stride=SSM_L), :] = y


def _ssm_scan(uz, bd, wz, wyt, laml, d_skip, bsz, s):
    e = d_skip.shape[1]
    nsg = e // LANES
    ns = SSM_PACK * SSM_STATE
    tb = min(8192, s)
    nt = s // tb
    tc = tb // SSM_L
    rows = SSM_L * LANES
    return pl.pallas_call(
        functools.partial(_ssm_scan_kernel, tc=tc),
        grid=(nsg, bsz, nt),
        in_specs=[pl.BlockSpec((tb, LANES), lambda g, b, t: (b * nt + t, g)),
                  pl.BlockSpec((1, SSM_L // 2, 2 * LANES, 2 * LANES), lambda g, b, t: (g, 0, 0, 0)),
                  pl.BlockSpec((1, rows, 2 * ns), lambda g, b, t: (g, 0, 0)),
                  pl.BlockSpec((1, rows, 2 * ns), lambda g, b, t: (g, 0, 0)),
                  pl.BlockSpec((1, 1, 2 * ns), lambda g, b, t: (g, 0, 0)),
                  pl.BlockSpec((1, LANES), lambda g, b, t: (0, g))],
        out_specs=pl.BlockSpec((tb, LANES), lambda g, b, t: (b * nt + t, g)),
        out_shape=jax.ShapeDtypeStruct((uz.shape[0], e), F32),
        scratch_shapes=[pltpu.VMEM((tc, rows), BF16), pltpu.VMEM((tc, 2 * ns), F32),
                        pltpu.VMEM((tc, 2 * ns), F32), pltpu.VMEM((1, 2 * ns), F32)],
        compiler_params=_cparams("parallel", "parallel", "arbitrary"),
        name="ssm_scan",
    )(uz, bd, wz, wyt, laml, d_skip)


def _ssm_layer(h, hn, ssq, w_all, layer, lam_re, lam_im, b_re, b_im, c_re, c_im, log_step, d_skip,
               wg_all, b_glu, wo_all, ple_gain, bsz, s):
    e = wo_all.shape[1]
    groups, nstate = lam_re.shape
    assert nstate == SSM_STATE and e == groups * SSM_GROUP and groups % SSM_PACK == 0
    nsg = groups // SSM_PACK
    ns = SSM_PACK * SSM_STATE
    lam = jnp.stack([lam_re.astype(F32).reshape(nsg, ns), lam_im.astype(F32).reshape(nsg, ns),
                     jnp.repeat(log_step.astype(F32), SSM_STATE).reshape(nsg, ns)], axis=1)
    eye = jnp.eye(SSM_PACK, dtype=F32)

    def expand_b(b):
        return jnp.einsum("sgpm,gh->sgmhp", b.astype(F32).reshape(nsg, SSM_PACK, SSM_STATE, SSM_GROUP),
                          eye).reshape(nsg, LANES, ns)

    def expand_c(c):
        return jnp.einsum("sgnp,gh->sgnhp", c.astype(F32).reshape(nsg, SSM_PACK, SSM_GROUP, SSM_STATE),
                          eye).reshape(nsg, LANES, ns)

    bfull = jnp.stack([expand_b(b_re), expand_b(b_im)], axis=1)
    cfull = jnp.stack([expand_c(c_re), expand_c(c_im)], axis=1)
    bd, wz, wyt, laml = _ssm_prep(lam, bfull, cfull)

    uz = _matmul(hn, w_all, w_row_blk=layer, out_dtype=F32, row_ssq=ssq, tm=MM_TM_WIDE)
    y = _ssm_scan(uz, bd, wz, wyt, laml, d_skip.astype(F32).reshape(1, e), bsz, s)
    y2 = _matmul(y, wg_all, w_row_blk=layer, out_dtype=BF16, mode="glu", cast_a=True,
                 extras=[(uz, "tile", {"col_off": e}), (b_glu.astype(F32).reshape(1, e), "row", {})])
    return _matmul(y2, wo_all, w_row_blk=layer, out_dtype=F32, mode="residual",
                   extras=[(h, "tile", {})], norm_gain=ple_gain, w_outer=True)


def kernel(x, p, norm_mix, fox_w_in, fox_b_f, fox_w_out, gdn_w_in, gdn_conv, gdn_a_log, gdn_dt_bias, gdn_norm, gdn_w_out, ssm_w_in, ssm_lam_re, ssm_lam_im, ssm_b_re, ssm_b_im, ssm_c_re, ssm_c_im, ssm_log_step, ssm_d, ssm_w_glu, ssm_b_glu, ssm_w_out, norm_ple, ple_w_proj, ple_w_gate, final_norm):
    bsz, s, d = x.shape
    depth, m = p.shape[0], bsz * s
    pdim = p.shape[-1]
    fox_width, gdn_width = fox_w_out.shape[1], gdn_w_out.shape[1]
    fox_wt, gdn_wt = jnp.swapaxes(fox_w_in, 1, 2), jnp.swapaxes(gdn_w_in, 1, 2)
    pad_rows = lambda t: jnp.pad(t, ((0, 0), (0, LANES - t.shape[1]), (0, 0)))
    fox_w = _cast_bf16(fox_wt, 4 * fox_width)
    fox_tail = _cast_bf16(pad_rows(fox_wt[:, 4 * fox_width:]))
    gdn_w = _cast_bf16(gdn_wt, 4 * gdn_width)
    gdn_tail = _cast_bf16(pad_rows(gdn_wt[:, 4 * gdn_width:]))
    fox_wo, gdn_wo = _cast_bf16(fox_w_out), _cast_bf16(gdn_w_out)
    ssm_w, ssm_wg, ssm_wo = _cast_bf16(ssm_w_in), _cast_bf16(ssm_w_glu), _cast_bf16(ssm_w_out)
    ple_wg, ple_wp = _cast_bf16(ple_w_gate), _cast_bf16(ple_w_proj)
    p_rows = p.reshape(depth * m, pdim)

    h = x.reshape(m, d).astype(F32)
    hn, ssq = _rmsnorm(h, norm_mix[0], BF16), None
    for i in range(depth):
        kind, j = i % 3, i // 3
        if kind == 0:
            h, hg, hssq = _fox_layer(h, hn, ssq, fox_w, j, fox_tail[j * LANES:(j + 1) * LANES],
                                     fox_b_f[j], fox_wo, norm_ple[i], bsz, s)
        elif kind == 1:
            h, hg, hssq = _gdn_layer(h, hn, ssq, gdn_w, j, gdn_tail[j * LANES:(j + 1) * LANES],
                                     gdn_conv[j], gdn_a_log[j], gdn_dt_bias[j], gdn_norm[j], gdn_wo,
                                     norm_ple[i], bsz, s)
        else:
            h, hg, hssq = _ssm_layer(h, hn, ssq, ssm_w, j, ssm_lam_re[j], ssm_lam_im[j], ssm_b_re[j],
                                     ssm_b_im[j], ssm_c_re[j], ssm_c_im[j], ssm_log_step[j], ssm_d[j],
                                     ssm_wg, ssm_b_glu[j], ssm_wo, norm_ple[i], bsz, s)
        extras = [(h, "tile", {}), (p_rows, "rows", {"row_blk": i * (m // min(MM_TM, m))}),
                  (ple_wp, "cols", {"row_blk": i, "rows": pdim})]
        outs = _matmul(hg, ple_wg, w_row_blk=i, out_dtype=F32, mode="ple", row_ssq=hssq, extras=extras,
                       norm_gain=norm_mix[i + 1] if i + 1 < depth else None, w_outer=True)
        h, hn, ssq = outs if i + 1 < depth else (outs, None, None)
    return _rmsnorm(h, final_norm, F32).reshape(bsz, s, d)
```
